```python
import jax, jax.numpy as jnp
from jax import lax
import numpy as np

D_MODEL = 1024
BATCH = 16
SEQ = 2048
DEPTH = 1

CHUNK = 64
HEAD_DIM = 64
N_HEADS_A = D_MODEL // (2 * HEAD_DIM)
N_HEADS_B = D_MODEL // (2 * HEAD_DIM)
D_A = N_HEADS_A * HEAD_DIM
D_B = N_HEADS_B * HEAD_DIM
D_MIX = D_A + D_B
LEFT_CHUNKS = 8
BAND = (LEFT_CHUNKS + 1) * CHUNK
MAX_REL = 128
SB_BLOCK = 128
N_EXPERTS = 256
TOP_K = 8
N_GROUPS = 8
TOPK_GROUPS = 4
EXPERTS_PER_GROUP = N_EXPERTS // N_GROUPS
D_EXPERT = D_MODEL // 4
ROUTED_SCALE = 2.5
ROW_BLOCK = 128
DN_ALPHA = (2 * DEPTH) ** 0.25
DN_BETA = (8 * DEPTH) ** -0.25
LN_EPS = 1e-5
RMS_EPS = 1e-6

kernel_name = 'hybrid_chunked_stickbreaking_moe_deepnorm'


def layer_norm(x, g, b):
    xf = x.astype(jnp.float32)
    mu = jnp.mean(xf, axis=-1, keepdims=True)
    var = jnp.mean(jnp.square(xf - mu), axis=-1, keepdims=True)
    return ((xf - mu) * lax.rsqrt(var + LN_EPS) * g.astype(jnp.float32) + b.astype(jnp.float32)).astype(x.dtype)


def rms_norm(x, g):
    xf = x.astype(jnp.float32)
    ms = jnp.mean(jnp.square(xf), axis=-1, keepdims=True)
    return (xf * lax.rsqrt(ms + RMS_EPS) * g.astype(jnp.float32)).astype(x.dtype)


def chunked_relpos_attention(q, k, v, rel_bias):
    B, S, H, Dh = q.shape
    n_chunks = S // CHUNK
    pad = LEFT_CHUNKS * CHUNK
    k_pad = jnp.pad(k, ((0, 0), (pad, 0), (0, 0), (0, 0)))
    v_pad = jnp.pad(v, ((0, 0), (pad, 0), (0, 0), (0, 0)))
    qi = jnp.arange(CHUNK)[:, None]
    kj = jnp.arange(BAND)[None, :]
    dist = jnp.clip(qi + pad - kj, -MAX_REL, MAX_REL) + MAX_REL
    bias = rel_bias[:, dist].astype(jnp.float32)
    scale = Dh ** -0.5

    def one_chunk(c):
        qc = lax.dynamic_slice_in_dim(q, c * CHUNK, CHUNK, axis=1)
        kc = lax.dynamic_slice_in_dim(k_pad, c * CHUNK, BAND, axis=1)
        vc = lax.dynamic_slice_in_dim(v_pad, c * CHUNK, BAND, axis=1)
        s = jnp.einsum('bqhd,bkhd->bhqk', qc, kc, preferred_element_type=jnp.float32) * scale + bias
        valid = kj >= pad - c * CHUNK
        s = jnp.where(valid, s, -jnp.inf)
        p = jax.nn.softmax(s, axis=-1)
        return jnp.einsum('bhqk,bkhd->bqhd', p.astype(v.dtype), vc)

    out = lax.map(one_chunk, jnp.arange(n_chunks))
    return out.transpose(1, 0, 2, 3, 4).reshape(B, S, H * Dh)


def stick_breaking_attention(q, k, v):
    B, S, H, Dh = q.shape
    n_blocks = S // SB_BLOCK
    scale = Dh ** -0.5
    key_pos = jnp.arange(S)

    def one_block(blk):
        q0 = blk * SB_BLOCK
        qb = lax.dynamic_slice_in_dim(q, q0, SB_BLOCK, axis=1)
        z = jnp.einsum('bqhd,bkhd->bhqk', qb, k, preferred_element_type=jnp.float32) * scale
        q_pos = q0 + jnp.arange(SB_BLOCK)
        causal = key_pos[None, :] < q_pos[:, None]
        log_beta = jax.nn.log_sigmoid(z)
        log_1m = jnp.where(causal, jax.nn.log_sigmoid(-z), 0.0)
        suffix = lax.cumsum(log_1m, axis=3, reverse=True) - log_1m
        a = jnp.where(causal, jnp.exp(log_beta + suffix), 0.0)
        return jnp.einsum('bhqk,bkhd->bqhd', a.astype(v.dtype), v)

    out = lax.map(one_block, jnp.arange(n_blocks))
    return out.transpose(1, 0, 2, 3, 4).reshape(B, S, H * Dh)


def hybrid_mixer(x, w_in, rel_bias, gain_a, gain_b, w_out):
    B, S, _ = x.shape
    proj = jnp.einsum('bsd,de->bse', x, w_in)
    qa, ka, va, qb, kb, vb = jnp.split(
        proj, [D_A, 2 * D_A, 3 * D_A, 3 * D_A + D_B, 3 * D_A + 2 * D_B], axis=-1)
    ha = lambda t: t.reshape(B, S, N_HEADS_A, HEAD_DIM)
    hb = lambda t: t.reshape(B, S, N_HEADS_B, HEAD_DIM)
    ya = chunked_relpos_attention(ha(qa), ha(ka), ha(va), rel_bias)
    yb = stick_breaking_attention(hb(qb), hb(kb), hb(vb))
    y = jnp.concatenate([rms_norm(ya, gain_a), rms_norm(yb, gain_b)], axis=-1)
    return jnp.einsum('bse,ed->bsd', y, w_out)


def moe_ffn(x, router_w, router_bias, w_gate, w_up, w_down, ws_gate, ws_up, ws_down):
    B, S, D = x.shape
    xt = x.reshape(-1, D)
    T = xt.shape[0]
    logits = jnp.einsum('td,de->te', xt, router_w, preferred_element_type=jnp.float32)
    scores = jax.nn.sigmoid(logits)
    sel = scores + router_bias.astype(jnp.float32)
    grp_score = lax.top_k(sel.reshape(T, N_GROUPS, EXPERTS_PER_GROUP), 2)[0].sum(-1)
    _, top_grp = lax.top_k(grp_score, TOPK_GROUPS)
    grp_mask = jnp.any(top_grp[:, :, None] == jnp.arange(N_GROUPS)[None, None, :], axis=1)
    sel = jnp.where(jnp.repeat(grp_mask, EXPERTS_PER_GROUP, axis=-1), sel, -jnp.inf)
    _, expert_idx = lax.top_k(sel, TOP_K)
    gate = jnp.take_along_axis(scores, expert_idx, axis=-1)
    gate = (gate / jnp.sum(gate, axis=-1, keepdims=True) * ROUTED_SCALE).astype(x.dtype)
    tk = T * TOP_K
    n_rows = tk + N_EXPERTS * ROW_BLOCK
    n_blocks = n_rows // ROW_BLOCK
    flat_e = expert_idx.reshape(-1)
    order = jnp.argsort(flat_e)
    sorted_e = flat_e[order]
    counts = jnp.bincount(flat_e, length=N_EXPERTS)
    padded = (counts + ROW_BLOCK - 1) // ROW_BLOCK * ROW_BLOCK
    start_sorted = jnp.cumsum(counts) - counts
    padded_ends = jnp.cumsum(padded)
    start_padded = padded_ends - padded
    dest = start_padded[sorted_e] + (jnp.arange(tk) - start_sorted[sorted_e])
    row_token = jnp.full((n_rows,), T, jnp.int32).at[dest].set((order // TOP_K).astype(jnp.int32))
    row_gate = jnp.zeros((n_rows,), x.dtype).at[dest].set(gate.reshape(-1)[order])
    block_expert = jnp.minimum(
        jnp.searchsorted(padded_ends, jnp.arange(n_blocks) * ROW_BLOCK, side='right'), N_EXPERTS - 1)
    xt_pad = jnp.concatenate([xt, jnp.zeros((1, D), xt.dtype)], axis=0)

    def expert_block(acc, blk):
        tok_b, gate_b, e = blk
        xb = xt_pad[tok_b]
        hb = jax.nn.silu(xb @ w_gate[e]) * (xb @ w_up[e])
        yb = (hb @ w_down[e]) * gate_b[:, None]
        return acc.at[tok_b].add(yb), None

    routed, _ = lax.scan(
        expert_block, jnp.zeros((T + 1, D), xt.dtype),
        (row_token.reshape(n_blocks, ROW_BLOCK), row_gate.reshape(n_blocks, ROW_BLOCK), block_expert))
    shared = (jax.nn.silu(xt @ ws_gate) * (xt @ ws_up)) @ ws_down
    return (routed[:T] + shared).reshape(B, S, D)


def setup_inputs(seed: int = 0) -> dict:
    key = jax.random.key(seed)
    ks = jax.random.split(key, 18)
    L = DEPTH
    nrm = lambda k, shape, scale: jax.random.normal(k, shape, jnp.float32) * scale
    return {
        'x': nrm(ks[0], (BATCH, SEQ, D_MODEL), 1.0),
        'w_in': nrm(ks[1], (L, D_MODEL, 3 * D_MIX), D_MODEL ** -0.5),
        'rel_bias': nrm(ks[2], (L, N_HEADS_A, 2 * MAX_REL + 1), 0.5),
        'gain_a': 1.0 + nrm(ks[3], (L, D_A), 0.02),
        'gain_b': 1.0 + nrm(ks[4], (L, D_B), 0.02),
        'w_out': nrm(ks[5], (L, D_MIX, D_MODEL), DN_BETA * D_MIX ** -0.5),
        'ln1_g': 1.0 + nrm(ks[6], (L, D_MODEL), 0.02),
        'ln1_b': nrm(ks[7], (L, D_MODEL), 0.02),
        'router_w': nrm(ks[8], (L, D_MODEL, N_EXPERTS), D_MODEL ** -0.5),
        'router_bias': nrm(ks[9], (L, N_EXPERTS), 0.01),
        'w_gate': nrm(ks[10], (L, N_EXPERTS, D_MODEL, D_EXPERT), D_MODEL ** -0.5),
        'w_up': nrm(ks[11], (L, N_EXPERTS, D_MODEL, D_EXPERT), D_MODEL ** -0.5),
        'w_down': nrm(ks[12], (L, N_EXPERTS, D_EXPERT, D_MODEL), DN_BETA * D_EXPERT ** -0.5),
        'ws_gate': nrm(ks[13], (L, D_MODEL, D_EXPERT), D_MODEL ** -0.5),
        'ws_up': nrm(ks[14], (L, D_MODEL, D_EXPERT), D_MODEL ** -0.5),
        'ws_down': nrm(ks[15], (L, D_EXPERT, D_MODEL), DN_BETA * D_EXPERT ** -0.5),
        'ln2_g': 1.0 + nrm(ks[16], (L, D_MODEL), 0.02),
        'ln2_b': nrm(ks[17], (L, D_MODEL), 0.02),
    }


def reference(x, w_in, rel_bias, gain_a, gain_b, w_out, ln1_g, ln1_b, router_w, router_bias,
              w_gate, w_up, w_down, ws_gate, ws_up, ws_down, ln2_g, ln2_b):
    for l in range(DEPTH):
        h = hybrid_mixer(x, w_in[l], rel_bias[l], gain_a[l], gain_b[l], w_out[l])
        x = layer_norm(DN_ALPHA * x + h, ln1_g[l], ln1_b[l])
        h = moe_ffn(x, router_w[l], router_bias[l], w_gate[l], w_up[l], w_down[l],
                    ws_gate[l], ws_up[l], ws_down[l])
        x = layer_norm(DN_ALPHA * x + h, ln2_g[l], ln2_b[l])
    return x
```

```python
import functools

import jax
import jax.numpy as jnp
from jax import lax
from jax.experimental import pallas as pl
from jax.experimental.pallas import tpu as pltpu

CHUNK = 64
HEAD_DIM = 64
LEFT_CHUNKS = 8
MAX_REL = 128
N_EXPERTS = 256
TOP_K = 8
N_GROUPS = 8
TOPK_GROUPS = 4
EXPERTS_PER_GROUP = N_EXPERTS // N_GROUPS
ROUTED_SCALE = 2.5
LN_EPS = 1e-5
RMS_EPS = 1e-6

LANES = 128
SUBLANES = 8
HEADS_PER_BLOCK = LANES // HEAD_DIM
QBLK_A = 2 * CHUNK
BAND_A = (LEFT_CHUNKS + 2) * CHUNK
TILE_B = 256
ROW_BLOCK = 256
TOK_BLOCK = 256
MASK_VALUE = -1e30
VMEM_LIMIT = 48 * 1024 * 1024

F32 = jnp.float32
BF16 = jnp.bfloat16


def _cparams(*sem):
    return pltpu.CompilerParams(dimension_semantics=sem, vmem_limit_bytes=VMEM_LIMIT)


def _dot(a, b):
    return jnp.dot(a, b, preferred_element_type=F32)


def _dot_nt(a, b):
    return lax.dot_general(a, b, (((1,), (1,)), ((), ())), preferred_element_type=F32)


def _qkv_kernel(x_ref, w_ref, o_ref, *, col_chunk):
    xb = x_ref[...].astype(BF16)
    for n in range(w_ref.shape[1] // col_chunk):
        cols = slice(n * col_chunk, (n + 1) * col_chunk)
        o_ref[:, cols] = _dot(xb, w_ref[:, cols]).astype(BF16)


def _qkv_proj(xt, w_b):
    T, D = xt.shape
    N = w_b.shape[1]
    tm = 512
    return pl.pallas_call(
        functools.partial(_qkv_kernel, col_chunk=512),
        grid=(T // tm,),
        in_specs=[pl.BlockSpec((tm, D), lambda i: (i, 0)),
                  pl.BlockSpec((D, N), lambda i: (0, 0))],
        out_specs=pl.BlockSpec((tm, N), lambda i: (i, 0)),
        out_shape=jax.ShapeDtypeStruct((T, N), BF16),
        compiler_params=_cparams("parallel"),
        name="qkv_proj",
    )(xt, w_b)


def _attn_a_kernel(q_ref, k_ref, v_ref, bias_ref, o_ref):
    S = q_ref.shape[0]
    nblk = S // QBLK_A
    lead = LEFT_CHUNKS * CHUNK
    n_edge = min(lead // QBLK_A, nblk)

    def block(p, kstart, nk, bias_off):
        q = q_ref[pl.ds(p * QBLK_A, QBLK_A), :]
        k = k_ref[pl.ds(kstart, nk), :]
        v = v_ref[pl.ds(kstart, nk), :]
        outs = []
        for h in range(HEADS_PER_BLOCK):
            hs = slice(h * HEAD_DIM, (h + 1) * HEAD_DIM)
            s = _dot_nt(q[:, hs], k[:, hs]) + bias_ref[h, :, bias_off:bias_off + nk]
            m = jnp.max(s, axis=-1, keepdims=True)
            e = jnp.exp(s - m)
            l = jnp.sum(e, axis=-1, keepdims=True)
            outs.append(_dot(e.astype(BF16), v[:, hs]) / l)
        o_ref[pl.ds(p * QBLK_A, QBLK_A), :] = jnp.concatenate(outs, axis=-1)

    for p in range(n_edge):
        block(p, 0, (p + 1) * QBLK_A, lead - p * QBLK_A)

    if nblk > n_edge:
        def body(p, c):
            block(p, pl.multiple_of(p * QBLK_A - lead, QBLK_A), BAND_A, 0)
            return c
        lax.fori_loop(n_edge, nblk, body, 0)


def _attn_a(qkv, bias_ext, B, S, d_a):
    n_hb = d_a // LANES
    seg = d_a // LANES
    return pl.pallas_call(
        _attn_a_kernel,
        grid=(B, n_hb),
        in_specs=[pl.BlockSpec((None, S, LANES), lambda b, h: (b, 0, h)),
                  pl.BlockSpec((None, S, LANES), lambda b, h: (b, 0, seg + h)),
                  pl.BlockSpec((None, S, LANES), lambda b, h: (b, 0, 2 * seg + h)),
                  pl.BlockSpec((HEADS_PER_BLOCK, QBLK_A, BAND_A), lambda b, h: (h, 0, 0))],
        out_specs=pl.BlockSpec((None, S, LANES), lambda b, h: (b, 0, h)),
        out_shape=jax.ShapeDtypeStruct((B, S, d_a), F32),
        compiler_params=_cparams("parallel", "parallel"),
        name="attn_chunked",
    )(qkv, qkv, qkv, bias_ext)


def _rel_bias_table(rel_bias):
    qi = jnp.arange(QBLK_A)[:, None]
    kj = jnp.arange(BAND_A)[None, :]
    dist = jnp.clip(qi + LEFT_CHUNKS * CHUNK - kj, -MAX_REL, MAX_REL) + MAX_REL
    qc = qi // CHUNK
    kc = kj // CHUNK
    allowed = (kc >= qc) & (kc <= qc + LEFT_CHUNKS)
    return jnp.where(allowed[None], rel_bias[:, dist].astype(F32), MASK_VALUE)


def _attn_b_kernel(q_ref, k_ref, v_ref, o_ref):
    qi = pl.program_id(2)
    t = TILE_B
    row = lax.broadcasted_iota(jnp.int32, (t, t), 0)
    col = lax.broadcasted_iota(jnp.int32, (t, t), 1)
    later = (row > col).astype(BF16)
    causal = col < row
    q = q_ref[...]
    outs = []
    for h in range(HEADS_PER_BLOCK):
        hs = slice(h * HEAD_DIM, (h + 1) * HEAD_DIM)
        qh = q[:, hs]

        def tile(j, diag):
            ks = pl.multiple_of(j * t, t)
            z = _dot_nt(qh, k_ref[pl.ds(ks, t), hs])
            soft = jnp.log1p(jnp.exp(-jnp.abs(z)))
            log_beta = jnp.minimum(z, 0.0) - soft
            log_1m = jnp.minimum(-z, 0.0) - soft
            if diag:
                log_1m = jnp.where(causal, log_1m, 0.0)
            hi = log_1m.astype(BF16)
            lo = (log_1m - hi.astype(F32)).astype(BF16)
            suffix = _dot(hi, later) + _dot(lo, later)
            return log_beta, log_1m, suffix, v_ref[pl.ds(ks, t), hs]

        log_beta, log_1m, suffix, vh = tile(qi, True)
        a = jnp.where(causal, jnp.exp(log_beta + suffix), 0.0)
        acc = _dot(a.astype(BF16), vh)
        carry = jnp.sum(log_1m, axis=-1, keepdims=True)

        def body(it, st):
            acc, carry = st
            log_beta, log_1m, suffix, vh = tile(qi - 1 - it, False)
            a = jnp.exp(log_beta + suffix + carry)
            acc = acc + _dot(a.astype(BF16), vh)
            carry = carry + jnp.sum(log_1m, axis=-1, keepdims=True)
            return acc, carry

        acc, carry = lax.fori_loop(0, qi, body, (acc, carry))
        outs.append(acc)
    o_ref[...] = jnp.concatenate(outs, axis=-1)


def _attn_b(qkv, B, S, d_a, d_b):
    n_hb = d_b // LANES
    base = 3 * d_a // LANES
    seg = d_b // LANES
    return pl.pallas_call(
        _attn_b_kernel,
        grid=(B, n_hb, S // TILE_B),
        in_specs=[pl.BlockSpec((None, TILE_B, LANES), lambda b, h, i: (b, i, base + h)),
                  pl.BlockSpec((None, S, LANES), lambda b, h, i: (b, 0, base + seg + h)),
                  pl.BlockSpec((None, S, LANES), lambda b, h, i: (b, 0, base + 2 * seg + h))],
        out_specs=pl.BlockSpec((None, TILE_B, LANES), lambda b, h, i: (b, i, h)),
        out_shape=jax.ShapeDtypeStruct((B, S, d_b), F32),
        compiler_params=_cparams("parallel", "parallel", "parallel"),
        name="attn_stickbreak",
    )(qkv, qkv, qkv)


def _layer_norm(r, g, b):
    mu = jnp.mean(r, axis=-1, keepdims=True)
    c = r - mu
    var = jnp.mean(c * c, axis=-1, keepdims=True)
    return c * lax.rsqrt(var + LN_EPS) * g + b


def _rms_norm(y, g):
    ms = jnp.mean(y * y, axis=-1, keepdims=True)
    return y * lax.rsqrt(ms + RMS_EPS) * g


def _silu(g):
    return g * jax.nn.sigmoid(g)


def _post_kernel(ya_ref, yb_ref, x_ref, ga_ref, gb_ref, wo_ref, g1_ref, b1_ref, rw_ref,
                 wsg_ref, wsu_ref, wsd_ref, x1_ref, base_ref, logit_ref, *, alpha):
    d_a = ya_ref.shape[1]
    na = _rms_norm(ya_ref[...], ga_ref[...]).astype(BF16)
    nb = _rms_norm(yb_ref[...], gb_ref[...]).astype(BF16)
    h = _dot(na, wo_ref[:d_a, :]) + _dot(nb, wo_ref[d_a:, :])
    x1 = _layer_norm(alpha * x_ref[...] + h, g1_ref[...], b1_ref[...])
    x1_ref[...] = x1
    xb = x1.astype(BF16)
    logit_ref[...] = _dot(xb, rw_ref[...])
    hs = _silu(_dot(xb, wsg_ref[...])) * _dot(xb, wsu_ref[...])
    base_ref[...] = alpha * x1 + _dot(hs.astype(BF16), wsd_ref[...])


def _post(ya, yb, xt, gain_a, gain_b, wo_b, g1, b1, rw_b, wsg_b, wsu_b, wsd_b, alpha):
    T, D = xt.shape
    d_a, d_b = ya.shape[1], yb.shape[1]
    E = rw_b.shape[1]
    De = wsg_b.shape[1]
    tm = TOK_BLOCK
    row = lambda i: (i, 0)
    fix = lambda i: (0, 0)
    return pl.pallas_call(
        functools.partial(_post_kernel, alpha=alpha),
        grid=(T // tm,),
        in_specs=[pl.BlockSpec((tm, d_a), row), pl.BlockSpec((tm, d_b), row), pl.BlockSpec((tm, D), row),
                  pl.BlockSpec((1, d_a), fix), pl.BlockSpec((1, d_b), fix),
                  pl.BlockSpec((d_a + d_b, D), fix), pl.BlockSpec((1, D), fix), pl.BlockSpec((1, D), fix),
                  pl.BlockSpec((D, E), fix), pl.BlockSpec((D, De), fix), pl.BlockSpec((D, De), fix),
                  pl.BlockSpec((De, D), fix)],
        out_specs=[pl.BlockSpec((tm, D), row), pl.BlockSpec((tm, D), row), pl.BlockSpec((tm, E), row)],
        out_shape=[jax.ShapeDtypeStruct((T, D), F32), jax.ShapeDtypeStruct((T, D), F32),
                   jax.ShapeDtypeStruct((T, E), F32)],
        compiler_params=_cparams("parallel"),
        name="post_attn",
    )(ya, yb, xt, gain_a, gain_b, wo_b, g1, b1, rw_b, wsg_b, wsu_b, wsd_b)


def _route_kernel(logit_ref, rbias_ref, idx_ref, gate_ref, rank_ref, cnt_ref, carry_ref):
    tm, E = logit_ref.shape
    neg = -jnp.inf

    @pl.when(pl.program_id(0) == 0)
    def _():
        carry_ref[...] = jnp.zeros_like(carry_ref)

    scores = jax.nn.sigmoid(logit_ref[...])
    sel = scores + rbias_ref[...]
    lane_i = lax.broadcasted_iota(jnp.int32, (tm, E), 1)
    grp = lane_i // EXPERTS_PER_GROUP
    lane = lane_i.astype(F32)

    def first_argmax(v):
        m = jnp.max(v, axis=-1, keepdims=True)
        return m, jnp.min(jnp.where(v == m, lane, float(E)), axis=-1, keepdims=True)

    grp_scores = []
    gs_full = jnp.zeros((tm, E), F32)
    for g in range(N_GROUPS):
        in_g = grp == g
        v = jnp.where(in_g, sel, neg)
        m1, i1 = first_argmax(v)
        m2 = jnp.max(jnp.where(lane == i1, neg, v), axis=-1, keepdims=True)
        grp_scores.append(m1 + m2)
        gs_full = jnp.where(in_g, m1 + m2, gs_full)
    beaten = jnp.zeros((tm, E), jnp.int32)
    for g in range(N_GROUPS):
        s = grp_scores[g]
        better = (s > gs_full) | ((s == gs_full) & (g < grp))
        beaten = beaten + better.astype(jnp.int32)
    selm = jnp.where(beaten < TOPK_GROUPS, sel, neg)

    out_lane = lax.broadcasted_iota(jnp.int32, (tm, LANES), 1)
    hits = []
    gates = []
    chosen = jnp.zeros((tm, E), F32)
    idx_out = jnp.zeros((tm, LANES), jnp.int32)
    gate_sum = jnp.zeros((tm, 1), F32)
    for k in range(TOP_K):
        _, ik = first_argmax(selm)
        hit = lane == ik
        gk = jnp.sum(jnp.where(hit, scores, 0.0), axis=-1, keepdims=True)
        selm = jnp.where(hit, neg, selm)
        chosen = jnp.where(hit, 1.0, chosen)
        idx_out = jnp.where(out_lane == k, ik.astype(jnp.int32), idx_out)
        gate_sum = gate_sum + gk
        hits.append(hit)
        gates.append(gk)

    row = lax.broadcasted_iota(jnp.int32, (tm, tm), 0)
    col = lax.broadcasted_iota(jnp.int32, (tm, tm), 1)
    earlier = (col < row).astype(BF16)
    before = _dot(earlier, chosen.astype(BF16)) + carry_ref[...]
    carry_ref[...] = carry_ref[...] + jnp.sum(chosen, axis=0, keepdims=True)
    cnt_ref[...] = carry_ref[...].astype(jnp.int32)

    gate_out = jnp.zeros((tm, LANES), F32)
    rank_out = jnp.zeros((tm, LANES), jnp.int32)
    for k in range(TOP_K):
        rk = jnp.sum(jnp.where(hits[k], before, 0.0), axis=-1, keepdims=True)
        rank_out = jnp.where(out_lane == k, rk.astype(jnp.int32), rank_out)
        gate_out = jnp.where(out_lane == k, gates[k] / gate_sum * ROUTED_SCALE, gate_out)
    idx_ref[...] = idx_out
    gate_ref[...] = gate_out
    rank_ref[...] = rank_out


def _route(logits, rbias):
    T, E = logits.shape
    tm = TOK_BLOCK
    row = lambda i: (i, 0)
    fix = lambda i: (0, 0)
    return pl.pallas_call(
        _route_kernel,
        grid=(T // tm,),
        in_specs=[pl.BlockSpec((tm, E), row), pl.BlockSpec((1, E), fix)],
        out_specs=[pl.BlockSpec((tm, LANES), row), pl.BlockSpec((tm, LANES), row),
                   pl.BlockSpec((tm, LANES), row), pl.BlockSpec((1, E), fix)],
        out_shape=[jax.ShapeDtypeStruct((T, LANES), jnp.int32), jax.ShapeDtypeStruct((T, LANES), F32),
                   jax.ShapeDtypeStruct((T, LANES), jnp.int32), jax.ShapeDtypeStruct((1, E), jnp.int32)],
        scratch_shapes=[pltpu.VMEM((1, E), F32)],
        compiler_params=_cparams("arbitrary"),
        name="route",
    )(logits, rbias)


def _dispatch_kernel(start_ref, cnt_ref, idx_ref, rank_ref, x_ref, xs_ref, zero_ref, sem, zsem):
    tb = x_ref.shape[0]

    def issue(i, c):
        for k in range(TOP_K):
            a = i * TOP_K + k
            d = start_ref[idx_ref[a]] + rank_ref[a]
            pltpu.make_async_copy(x_ref.at[pl.ds(i, 1), :], xs_ref.at[pl.ds(d, 1), :], sem).start()
        return c
    lax.fori_loop(0, tb, issue, 0)

    @pl.when(pl.program_id(0) == pl.num_programs(0) - 1)
    def _():
        zero_ref[...] = jnp.zeros_like(zero_ref)

        def pad_copies(e):
            n = cnt_ref[e]
            pad = (ROW_BLOCK - n % ROW_BLOCK) % ROW_BLOCK
            first = start_ref[e] + n
            copies = []
            for r in range(SUBLANES - 1):
                copies.append((r < (pad & (SUBLANES - 1)), pltpu.make_async_copy(
                    zero_ref.at[pl.ds(0, 1), :], xs_ref.at[pl.ds(first + r, 1), :], zsem)))
            size = SUBLANES
            while size < ROW_BLOCK:
                off = pl.multiple_of(first + (pad & (size - 1)), SUBLANES)
                copies.append(((pad & size) != 0, pltpu.make_async_copy(
                    zero_ref.at[pl.ds(0, size), :], xs_ref.at[pl.ds(off, size), :], zsem)))
                size *= 2
            return copies

        def fill(e, c):
            for on, cp in pad_copies(e):
                pl.when(on)(cp.start)
            return c

        def drain(e, c):
            for on, cp in pad_copies(e):
                pl.when(on)(cp.wait)
            return c
        lax.fori_loop(0, cnt_ref.shape[0], fill, 0)
        lax.fori_loop(0, cnt_ref.shape[0], drain, 0)

    for k in range(TOP_K):
        pltpu.make_async_copy(x_ref, xs_ref.at[pl.ds(0, tb), :], sem).wait()


def _dispatch(start, cnt, idx_flat, rank_flat, x1, n_rows):
    T, D = x1.shape
    tb = TOK_BLOCK
    grid_spec = pltpu.PrefetchScalarGridSpec(
        num_scalar_prefetch=2,
        grid=(T // tb,),
        in_specs=[pl.BlockSpec((tb * TOP_K,), lambda i, s, c: (i,), memory_space=pltpu.SMEM),
                  pl.BlockSpec((tb * TOP_K,), lambda i, s, c: (i,), memory_space=pltpu.SMEM),
                  pl.BlockSpec((tb, D), lambda i, s, c: (i, 0))],
        out_specs=pl.BlockSpec(memory_space=pl.ANY),
        scratch_shapes=[pltpu.VMEM((ROW_BLOCK // 2, D), F32),
                        pltpu.SemaphoreType.DMA(()), pltpu.SemaphoreType.DMA(())],
    )
    return pl.pallas_call(
        _dispatch_kernel,
        grid_spec=grid_spec,
        out_shape=jax.ShapeDtypeStruct((n_rows, D), F32),
        compiler_params=_cparams("arbitrary"),
        name="dispatch",
    )(start, cnt, idx_flat, rank_flat, x1)


def _expert_kernel(be_ref, nu_ref, xs_ref, wg_ref, wu_ref, wd_ref, ys_ref, wgu_s, wd_s):
    b = pl.program_id(0)
    De = wg_ref.shape[2]
    active = b < nu_ref[0]
    first = (b == 0) | (be_ref[b] != be_ref[jnp.maximum(b - 1, 0)])

    @pl.when(active & first)
    def _():
        wgu_s[:, :De] = wg_ref[0].astype(BF16)
        wgu_s[:, De:] = wu_ref[0].astype(BF16)
        wd_s[...] = wd_ref[0].astype(BF16)

    @pl.when(active)
    def _():
        gu = _dot(xs_ref[...].astype(BF16), wgu_s[...])
        h = _silu(gu[:, :De]) * gu[:, De:]
        ys_ref[...] = _dot(h.astype(BF16), wd_s[...])


def _experts(block_expert, n_used, xs, w_gate, w_up, w_down):
    n_rows, D = xs.shape
    De = w_gate.shape[2]
    last = lambda b, be, nu: jnp.minimum(b, nu[0] - 1)
    grid_spec = pltpu.PrefetchScalarGridSpec(
        num_scalar_prefetch=2,
        grid=(n_rows // ROW_BLOCK,),
        in_specs=[pl.BlockSpec((ROW_BLOCK, D), lambda b, be, nu: (last(b, be, nu), 0)),
                  pl.BlockSpec((1, D, De), lambda b, be, nu: (be[last(b, be, nu)], 0, 0)),
                  pl.BlockSpec((1, D, De), lambda b, be, nu: (be[last(b, be, nu)], 0, 0)),
                  pl.BlockSpec((1, De, D), lambda b, be, nu: (be[last(b, be, nu)], 0, 0))],
        out_specs=pl.BlockSpec((ROW_BLOCK, D), lambda b, be, nu: (last(b, be, nu), 0)),
        scratch_shapes=[pltpu.VMEM((D, 2 * De), BF16), pltpu.VMEM((De, D), BF16)],
    )
    return pl.pallas_call(
        _expert_kernel,
        grid_spec=grid_spec,
        out_shape=jax.ShapeDtypeStruct((n_rows, D), F32),
        compiler_params=_cparams("arbitrary"),
        name="expert_ffn",
    )(block_expert, n_used, xs, w_gate, w_up, w_down)


def _combine_kernel(start_ref, idx_ref, rank_ref, gate_ref, base_ref, g_ref, b_ref, ys_ref, o_ref, buf, sem):
    tb = base_ref.shape[0]

    def issue(i, c):
        for k in range(TOP_K):
            a = i * TOP_K + k
            d = start_ref[idx_ref[a]] + rank_ref[a]
            pltpu.make_async_copy(ys_ref.at[pl.ds(d, 1), :], buf.at[k, pl.ds(i, 1), :], sem).start()
        return c
    lax.fori_loop(0, tb, issue, 0)
    for k in range(TOP_K):
        pltpu.make_async_copy(ys_ref.at[pl.ds(0, tb), :], buf.at[k], sem).wait()

    gate = gate_ref[...]
    acc = base_ref[...]
    for k in range(TOP_K):
        acc = acc + gate[:, k:k + 1] * buf[k]
    o_ref[...] = _layer_norm(acc, g_ref[...], b_ref[...])


def _combine(start, idx_flat, rank_flat, gate, base, g2, b2, ys):
    T, D = base.shape
    tb = TOK_BLOCK
    grid_spec = pltpu.PrefetchScalarGridSpec(
        num_scalar_prefetch=1,
        grid=(T // tb,),
        in_specs=[pl.BlockSpec((tb * TOP_K,), lambda i, s: (i,), memory_space=pltpu.SMEM),
                  pl.BlockSpec((tb * TOP_K,), lambda i, s: (i,), memory_space=pltpu.SMEM),
                  pl.BlockSpec((tb, LANES), lambda i, s: (i, 0)),
                  pl.BlockSpec((tb, D), lambda i, s: (i, 0)),
                  pl.BlockSpec((1, D), lambda i, s: (0, 0)),
                  pl.BlockSpec((1, D), lambda i, s: (0, 0)),
                  pl.BlockSpec(memory_space=pl.ANY)],
        out_specs=pl.BlockSpec((tb, D), lambda i, s: (i, 0)),
        scratch_shapes=[pltpu.VMEM((TOP_K, tb, D), F32), pltpu.SemaphoreType.DMA(())],
    )
    return pl.pallas_call(
        _combine_kernel,
        grid_spec=grid_spec,
        out_shape=jax.ShapeDtypeStruct((T, D), F32),
        compiler_params=_cparams("arbitrary"),
        name="combine",
    )(start, idx_flat, rank_flat, gate, base, g2, b2, ys)


def _layer(x, w_in, rel_bias, gain_a, gain_b, w_out, ln1_g, ln1_b, router_w, router_bias,
           w_gate, w_up, w_down, ws_gate, ws_up, ws_down, ln2_g, ln2_b, alpha):
    B, S, D = x.shape
    T = B * S
    d_a = gain_a.shape[0]
    d_b = gain_b.shape[0]
    assert S % TILE_B == 0 and T % 512 == 0 and d_a % LANES == 0 and d_b % LANES == 0
    xt = x.reshape(T, D)

    scale = HEAD_DIM ** -0.5
    col = jnp.arange(w_in.shape[1])
    is_q = (col < d_a) | ((col >= 3 * d_a) & (col < 3 * d_a + d_b))
    w_in_b = (w_in * jnp.where(is_q, scale, 1.0)[None, :]).astype(BF16)

    qkv = _qkv_proj(xt, w_in_b).reshape(B, S, -1)
    ya = _attn_a(qkv, _rel_bias_table(rel_bias), B, S, d_a).reshape(T, d_a)
    yb = _attn_b(qkv, B, S, d_a, d_b).reshape(T, d_b)

    x1, base, logits = _post(
        ya, yb, xt, gain_a[None], gain_b[None], w_out.astype(BF16), ln1_g[None], ln1_b[None],
        router_w.astype(BF16), ws_gate.astype(BF16), ws_up.astype(BF16), ws_down.astype(BF16), alpha)

    idx, gate, rank, cnt = _route(logits, router_bias[None].astype(F32))
    cnt = cnt[0]
    padded = (cnt + ROW_BLOCK - 1) // ROW_BLOCK * ROW_BLOCK
    ends = jnp.cumsum(padded)
    start = (ends - padded).astype(jnp.int32)
    n_rows = T * TOP_K + N_EXPERTS * ROW_BLOCK
    n_blocks = n_rows // ROW_BLOCK
    block_expert = jnp.minimum(
        jnp.searchsorted(ends, jnp.arange(n_blocks) * ROW_BLOCK, side='right'), N_EXPERTS - 1).astype(jnp.int32)
    n_used = (ends[-1:] // ROW_BLOCK).astype(jnp.int32)
    idx_flat = idx[:, :TOP_K].reshape(-1)
    rank_flat = rank[:, :TOP_K].reshape(-1)

    xs = _dispatch(start, cnt, idx_flat, rank_flat, x1, n_rows)
    ys = _experts(block_expert, n_used, xs, w_gate, w_up, w_down)
    out = _combine(start, idx_flat, rank_flat, gate, base, ln2_g[None], ln2_b[None], ys)
    return out.reshape(B, S, D)


def kernel(x, w_in, rel_bias, gain_a, gain_b, w_out, ln1_g, ln1_b, router_w, router_bias,
           w_gate, w_up, w_down, ws_gate, ws_up, ws_down, ln2_g, ln2_b):
    depth = w_in.shape[0]
    alpha = (2 * depth) ** 0.25
    for l in range(depth):
        x = _layer(x, w_in[l], rel_bias[l], gain_a[l], gain_b[l], w_out[l], ln1_g[l], ln1_b[l],
                   router_w[l], router_bias[l], w_gate[l], w_up[l], w_down[l],
                   ws_gate[l], ws_up[l], ws_down[l], ln2_g[l], ln2_b[l], alpha)
    return x
```

```python
import functools

import jax
import jax.numpy as jnp
from jax import lax
from jax.experimental import pallas as pl
from jax.experimental.pallas import tpu as pltpu

CHUNK = 64
HEAD_DIM = 64
LEFT_CHUNKS = 8
MAX_REL = 128
N_EXPERTS = 256
TOP_K = 8
N_GROUPS = 8
TOPK_GROUPS = 4
EXPERTS_PER_GROUP = N_EXPERTS // N_GROUPS
ROUTED_SCALE = 2.5
LN_EPS = 1e-5
RMS_EPS = 1e-6

LANES = 128
SUBLANES = 8
HEADS_PER_BLOCK = LANES // HEAD_DIM
QBLK_A = 2 * CHUNK
BAND_A = (LEFT_CHUNKS + 2) * CHUNK
BIAS_W = BAND_A + QBLK_A
TILE_B = 256
ROW_BLOCK = 256
TOK_BLOCK = 256
MASK_VALUE = -1e30
VMEM_LIMIT = 48 * 1024 * 1024

F32 = jnp.float32
BF16 = jnp.bfloat16


def _cparams(*sem, flags=None):
    return pltpu.CompilerParams(dimension_semantics=sem, vmem_limit_bytes=VMEM_LIMIT, flags=flags)


def _dot(a, b):
    return jnp.dot(a, b, preferred_element_type=F32)


def _dot_nt(a, b):
    return lax.dot_general(a, b, (((1,), (1,)), ((), ())), preferred_element_type=F32)


def _qkv_kernel(x_ref, w_ref, o_ref, *, col_chunk):
    xb = x_ref[...].astype(BF16)
    for n in range(w_ref.shape[1] // col_chunk):
        cols = slice(n * col_chunk, (n + 1) * col_chunk)
        o_ref[:, cols] = _dot(xb, w_ref[:, cols]).astype(BF16)


def _qkv_proj(xt, w_b):
    T, D = xt.shape
    N = w_b.shape[1]
    tm = 512
    return pl.pallas_call(
        functools.partial(_qkv_kernel, col_chunk=512),
        grid=(T // tm,),
        in_specs=[pl.BlockSpec((tm, D), lambda i: (i, 0)),
                  pl.BlockSpec((D, N), lambda i: (0, 0))],
        out_specs=pl.BlockSpec((tm, N), lambda i: (i, 0)),
        out_shape=jax.ShapeDtypeStruct((T, N), BF16),
        compiler_params=_cparams("parallel"),
        name="qkv_proj",
    )(xt, w_b)


def _attn_a_kernel(q_ref, k_ref, v_ref, w_ref, o_ref, bias_ref):
    S = q_ref.shape[0]
    nblk = S // QBLK_A
    lead = LEFT_CHUNKS * CHUNK
    n_edge = min(lead // QBLK_A, nblk)

    qc = lax.broadcasted_iota(jnp.int32, (QBLK_A, BAND_A), 0) // CHUNK
    kc = lax.broadcasted_iota(jnp.int32, (QBLK_A, BAND_A), 1) // CHUNK
    allowed = (kc >= qc) & (kc <= qc + LEFT_CHUNKS)
    for h in range(HEADS_PER_BLOCK):
        wb = jnp.broadcast_to(w_ref[h], (QBLK_A, BIAS_W))
        toeplitz = pltpu.roll(wb, BIAS_W - (QBLK_A - 1), 1, stride=1, stride_axis=0)
        bias_ref[h] = jnp.where(allowed, toeplitz[:, :BAND_A], MASK_VALUE)

    def block(p, kstart, nk, bias_off):
        q = q_ref[pl.ds(p * QBLK_A, QBLK_A), :]
        k = k_ref[pl.ds(kstart, nk), :]
        v = v_ref[pl.ds(kstart, nk), :]
        outs = []
        for h in range(HEADS_PER_BLOCK):
            hs = slice(h * HEAD_DIM, (h + 1) * HEAD_DIM)
            s = _dot_nt(q[:, hs], k[:, hs]) + bias_ref[h, :, bias_off:bias_off + nk]
            m = jnp.max(s, axis=-1, keepdims=True)
            e = jnp.exp(s - m)
            l = jnp.sum(e, axis=-1, keepdims=True)
            outs.append(_dot(e.astype(BF16), v[:, hs]) / l)
        o_ref[pl.ds(p * QBLK_A, QBLK_A), :] = jnp.concatenate(outs, axis=-1)

    for p in range(n_edge):
        block(p, 0, (p + 1) * QBLK_A, lead - p * QBLK_A)

    if nblk > n_edge:
        def body(p, c):
            block(p, pl.multiple_of(p * QBLK_A - lead, QBLK_A), BAND_A, 0)
            return c
        lax.fori_loop(n_edge, nblk, body, 0)


def _attn_a(qkv, bias_w, B, S, d_a):
    n_hb = d_a // LANES
    seg = d_a // LANES
    return pl.pallas_call(
        _attn_a_kernel,
        grid=(B, n_hb),
        in_specs=[pl.BlockSpec((None, S, LANES), lambda b, h: (b, 0, h)),
                  pl.BlockSpec((None, S, LANES), lambda b, h: (b, 0, seg + h)),
                  pl.BlockSpec((None, S, LANES), lambda b, h: (b, 0, 2 * seg + h)),
                  pl.BlockSpec((HEADS_PER_BLOCK, 1, BIAS_W), lambda b, h: (h, 0, 0))],
        out_specs=pl.BlockSpec((None, S, LANES), lambda b, h: (b, 0, h)),
        out_shape=jax.ShapeDtypeStruct((B, S, d_a), F32),
        scratch_shapes=[pltpu.VMEM((HEADS_PER_BLOCK, QBLK_A, BAND_A), F32)],
        compiler_params=_cparams("parallel", "parallel"),
        name="attn_chunked",
    )(qkv, qkv, qkv, bias_w)


def _rel_bias_by_offset(rel_bias):
    dist = jnp.clip(BAND_A - 1 - jnp.arange(BIAS_W), -MAX_REL, MAX_REL) + MAX_REL
    return rel_bias[:, dist].astype(F32)[:, None, :]


def _attn_b_kernel(q_ref, k_ref, v_ref, o_ref):
    qi = pl.program_id(2)
    t = TILE_B
    row = lax.broadcasted_iota(jnp.int32, (t, t), 0)
    col = lax.broadcasted_iota(jnp.int32, (t, t), 1)
    later = (row > col).astype(BF16)
    causal = col < row
    q = q_ref[...]
    heads = [slice(h * HEAD_DIM, (h + 1) * HEAD_DIM) for h in range(HEADS_PER_BLOCK)]
    qs = [q[:, hs] for hs in heads]

    def tiles(js, state, diag):
        accs, carries = list(state[0::2]), list(state[1::2])
        work = []
        for j in js:
            ks = pl.multiple_of(j * t, t)
            kt = k_ref[pl.ds(ks, t), :]
            vt = v_ref[pl.ds(ks, t), :]
            for h, hs in enumerate(heads):
                work.append(dict(h=h, v=vt[:, hs], z=_dot_nt(qs[h], kt[:, hs])))
        for w in work:
            z = w["z"]
            sp = jnp.maximum(z, 0.0) + jnp.log(1.0 + jnp.exp(-jnp.abs(z)))
            w["log_beta"] = z - sp
            if diag:
                sp = jnp.where(causal, sp, 0.0)
            hi = sp.astype(BF16)
            lo = (sp - hi.astype(F32)).astype(BF16)
            w["suffix"] = _dot(hi, later) + _dot(lo, later)
            w["rowsum"] = jnp.sum(sp, axis=-1, keepdims=True)
        for w in work:
            h = w["h"]
            if diag:
                a = jnp.where(causal, jnp.exp(w["log_beta"] - w["suffix"]), 0.0)
                accs[h] = _dot(a.astype(BF16), w["v"])
                carries[h] = w["rowsum"]
            else:
                a = jnp.exp(w["log_beta"] - w["suffix"] - carries[h])
                accs[h] = accs[h] + _dot(a.astype(BF16), w["v"])
                carries[h] = carries[h] + w["rowsum"]
        return tuple(x for pair in zip(accs, carries) for x in pair)

    state = tiles([qi], (None,) * (2 * HEADS_PER_BLOCK), True)
    state = lax.cond(qi % 2 == 1, lambda st: tiles([qi - 1], st, False), lambda st: st, state)
    first = qi - 1 - qi % 2

    def body(it, st):
        j = first - 2 * it
        return tiles([j, j - 1], st, False)

    state = lax.fori_loop(0, qi // 2, body, state)
    o_ref[...] = jnp.concatenate(state[0::2], axis=-1)


def _attn_b(qkv, B, S, d_a, d_b):
    n_hb = d_b // LANES
    base = 3 * d_a // LANES
    seg = d_b // LANES
    return pl.pallas_call(
        _attn_b_kernel,
        grid=(B, n_hb, S // TILE_B),
        in_specs=[pl.BlockSpec((None, TILE_B, LANES), lambda b, h, i: (b, i, base + h)),
                  pl.BlockSpec((None, S, LANES), lambda b, h, i: (b, 0, base + seg + h)),
                  pl.BlockSpec((None, S, LANES), lambda b, h, i: (b, 0, base + 2 * seg + h))],
        out_specs=pl.BlockSpec((None, TILE_B, LANES), lambda b, h, i: (b, i, h)),
        out_shape=jax.ShapeDtypeStruct((B, S, d_b), F32),
        compiler_params=_cparams("parallel", "parallel", "parallel"),
        name="attn_stickbreak",
    )(qkv, qkv, qkv)


def _layer_norm(r, g, b):
    mu = jnp.mean(r, axis=-1, keepdims=True)
    c = r - mu
    var = jnp.mean(c * c, axis=-1, keepdims=True)
    return c * lax.rsqrt(var + LN_EPS) * g + b


def _rms_norm(y, g):
    ms = jnp.mean(y * y, axis=-1, keepdims=True)
    return y * lax.rsqrt(ms + RMS_EPS) * g


def _silu(g):
    return g * jax.nn.sigmoid(g)


def _post_kernel(ya_ref, yb_ref, x_ref, ga_ref, gb_ref, wo_ref, g1_ref, b1_ref, rw_ref,
                 wsg_ref, wsu_ref, wsd_ref, x1_ref, base_ref, logit_ref, *, alpha):
    d_a = ya_ref.shape[1]
    na = _rms_norm(ya_ref[...], ga_ref[...]).astype(BF16)
    nb = _rms_norm(yb_ref[...], gb_ref[...]).astype(BF16)
    h = _dot(na, wo_ref[:d_a, :]) + _dot(nb, wo_ref[d_a:, :])
    x1 = _layer_norm(alpha * x_ref[...] + h, g1_ref[...], b1_ref[...])
    x1_ref[...] = x1
    xb = x1.astype(BF16)
    logit_ref[...] = _dot(xb, rw_ref[...])
    hs = _silu(_dot(xb, wsg_ref[...])) * _dot(xb, wsu_ref[...])
    base_ref[...] = alpha * x1 + _dot(hs.astype(BF16), wsd_ref[...])


def _post(ya, yb, xt, gain_a, gain_b, wo_b, g1, b1, rw_b, wsg_b, wsu_b, wsd_b, alpha):
    T, D = xt.shape
    d_a, d_b = ya.shape[1], yb.shape[1]
    E = rw_b.shape[1]
    De = wsg_b.shape[1]
    tm = TOK_BLOCK
    row = lambda i: (i, 0)
    fix = lambda i: (0, 0)
    return pl.pallas_call(
        functools.partial(_post_kernel, alpha=alpha),
        grid=(T // tm,),
        in_specs=[pl.BlockSpec((tm, d_a), row), pl.BlockSpec((tm, d_b), row), pl.BlockSpec((tm, D), row),
                  pl.BlockSpec((1, d_a), fix), pl.BlockSpec((1, d_b), fix),
                  pl.BlockSpec((d_a + d_b, D), fix), pl.BlockSpec((1, D), fix), pl.BlockSpec((1, D), fix),
                  pl.BlockSpec((D, E), fix), pl.BlockSpec((D, De), fix), pl.BlockSpec((D, De), fix),
                  pl.BlockSpec((De, D), fix)],
        out_specs=[pl.BlockSpec((tm, D), row), pl.BlockSpec((tm, D), row), pl.BlockSpec((tm, E), row)],
        out_shape=[jax.ShapeDtypeStruct((T, D), F32), jax.ShapeDtypeStruct((T, D), F32),
                   jax.ShapeDtypeStruct((T, E), F32)],
        compiler_params=_cparams("parallel"),
        name="post_attn",
    )(ya, yb, xt, gain_a, gain_b, wo_b, g1, b1, rw_b, wsg_b, wsu_b, wsd_b)


def _route_kernel(logit_ref, rbias_ref, idx_ref, gate_ref, rank_ref, cnt_ref, carry_ref):
    tm, E = logit_ref.shape
    neg = -jnp.inf

    @pl.when(pl.program_id(0) == 0)
    def _():
        carry_ref[...] = jnp.zeros_like(carry_ref)

    scores = jax.nn.sigmoid(logit_ref[...])
    sel = scores + rbias_ref[...]
    lane_i = lax.broadcasted_iota(jnp.int32, (tm, E), 1)
    grp = lane_i // EXPERTS_PER_GROUP
    lane = lane_i.astype(F32)

    def first_argmax(v):
        m = jnp.max(v, axis=-1, keepdims=True)
        return m, jnp.min(jnp.where(v == m, lane, float(E)), axis=-1, keepdims=True)

    grp_scores = []
    gs_full = jnp.zeros((tm, E), F32)
    for g in range(N_GROUPS):
        in_g = grp == g
        v = jnp.where(in_g, sel, neg)
        m1, i1 = first_argmax(v)
        m2 = jnp.max(jnp.where(lane == i1, neg, v), axis=-1, keepdims=True)
        grp_scores.append(m1 + m2)
        gs_full = jnp.where(in_g, m1 + m2, gs_full)
    beaten = jnp.zeros((tm, E), jnp.int32)
    for g in range(N_GROUPS):
        s = grp_scores[g]
        better = (s > gs_full) | ((s == gs_full) & (g < grp))
        beaten = beaten + better.astype(jnp.int32)
    selm = jnp.where(beaten < TOPK_GROUPS, sel, neg)

    out_lane = lax.broadcasted_iota(jnp.int32, (tm, LANES), 1)
    hits = []
    gates = []
    chosen = jnp.zeros((tm, E), F32)
    idx_out = jnp.zeros((tm, LANES), jnp.int32)
    gate_sum = jnp.zeros((tm, 1), F32)
    for k in range(TOP_K):
        _, ik = first_argmax(selm)
        hit = lane == ik
        gk = jnp.sum(jnp.where(hit, scores, 0.0), axis=-1, keepdims=True)
        selm = jnp.where(hit, neg, selm)
        chosen = jnp.where(hit, 1.0, chosen)
        idx_out = jnp.where(out_lane == k, ik.astype(jnp.int32), idx_out)
        gate_sum = gate_sum + gk
        hits.append(hit)
        gates.append(gk)

    row = lax.broadcasted_iota(jnp.int32, (tm, tm), 0)
    col = lax.broadcasted_iota(jnp.int32, (tm, tm), 1)
    earlier = (col < row).astype(BF16)
    before = _dot(earlier, chosen.astype(BF16)) + carry_ref[...]
    carry_ref[...] = carry_ref[...] + jnp.sum(chosen, axis=0, keepdims=True)
    cnt_ref[...] = carry_ref[...].astype(jnp.int32)

    gate_out = jnp.zeros((tm, LANES), F32)
    rank_out = jnp.zeros((tm, LANES), jnp.int32)
    for k in range(TOP_K):
        rk = jnp.sum(jnp.where(hits[k], before, 0.0), axis=-1, keepdims=True)
        rank_out = jnp.where(out_lane == k, rk.astype(jnp.int32), rank_out)
        gate_out = jnp.where(out_lane == k, gates[k] / gate_sum * ROUTED_SCALE, gate_out)
    idx_ref[...] = idx_out
    gate_ref[...] = gate_out
    rank_ref[...] = rank_out


def _route(logits, rbias):
    T, E = logits.shape
    tm = TOK_BLOCK
    row = lambda i: (i, 0)
    fix = lambda i: (0, 0)
    return pl.pallas_call(
        _route_kernel,
        grid=(T // tm,),
        in_specs=[pl.BlockSpec((tm, E), row), pl.BlockSpec((1, E), fix)],
        out_specs=[pl.BlockSpec((tm, LANES), row), pl.BlockSpec((tm, LANES), row),
                   pl.BlockSpec((tm, LANES), row), pl.BlockSpec((1, E), fix)],
        out_shape=[jax.ShapeDtypeStruct((T, LANES), jnp.int32), jax.ShapeDtypeStruct((T, LANES), F32),
                   jax.ShapeDtypeStruct((T, LANES), jnp.int32), jax.ShapeDtypeStruct((1, E), jnp.int32)],
        scratch_shapes=[pltpu.VMEM((1, E), F32)],
        compiler_params=_cparams("arbitrary"),
        name="route",
    )(logits, rbias)


def _dispatch_kernel(start_ref, cnt_ref, idx_ref, rank_ref, x_ref, xs_ref, zero_ref, sem, zsem):
    tb = x_ref.shape[0]

    def issue(i, c):
        for k in range(TOP_K):
            a = i * TOP_K + k
            d = start_ref[idx_ref[a]] + rank_ref[a]
            pltpu.make_async_copy(x_ref.at[pl.ds(i, 1), :], xs_ref.at[pl.ds(d, 1), :], sem).start()
        return c
    lax.fori_loop(0, tb, issue, 0)

    @pl.when(pl.program_id(0) == pl.num_programs(0) - 1)
    def _():
        zero_ref[...] = jnp.zeros_like(zero_ref)

        def pad_copies(e):
            n = cnt_ref[e]
            pad = (ROW_BLOCK - n % ROW_BLOCK) % ROW_BLOCK
            first = start_ref[e] + n
            copies = []
            for r in range(SUBLANES - 1):
                copies.append((r < (pad & (SUBLANES - 1)), pltpu.make_async_copy(
                    zero_ref.at[pl.ds(0, 1), :], xs_ref.at[pl.ds(first + r, 1), :], zsem)))
            size = SUBLANES
            while size < ROW_BLOCK:
                off = pl.multiple_of(first + (pad & (size - 1)), SUBLANES)
                copies.append(((pad & size) != 0, pltpu.make_async_copy(
                    zero_ref.at[pl.ds(0, size), :], xs_ref.at[pl.ds(off, size), :], zsem)))
                size *= 2
            return copies

        def fill(e, c):
            for on, cp in pad_copies(e):
                pl.when(on)(cp.start)
            return c

        def drain(e, c):
            for on, cp in pad_copies(e):
                pl.when(on)(cp.wait)
            return c
        lax.fori_loop(0, cnt_ref.shape[0], fill, 0)
        lax.fori_loop(0, cnt_ref.shape[0], drain, 0)

    for k in range(TOP_K):
        pltpu.make_async_copy(x_ref, xs_ref.at[pl.ds(0, tb), :], sem).wait()


def _dispatch(start, cnt, idx_flat, rank_flat, x1, n_rows):
    T, D = x1.shape
    tb = TOK_BLOCK
    grid_spec = pltpu.PrefetchScalarGridSpec(
        num_scalar_prefetch=2,
        grid=(T // tb,),
        in_specs=[pl.BlockSpec((tb * TOP_K,), lambda i, s, c: (i,), memory_space=pltpu.SMEM),
                  pl.BlockSpec((tb * TOP_K,), lambda i, s, c: (i,), memory_space=pltpu.SMEM),
                  pl.BlockSpec((tb, D), lambda i, s, c: (i, 0))],
        out_specs=pl.BlockSpec(memory_space=pl.ANY),
        scratch_shapes=[pltpu.VMEM((ROW_BLOCK // 2, D), F32),
                        pltpu.SemaphoreType.DMA(()), pltpu.SemaphoreType.DMA(())],
    )
    return pl.pallas_call(
        _dispatch_kernel,
        grid_spec=grid_spec,
        out_shape=jax.ShapeDtypeStruct((n_rows, D), F32),
        compiler_params=_cparams("arbitrary"),
        name="dispatch",
    )(start, cnt, idx_flat, rank_flat, x1)


def _expert_kernel(be_ref, nu_ref, xs_ref, wg_ref, wu_ref, wd_ref, ys_ref, wgu_s, wd_s):
    b = pl.program_id(0)
    De = wg_ref.shape[2]
    active = b < nu_ref[0]
    first = (b == 0) | (be_ref[b] != be_ref[jnp.maximum(b - 1, 0)])

    @pl.when(active & first)
    def _():
        wgu_s[:, :De] = wg_ref[0].astype(BF16)
        wgu_s[:, De:] = wu_ref[0].astype(BF16)
        wd_s[...] = wd_ref[0].astype(BF16)

    @pl.when(active)
    def _():
        gu = _dot(xs_ref[...].astype(BF16), wgu_s[...])
        h = _silu(gu[:, :De]) * gu[:, De:]
        ys_ref[...] = _dot(h.astype(BF16), wd_s[...])


def _experts(block_expert, n_used, xs, w_gate, w_up, w_down):
    n_rows, D = xs.shape
    De = w_gate.shape[2]
    last = lambda b, be, nu: jnp.minimum(b, nu[0] - 1)
    grid_spec = pltpu.PrefetchScalarGridSpec(
        num_scalar_prefetch=2,
        grid=(n_rows // ROW_BLOCK,),
        in_specs=[pl.BlockSpec((ROW_BLOCK, D), lambda b, be, nu: (last(b, be, nu), 0)),
                  pl.BlockSpec((1, D, De), lambda b, be, nu: (be[last(b, be, nu)], 0, 0)),
                  pl.BlockSpec((1, D, De), lambda b, be, nu: (be[last(b, be, nu)], 0, 0)),
                  pl.BlockSpec((1, De, D), lambda b, be, nu: (be[last(b, be, nu)], 0, 0))],
        out_specs=pl.BlockSpec((ROW_BLOCK, D), lambda b, be, nu: (last(b, be, nu), 0)),
        scratch_shapes=[pltpu.VMEM((D, 2 * De), BF16), pltpu.VMEM((De, D), BF16)],
    )
    return pl.pallas_call(
        _expert_kernel,
        grid_spec=grid_spec,
        out_shape=jax.ShapeDtypeStruct((n_rows, D), F32),
        compiler_params=_cparams("arbitrary"),
        name="expert_ffn",
    )(block_expert, n_used, xs, w_gate, w_up, w_down)


def _combine_kernel(start_ref, idx_ref, rank_ref, gate_ref, base_ref, g_ref, b_ref, ys_ref, o_ref, buf, sem):
    tb = base_ref.shape[0]

    def issue(i, c):
        for k in range(TOP_K):
            a = i * TOP_K + k
            d = start_ref[idx_ref[a]] + rank_ref[a]
            pltpu.make_async_copy(ys_ref.at[pl.ds(d, 1), :], buf.at[k, pl.ds(i, 1), :], sem).start()
        return c
    lax.fori_loop(0, tb, issue, 0)
    for k in range(TOP_K):
        pltpu.make_async_copy(ys_ref.at[pl.ds(0, tb), :], buf.at[k], sem).wait()

    gate = gate_ref[...]
    acc = base_ref[...]
    for k in range(TOP_K):
        acc = acc + gate[:, k:k + 1] * buf[k]
    o_ref[...] = _layer_norm(acc, g_ref[...], b_ref[...])


def _combine(start, idx_flat, rank_flat, gate, base, g2, b2, ys):
    T, D = base.shape
    tb = TOK_BLOCK
    grid_spec = pltpu.PrefetchScalarGridSpec(
        num_scalar_prefetch=1,
        grid=(T // tb,),
        in_specs=[pl.BlockSpec((tb * TOP_K,), lambda i, s: (i,), memory_space=pltpu.SMEM),
                  pl.BlockSpec((tb * TOP_K,), lambda i, s: (i,), memory_space=pltpu.SMEM),
                  pl.BlockSpec((tb, LANES), lambda i, s: (i, 0)),
                  pl.BlockSpec((tb, D), lambda i, s: (i, 0)),
                  pl.BlockSpec((1, D), lambda i, s: (0, 0)),
                  pl.BlockSpec((1, D), lambda i, s: (0, 0)),
                  pl.BlockSpec(memory_space=pl.ANY)],
        out_specs=pl.BlockSpec((tb, D), lambda i, s: (i, 0)),
        scratch_shapes=[pltpu.VMEM((TOP_K, tb, D), F32), pltpu.SemaphoreType.DMA(())],
    )
    return pl.pallas_call(
        _combine_kernel,
        grid_spec=grid_spec,
        out_shape=jax.ShapeDtypeStruct((T, D), F32),
        compiler_params=_cparams("arbitrary"),
        name="combine",
    )(start, idx_flat, rank_flat, gate, base, g2, b2, ys)


def _layer(x, w_in, rel_bias, gain_a, gain_b, w_out, ln1_g, ln1_b, router_w, router_bias,
           w_gate, w_up, w_down, ws_gate, ws_up, ws_down, ln2_g, ln2_b, alpha):
    B, S, D = x.shape
    T = B * S
    d_a = gain_a.shape[0]
    d_b = gain_b.shape[0]
    assert S % TILE_B == 0 and T % 512 == 0 and d_a % LANES == 0 and d_b % LANES == 0
    xt = x.reshape(T, D)

    scale = HEAD_DIM ** -0.5
    col = jnp.arange(w_in.shape[1])
    is_q = (col < d_a) | ((col >= 3 * d_a) & (col < 3 * d_a + d_b))
    w_in_b = (w_in * jnp.where(is_q, scale, 1.0)[None, :]).astype(BF16)

    qkv = _qkv_proj(xt, w_in_b).reshape(B, S, -1)
    ya = _attn_a(qkv, _rel_bias_by_offset(rel_bias), B, S, d_a).reshape(T, d_a)
    yb = _attn_b(qkv, B, S, d_a, d_b).reshape(T, d_b)

    x1, base, logits = _post(
        ya, yb, xt, gain_a[None], gain_b[None], w_out.astype(BF16), ln1_g[None], ln1_b[None],
        router_w.astype(BF16), ws_gate.astype(BF16), ws_up.astype(BF16), ws_down.astype(BF16), alpha)

    idx, gate, rank, cnt = _route(logits, router_bias[None].astype(F32))
    cnt = cnt[0]
    padded = (cnt + ROW_BLOCK - 1) // ROW_BLOCK * ROW_BLOCK
    ends = jnp.cumsum(padded)
    start = (ends - padded).astype(jnp.int32)
    n_rows = T * TOP_K + N_EXPERTS * ROW_BLOCK
    n_blocks = n_rows // ROW_BLOCK
    block_expert = jnp.minimum(
        jnp.searchsorted(ends, jnp.arange(n_blocks) * ROW_BLOCK, side='right'), N_EXPERTS - 1).astype(jnp.int32)
    n_used = (ends[-1:] // ROW_BLOCK).astype(jnp.int32)
    idx_flat = idx[:, :TOP_K].reshape(-1)
    rank_flat = rank[:, :TOP_K].reshape(-1)

    xs = _dispatch(start, cnt, idx_flat, rank_flat, x1, n_rows)
    ys = _experts(block_expert, n_used, xs, w_gate, w_up, w_down)
    out = _combine(start, idx_flat, rank_flat, gate, base, ln2_g[None], ln2_b[None], ys)
    return out.reshape(B, S, D)


def kernel(x, w_in, rel_bias, gain_a, gain_b, w_out, ln1_g, ln1_b, router_w, router_bias,
           w_gate, w_up, w_down, ws_gate, ws_up, ws_down, ln2_g, ln2_b):
    depth = w_in.shape[0]
    alpha = (2 * depth) ** 0.25
    for l in range(depth):
        x = _layer(x, w_in[l], rel_bias[l], gain_a[l], gain_b[l], w_out[l], ln1_g[l], ln1_b[l],
                   router_w[l], router_bias[l], w_gate[l], w_up[l], w_down[l],
                   ws_gate[l], ws_up[l], ws_down[l], ln2_g[l], ln2_b[l], alpha)
    return x
```

```python
import functools

import jax
import jax.numpy as jnp
from jax import lax
from jax.experimental import pallas as pl
from jax.experimental.pallas import tpu as pltpu
from jax.experimental.pallas import tpu_sc as plsc

CHUNK = 64
HEAD_DIM = 64
LEFT_CHUNKS = 8
MAX_REL = 128
N_EXPERTS = 256
TOP_K = 8
N_GROUPS = 8
TOPK_GROUPS = 4
EXPERTS_PER_GROUP = N_EXPERTS // N_GROUPS
ROUTED_SCALE = 2.5
LN_EPS = 1e-5
RMS_EPS = 1e-6

LANES = 128
SUBLANES = 8
HEADS_PER_BLOCK = LANES // HEAD_DIM
QBLK_A = 2 * CHUNK
BAND_A = (LEFT_CHUNKS + 2) * CHUNK
BIAS_W = BAND_A + QBLK_A
TILE_B = 256
ROW_BLOCK = 256
TOK_BLOCK = 256
MASK_VALUE = -1e30
VMEM_LIMIT = 48 * 1024 * 1024

F32 = jnp.float32
BF16 = jnp.bfloat16


def _cparams(*sem, flags=None):
    return pltpu.CompilerParams(dimension_semantics=sem, vmem_limit_bytes=VMEM_LIMIT, flags=flags)


def _dot(a, b):
    return jnp.dot(a, b, preferred_element_type=F32)


def _dot_nt(a, b):
    return lax.dot_general(a, b, (((1,), (1,)), ((), ())), preferred_element_type=F32)


def _qkv_kernel(x_ref, w_ref, o_ref, *, col_chunk):
    xb = x_ref[...].astype(BF16)
    for n in range(w_ref.shape[1] // col_chunk):
        cols = slice(n * col_chunk, (n + 1) * col_chunk)
        o_ref[:, cols] = _dot(xb, w_ref[:, cols]).astype(BF16)


def _qkv_proj(xt, w_b):
    T, D = xt.shape
    N = w_b.shape[1]
    tm = 512
    return pl.pallas_call(
        functools.partial(_qkv_kernel, col_chunk=512),
        grid=(T // tm,),
        in_specs=[pl.BlockSpec((tm, D), lambda i: (i, 0)),
                  pl.BlockSpec((D, N), lambda i: (0, 0))],
        out_specs=pl.BlockSpec((tm, N), lambda i: (i, 0)),
        out_shape=jax.ShapeDtypeStruct((T, N), BF16),
        compiler_params=_cparams("parallel"),
        name="qkv_proj",
    )(xt, w_b)


def _attn_a_kernel(q_ref, k_ref, v_ref, w_ref, o_ref, bias_ref):
    S = q_ref.shape[0]
    nblk = S // QBLK_A
    lead = LEFT_CHUNKS * CHUNK
    n_edge = min(lead // QBLK_A, nblk)

    qc = lax.broadcasted_iota(jnp.int32, (QBLK_A, BAND_A), 0) // CHUNK
    kc = lax.broadcasted_iota(jnp.int32, (QBLK_A, BAND_A), 1) // CHUNK
    allowed = (kc >= qc) & (kc <= qc + LEFT_CHUNKS)
    for h in range(HEADS_PER_BLOCK):
        wb = jnp.broadcast_to(w_ref[h], (QBLK_A, BIAS_W))
        toeplitz = pltpu.roll(wb, BIAS_W - (QBLK_A - 1), 1, stride=1, stride_axis=0)
        bias_ref[h] = jnp.where(allowed, toeplitz[:, :BAND_A], MASK_VALUE)

    def blocks(specs):
        work = []
        for p, kstart, nk, bias_off in specs:
            q = q_ref[pl.ds(p * QBLK_A, QBLK_A), :]
            k = k_ref[pl.ds(kstart, nk), :]
            v = v_ref[pl.ds(kstart, nk), :]
            for h in range(HEADS_PER_BLOCK):
                hs = slice(h * HEAD_DIM, (h + 1) * HEAD_DIM)
                work.append(dict(s=_dot_nt(q[:, hs], k[:, hs]), v=v[:, hs],
                                 bias=bias_ref[h, :, bias_off:bias_off + nk]))
        for w in work:
            s = w["s"] + w["bias"]
            e = jnp.exp(s - jnp.max(s, axis=-1, keepdims=True))
            w["l"] = jnp.sum(e, axis=-1, keepdims=True)
            w["e"] = e.astype(BF16)
        outs = [_dot(w["e"], w["v"]) / w["l"] for w in work]
        for i, spec in enumerate(specs):
            o_ref[pl.ds(spec[0] * QBLK_A, QBLK_A), :] = jnp.concatenate(
                outs[i * HEADS_PER_BLOCK:(i + 1) * HEADS_PER_BLOCK], axis=-1)

    blocks([(p, 0, (p + 1) * QBLK_A, lead - p * QBLK_A) for p in range(n_edge)])

    def full(p):
        return (p, pl.multiple_of(p * QBLK_A - lead, QBLK_A), BAND_A, 0)

    n_full = nblk - n_edge
    if n_full % 2:
        blocks([full(n_edge)])
    if n_full >= 2:
        def body(i, c):
            p = n_edge + n_full % 2 + 2 * i
            blocks([full(p), full(p + 1)])
            return c
        lax.fori_loop(0, n_full // 2, body, 0)


def _attn_a(qkv, bias_w, B, S, d_a):
    n_hb = d_a // LANES
    seg = d_a // LANES
    return pl.pallas_call(
        _attn_a_kernel,
        grid=(B, n_hb),
        in_specs=[pl.BlockSpec((None, S, LANES), lambda b, h: (b, 0, h)),
                  pl.BlockSpec((None, S, LANES), lambda b, h: (b, 0, seg + h)),
                  pl.BlockSpec((None, S, LANES), lambda b, h: (b, 0, 2 * seg + h)),
                  pl.BlockSpec((HEADS_PER_BLOCK, 1, BIAS_W), lambda b, h: (h, 0, 0))],
        out_specs=pl.BlockSpec((None, S, LANES), lambda b, h: (b, 0, h)),
        out_shape=jax.ShapeDtypeStruct((B, S, d_a), F32),
        scratch_shapes=[pltpu.VMEM((HEADS_PER_BLOCK, QBLK_A, BAND_A), F32)],
        compiler_params=_cparams("parallel", "parallel"),
        name="attn_chunked",
    )(qkv, qkv, qkv, bias_w)


def _rel_bias_by_offset(rel_bias):
    dist = jnp.clip(BAND_A - 1 - jnp.arange(BIAS_W), -MAX_REL, MAX_REL) + MAX_REL
    return rel_bias[:, dist].astype(F32)[:, None, :]


def _attn_b_kernel(q_ref, k_ref, v_ref, o_ref):
    qi = pl.program_id(2)
    t = TILE_B
    row = lax.broadcasted_iota(jnp.int32, (t, t), 0)
    col = lax.broadcasted_iota(jnp.int32, (t, t), 1)
    later = (row > col).astype(BF16)
    causal = col < row
    q = q_ref[...]
    heads = [slice(h * HEAD_DIM, (h + 1) * HEAD_DIM) for h in range(HEADS_PER_BLOCK)]
    qs = [q[:, hs] for hs in heads]

    def tiles(js, state, first_is_diag=False):
        accs, carries = list(state[0::2]), list(state[1::2])
        work = []
        for n, j in enumerate(js):
            ks = pl.multiple_of(j * t, t)
            kt = k_ref[pl.ds(ks, t), :]
            vt = v_ref[pl.ds(ks, t), :]
            for h, hs in enumerate(heads):
                work.append(dict(h=h, diag=first_is_diag and n == 0, v=vt[:, hs],
                                 z=_dot_nt(qs[h], kt[:, hs])))
        for w in work:
            z = w["z"]
            sp = jnp.maximum(z, 0.0) + jnp.log(1.0 + jnp.exp(-jnp.abs(z)))
            w["log_beta"] = z - sp
            if w["diag"]:
                sp = jnp.where(causal, sp, 0.0)
            hi = sp.astype(BF16)
            lo = (sp - hi.astype(F32)).astype(BF16)
            w["suffix"] = _dot(hi, later) + _dot(lo, later)
            w["rowsum"] = jnp.sum(sp, axis=-1, keepdims=True)
        for w in work:
            h = w["h"]
            if w["diag"]:
                a = jnp.where(causal, jnp.exp(w["log_beta"] - w["suffix"]), 0.0)
                accs[h] = _dot(a.astype(BF16), w["v"])
                carries[h] = w["rowsum"]
            else:
                a = jnp.exp(w["log_beta"] - w["suffix"] - carries[h])
                accs[h] = accs[h] + _dot(a.astype(BF16), w["v"])
                carries[h] = carries[h] + w["rowsum"]
        return tuple(x for pair in zip(accs, carries) for x in pair)

    empty = (None,) * (2 * HEADS_PER_BLOCK)
    state = lax.cond(qi % 2 == 1, lambda: tiles([qi, qi - 1], empty, True), lambda: tiles([qi], empty, True))
    first = qi - 1 - qi % 2

    def body(it, st):
        j = first - 2 * it
        return tiles([j, j - 1], st)

    state = lax.fori_loop(0, qi // 2, body, state)
    o_ref[...] = jnp.concatenate(state[0::2], axis=-1)


def _attn_b(qkv, B, S, d_a, d_b):
    n_hb = d_b // LANES
    base = 3 * d_a // LANES
    seg = d_b // LANES
    return pl.pallas_call(
        _attn_b_kernel,
        grid=(B, n_hb, S // TILE_B),
        in_specs=[pl.BlockSpec((None, TILE_B, LANES), lambda b, h, i: (b, i, base + h)),
                  pl.BlockSpec((None, S, LANES), lambda b, h, i: (b, 0, base + seg + h)),
                  pl.BlockSpec((None, S, LANES), lambda b, h, i: (b, 0, base + 2 * seg + h))],
        out_specs=pl.BlockSpec((None, TILE_B, LANES), lambda b, h, i: (b, i, h)),
        out_shape=jax.ShapeDtypeStruct((B, S, d_b), F32),
        compiler_params=_cparams("parallel", "parallel", "parallel"),
        name="attn_stickbreak",
    )(qkv, qkv, qkv)


def _layer_norm(r, g, b):
    mu = jnp.mean(r, axis=-1, keepdims=True)
    c = r - mu
    var = jnp.mean(c * c, axis=-1, keepdims=True)
    return c * lax.rsqrt(var + LN_EPS) * g + b


def _rms_norm(y, g):
    ms = jnp.mean(y * y, axis=-1, keepdims=True)
    return y * lax.rsqrt(ms + RMS_EPS) * g


def _silu(g):
    return g * jax.nn.sigmoid(g)


def _post_kernel(ya_ref, yb_ref, x_ref, ga_ref, gb_ref, wo_ref, g1_ref, b1_ref, rw_ref,
                 wsg_ref, wsu_ref, wsd_ref, x1_ref, base_ref, logit_ref, *, alpha):
    d_a = ya_ref.shape[1]
    na = _rms_norm(ya_ref[...], ga_ref[...]).astype(BF16)
    nb = _rms_norm(yb_ref[...], gb_ref[...]).astype(BF16)
    h = _dot(na, wo_ref[:d_a, :]) + _dot(nb, wo_ref[d_a:, :])
    x1 = _layer_norm(alpha * x_ref[...] + h, g1_ref[...], b1_ref[...])
    x1_ref[...] = x1
    xb = x1.astype(BF16)
    logit_ref[...] = _dot(xb, rw_ref[...])
    hs = _silu(_dot(xb, wsg_ref[...])) * _dot(xb, wsu_ref[...])
    base_ref[...] = alpha * x1 + _dot(hs.astype(BF16), wsd_ref[...])


def _post(ya, yb, xt, gain_a, gain_b, wo_b, g1, b1, rw_b, wsg_b, wsu_b, wsd_b, alpha):
    T, D = xt.shape
    d_a, d_b = ya.shape[1], yb.shape[1]
    E = rw_b.shape[1]
    De = wsg_b.shape[1]
    tm = TOK_BLOCK
    row = lambda i: (i, 0)
    fix = lambda i: (0, 0)
    return pl.pallas_call(
        functools.partial(_post_kernel, alpha=alpha),
        grid=(T // tm,),
        in_specs=[pl.BlockSpec((tm, d_a), row), pl.BlockSpec((tm, d_b), row), pl.BlockSpec((tm, D), row),
                  pl.BlockSpec((1, d_a), fix), pl.BlockSpec((1, d_b), fix),
                  pl.BlockSpec((d_a + d_b, D), fix), pl.BlockSpec((1, D), fix), pl.BlockSpec((1, D), fix),
                  pl.BlockSpec((D, E), fix), pl.BlockSpec((D, De), fix), pl.BlockSpec((D, De), fix),
                  pl.BlockSpec((De, D), fix)],
        out_specs=[pl.BlockSpec((tm, D), row), pl.BlockSpec((tm, D), row), pl.BlockSpec((tm, E), row)],
        out_shape=[jax.ShapeDtypeStruct((T, D), F32), jax.ShapeDtypeStruct((T, D), F32),
                   jax.ShapeDtypeStruct((T, E), F32)],
        compiler_params=_cparams("parallel"),
        name="post_attn",
    )(ya, yb, xt, gain_a, gain_b, wo_b, g1, b1, rw_b, wsg_b, wsu_b, wsd_b)


def _route_kernel(logit_ref, rbias_ref, idx_ref, gate_ref, rank_ref, cnt_ref, carry_ref):
    tm, E = logit_ref.shape
    neg = -jnp.inf

    @pl.when(pl.program_id(0) == 0)
    def _():
        carry_ref[...] = jnp.zeros_like(carry_ref)

    scores = jax.nn.sigmoid(logit_ref[...])
    sel = scores + rbias_ref[...]
    lane_i = lax.broadcasted_iota(jnp.int32, (tm, E), 1)
    grp = lane_i // EXPERTS_PER_GROUP
    lane = lane_i.astype(F32)

    def first_argmax(v):
        m = jnp.max(v, axis=-1, keepdims=True)
        return m, jnp.min(jnp.where(v == m, lane, float(E)), axis=-1, keepdims=True)

    grp_scores = []
    gs_full = jnp.zeros((tm, E), F32)
    for g in range(N_GROUPS):
        in_g = grp == g
        v = jnp.where(in_g, sel, neg)
        m1, i1 = first_argmax(v)
        m2 = jnp.max(jnp.where(lane == i1, neg, v), axis=-1, keepdims=True)
        grp_scores.append(m1 + m2)
        gs_full = jnp.where(in_g, m1 + m2, gs_full)
    beaten = jnp.zeros((tm, E), jnp.int32)
    for g in range(N_GROUPS):
        s = grp_scores[g]
        better = (s > gs_full) | ((s == gs_full) & (g < grp))
        beaten = beaten + better.astype(jnp.int32)
    selm = jnp.where(beaten < TOPK_GROUPS, sel, neg)

    out_lane = lax.broadcasted_iota(jnp.int32, (tm, LANES), 1)
    hits = []
    gates = []
    chosen = jnp.zeros((tm, E), F32)
    idx_out = jnp.zeros((tm, LANES), jnp.int32)
    gate_sum = jnp.zeros((tm, 1), F32)
    for k in range(TOP_K):
        _, ik = first_argmax(selm)
        hit = lane == ik
        gk = jnp.sum(jnp.where(hit, scores, 0.0), axis=-1, keepdims=True)
        selm = jnp.where(hit, neg, selm)
        chosen = jnp.where(hit, 1.0, chosen)
        idx_out = jnp.where(out_lane == k, ik.astype(jnp.int32), idx_out)
        gate_sum = gate_sum + gk
        hits.append(hit)
        gates.append(gk)

    row = lax.broadcasted_iota(jnp.int32, (tm, tm), 0)
    col = lax.broadcasted_iota(jnp.int32, (tm, tm), 1)
    earlier = (col < row).astype(BF16)
    before = _dot(earlier, chosen.astype(BF16)) + carry_ref[...]
    carry_ref[...] = carry_ref[...] + jnp.sum(chosen, axis=0, keepdims=True)
    cnt_ref[...] = carry_ref[...].astype(jnp.int32)

    gate_out = jnp.zeros((tm, LANES), F32)
    rank_out = jnp.zeros((tm, LANES), jnp.int32)
    for k in range(TOP_K):
        rk = jnp.sum(jnp.where(hits[k], before, 0.0), axis=-1, keepdims=True)
        rank_out = jnp.where(out_lane == k, rk.astype(jnp.int32), rank_out)
        gate_out = jnp.where(out_lane == k, gates[k] / gate_sum * ROUTED_SCALE, gate_out)
    idx_ref[...] = idx_out
    gate_ref[...] = gate_out
    rank_ref[...] = rank_out


def _route(logits, rbias):
    T, E = logits.shape
    tm = TOK_BLOCK
    row = lambda i: (i, 0)
    fix = lambda i: (0, 0)
    return pl.pallas_call(
        _route_kernel,
        grid=(T // tm,),
        in_specs=[pl.BlockSpec((tm, E), row), pl.BlockSpec((1, E), fix)],
        out_specs=[pl.BlockSpec((tm, LANES), row), pl.BlockSpec((tm, LANES), row),
                   pl.BlockSpec((tm, LANES), row), pl.BlockSpec((1, E), fix)],
        out_shape=[jax.ShapeDtypeStruct((T, LANES), jnp.int32), jax.ShapeDtypeStruct((T, LANES), F32),
                   jax.ShapeDtypeStruct((T, LANES), jnp.int32), jax.ShapeDtypeStruct((1, E), jnp.int32)],
        scratch_shapes=[pltpu.VMEM((1, E), F32)],
        compiler_params=_cparams("arbitrary"),
        name="route",
    )(logits, rbias)


def _dispatch_kernel(start_ref, cnt_ref, idx_ref, rank_ref, x_ref, xs_ref, zero_ref, sem, zsem):
    tb = x_ref.shape[0]

    def issue(i, c):
        for k in range(TOP_K):
            a = i * TOP_K + k
            d = start_ref[idx_ref[a]] + rank_ref[a]
            pltpu.make_async_copy(x_ref.at[pl.ds(i, 1), :], xs_ref.at[pl.ds(d, 1), :], sem).start()
        return c
    lax.fori_loop(0, tb, issue, 0)

    @pl.when(pl.program_id(0) == pl.num_programs(0) - 1)
    def _():
        zero_ref[...] = jnp.zeros_like(zero_ref)

        def pad_copies(e):
            n = cnt_ref[e]
            pad = (ROW_BLOCK - n % ROW_BLOCK) % ROW_BLOCK
            first = start_ref[e] + n
            copies = []
            for r in range(SUBLANES - 1):
                copies.append((r < (pad & (SUBLANES - 1)), pltpu.make_async_copy(
                    zero_ref.at[pl.ds(0, 1), :], xs_ref.at[pl.ds(first + r, 1), :], zsem)))
            size = SUBLANES
            while size < ROW_BLOCK:
                off = pl.multiple_of(first + (pad & (size - 1)), SUBLANES)
                copies.append(((pad & size) != 0, pltpu.make_async_copy(
                    zero_ref.at[pl.ds(0, size), :], xs_ref.at[pl.ds(off, size), :], zsem)))
                size *= 2
            return copies

        def fill(e, c):
            for on, cp in pad_copies(e):
                pl.when(on)(cp.start)
            return c

        def drain(e, c):
            for on, cp in pad_copies(e):
                pl.when(on)(cp.wait)
            return c
        lax.fori_loop(0, cnt_ref.shape[0], fill, 0)
        lax.fori_loop(0, cnt_ref.shape[0], drain, 0)

    for k in range(TOP_K):
        pltpu.make_async_copy(x_ref, xs_ref.at[pl.ds(0, tb), :], sem).wait()


def _dispatch(start, cnt, idx_flat, rank_flat, x1, n_rows):
    T, D = x1.shape
    tb = TOK_BLOCK
    grid_spec = pltpu.PrefetchScalarGridSpec(
        num_scalar_prefetch=2,
        grid=(T // tb,),
        in_specs=[pl.BlockSpec((tb * TOP_K,), lambda i, s, c: (i,), memory_space=pltpu.SMEM),
                  pl.BlockSpec((tb * TOP_K,), lambda i, s, c: (i,), memory_space=pltpu.SMEM),
                  pl.BlockSpec((tb, D), lambda i, s, c: (i, 0))],
        out_specs=pl.BlockSpec(memory_space=pl.ANY),
        scratch_shapes=[pltpu.VMEM((ROW_BLOCK // 2, D), F32),
                        pltpu.SemaphoreType.DMA(()), pltpu.SemaphoreType.DMA(())],
    )
    return pl.pallas_call(
        _dispatch_kernel,
        grid_spec=grid_spec,
        out_shape=jax.ShapeDtypeStruct((n_rows, D), F32),
        compiler_params=_cparams("arbitrary"),
        name="dispatch",
    )(start, cnt, idx_flat, rank_flat, x1)


def _expert_kernel(be_ref, nu_ref, xs_ref, wg_ref, wu_ref, wd_ref, ys_ref, wgu_s, wd_s):
    b = pl.program_id(0)
    De = wg_ref.shape[2]
    active = b < nu_ref[0]
    first = (b == 0) | (be_ref[b] != be_ref[jnp.maximum(b - 1, 0)])

    @pl.when(active & first)
    def _():
        wgu_s[:, :De] = wg_ref[0].astype(BF16)
        wgu_s[:, De:] = wu_ref[0].astype(BF16)
        wd_s[...] = wd_ref[0].astype(BF16)

    @pl.when(active)
    def _():
        gu = _dot(xs_ref[...].astype(BF16), wgu_s[...])
        h = _silu(gu[:, :De]) * gu[:, De:]
        ys_ref[...] = _dot(h.astype(BF16), wd_s[...])


def _experts(block_expert, n_used, xs, w_gate, w_up, w_down):
    n_rows, D = xs.shape
    De = w_gate.shape[2]
    last = lambda b, be, nu: jnp.minimum(b, nu[0] - 1)
    grid_spec = pltpu.PrefetchScalarGridSpec(
        num_scalar_prefetch=2,
        grid=(n_rows // ROW_BLOCK,),
        in_specs=[pl.BlockSpec((ROW_BLOCK, D), lambda b, be, nu: (last(b, be, nu), 0)),
                  pl.BlockSpec((1, D, De), lambda b, be, nu: (be[last(b, be, nu)], 0, 0)),
                  pl.BlockSpec((1, D, De), lambda b, be, nu: (be[last(b, be, nu)], 0, 0)),
                  pl.BlockSpec((1, De, D), lambda b, be, nu: (be[last(b, be, nu)], 0, 0))],
        out_specs=pl.BlockSpec((ROW_BLOCK, D), lambda b, be, nu: (last(b, be, nu), 0)),
        scratch_shapes=[pltpu.VMEM((D, 2 * De), BF16), pltpu.VMEM((De, D), BF16)],
    )
    return pl.pallas_call(
        _expert_kernel,
        grid_spec=grid_spec,
        out_shape=jax.ShapeDtypeStruct((n_rows, D), F32),
        compiler_params=_cparams("arbitrary"),
        name="expert_ffn",
    )(block_expert, n_used, xs, w_gate, w_up, w_down)


def _combine_kernel(start_ref, idx_ref, rank_ref, gate_ref, base_ref, g_ref, b_ref, ys_ref, o_ref, buf, sem):
    tb = base_ref.shape[0]

    def issue(i, c):
        for k in range(TOP_K):
            a = i * TOP_K + k
            d = start_ref[idx_ref[a]] + rank_ref[a]
            pltpu.make_async_copy(ys_ref.at[pl.ds(d, 1), :], buf.at[k, pl.ds(i, 1), :], sem).start()
        return c
    lax.fori_loop(0, tb, issue, 0)
    for k in range(TOP_K):
        pltpu.make_async_copy(ys_ref.at[pl.ds(0, tb), :], buf.at[k], sem).wait()

    gate = gate_ref[...]
    acc = base_ref[...]
    for k in range(TOP_K):
        acc = acc + gate[:, k:k + 1] * buf[k]
    o_ref[...] = _layer_norm(acc, g_ref[...], b_ref[...])


def _combine(start, idx_flat, rank_flat, gate, base, g2, b2, ys):
    T, D = base.shape
    tb = TOK_BLOCK
    grid_spec = pltpu.PrefetchScalarGridSpec(
        num_scalar_prefetch=1,
        grid=(T // tb,),
        in_specs=[pl.BlockSpec((tb * TOP_K,), lambda i, s: (i,), memory_space=pltpu.SMEM),
                  pl.BlockSpec((tb * TOP_K,), lambda i, s: (i,), memory_space=pltpu.SMEM),
                  pl.BlockSpec((tb, LANES), lambda i, s: (i, 0)),
                  pl.BlockSpec((tb, D), lambda i, s: (i, 0)),
                  pl.BlockSpec((1, D), lambda i, s: (0, 0)),
                  pl.BlockSpec((1, D), lambda i, s: (0, 0)),
                  pl.BlockSpec(memory_space=pl.ANY)],
        out_specs=pl.BlockSpec((tb, D), lambda i, s: (i, 0)),
        scratch_shapes=[pltpu.VMEM((TOP_K, tb, D), F32), pltpu.SemaphoreType.DMA(())],
    )
    return pl.pallas_call(
        _combine_kernel,
        grid_spec=grid_spec,
        out_shape=jax.ShapeDtypeStruct((T, D), F32),
        compiler_params=_cparams("arbitrary"),
        name="combine",
    )(start, idx_flat, rank_flat, gate, base, g2, b2, ys)


def _dest_kernel(idx_ref, rank_ref, start_ref, dest_ref):
    tm = idx_ref.shape[0]
    E = start_ref.shape[1]
    lane = lax.broadcasted_iota(jnp.int32, (tm, E), 1)
    out_lane = lax.broadcasted_iota(jnp.int32, (tm, LANES), 1)
    idx = idx_ref[...]
    start = start_ref[...]
    dest = jnp.zeros((tm, LANES), F32)
    for k in range(TOP_K):
        sk = jnp.sum(jnp.where(lane == idx[:, k:k + 1], start, 0.0), axis=-1, keepdims=True)
        dest = jnp.where(out_lane == k, sk, dest)
    dest_ref[...] = dest.astype(jnp.int32) + rank_ref[...]


def _dest(idx, rank, start_f):
    T = idx.shape[0]
    E = start_f.shape[1]
    tm = TOK_BLOCK
    row = lambda i: (i, 0)
    return pl.pallas_call(
        _dest_kernel,
        grid=(T // tm,),
        in_specs=[pl.BlockSpec((tm, LANES), row), pl.BlockSpec((tm, LANES), row),
                  pl.BlockSpec((1, E), lambda i: (0, 0))],
        out_specs=pl.BlockSpec((tm, LANES), row),
        out_shape=jax.ShapeDtypeStruct((T, LANES), jnp.int32),
        compiler_params=_cparams("parallel"),
        name="dest_rows",
    )(idx, rank, start_f)


SC_WINDOW = 64


def _sc_mesh():
    return plsc.VectorSubcoreMesh(core_axis_name="core", subcore_axis_name="subcore")


def _sc_workers():
    info = plsc.get_sparse_core_info()
    return info.num_cores, info.num_cores * info.num_subcores


def _sc_worker_id(num_cores):
    return lax.axis_index("subcore") * num_cores + lax.axis_index("core")


def _sc_scatter_rows(x, dest_flat, n_rows):
    T, D = x.shape
    K = dest_flat.shape[0] // T
    nc, nw = _sc_workers()
    per_w = T // nw
    assert T % nw == 0 and per_w % SC_WINDOW == 0

    @functools.partial(
        pl.kernel, out_type=jax.ShapeDtypeStruct((n_rows, D), x.dtype), mesh=_sc_mesh(),
        scratch_types=[pltpu.VMEM((SC_WINDOW,), jnp.int32), pltpu.VMEM((SC_WINDOW, D), x.dtype)],
        name="sc_dispatch")
    def run(x_hbm, i_hbm, o_hbm, idx_v, rows_v):
        first = _sc_worker_id(nc) * per_w

        @pl.loop(0, per_w // SC_WINDOW)
        def _(c):
            base = first + c * SC_WINDOW
            pltpu.sync_copy(x_hbm.at[pl.ds(base, SC_WINDOW)], rows_v)
            for k in range(K):
                pltpu.sync_copy(i_hbm.at[pl.ds(k * T + base, SC_WINDOW)], idx_v)
                pltpu.sync_copy(rows_v, o_hbm.at[idx_v])

    return run(x, dest_flat)


def _sc_gather_rows(ys, dest_flat):
    N = dest_flat.shape[0]
    D = ys.shape[1]
    nc, nw = _sc_workers()
    per_w = N // nw
    assert N % nw == 0 and per_w % SC_WINDOW == 0

    @functools.partial(
        pl.kernel, out_type=jax.ShapeDtypeStruct((N, D), ys.dtype), mesh=_sc_mesh(),
        scratch_types=[pltpu.VMEM((SC_WINDOW,), jnp.int32), pltpu.VMEM((SC_WINDOW, D), ys.dtype)],
        name="sc_combine_gather")
    def run(y_hbm, i_hbm, o_hbm, idx_v, rows_v):
        first = _sc_worker_id(nc) * per_w

        @pl.loop(0, per_w // SC_WINDOW)
        def _(c):
            base = first + c * SC_WINDOW
            pltpu.sync_copy(i_hbm.at[pl.ds(base, SC_WINDOW)], idx_v)
            pltpu.sync_copy(y_hbm.at[idx_v], rows_v)
            pltpu.sync_copy(rows_v, o_hbm.at[pl.ds(base, SC_WINDOW)])

    return run(ys, dest_flat)


def _final_kernel(z_ref, gate_ref, base_ref, g_ref, b_ref, o_ref):
    gate = gate_ref[...]
    acc = base_ref[...]
    for k in range(TOP_K):
        acc = acc + gate[:, k:k + 1] * z_ref[k]
    o_ref[...] = _layer_norm(acc, g_ref[...], b_ref[...])


def _final(z, gate, base, g2, b2):
    T, D = base.shape
    tb = TOK_BLOCK
    return pl.pallas_call(
        _final_kernel,
        grid=(T // tb,),
        in_specs=[pl.BlockSpec((TOP_K, tb, D), lambda i: (0, i, 0)),
                  pl.BlockSpec((tb, LANES), lambda i: (i, 0)),
                  pl.BlockSpec((tb, D), lambda i: (i, 0)),
                  pl.BlockSpec((1, D), lambda i: (0, 0)),
                  pl.BlockSpec((1, D), lambda i: (0, 0))],
        out_specs=pl.BlockSpec((tb, D), lambda i: (i, 0)),
        out_shape=jax.ShapeDtypeStruct((T, D), F32),
        compiler_params=_cparams("parallel"),
        name="final_sum_ln",
    )(z, gate, base, g2, b2)


def _layer(x, w_in, rel_bias, gain_a, gain_b, w_out, ln1_g, ln1_b, router_w, router_bias,
           w_gate, w_up, w_down, ws_gate, ws_up, ws_down, ln2_g, ln2_b, alpha):
    B, S, D = x.shape
    T = B * S
    d_a = gain_a.shape[0]
    d_b = gain_b.shape[0]
    assert S % TILE_B == 0 and T % 512 == 0 and d_a % LANES == 0 and d_b % LANES == 0
    xt = x.reshape(T, D)

    scale = HEAD_DIM ** -0.5
    col = jnp.arange(w_in.shape[1])
    is_q = (col < d_a) | ((col >= 3 * d_a) & (col < 3 * d_a + d_b))
    w_in_b = (w_in * jnp.where(is_q, scale, 1.0)[None, :]).astype(BF16)

    qkv = _qkv_proj(xt, w_in_b).reshape(B, S, -1)
    ya = _attn_a(qkv, _rel_bias_by_offset(rel_bias), B, S, d_a).reshape(T, d_a)
    yb = _attn_b(qkv, B, S, d_a, d_b).reshape(T, d_b)

    x1, base, logits = _post(
        ya, yb, xt, gain_a[None], gain_b[None], w_out.astype(BF16), ln1_g[None], ln1_b[None],
        router_w.astype(BF16), ws_gate.astype(BF16), ws_up.astype(BF16), ws_down.astype(BF16), alpha)

    idx, gate, rank, cnt = _route(logits, router_bias[None].astype(F32))
    cnt = cnt[0]
    padded = (cnt + ROW_BLOCK - 1) // ROW_BLOCK * ROW_BLOCK
    ends = jnp.cumsum(padded)
    start = (ends - padded).astype(jnp.int32)
    n_rows = T * TOP_K + N_EXPERTS * ROW_BLOCK
    n_blocks = n_rows // ROW_BLOCK
    block_expert = jnp.minimum(
        jnp.searchsorted(ends, jnp.arange(n_blocks) * ROW_BLOCK, side='right'), N_EXPERTS - 1).astype(jnp.int32)
    n_used = (ends[-1:] // ROW_BLOCK).astype(jnp.int32)
    dest = _dest(idx, rank, start.astype(F32)[None])
    dest_flat = dest[:, :TOP_K].T.reshape(-1)

    xs = _sc_scatter_rows(x1, dest_flat, n_rows)
    ys = _experts(block_expert, n_used, xs, w_gate, w_up, w_down)
    z = _sc_gather_rows(ys, dest_flat).reshape(TOP_K, T, D)
    out = _final(z, gate, base, ln2_g[None], ln2_b[None])
    return out.reshape(B, S, D)


def kernel(x, w_in, rel_bias, gain_a, gain_b, w_out, ln1_g, ln1_b, router_w, router_bias,
           w_gate, w_up, w_down, ws_gate, ws_up, ws_down, ln2_g, ln2_b):
    depth = w_in.shape[0]
    alpha = (2 * depth) ** 0.25
    for l in range(depth):
        x = _layer(x, w_in[l], rel_bias[l], gain_a[l], gain_b[l], w_out[l], ln1_g[l], ln1_b[l],
                   router_w[l], router_bias[l], w_gate[l], w_up[l], w_down[l],
                   ws_gate[l], ws_up[l], ws_down[l], ln2_g[l], ln2_b[l], alpha)
    return x
```

```python
import functools

import jax
import jax.numpy as jnp
from jax import lax
from jax.experimental import pallas as pl
from jax.experimental.pallas import tpu as pltpu
from jax.experimental.pallas import tpu_sc as plsc

CHUNK = 64
HEAD_DIM = 64
LEFT_CHUNKS = 8
MAX_REL = 128
N_EXPERTS = 256
TOP_K = 8
N_GROUPS = 8
TOPK_GROUPS = 4
EXPERTS_PER_GROUP = N_EXPERTS // N_GROUPS
ROUTED_SCALE = 2.5
LN_EPS = 1e-5
RMS_EPS = 1e-6

LANES = 128
SUBLANES = 8
HEADS_PER_BLOCK = LANES // HEAD_DIM
QBLK_A = 2 * CHUNK
BAND_A = (LEFT_CHUNKS + 2) * CHUNK
BIAS_W = BAND_A + QBLK_A
TILE_B = 256
ROW_BLOCK = 256
TOK_BLOCK = 256
MASK_VALUE = -1e30
VMEM_LIMIT = 48 * 1024 * 1024

F32 = jnp.float32
BF16 = jnp.bfloat16


def _cparams(*sem, flags=None):
    return pltpu.CompilerParams(dimension_semantics=sem, vmem_limit_bytes=VMEM_LIMIT, flags=flags)


def _dot(a, b):
    return jnp.dot(a, b, preferred_element_type=F32)


def _pack_bf16_halves(x):
    half = x.shape[1] // 2
    bits = lax.bitcast_convert_type(x.astype(BF16).astype(F32), jnp.uint32)
    return (bits[:, :half] >> 16) | (bits[:, half:] & jnp.uint32(0xFFFF0000))


def _unpack_bf16_halves(w):
    lo = lax.bitcast_convert_type(w << 16, F32)
    hi = lax.bitcast_convert_type(w & jnp.uint32(0xFFFF0000), F32)
    return lo, hi


def _dot_nt(a, b):
    return lax.dot_general(a, b, (((1,), (1,)), ((), ())), preferred_element_type=F32)


def _qkv_kernel(x_ref, w_ref, o_ref, *, col_chunk):
    xb = x_ref[...].astype(BF16)
    for n in range(w_ref.shape[1] // col_chunk):
        cols = slice(n * col_chunk, (n + 1) * col_chunk)
        o_ref[:, cols] = _dot(xb, w_ref[:, cols]).astype(BF16)


def _qkv_proj(xt, w_b):
    T, D = xt.shape
    N = w_b.shape[1]
    tm = 512
    return pl.pallas_call(
        functools.partial(_qkv_kernel, col_chunk=512),
        grid=(T // tm,),
        in_specs=[pl.BlockSpec((tm, D), lambda i: (i, 0)),
                  pl.BlockSpec((D, N), lambda i: (0, 0))],
        out_specs=pl.BlockSpec((tm, N), lambda i: (i, 0)),
        out_shape=jax.ShapeDtypeStruct((T, N), BF16),
        compiler_params=_cparams("parallel"),
        name="qkv_proj",
    )(xt, w_b)


def _attn_a_kernel(q_ref, k_ref, v_ref, w_ref, o_ref, bias_ref):
    S = q_ref.shape[0]
    nblk = S // QBLK_A
    lead = LEFT_CHUNKS * CHUNK
    n_edge = min(lead // QBLK_A, nblk)

    qc = lax.broadcasted_iota(jnp.int32, (QBLK_A, BAND_A), 0) // CHUNK
    kc = lax.broadcasted_iota(jnp.int32, (QBLK_A, BAND_A), 1) // CHUNK
    allowed = (kc >= qc) & (kc <= qc + LEFT_CHUNKS)
    for h in range(HEADS_PER_BLOCK):
        wb = jnp.broadcast_to(w_ref[h], (QBLK_A, BIAS_W))
        toeplitz = pltpu.roll(wb, BIAS_W - (QBLK_A - 1), 1, stride=1, stride_axis=0)
        bias_ref[h] = jnp.where(allowed, toeplitz[:, :BAND_A], MASK_VALUE)

    def blocks(specs):
        work = []
        for p, kstart, nk, bias_off in specs:
            q = q_ref[pl.ds(p * QBLK_A, QBLK_A), :]
            k = k_ref[pl.ds(kstart, nk), :]
            v = v_ref[pl.ds(kstart, nk), :]
            for h in range(HEADS_PER_BLOCK):
                hs = slice(h * HEAD_DIM, (h + 1) * HEAD_DIM)
                work.append(dict(s=_dot_nt(q[:, hs], k[:, hs]), v=v[:, hs],
                                 bias=bias_ref[h, :, bias_off:bias_off + nk]))
        for w in work:
            s = w["s"] + w["bias"]
            e = jnp.exp(s - jnp.max(s, axis=-1, keepdims=True))
            w["l"] = jnp.sum(e, axis=-1, keepdims=True)
            w["e"] = e.astype(BF16)
        outs = [_dot(w["e"], w["v"]) / w["l"] for w in work]
        for i, spec in enumerate(specs):
            o_ref[pl.ds(spec[0] * QBLK_A, QBLK_A), :] = jnp.concatenate(
                outs[i * HEADS_PER_BLOCK:(i + 1) * HEADS_PER_BLOCK], axis=-1)

    blocks([(p, 0, (p + 1) * QBLK_A, lead - p * QBLK_A) for p in range(n_edge)])

    def full(p):
        return (p, pl.multiple_of(p * QBLK_A - lead, QBLK_A), BAND_A, 0)

    n_full = nblk - n_edge
    if n_full % 2:
        blocks([full(n_edge)])
    if n_full >= 2:
        def body(i, c):
            p = n_edge + n_full % 2 + 2 * i
            blocks([full(p), full(p + 1)])
            return c
        lax.fori_loop(0, n_full // 2, body, 0)


def _attn_a(qkv, bias_w, B, S, d_a):
    n_hb = d_a // LANES
    seg = d_a // LANES
    return pl.pallas_call(
        _attn_a_kernel,
        grid=(B, n_hb),
        in_specs=[pl.BlockSpec((None, S, LANES), lambda b, h: (b, 0, h)),
                  pl.BlockSpec((None, S, LANES), lambda b, h: (b, 0, seg + h)),
                  pl.BlockSpec((None, S, LANES), lambda b, h: (b, 0, 2 * seg + h)),
                  pl.BlockSpec((HEADS_PER_BLOCK, 1, BIAS_W), lambda b, h: (h, 0, 0))],
        out_specs=pl.BlockSpec((None, S, LANES), lambda b, h: (b, 0, h)),
        out_shape=jax.ShapeDtypeStruct((B, S, d_a), F32),
        scratch_shapes=[pltpu.VMEM((HEADS_PER_BLOCK, QBLK_A, BAND_A), F32)],
        compiler_params=_cparams("parallel", "parallel"),
        name="attn_chunked",
    )(qkv, qkv, qkv, bias_w)


def _rel_bias_by_offset(rel_bias):
    dist = jnp.clip(BAND_A - 1 - jnp.arange(BIAS_W), -MAX_REL, MAX_REL) + MAX_REL
    return rel_bias[:, dist].astype(F32)[:, None, :]


def _attn_b_kernel(q_ref, k_ref, v_ref, o_ref):
    qi = pl.program_id(2)
    t = TILE_B
    row = lax.broadcasted_iota(jnp.int32, (t, t), 0)
    col = lax.broadcasted_iota(jnp.int32, (t, t), 1)
    later = (row > col).astype(BF16)
    causal = col < row
    q = q_ref[...]
    heads = [slice(h * HEAD_DIM, (h + 1) * HEAD_DIM) for h in range(HEADS_PER_BLOCK)]
    qs = [q[:, hs] for hs in heads]

    def tiles(js, state, first_is_diag=False):
        accs, carries = list(state[0::2]), list(state[1::2])
        work = []
        for n, j in enumerate(js):
            ks = pl.multiple_of(j * t, t)
            kt = k_ref[pl.ds(ks, t), :]
            vt = v_ref[pl.ds(ks, t), :]
            for h, hs in enumerate(heads):
                work.append(dict(h=h, diag=first_is_diag and n == 0, v=vt[:, hs],
                                 z=_dot_nt(qs[h], kt[:, hs])))
        for w in work:
            z = w["z"]
            sp = jnp.maximum(z, 0.0) + jnp.log(1.0 + jnp.exp(-jnp.abs(z)))
            w["log_beta"] = z - sp
            if w["diag"]:
                sp = jnp.where(causal, sp, 0.0)
            hi = sp.astype(BF16)
            lo = (sp - hi.astype(F32)).astype(BF16)
            w["suffix"] = _dot(hi, later) + _dot(lo, later)
            w["rowsum"] = jnp.sum(sp, axis=-1, keepdims=True)
        for w in work:
            h = w["h"]
            if w["diag"]:
                a = jnp.where(causal, jnp.exp(w["log_beta"] - w["suffix"]), 0.0)
                accs[h] = _dot(a.astype(BF16), w["v"])
                carries[h] = w["rowsum"]
            else:
                a = jnp.exp(w["log_beta"] - w["suffix"] - carries[h])
                accs[h] = accs[h] + _dot(a.astype(BF16), w["v"])
                carries[h] = carries[h] + w["rowsum"]
        return tuple(x for pair in zip(accs, carries) for x in pair)

    empty = (None,) * (2 * HEADS_PER_BLOCK)
    state = lax.cond(qi % 2 == 1, lambda: tiles([qi, qi - 1], empty, True), lambda: tiles([qi], empty, True))
    first = qi - 1 - qi % 2

    def body(it, st):
        j = first - 2 * it
        return tiles([j, j - 1], st)

    state = lax.fori_loop(0, qi // 2, body, state)
    o_ref[...] = jnp.concatenate(state[0::2], axis=-1)


def _attn_b(qkv, B, S, d_a, d_b):
    n_hb = d_b // LANES
    base = 3 * d_a // LANES
    seg = d_b // LANES
    return pl.pallas_call(
        _attn_b_kernel,
        grid=(B, n_hb, S // TILE_B),
        in_specs=[pl.BlockSpec((None, TILE_B, LANES), lambda b, h, i: (b, i, base + h)),
                  pl.BlockSpec((None, S, LANES), lambda b, h, i: (b, 0, base + seg + h)),
                  pl.BlockSpec((None, S, LANES), lambda b, h, i: (b, 0, base + 2 * seg + h))],
        out_specs=pl.BlockSpec((None, TILE_B, LANES), lambda b, h, i: (b, i, h)),
        out_shape=jax.ShapeDtypeStruct((B, S, d_b), F32),
        compiler_params=_cparams("parallel", "parallel", "parallel"),
        name="attn_stickbreak",
    )(qkv, qkv, qkv)


def _layer_norm(r, g, b):
    mu = jnp.mean(r, axis=-1, keepdims=True)
    c = r - mu
    var = jnp.mean(c * c, axis=-1, keepdims=True)
    return c * lax.rsqrt(var + LN_EPS) * g + b


def _rms_norm(y, g):
    ms = jnp.mean(y * y, axis=-1, keepdims=True)
    return y * lax.rsqrt(ms + RMS_EPS) * g


def _silu(g):
    return g * jax.nn.sigmoid(g)


def _post_kernel(ya_ref, yb_ref, x_ref, ga_ref, gb_ref, wo_ref, g1_ref, b1_ref, rw_ref,
                 wsg_ref, wsu_ref, wsd_ref, x1_ref, base_ref, logit_ref, *, alpha):
    d_a = ya_ref.shape[1]
    na = _rms_norm(ya_ref[...], ga_ref[...]).astype(BF16)
    nb = _rms_norm(yb_ref[...], gb_ref[...]).astype(BF16)
    h = _dot(na, wo_ref[:d_a, :]) + _dot(nb, wo_ref[d_a:, :])
    x1 = _layer_norm(alpha * x_ref[...] + h, g1_ref[...], b1_ref[...])
    x1_ref[...] = _pack_bf16_halves(x1)
    xb = x1.astype(BF16)
    logit_ref[...] = _dot(xb, rw_ref[...])
    hs = _silu(_dot(xb, wsg_ref[...])) * _dot(xb, wsu_ref[...])
    base_ref[...] = alpha * x1 + _dot(hs.astype(BF16), wsd_ref[...])


def _post(ya, yb, xt, gain_a, gain_b, wo_b, g1, b1, rw_b, wsg_b, wsu_b, wsd_b, alpha):
    T, D = xt.shape
    d_a, d_b = ya.shape[1], yb.shape[1]
    E = rw_b.shape[1]
    De = wsg_b.shape[1]
    tm = TOK_BLOCK
    row = lambda i: (i, 0)
    fix = lambda i: (0, 0)
    return pl.pallas_call(
        functools.partial(_post_kernel, alpha=alpha),
        grid=(T // tm,),
        in_specs=[pl.BlockSpec((tm, d_a), row), pl.BlockSpec((tm, d_b), row), pl.BlockSpec((tm, D), row),
                  pl.BlockSpec((1, d_a), fix), pl.BlockSpec((1, d_b), fix),
                  pl.BlockSpec((d_a + d_b, D), fix), pl.BlockSpec((1, D), fix), pl.BlockSpec((1, D), fix),
                  pl.BlockSpec((D, E), fix), pl.BlockSpec((D, De), fix), pl.BlockSpec((D, De), fix),
                  pl.BlockSpec((De, D), fix)],
        out_specs=[pl.BlockSpec((tm, D // 2), row), pl.BlockSpec((tm, D), row), pl.BlockSpec((tm, E), row)],
        out_shape=[jax.ShapeDtypeStruct((T, D // 2), jnp.uint32), jax.ShapeDtypeStruct((T, D), F32),
                   jax.ShapeDtypeStruct((T, E), F32)],
        compiler_params=_cparams("parallel"),
        name="post_attn",
    )(ya, yb, xt, gain_a, gain_b, wo_b, g1, b1, rw_b, wsg_b, wsu_b, wsd_b)


def _route_kernel(logit_ref, rbias_ref, idx_ref, gate_ref, rank_ref, cnt_ref, carry_ref):
    tm, E = logit_ref.shape
    neg = -jnp.inf

    @pl.when(pl.program_id(0) == 0)
    def _():
        carry_ref[...] = jnp.zeros_like(carry_ref)

    scores = jax.nn.sigmoid(logit_ref[...])
    sel = scores + rbias_ref[...]
    lane_i = lax.broadcasted_iota(jnp.int32, (tm, E), 1)
    grp = lane_i // EXPERTS_PER_GROUP
    lane = lane_i.astype(F32)

    def first_argmax(v):
        m = jnp.max(v, axis=-1, keepdims=True)
        return m, jnp.min(jnp.where(v == m, lane, float(E)), axis=-1, keepdims=True)

    grp_scores = []
    gs_full = jnp.zeros((tm, E), F32)
    for g in range(N_GROUPS):
        in_g = grp == g
        v = jnp.where(in_g, sel, neg)
        m1, i1 = first_argmax(v)
        m2 = jnp.max(jnp.where(lane == i1, neg, v), axis=-1, keepdims=True)
        grp_scores.append(m1 + m2)
        gs_full = jnp.where(in_g, m1 + m2, gs_full)
    beaten = jnp.zeros((tm, E), jnp.int32)
    for g in range(N_GROUPS):
        s = grp_scores[g]
        better = (s > gs_full) | ((s == gs_full) & (g < grp))
        beaten = beaten + better.astype(jnp.int32)
    selm = jnp.where(beaten < TOPK_GROUPS, sel, neg)

    out_lane = lax.broadcasted_iota(jnp.int32, (tm, LANES), 1)
    hits = []
    gates = []
    chosen = jnp.zeros((tm, E), F32)
    idx_out = jnp.zeros((tm, LANES), jnp.int32)
    gate_sum = jnp.zeros((tm, 1), F32)
    for k in range(TOP_K):
        _, ik = first_argmax(selm)
        hit = lane == ik
        gk = jnp.sum(jnp.where(hit, scores, 0.0), axis=-1, keepdims=True)
        selm = jnp.where(hit, neg, selm)
        chosen = jnp.where(hit, 1.0, chosen)
        idx_out = jnp.where(out_lane == k, ik.astype(jnp.int32), idx_out)
        gate_sum = gate_sum + gk
        hits.append(hit)
        gates.append(gk)

    row = lax.broadcasted_iota(jnp.int32, (tm, tm), 0)
    col = lax.broadcasted_iota(jnp.int32, (tm, tm), 1)
    earlier = (col < row).astype(BF16)
    before = _dot(earlier, chosen.astype(BF16)) + carry_ref[...]
    carry_ref[...] = carry_ref[...] + jnp.sum(chosen, axis=0, keepdims=True)
    cnt_ref[...] = carry_ref[...].astype(jnp.int32)

    gate_out = jnp.zeros((tm, LANES), F32)
    rank_out = jnp.zeros((tm, LANES), jnp.int32)
    for k in range(TOP_K):
        rk = jnp.sum(jnp.where(hits[k], before, 0.0), axis=-1, keepdims=True)
        rank_out = jnp.where(out_lane == k, rk.astype(jnp.int32), rank_out)
        gate_out = jnp.where(out_lane == k, gates[k] / gate_sum * ROUTED_SCALE, gate_out)
    idx_ref[...] = idx_out
    gate_ref[...] = gate_out
    rank_ref[...] = rank_out


def _route(logits, rbias):
    T, E = logits.shape
    tm = TOK_BLOCK
    row = lambda i: (i, 0)
    fix = lambda i: (0, 0)
    return pl.pallas_call(
        _route_kernel,
        grid=(T // tm,),
        in_specs=[pl.BlockSpec((tm, E), row), pl.BlockSpec((1, E), fix)],
        out_specs=[pl.BlockSpec((tm, LANES), row), pl.BlockSpec((tm, LANES), row),
                   pl.BlockSpec((tm, LANES), row), pl.BlockSpec((1, E), fix)],
        out_shape=[jax.ShapeDtypeStruct((T, LANES), jnp.int32), jax.ShapeDtypeStruct((T, LANES), F32),
                   jax.ShapeDtypeStruct((T, LANES), jnp.int32), jax.ShapeDtypeStruct((1, E), jnp.int32)],
        scratch_shapes=[pltpu.VMEM((1, E), F32)],
        compiler_params=_cparams("arbitrary"),
        name="route",
    )(logits, rbias)


def _dispatch_kernel(start_ref, cnt_ref, idx_ref, rank_ref, x_ref, xs_ref, zero_ref, sem, zsem):
    tb = x_ref.shape[0]

    def issue(i, c):
        for k in range(TOP_K):
            a = i * TOP_K + k
            d = start_ref[idx_ref[a]] + rank_ref[a]
            pltpu.make_async_copy(x_ref.at[pl.ds(i, 1), :], xs_ref.at[pl.ds(d, 1), :], sem).start()
        return c
    lax.fori_loop(0, tb, issue, 0)

    @pl.when(pl.program_id(0) == pl.num_programs(0) - 1)
    def _():
        zero_ref[...] = jnp.zeros_like(zero_ref)

        def pad_copies(e):
            n = cnt_ref[e]
            pad = (ROW_BLOCK - n % ROW_BLOCK) % ROW_BLOCK
            first = start_ref[e] + n
            copies = []
            for r in range(SUBLANES - 1):
                copies.append((r < (pad & (SUBLANES - 1)), pltpu.make_async_copy(
                    zero_ref.at[pl.ds(0, 1), :], xs_ref.at[pl.ds(first + r, 1), :], zsem)))
            size = SUBLANES
            while size < ROW_BLOCK:
                off = pl.multiple_of(first + (pad & (size - 1)), SUBLANES)
                copies.append(((pad & size) != 0, pltpu.make_async_copy(
                    zero_ref.at[pl.ds(0, size), :], xs_ref.at[pl.ds(off, size), :], zsem)))
                size *= 2
            return copies

        def fill(e, c):
            for on, cp in pad_copies(e):
                pl.when(on)(cp.start)
            return c

        def drain(e, c):
            for on, cp in pad_copies(e):
                pl.when(on)(cp.wait)
            return c
        lax.fori_loop(0, cnt_ref.shape[0], fill, 0)
        lax.fori_loop(0, cnt_ref.shape[0], drain, 0)

    for k in range(TOP_K):
        pltpu.make_async_copy(x_ref, xs_ref.at[pl.ds(0, tb), :], sem).wait()


def _dispatch(start, cnt, idx_flat, rank_flat, x1, n_rows):
    T, D = x1.shape
    tb = TOK_BLOCK
    grid_spec = pltpu.PrefetchScalarGridSpec(
        num_scalar_prefetch=2,
        grid=(T // tb,),
        in_specs=[pl.BlockSpec((tb * TOP_K,), lambda i, s, c: (i,), memory_space=pltpu.SMEM),
                  pl.BlockSpec((tb * TOP_K,), lambda i, s, c: (i,), memory_space=pltpu.SMEM),
                  pl.BlockSpec((tb, D), lambda i, s, c: (i, 0))],
        out_specs=pl.BlockSpec(memory_space=pl.ANY),
        scratch_shapes=[pltpu.VMEM((ROW_BLOCK // 2, D), F32),
                        pltpu.SemaphoreType.DMA(()), pltpu.SemaphoreType.DMA(())],
    )
    return pl.pallas_call(
        _dispatch_kernel,
        grid_spec=grid_spec,
        out_shape=jax.ShapeDtypeStruct((n_rows, D), F32),
        compiler_params=_cparams("arbitrary"),
        name="dispatch",
    )(start, cnt, idx_flat, rank_flat, x1)


def _expert_kernel(be_ref, nu_ref, xs_ref, wg_ref, wu_ref, wd_ref, ys_ref, wgu_s, wd_s):
    b = pl.program_id(0)
    De = wg_ref.shape[2]
    active = b < nu_ref[0]
    first = (b == 0) | (be_ref[b] != be_ref[jnp.maximum(b - 1, 0)])

    @pl.when(active & first)
    def _():
        wgu_s[:, :De] = wg_ref[0].astype(BF16)
        wgu_s[:, De:] = wu_ref[0].astype(BF16)
        wd_s[...] = wd_ref[0].astype(BF16)

    @pl.when(active)
    def _():
        half = xs_ref.shape[0] // 2
        dh = xs_ref.shape[1]
        gus = []
        for r in range(2):
            lo, hi = _unpack_bf16_halves(xs_ref[r * half:(r + 1) * half, :])
            gus.append(_dot(lo.astype(BF16), wgu_s[:dh, :]) + _dot(hi.astype(BF16), wgu_s[dh:, :]))
        hs = [(_silu(gu[:, :De]) * gu[:, De:]).astype(BF16) for gu in gus]
        for r in range(2):
            ys_ref[r * half:(r + 1) * half, :] = _pack_bf16_halves(_dot(hs[r], wd_s[...]))


def _experts(block_expert, n_used, xs, w_gate, w_up, w_down):
    n_rows, Dh = xs.shape
    D = 2 * Dh
    De = w_gate.shape[2]
    last = lambda b, be, nu: jnp.minimum(b, nu[0] - 1)
    grid_spec = pltpu.PrefetchScalarGridSpec(
        num_scalar_prefetch=2,
        grid=(n_rows // ROW_BLOCK,),
        in_specs=[pl.BlockSpec((ROW_BLOCK, Dh), lambda b, be, nu: (last(b, be, nu), 0)),
                  pl.BlockSpec((1, D, De), lambda b, be, nu: (be[last(b, be, nu)], 0, 0)),
                  pl.BlockSpec((1, D, De), lambda b, be, nu: (be[last(b, be, nu)], 0, 0)),
                  pl.BlockSpec((1, De, D), lambda b, be, nu: (be[last(b, be, nu)], 0, 0))],
        out_specs=pl.BlockSpec((ROW_BLOCK, Dh), lambda b, be, nu: (last(b, be, nu), 0)),
        scratch_shapes=[pltpu.VMEM((D, 2 * De), BF16), pltpu.VMEM((De, D), BF16)],
    )
    return pl.pallas_call(
        _expert_kernel,
        grid_spec=grid_spec,
        out_shape=jax.ShapeDtypeStruct((n_rows, Dh), jnp.uint32),
        compiler_params=_cparams("arbitrary"),
        name="expert_ffn",
    )(block_expert, n_used, xs, w_gate, w_up, w_down)


def _combine_kernel(start_ref, idx_ref, rank_ref, gate_ref, base_ref, g_ref, b_ref, ys_ref, o_ref, buf, sem):
    tb = base_ref.shape[0]

    def issue(i, c):
        for k in range(TOP_K):
            a = i * TOP_K + k
            d = start_ref[idx_ref[a]] + rank_ref[a]
            pltpu.make_async_copy(ys_ref.at[pl.ds(d, 1), :], buf.at[k, pl.ds(i, 1), :], sem).start()
        return c
    lax.fori_loop(0, tb, issue, 0)
    for k in range(TOP_K):
        pltpu.make_async_copy(ys_ref.at[pl.ds(0, tb), :], buf.at[k], sem).wait()

    gate = gate_ref[...]
    acc = base_ref[...]
    for k in range(TOP_K):
        acc = acc + gate[:, k:k + 1] * buf[k]
    o_ref[...] = _layer_norm(acc, g_ref[...], b_ref[...])


def _combine(start, idx_flat, rank_flat, gate, base, g2, b2, ys):
    T, D = base.shape
    tb = TOK_BLOCK
    grid_spec = pltpu.PrefetchScalarGridSpec(
        num_scalar_prefetch=1,
        grid=(T // tb,),
        in_specs=[pl.BlockSpec((tb * TOP_K,), lambda i, s: (i,), memory_space=pltpu.SMEM),
                  pl.BlockSpec((tb * TOP_K,), lambda i, s: (i,), memory_space=pltpu.SMEM),
                  pl.BlockSpec((tb, LANES), lambda i, s: (i, 0)),
                  pl.BlockSpec((tb, D), lambda i, s: (i, 0)),
                  pl.BlockSpec((1, D), lambda i, s: (0, 0)),
                  pl.BlockSpec((1, D), lambda i, s: (0, 0)),
                  pl.BlockSpec(memory_space=pl.ANY)],
        out_specs=pl.BlockSpec((tb, D), lambda i, s: (i, 0)),
        scratch_shapes=[pltpu.VMEM((TOP_K, tb, D), F32), pltpu.SemaphoreType.DMA(())],
    )
    return pl.pallas_call(
        _combine_kernel,
        grid_spec=grid_spec,
        out_shape=jax.ShapeDtypeStruct((T, D), F32),
        compiler_params=_cparams("arbitrary"),
        name="combine",
    )(start, idx_flat, rank_flat, gate, base, g2, b2, ys)


def _dest_kernel(idx_ref, rank_ref, start_ref, dest_ref):
    tm = idx_ref.shape[0]
    E = start_ref.shape[1]
    lane = lax.broadcasted_iota(jnp.int32, (tm, E), 1)
    out_lane = lax.broadcasted_iota(jnp.int32, (tm, LANES), 1)
    idx = idx_ref[...]
    start = start_ref[...]
    dest = jnp.zeros((tm, LANES), F32)
    for k in range(TOP_K):
        sk = jnp.sum(jnp.where(lane == idx[:, k:k + 1], start, 0.0), axis=-1, keepdims=True)
        dest = jnp.where(out_lane == k, sk, dest)
    dest_ref[...] = dest.astype(jnp.int32) + rank_ref[...]


def _dest(idx, rank, start_f):
    T = idx.shape[0]
    E = start_f.shape[1]
    tm = TOK_BLOCK
    row = lambda i: (i, 0)
    return pl.pallas_call(
        _dest_kernel,
        grid=(T // tm,),
        in_specs=[pl.BlockSpec((tm, LANES), row), pl.BlockSpec((tm, LANES), row),
                  pl.BlockSpec((1, E), lambda i: (0, 0))],
        out_specs=pl.BlockSpec((tm, LANES), row),
        out_shape=jax.ShapeDtypeStruct((T, LANES), jnp.int32),
        compiler_params=_cparams("parallel"),
        name="dest_rows",
    )(idx, rank, start_f)


SC_WINDOW = 128


def _sc_mesh():
    return plsc.VectorSubcoreMesh(core_axis_name="core", subcore_axis_name="subcore")


def _sc_workers():
    info = plsc.get_sparse_core_info()
    return info.num_cores, info.num_cores * info.num_subcores


def _sc_worker_id(num_cores):
    return lax.axis_index("subcore") * num_cores + lax.axis_index("core")


def _sc_scatter_rows(x, dest_flat, n_rows):
    T, D = x.shape
    K = dest_flat.shape[0] // T
    nc, nw = _sc_workers()
    per_w = T // nw
    assert T % nw == 0 and per_w % SC_WINDOW == 0

    @functools.partial(
        pl.kernel, out_type=jax.ShapeDtypeStruct((n_rows, D), x.dtype), mesh=_sc_mesh(),
        scratch_types=[pltpu.VMEM((SC_WINDOW,), jnp.int32), pltpu.VMEM((SC_WINDOW, D), x.dtype)],
        name="sc_dispatch")
    def run(x_hbm, i_hbm, o_hbm, idx_v, rows_v):
        first = _sc_worker_id(nc) * per_w

        @pl.loop(0, per_w // SC_WINDOW)
        def _(c):
            base = first + c * SC_WINDOW
            pltpu.sync_copy(x_hbm.at[pl.ds(base, SC_WINDOW)], rows_v)
            for k in range(K):
                pltpu.sync_copy(i_hbm.at[pl.ds(k * T + base, SC_WINDOW)], idx_v)
                pltpu.sync_copy(rows_v, o_hbm.at[idx_v])

    return run(x, dest_flat)


def _sc_gather_rows(ys, dest_flat):
    N = dest_flat.shape[0]
    D = ys.shape[1]
    nc, nw = _sc_workers()
    per_w = N // nw
    assert N % nw == 0 and per_w % SC_WINDOW == 0

    @functools.partial(
        pl.kernel, out_type=jax.ShapeDtypeStruct((N, D), ys.dtype), mesh=_sc_mesh(),
        scratch_types=[pltpu.VMEM((SC_WINDOW,), jnp.int32), pltpu.VMEM((SC_WINDOW, D), ys.dtype)],
        name="sc_combine_gather")
    def run(y_hbm, i_hbm, o_hbm, idx_v, rows_v):
        first = _sc_worker_id(nc) * per_w

        @pl.loop(0, per_w // SC_WINDOW)
        def _(c):
            base = first + c * SC_WINDOW
            pltpu.sync_copy(i_hbm.at[pl.ds(base, SC_WINDOW)], idx_v)
            pltpu.sync_copy(y_hbm.at[idx_v], rows_v)
            pltpu.sync_copy(rows_v, o_hbm.at[pl.ds(base, SC_WINDOW)])

    return run(ys, dest_flat)


def _final_kernel(z_ref, gate_ref, base_ref, g_ref, b_ref, o_ref):
    gate = gate_ref[...]
    acc_lo = acc_hi = None
    for k in range(TOP_K):
        lo, hi = _unpack_bf16_halves(z_ref[k])
        g = gate[:, k:k + 1]
        acc_lo = g * lo if acc_lo is None else acc_lo + g * lo
        acc_hi = g * hi if acc_hi is None else acc_hi + g * hi
    acc = base_ref[...] + jnp.concatenate([acc_lo, acc_hi], axis=-1)
    o_ref[...] = _layer_norm(acc, g_ref[...], b_ref[...])


def _final(z, gate, base, g2, b2):
    T, D = base.shape
    tb = TOK_BLOCK
    return pl.pallas_call(
        _final_kernel,
        grid=(T // tb,),
        in_specs=[pl.BlockSpec((TOP_K, tb, D // 2), lambda i: (0, i, 0)),
                  pl.BlockSpec((tb, LANES), lambda i: (i, 0)),
                  pl.BlockSpec((tb, D), lambda i: (i, 0)),
                  pl.BlockSpec((1, D), lambda i: (0, 0)),
                  pl.BlockSpec((1, D), lambda i: (0, 0))],
        out_specs=pl.BlockSpec((tb, D), lambda i: (i, 0)),
        out_shape=jax.ShapeDtypeStruct((T, D), F32),
        compiler_params=_cparams("parallel"),
        name="final_sum_ln",
    )(z, gate, base, g2, b2)


def _layer(x, w_in, rel_bias, gain_a, gain_b, w_out, ln1_g, ln1_b, router_w, router_bias,
           w_gate, w_up, w_down, ws_gate, ws_up, ws_down, ln2_g, ln2_b, alpha):
    B, S, D = x.shape
    T = B * S
    d_a = gain_a.shape[0]
    d_b = gain_b.shape[0]
    assert S % TILE_B == 0 and T % 512 == 0 and d_a % LANES == 0 and d_b % LANES == 0
    xt = x.reshape(T, D)

    scale = HEAD_DIM ** -0.5
    col = jnp.arange(w_in.shape[1])
    is_q = (col < d_a) | ((col >= 3 * d_a) & (col < 3 * d_a + d_b))
    w_in_b = (w_in * jnp.where(is_q, scale, 1.0)[None, :]).astype(BF16)

    qkv = _qkv_proj(xt, w_in_b).reshape(B, S, -1)
    ya = _attn_a(qkv, _rel_bias_by_offset(rel_bias), B, S, d_a).reshape(T, d_a)
    yb = _attn_b(qkv, B, S, d_a, d_b).reshape(T, d_b)

    x1, base, logits = _post(
        ya, yb, xt, gain_a[None], gain_b[None], w_out.astype(BF16), ln1_g[None], ln1_b[None],
        router_w.astype(BF16), ws_gate.astype(BF16), ws_up.astype(BF16), ws_down.astype(BF16), alpha)

    idx, gate, rank, cnt = _route(logits, router_bias[None].astype(F32))
    cnt = cnt[0]
    padded = (cnt + ROW_BLOCK - 1) // ROW_BLOCK * ROW_BLOCK
    ends = jnp.cumsum(padded)
    start = (ends - padded).astype(jnp.int32)
    n_rows = T * TOP_K + N_EXPERTS * ROW_BLOCK
    n_blocks = n_rows // ROW_BLOCK
    block_start = jnp.arange(n_blocks, dtype=jnp.int32) * ROW_BLOCK
    block_expert = jnp.minimum(jnp.sum(ends[None, :] <= block_start[:, None], axis=1), N_EXPERTS - 1).astype(jnp.int32)
    n_used = (ends[-1:] // ROW_BLOCK).astype(jnp.int32)
    dest = _dest(idx, rank, start.astype(F32)[None])
    dest_flat = dest[:, :TOP_K].T.reshape(-1)

    xs = _sc_scatter_rows(x1, dest_flat, n_rows)
    ys = _experts(block_expert, n_used, xs, w_gate, w_up, w_down)
    z = _sc_gather_rows(ys, dest_flat).reshape(TOP_K, T, D // 2)
    out = _final(z, gate, base, ln2_g[None], ln2_b[None])
    return out.reshape(B, S, D)


def kernel(x, w_in, rel_bias, gain_a, gain_b, w_out, ln1_g, ln1_b, router_w, router_bias,
           w_gate, w_up, w_down, ws_gate, ws_up, ws_down, ln2_g, ln2_b):
    depth = w_in.shape[0]
    alpha = (2 * depth) ** 0.25
    for l in range(depth):
        x = _layer(x, w_in[l], rel_bias[l], gain_a[l], gain_b[l], w_out[l], ln1_g[l], ln1_b[l],
                   router_w[l], router_bias[l], w_gate[l], w_up[l], w_down[l],
                   ws_gate[l], ws_up[l], ws_down[l], ln2_g[l], ln2_b[l], alpha)
    return x
```

```python
import functools

import jax
import jax.numpy as jnp
from jax import lax
from jax.experimental import pallas as pl
from jax.experimental.pallas import tpu as pltpu
from jax.experimental.pallas import tpu_sc as plsc

CHUNK = 64
HEAD_DIM = 64
LEFT_CHUNKS = 8
MAX_REL = 128
N_EXPERTS = 256
TOP_K = 8
N_GROUPS = 8
TOPK_GROUPS = 4
EXPERTS_PER_GROUP = N_EXPERTS // N_GROUPS
ROUTED_SCALE = 2.5
LN_EPS = 1e-5
RMS_EPS = 1e-6

LANES = 128
SUBLANES = 8
HEADS_PER_BLOCK = LANES // HEAD_DIM
QBLK_A = 2 * CHUNK
BAND_A = (LEFT_CHUNKS + 2) * CHUNK
BIAS_W = BAND_A + QBLK_A
TILE_B = 256
ROW_BLOCK = 256
TOK_BLOCK = 256
MASK_VALUE = -1e30
VMEM_LIMIT = 48 * 1024 * 1024

F32 = jnp.float32
BF16 = jnp.bfloat16


def _cparams(*sem, flags=None):
    return pltpu.CompilerParams(dimension_semantics=sem, vmem_limit_bytes=VMEM_LIMIT, flags=flags)


def _dot(a, b):
    return jnp.dot(a, b, preferred_element_type=F32)


def _dot_nt(a, b):
    return lax.dot_general(a, b, (((1,), (1,)), ((), ())), preferred_element_type=F32)


def _pack_bf16_halves(x):
    half = x.shape[1] // 2
    bits = lax.bitcast_convert_type(x.astype(BF16).astype(F32), jnp.uint32)
    return (bits[:, :half] >> 16) | (bits[:, half:] & jnp.uint32(0xFFFF0000))


def _unpack_bf16_halves(w):
    lo = lax.bitcast_convert_type(w << 16, F32)
    hi = lax.bitcast_convert_type(w & jnp.uint32(0xFFFF0000), F32)
    return lo, hi


def _qkv_kernel(x_ref, w_ref, o_ref, *, col_chunk):
    xb = x_ref[...].astype(BF16)
    for n in range(w_ref.shape[1] // col_chunk):
        cols = slice(n * col_chunk, (n + 1) * col_chunk)
        o_ref[:, cols] = _dot(xb, w_ref[:, cols]).astype(BF16)


def _qkv_proj(xt, w_b):
    T, D = xt.shape
    N = w_b.shape[1]
    tm = 512
    return pl.pallas_call(
        functools.partial(_qkv_kernel, col_chunk=512),
        grid=(T // tm,),
        in_specs=[pl.BlockSpec((tm, D), lambda i: (i, 0)),
                  pl.BlockSpec((D, N), lambda i: (0, 0))],
        out_specs=pl.BlockSpec((tm, N), lambda i: (i, 0)),
        out_shape=jax.ShapeDtypeStruct((T, N), BF16),
        compiler_params=_cparams("parallel"),
        name="qkv_proj",
    )(xt, w_b)


def _attn_a_kernel(q_ref, k_ref, v_ref, w_ref, o_ref, bias_ref):
    S = q_ref.shape[0]
    nblk = S // QBLK_A
    lead = LEFT_CHUNKS * CHUNK
    n_edge = min(lead // QBLK_A, nblk)

    qc = lax.broadcasted_iota(jnp.int32, (QBLK_A, BAND_A), 0) // CHUNK
    kc = lax.broadcasted_iota(jnp.int32, (QBLK_A, BAND_A), 1) // CHUNK
    allowed = (kc >= qc) & (kc <= qc + LEFT_CHUNKS)
    for h in range(HEADS_PER_BLOCK):
        wb = jnp.broadcast_to(w_ref[h], (QBLK_A, BIAS_W))
        toeplitz = pltpu.roll(wb, BIAS_W - (QBLK_A - 1), 1, stride=1, stride_axis=0)
        bias_ref[h] = jnp.where(allowed, toeplitz[:, :BAND_A], MASK_VALUE)

    def blocks(specs):
        work = []
        for p, kstart, nk, bias_off in specs:
            q = q_ref[pl.ds(p * QBLK_A, QBLK_A), :]
            k = k_ref[pl.ds(kstart, nk), :]
            v = v_ref[pl.ds(kstart, nk), :]
            for h in range(HEADS_PER_BLOCK):
                hs = slice(h * HEAD_DIM, (h + 1) * HEAD_DIM)
                work.append(dict(s=_dot_nt(q[:, hs], k[:, hs]), v=v[:, hs],
                                 bias=bias_ref[h, :, bias_off:bias_off + nk]))
        for w in work:
            s = w["s"] + w["bias"]
            e = jnp.exp(s - jnp.max(s, axis=-1, keepdims=True))
            w["l"] = jnp.sum(e, axis=-1, keepdims=True)
            w["e"] = e.astype(BF16)
        outs = [_dot(w["e"], w["v"]) / w["l"] for w in work]
        for i, spec in enumerate(specs):
            o_ref[pl.ds(spec[0] * QBLK_A, QBLK_A), :] = jnp.concatenate(
                outs[i * HEADS_PER_BLOCK:(i + 1) * HEADS_PER_BLOCK], axis=-1)

    blocks([(p, 0, (p + 1) * QBLK_A, lead - p * QBLK_A) for p in range(n_edge)])

    def full(p):
        return (p, pl.multiple_of(p * QBLK_A - lead, QBLK_A), BAND_A, 0)

    n_full = nblk - n_edge
    if n_full % 2:
        blocks([full(n_edge)])
    if n_full >= 2:
        def body(i, c):
            p = n_edge + n_full % 2 + 2 * i
            blocks([full(p), full(p + 1)])
            return c
        lax.fori_loop(0, n_full // 2, body, 0)


def _attn_a(qkv, bias_w, B, S, d_a):
    n_hb = d_a // LANES
    seg = d_a // LANES
    return pl.pallas_call(
        _attn_a_kernel,
        grid=(B, n_hb),
        in_specs=[pl.BlockSpec((None, S, LANES), lambda b, h: (b, 0, h)),
                  pl.BlockSpec((None, S, LANES), lambda b, h: (b, 0, seg + h)),
                  pl.BlockSpec((None, S, LANES), lambda b, h: (b, 0, 2 * seg + h)),
                  pl.BlockSpec((HEADS_PER_BLOCK, 1, BIAS_W), lambda b, h: (h, 0, 0))],
        out_specs=pl.BlockSpec((None, S, LANES), lambda b, h: (b, 0, h)),
        out_shape=jax.ShapeDtypeStruct((B, S, d_a), F32),
        scratch_shapes=[pltpu.VMEM((HEADS_PER_BLOCK, QBLK_A, BAND_A), F32)],
        compiler_params=_cparams("parallel", "parallel"),
        name="attn_chunked",
    )(qkv, qkv, qkv, bias_w)


def _rel_bias_by_offset(rel_bias):
    dist = jnp.clip(BAND_A - 1 - jnp.arange(BIAS_W), -MAX_REL, MAX_REL) + MAX_REL
    return rel_bias[:, dist].astype(F32)[:, None, :]


def _attn_b_kernel(q_ref, k_ref, v_ref, o_ref):
    qi = pl.program_id(2)
    t = TILE_B
    row = lax.broadcasted_iota(jnp.int32, (t, t), 0)
    col = lax.broadcasted_iota(jnp.int32, (t, t), 1)
    later = (row > col).astype(BF16)
    causal = col < row
    q = q_ref[...]
    heads = [slice(h * HEAD_DIM, (h + 1) * HEAD_DIM) for h in range(HEADS_PER_BLOCK)]
    qs = [q[:, hs] for hs in heads]

    def tiles(js, state, first_is_diag=False):
        accs, carries = list(state[0::2]), list(state[1::2])
        work = []
        for n, j in enumerate(js):
            ks = pl.multiple_of(j * t, t)
            kt = k_ref[pl.ds(ks, t), :]
            vt = v_ref[pl.ds(ks, t), :]
            for h, hs in enumerate(heads):
                work.append(dict(h=h, diag=first_is_diag and n == 0, v=vt[:, hs],
                                 z=_dot_nt(qs[h], kt[:, hs])))
        for w in work:
            z = w["z"]
            sp = jnp.maximum(z, 0.0) + jnp.log(1.0 + jnp.exp(-jnp.abs(z)))
            w["log_beta"] = z - sp
            if w["diag"]:
                sp = jnp.where(causal, sp, 0.0)
            hi = sp.astype(BF16)
            lo = (sp - hi.astype(F32)).astype(BF16)
            w["suffix"] = _dot(hi, later) + _dot(lo, later)
            w["rowsum"] = jnp.sum(sp, axis=-1, keepdims=True)
        for w in work:
            h = w["h"]
            if w["diag"]:
                a = jnp.where(causal, jnp.exp(w["log_beta"] - w["suffix"]), 0.0)
                accs[h] = _dot(a.astype(BF16), w["v"])
                carries[h] = w["rowsum"]
            else:
                a = jnp.exp(w["log_beta"] - w["suffix"] - carries[h])
                accs[h] = accs[h] + _dot(a.astype(BF16), w["v"])
                carries[h] = carries[h] + w["rowsum"]
        return tuple(x for pair in zip(accs, carries) for x in pair)

    empty = (None,) * (2 * HEADS_PER_BLOCK)
    state = lax.cond(qi % 2 == 1, lambda: tiles([qi, qi - 1], empty, True), lambda: tiles([qi], empty, True))
    first = qi - 1 - qi % 2

    def body(it, st):
        j = first - 2 * it
        return tiles([j, j - 1], st)

    state = lax.fori_loop(0, qi // 2, body, state)
    o_ref[...] = jnp.concatenate(state[0::2], axis=-1)


def _attn_b(qkv, B, S, d_a, d_b):
    n_hb = d_b // LANES
    base = 3 * d_a // LANES
    seg = d_b // LANES
    return pl.pallas_call(
        _attn_b_kernel,
        grid=(B, n_hb, S // TILE_B),
        in_specs=[pl.BlockSpec((None, TILE_B, LANES), lambda b, h, i: (b, i, base + h)),
                  pl.BlockSpec((None, S, LANES), lambda b, h, i: (b, 0, base + seg + h)),
                  pl.BlockSpec((None, S, LANES), lambda b, h, i: (b, 0, base + 2 * seg + h))],
        out_specs=pl.BlockSpec((None, TILE_B, LANES), lambda b, h, i: (b, i, h)),
        out_shape=jax.ShapeDtypeStruct((B, S, d_b), F32),
        compiler_params=_cparams("parallel", "parallel", "parallel"),
        name="attn_stickbreak",
    )(qkv, qkv, qkv)


def _layer_norm(r, g, b):
    mu = jnp.mean(r, axis=-1, keepdims=True)
    c = r - mu
    var = jnp.mean(c * c, axis=-1, keepdims=True)
    return c * lax.rsqrt(var + LN_EPS) * g + b


def _rms_norm(y, g):
    ms = jnp.mean(y * y, axis=-1, keepdims=True)
    return y * lax.rsqrt(ms + RMS_EPS) * g


def _silu(g):
    return g * jax.nn.sigmoid(g)


def _post_kernel(ya_ref, yb_ref, x_ref, ga_ref, gb_ref, wo_ref, g1_ref, b1_ref, rw_ref,
                 wsg_ref, wsu_ref, wsd_ref, x1_ref, base_ref, logit_ref, *, alpha):
    d_a = ya_ref.shape[1]
    na = _rms_norm(ya_ref[...], ga_ref[...]).astype(BF16)
    nb = _rms_norm(yb_ref[...], gb_ref[...]).astype(BF16)
    h = _dot(na, wo_ref[:d_a, :]) + _dot(nb, wo_ref[d_a:, :])
    x1 = _layer_norm(alpha * x_ref[...] + h, g1_ref[...], b1_ref[...])
    x1_ref[...] = _pack_bf16_halves(x1)
    xb = x1.astype(BF16)
    logit_ref[...] = _dot_nt(rw_ref[...], xb)
    hs = _silu(_dot(xb, wsg_ref[...])) * _dot(xb, wsu_ref[...])
    base_ref[...] = alpha * x1 + _dot(hs.astype(BF16), wsd_ref[...])


def _post(ya, yb, xt, gain_a, gain_b, wo_b, g1, b1, rw_b, wsg_b, wsu_b, wsd_b, alpha):
    T, D = xt.shape
    d_a, d_b = ya.shape[1], yb.shape[1]
    E = rw_b.shape[0]
    De = wsg_b.shape[1]
    tm = TOK_BLOCK
    row = lambda i: (i, 0)
    fix = lambda i: (0, 0)
    return pl.pallas_call(
        functools.partial(_post_kernel, alpha=alpha),
        grid=(T // tm,),
        in_specs=[pl.BlockSpec((tm, d_a), row), pl.BlockSpec((tm, d_b), row), pl.BlockSpec((tm, D), row),
                  pl.BlockSpec((1, d_a), fix), pl.BlockSpec((1, d_b), fix),
                  pl.BlockSpec((d_a + d_b, D), fix), pl.BlockSpec((1, D), fix), pl.BlockSpec((1, D), fix),
                  pl.BlockSpec((E, D), fix), pl.BlockSpec((D, De), fix), pl.BlockSpec((D, De), fix),
                  pl.BlockSpec((De, D), fix)],
        out_specs=[pl.BlockSpec((tm, D // 2), row), pl.BlockSpec((tm, D), row),
                   pl.BlockSpec((E, tm), lambda i: (0, i))],
        out_shape=[jax.ShapeDtypeStruct((T, D // 2), jnp.uint32), jax.ShapeDtypeStruct((T, D), F32),
                   jax.ShapeDtypeStruct((E, T), F32)],
        compiler_params=_cparams("parallel"),
        name="post_attn",
    )(ya, yb, xt, gain_a, gain_b, wo_b, g1, b1, rw_b, wsg_b, wsu_b, wsd_b)


def _route_kernel(logit_ref, rbias_ref, idx_ref, gate_ref, rank_ref, cnt_ref, carry_ref):
    E, tm = logit_ref.shape
    neg = -jnp.inf

    @pl.when(pl.program_id(0) == 0)
    def _():
        carry_ref[...] = jnp.zeros_like(carry_ref)

    scores = jax.nn.sigmoid(logit_ref[...])
    sel = scores + rbias_ref[...]
    eidx = lax.broadcasted_iota(jnp.int32, (E, tm), 0).astype(F32)

    def first_argmax(v, ids):
        m = jnp.max(v, axis=0, keepdims=True)
        return m, jnp.min(jnp.where(v == m, ids, float(E)), axis=0, keepdims=True)

    grp_scores = []
    ids = lax.broadcasted_iota(jnp.int32, (EXPERTS_PER_GROUP, tm), 0).astype(F32)
    for g in range(N_GROUPS):
        v = sel[g * EXPERTS_PER_GROUP:(g + 1) * EXPERTS_PER_GROUP, :]
        m1, i1 = first_argmax(v, ids)
        m2 = jnp.max(jnp.where(ids == i1, neg, v), axis=0, keepdims=True)
        grp_scores.append(m1 + m2)
    parts = []
    for g in range(N_GROUPS):
        beaten = jnp.zeros((1, tm), jnp.int32)
        for o in range(N_GROUPS):
            if o != g:
                s, t = grp_scores[o], grp_scores[g]
                beaten = beaten + ((s > t) | ((s == t) & (o < g))).astype(jnp.int32)
        rows = slice(g * EXPERTS_PER_GROUP, (g + 1) * EXPERTS_PER_GROUP)
        parts.append(jnp.where(beaten < TOPK_GROUPS, sel[rows, :], neg))
    selm = jnp.concatenate(parts, axis=0)

    hits, gates, ids_k = [], [], []
    chosen = jnp.zeros((E, tm), F32)
    gate_sum = jnp.zeros((1, tm), F32)
    for k in range(TOP_K):
        _, ik = first_argmax(selm, eidx)
        hit = eidx == ik
        gk = jnp.sum(jnp.where(hit, scores, 0.0), axis=0, keepdims=True)
        selm = jnp.where(hit, neg, selm)
        chosen = jnp.where(hit, 1.0, chosen)
        gate_sum = gate_sum + gk
        hits.append(hit)
        gates.append(gk)
        ids_k.append(ik)

    row = lax.broadcasted_iota(jnp.int32, (tm, tm), 0)
    col = lax.broadcasted_iota(jnp.int32, (tm, tm), 1)
    earlier = (row < col).astype(BF16)
    before = _dot(chosen.astype(BF16), earlier) + carry_ref[...]
    carry_ref[...] = carry_ref[...] + jnp.sum(chosen, axis=1, keepdims=True)
    cnt_ref[...] = carry_ref[...].astype(jnp.int32)

    ranks = [jnp.sum(jnp.where(hit, before, 0.0), axis=0, keepdims=True) for hit in hits]
    idx_ref[...] = jnp.concatenate(ids_k, axis=0).astype(jnp.int32)
    rank_ref[...] = jnp.concatenate(ranks, axis=0).astype(jnp.int32)
    gate_ref[...] = jnp.concatenate(gates, axis=0) / gate_sum * ROUTED_SCALE


def _route(logits_t, rbias):
    E, T = logits_t.shape
    tm = TOK_BLOCK
    col = lambda i: (0, i)
    fix = lambda i: (0, 0)
    return pl.pallas_call(
        _route_kernel,
        grid=(T // tm,),
        in_specs=[pl.BlockSpec((E, tm), col), pl.BlockSpec((E, 1), fix)],
        out_specs=[pl.BlockSpec((TOP_K, tm), col), pl.BlockSpec((TOP_K, tm), col),
                   pl.BlockSpec((TOP_K, tm), col), pl.BlockSpec((E, 1), fix)],
        out_shape=[jax.ShapeDtypeStruct((TOP_K, T), jnp.int32), jax.ShapeDtypeStruct((TOP_K, T), F32),
                   jax.ShapeDtypeStruct((TOP_K, T), jnp.int32), jax.ShapeDtypeStruct((E, 1), jnp.int32)],
        scratch_shapes=[pltpu.VMEM((E, 1), F32)],
        compiler_params=_cparams("arbitrary"),
        name="route",
    )(logits_t, rbias)


def _dest_kernel(idx_ref, rank_ref, start_ref, dest_ref):
    K, tm = idx_ref.shape
    E = start_ref.shape[0]
    eidx = lax.broadcasted_iota(jnp.int32, (E, tm), 0)
    idx = idx_ref[...]
    start = start_ref[...]
    rows = [jnp.sum(jnp.where(eidx == idx[k:k + 1, :], start, 0.0), axis=0, keepdims=True) for k in range(K)]
    dest_ref[...] = jnp.concatenate(rows, axis=0).astype(jnp.int32) + rank_ref[...]


def _dest(idx, rank, start_f):
    K, T = idx.shape
    E = start_f.shape[0]
    tm = 2 * TOK_BLOCK
    col = lambda i: (0, i)
    return pl.pallas_call(
        _dest_kernel,
        grid=(T // tm,),
        in_specs=[pl.BlockSpec((K, tm), col), pl.BlockSpec((K, tm), col), pl.BlockSpec((E, 1), lambda i: (0, 0))],
        out_specs=pl.BlockSpec((K, tm), col),
        out_shape=jax.ShapeDtypeStruct((K, T), jnp.int32),
        compiler_params=_cparams("parallel"),
        name="dest_rows",
    )(idx, rank, start_f)


def _expert_kernel(first_ref, nblk_ref, nused_ref, xs_hbm, wg_ref, wu_ref, wd_ref, ys_hbm,
                   xbuf, ybuf, wgu_s, wd_s, in_sem, out_sem):
    e = pl.program_id(0)
    De = wg_ref.shape[2]
    n = nblk_ref[e]
    g0 = first_ref[e]
    n_used = nused_ref[0]

    def rows(g):
        return pl.ds(pl.multiple_of(g * ROW_BLOCK, ROW_BLOCK), ROW_BLOCK)

    def in_copy(g, slot):
        return pltpu.make_async_copy(xs_hbm.at[rows(g), :], xbuf.at[slot], in_sem.at[slot])

    def out_copy(g, slot):
        return pltpu.make_async_copy(ybuf.at[slot], ys_hbm.at[rows(g), :], out_sem.at[slot])

    @pl.when((e == 0) & (n_used > 0))
    def _():
        in_copy(0, 0).start()

    @pl.when(n > 0)
    def _():
        wgu_s[:, :De] = wg_ref[0].astype(BF16)
        wgu_s[:, De:] = wu_ref[0].astype(BF16)
        wd_s[...] = wd_ref[0].astype(BF16)

        def block(j, c):
            g = g0 + j
            slot = g & 1
            in_copy(g, slot).wait()

            @pl.when(g + 1 < n_used)
            def _():
                in_copy(g + 1, 1 - slot).start()

            @pl.when(g >= 2)
            def _():
                out_copy(g - 2, slot).wait()

            half = ROW_BLOCK // 2
            dh = xbuf.shape[2]
            gus = []
            for r in range(2):
                lo, hi = _unpack_bf16_halves(xbuf[slot, r * half:(r + 1) * half, :])
                gus.append(_dot(lo.astype(BF16), wgu_s[:dh, :]) + _dot(hi.astype(BF16), wgu_s[dh:, :]))
            hs = [(_silu(gu[:, :De]) * gu[:, De:]).astype(BF16) for gu in gus]
            for r in range(2):
                ybuf[slot, r * half:(r + 1) * half, :] = _pack_bf16_halves(_dot(hs[r], wd_s[...]))
            out_copy(g, slot).start()
            return c

        lax.fori_loop(0, n, block, 0)

    @pl.when(e == pl.num_programs(0) - 1)
    def _():
        for back in (1, 2):
            @pl.when(n_used >= back)
            def _():
                out_copy(n_used - back, (n_used - back) & 1).wait()


def _experts(first_block, n_blocks, n_used, xs, w_gate, w_up, w_down):
    n_rows, Dh = xs.shape
    D = 2 * Dh
    E, _, De = w_gate.shape
    grid_spec = pltpu.PrefetchScalarGridSpec(
        num_scalar_prefetch=3,
        grid=(E,),
        in_specs=[pl.BlockSpec(memory_space=pl.ANY),
                  pl.BlockSpec((1, D, De), lambda e, *_: (e, 0, 0)),
                  pl.BlockSpec((1, D, De), lambda e, *_: (e, 0, 0)),
                  pl.BlockSpec((1, De, D), lambda e, *_: (e, 0, 0))],
        out_specs=pl.BlockSpec(memory_space=pl.ANY),
        scratch_shapes=[pltpu.VMEM((2, ROW_BLOCK, Dh), jnp.uint32), pltpu.VMEM((2, ROW_BLOCK, Dh), jnp.uint32),
                        pltpu.VMEM((D, 2 * De), BF16), pltpu.VMEM((De, D), BF16),
                        pltpu.SemaphoreType.DMA((2,)), pltpu.SemaphoreType.DMA((2,))],
    )
    return pl.pallas_call(
        _expert_kernel,
        grid_spec=grid_spec,
        out_shape=jax.ShapeDtypeStruct((n_rows, Dh), jnp.uint32),
        compiler_params=_cparams("arbitrary"),
        name="expert_ffn",
    )(first_block, n_blocks, n_used, xs, w_gate, w_up, w_down)


SC_WINDOW = 128


def _sc_mesh():
    return plsc.VectorSubcoreMesh(core_axis_name="core", subcore_axis_name="subcore")


def _sc_workers():
    info = plsc.get_sparse_core_info()
    return info.num_cores, info.num_cores * info.num_subcores


def _sc_worker_id(num_cores):
    return lax.axis_index("subcore") * num_cores + lax.axis_index("core")


def _sc_scatter_rows(x, dest_flat, n_rows):
    T, D = x.shape
    K = dest_flat.shape[0] // T
    nc, nw = _sc_workers()
    per_w = T // nw
    assert T % nw == 0 and per_w % SC_WINDOW == 0

    @functools.partial(
        pl.kernel, out_type=jax.ShapeDtypeStruct((n_rows, D), x.dtype), mesh=_sc_mesh(),
        scratch_types=[pltpu.VMEM((SC_WINDOW,), jnp.int32), pltpu.VMEM((SC_WINDOW, D), x.dtype)],
        name="sc_dispatch")
    def run(x_hbm, i_hbm, o_hbm, idx_v, rows_v):
        first = _sc_worker_id(nc) * per_w

        @pl.loop(0, per_w // SC_WINDOW)
        def _(c):
            base = first + c * SC_WINDOW
            pltpu.sync_copy(x_hbm.at[pl.ds(base, SC_WINDOW)], rows_v)
            for k in range(K):
                pltpu.sync_copy(i_hbm.at[pl.ds(k * T + base, SC_WINDOW)], idx_v)
                pltpu.sync_copy(rows_v, o_hbm.at[idx_v])

    return run(x, dest_flat)


def _sc_gather_rows(ys, dest_flat):
    N = dest_flat.shape[0]
    D = ys.shape[1]
    nc, nw = _sc_workers()
    per_w = N // nw
    assert N % nw == 0 and per_w % SC_WINDOW == 0

    @functools.partial(
        pl.kernel, out_type=jax.ShapeDtypeStruct((N, D), ys.dtype), mesh=_sc_mesh(),
        scratch_types=[pltpu.VMEM((SC_WINDOW,), jnp.int32), pltpu.VMEM((SC_WINDOW, D), ys.dtype)],
        name="sc_combine_gather")
    def run(y_hbm, i_hbm, o_hbm, idx_v, rows_v):
        first = _sc_worker_id(nc) * per_w

        @pl.loop(0, per_w // SC_WINDOW)
        def _(c):
            base = first + c * SC_WINDOW
            pltpu.sync_copy(i_hbm.at[pl.ds(base, SC_WINDOW)], idx_v)
            pltpu.sync_copy(y_hbm.at[idx_v], rows_v)
            pltpu.sync_copy(rows_v, o_hbm.at[pl.ds(base, SC_WINDOW)])

    return run(ys, dest_flat)


def _final_kernel(z_ref, gate_ref, base_ref, g_ref, b_ref, o_ref):
    gate = gate_ref[...]
    acc_lo = acc_hi = None
    for k in range(TOP_K):
        lo, hi = _unpack_bf16_halves(z_ref[k])
        g = gate[:, k:k + 1]
        acc_lo = g * lo if acc_lo is None else acc_lo + g * lo
        acc_hi = g * hi if acc_hi is None else acc_hi + g * hi
    acc = base_ref[...] + jnp.concatenate([acc_lo, acc_hi], axis=-1)
    o_ref[...] = _layer_norm(acc, g_ref[...], b_ref[...])


def _final(z, gate, base, g2, b2):
    T, D = base.shape
    tb = TOK_BLOCK
    return pl.pallas_call(
        _final_kernel,
        grid=(T // tb,),
        in_specs=[pl.BlockSpec((TOP_K, tb, D // 2), lambda i: (0, i, 0)),
                  pl.BlockSpec((tb, TOP_K), lambda i: (i, 0)),
                  pl.BlockSpec((tb, D), lambda i: (i, 0)),
                  pl.BlockSpec((1, D), lambda i: (0, 0)),
                  pl.BlockSpec((1, D), lambda i: (0, 0))],
        out_specs=pl.BlockSpec((tb, D), lambda i: (i, 0)),
        out_shape=jax.ShapeDtypeStruct((T, D), F32),
        compiler_params=_cparams("parallel"),
        name="final_sum_ln",
    )(z, gate, base, g2, b2)


def _layer(x, w_in, rel_bias, gain_a, gain_b, w_out, ln1_g, ln1_b, router_w, router_bias,
           w_gate, w_up, w_down, ws_gate, ws_up, ws_down, ln2_g, ln2_b, alpha):
    B, S, D = x.shape
    T = B * S
    d_a = gain_a.shape[0]
    d_b = gain_b.shape[0]
    assert S % TILE_B == 0 and T % 512 == 0 and d_a % LANES == 0 and d_b % LANES == 0
    xt = x.reshape(T, D)

    scale = HEAD_DIM ** -0.5
    col = jnp.arange(w_in.shape[1])
    is_q = (col < d_a) | ((col >= 3 * d_a) & (col < 3 * d_a + d_b))
    w_in_b = (w_in * jnp.where(is_q, scale, 1.0)[None, :]).astype(BF16)

    qkv = _qkv_proj(xt, w_in_b).reshape(B, S, -1)
    ya = _attn_a(qkv, _rel_bias_by_offset(rel_bias), B, S, d_a).reshape(T, d_a)
    yb = _attn_b(qkv, B, S, d_a, d_b).reshape(T, d_b)

    x1p, base, logits_t = _post(
        ya, yb, xt, gain_a[None], gain_b[None], w_out.astype(BF16), ln1_g[None], ln1_b[None],
        router_w.T.astype(BF16), ws_gate.astype(BF16), ws_up.astype(BF16), ws_down.astype(BF16), alpha)

    idx, gate, rank, cnt = _route(logits_t, router_bias[:, None].astype(F32))
    cnt = cnt[:, 0]
    padded = (cnt + ROW_BLOCK - 1) // ROW_BLOCK * ROW_BLOCK
    ends = jnp.cumsum(padded)
    start = (ends - padded).astype(jnp.int32)
    n_rows = T * TOP_K + N_EXPERTS * ROW_BLOCK
    first_block = start // ROW_BLOCK
    n_blocks = (padded // ROW_BLOCK).astype(jnp.int32)
    n_used = (ends[-1:] // ROW_BLOCK).astype(jnp.int32)
    dest_flat = _dest(idx, rank, start.astype(F32)[:, None]).reshape(-1)

    xs = _sc_scatter_rows(x1p, dest_flat, n_rows)
    ys = _experts(first_block, n_blocks, n_used, xs, w_gate, w_up, w_down)
    z = _sc_gather_rows(ys, dest_flat).reshape(TOP_K, T, D // 2)
    out = _final(z, gate.T, base, ln2_g[None], ln2_b[None])
    return out.reshape(B, S, D)


def kernel(x, w_in, rel_bias, gain_a, gain_b, w_out, ln1_g, ln1_b, router_w, router_bias,
           w_gate, w_up, w_down, ws_gate, ws_up, ws_down, ln2_g, ln2_b):
    depth = w_in.shape[0]
    alpha = (2 * depth) ** 0.25
    for l in range(depth):
        x = _layer(x, w_in[l], rel_bias[l], gain_a[l], gain_b[l], w_out[l], ln1_g[l], ln1_b[l],
                   router_w[l], router_bias[l], w_gate[l], w_up[l], w_down[l],
                   ws_gate[l], ws_up[l], ws_down[l], ln2_g[l], ln2_b[l], alpha)
    return x
```

```python
import functools

import jax
import jax.numpy as jnp
from jax import lax
from jax.experimental import pallas as pl
from jax.experimental.pallas import tpu as pltpu
from jax.experimental.pallas import tpu_sc as plsc

CHUNK = 64
HEAD_DIM = 64
LEFT_CHUNKS = 8
MAX_REL = 128
N_EXPERTS = 256
TOP_K = 8
N_GROUPS = 8
TOPK_GROUPS = 4
EXPERTS_PER_GROUP = N_EXPERTS // N_GROUPS
ROUTED_SCALE = 2.5
LN_EPS = 1e-5
RMS_EPS = 1e-6

LANES = 128
SUBLANES = 8
HEADS_PER_BLOCK = LANES // HEAD_DIM
QBLK_A = 2 * CHUNK
BAND_A = (LEFT_CHUNKS + 2) * CHUNK
BIAS_W = BAND_A + QBLK_A
TILE_B = 256
ROW_BLOCK = 256
EXPERT_RING = 4
TOK_BLOCK = 256
MASK_VALUE = -1e30
VMEM_LIMIT = 48 * 1024 * 1024

F32 = jnp.float32
BF16 = jnp.bfloat16


def _cparams(*sem, flags=None):
    return pltpu.CompilerParams(dimension_semantics=sem, vmem_limit_bytes=VMEM_LIMIT, flags=flags)


def _dot(a, b):
    return jnp.dot(a, b, preferred_element_type=F32)


def _dot_nt(a, b):
    return lax.dot_general(a, b, (((1,), (1,)), ((), ())), preferred_element_type=F32)


def _pack_bf16_halves(x):
    half = x.shape[1] // 2
    bits = lax.bitcast_convert_type(x.astype(BF16).astype(F32), jnp.uint32)
    return (bits[:, :half] >> 16) | (bits[:, half:] & jnp.uint32(0xFFFF0000))


def _unpack_bf16_halves(w):
    lo = lax.bitcast_convert_type(w << 16, F32)
    hi = lax.bitcast_convert_type(w & jnp.uint32(0xFFFF0000), F32)
    return lo, hi


def _qkv_kernel(x_ref, w_ref, o_ref, *, col_chunk):
    xb = x_ref[...].astype(BF16)
    for n in range(w_ref.shape[1] // col_chunk):
        cols = slice(n * col_chunk, (n + 1) * col_chunk)
        o_ref[:, cols] = _dot(xb, w_ref[:, cols]).astype(BF16)


def _qkv_proj(xt, w_b):
    T, D = xt.shape
    N = w_b.shape[1]
    tm = 512
    return pl.pallas_call(
        functools.partial(_qkv_kernel, col_chunk=512),
        grid=(T // tm,),
        in_specs=[pl.BlockSpec((tm, D), lambda i: (i, 0)),
                  pl.BlockSpec((D, N), lambda i: (0, 0))],
        out_specs=pl.BlockSpec((tm, N), lambda i: (i, 0)),
        out_shape=jax.ShapeDtypeStruct((T, N), BF16),
        compiler_params=_cparams("parallel"),
        name="qkv_proj",
    )(xt, w_b)


def _attn_a_kernel(q_ref, k_ref, v_ref, w_ref, o_ref, bias_ref):
    S = q_ref.shape[0]
    nblk = S // QBLK_A
    lead = LEFT_CHUNKS * CHUNK
    n_edge = min(lead // QBLK_A, nblk)

    qc = lax.broadcasted_iota(jnp.int32, (QBLK_A, BAND_A), 0) // CHUNK
    kc = lax.broadcasted_iota(jnp.int32, (QBLK_A, BAND_A), 1) // CHUNK
    allowed = (kc >= qc) & (kc <= qc + LEFT_CHUNKS)
    for h in range(HEADS_PER_BLOCK):
        wb = jnp.broadcast_to(w_ref[h], (QBLK_A, BIAS_W))
        toeplitz = pltpu.roll(wb, BIAS_W - (QBLK_A - 1), 1, stride=1, stride_axis=0)
        bias_ref[h] = jnp.where(allowed, toeplitz[:, :BAND_A], MASK_VALUE)

    def blocks(specs):
        work = []
        for p, kstart, nk, bias_off in specs:
            q = q_ref[pl.ds(p * QBLK_A, QBLK_A), :]
            k = k_ref[pl.ds(kstart, nk), :]
            v = v_ref[pl.ds(kstart, nk), :]
            for h in range(HEADS_PER_BLOCK):
                hs = slice(h * HEAD_DIM, (h + 1) * HEAD_DIM)
                work.append(dict(s=_dot_nt(q[:, hs], k[:, hs]), v=v[:, hs],
                                 bias=bias_ref[h, :, bias_off:bias_off + nk]))
        for w in work:
            s = w["s"] + w["bias"]
            e = jnp.exp(s - jnp.max(s, axis=-1, keepdims=True))
            w["l"] = jnp.sum(e, axis=-1, keepdims=True)
            w["e"] = e.astype(BF16)
        outs = [_dot(w["e"], w["v"]) / w["l"] for w in work]
        for i, spec in enumerate(specs):
            o_ref[pl.ds(spec[0] * QBLK_A, QBLK_A), :] = jnp.concatenate(
                outs[i * HEADS_PER_BLOCK:(i + 1) * HEADS_PER_BLOCK], axis=-1)

    blocks([(p, 0, (p + 1) * QBLK_A, lead - p * QBLK_A) for p in range(n_edge)])

    def full(p):
        return (p, pl.multiple_of(p * QBLK_A - lead, QBLK_A), BAND_A, 0)

    n_full = nblk - n_edge
    if n_full % 2:
        blocks([full(n_edge)])
    if n_full >= 2:
        def body(i, c):
            p = n_edge + n_full % 2 + 2 * i
            blocks([full(p), full(p + 1)])
            return c
        lax.fori_loop(0, n_full // 2, body, 0)


def _attn_a(qkv, bias_w, B, S, d_a):
    n_hb = d_a // LANES
    seg = d_a // LANES
    return pl.pallas_call(
        _attn_a_kernel,
        grid=(B, n_hb),
        in_specs=[pl.BlockSpec((None, S, LANES), lambda b, h: (b, 0, h)),
                  pl.BlockSpec((None, S, LANES), lambda b, h: (b, 0, seg + h)),
                  pl.BlockSpec((None, S, LANES), lambda b, h: (b, 0, 2 * seg + h)),
                  pl.BlockSpec((HEADS_PER_BLOCK, 1, BIAS_W), lambda b, h: (h, 0, 0))],
        out_specs=pl.BlockSpec((None, S, LANES), lambda b, h: (b, 0, h)),
        out_shape=jax.ShapeDtypeStruct((B, S, d_a), F32),
        scratch_shapes=[pltpu.VMEM((HEADS_PER_BLOCK, QBLK_A, BAND_A), F32)],
        compiler_params=_cparams("parallel", "parallel"),
        name="attn_chunked",
    )(qkv, qkv, qkv, bias_w)


def _rel_bias_by_offset(rel_bias):
    dist = jnp.clip(BAND_A - 1 - jnp.arange(BIAS_W), -MAX_REL, MAX_REL) + MAX_REL
    return rel_bias[:, dist].astype(F32)[:, None, :]


def _attn_b_kernel(q_ref, k_ref, v_ref, o_ref):
    qi = pl.program_id(2)
    t = TILE_B
    row = lax.broadcasted_iota(jnp.int32, (t, t), 0)
    col = lax.broadcasted_iota(jnp.int32, (t, t), 1)
    later = (row > col).astype(BF16)
    causal = col < row
    q = q_ref[...]
    heads = [slice(h * HEAD_DIM, (h + 1) * HEAD_DIM) for h in range(HEADS_PER_BLOCK)]
    qs = [q[:, hs] for hs in heads]

    def tiles(js, state, first_is_diag=False):
        accs, carries = list(state[0::2]), list(state[1::2])
        work = []
        for n, j in enumerate(js):
            ks = pl.multiple_of(j * t, t)
            kt = k_ref[pl.ds(ks, t), :]
            vt = v_ref[pl.ds(ks, t), :]
            for h, hs in enumerate(heads):
                work.append(dict(h=h, diag=first_is_diag and n == 0, v=vt[:, hs],
                                 z=_dot_nt(qs[h], kt[:, hs])))
        for w in work:
            z = w["z"]
            sp = jnp.maximum(z, 0.0) + jnp.log(1.0 + jnp.exp(-jnp.abs(z)))
            w["log_beta"] = z - sp
            if w["diag"]:
                sp = jnp.where(causal, sp, 0.0)
            hi = sp.astype(BF16)
            lo = (sp - hi.astype(F32)).astype(BF16)
            w["suffix"] = _dot(hi, later) + _dot(lo, later)
            w["rowsum"] = jnp.sum(sp, axis=-1, keepdims=True)
        for w in work:
            h = w["h"]
            if w["diag"]:
                a = jnp.where(causal, jnp.exp(w["log_beta"] - w["suffix"]), 0.0)
                accs[h] = _dot(a.astype(BF16), w["v"])
                carries[h] = w["rowsum"]
            else:
                a = jnp.exp(w["log_beta"] - w["suffix"] - carries[h])
                accs[h] = accs[h] + _dot(a.astype(BF16), w["v"])
                carries[h] = carries[h] + w["rowsum"]
        return tuple(x for pair in zip(accs, carries) for x in pair)

    empty = (None,) * (2 * HEADS_PER_BLOCK)
    state = lax.cond(qi % 2 == 1, lambda: tiles([qi, qi - 1], empty, True), lambda: tiles([qi], empty, True))
    first = qi - 1 - qi % 2

    def body(it, st):
        j = first - 2 * it
        return tiles([j, j - 1], st)

    state = lax.fori_loop(0, qi // 2, body, state)
    o_ref[...] = jnp.concatenate(state[0::2], axis=-1)


def _attn_b(qkv, B, S, d_a, d_b):
    n_hb = d_b // LANES
    base = 3 * d_a // LANES
    seg = d_b // LANES
    return pl.pallas_call(
        _attn_b_kernel,
        grid=(B, n_hb, S // TILE_B),
        in_specs=[pl.BlockSpec((None, TILE_B, LANES), lambda b, h, i: (b, i, base + h)),
                  pl.BlockSpec((None, S, LANES), lambda b, h, i: (b, 0, base + seg + h)),
                  pl.BlockSpec((None, S, LANES), lambda b, h, i: (b, 0, base + 2 * seg + h))],
        out_specs=pl.BlockSpec((None, TILE_B, LANES), lambda b, h, i: (b, i, h)),
        out_shape=jax.ShapeDtypeStruct((B, S, d_b), F32),
        compiler_params=_cparams("parallel", "parallel", "parallel"),
        name="attn_stickbreak",
    )(qkv, qkv, qkv)


def _layer_norm(r, g, b):
    mu = jnp.mean(r, axis=-1, keepdims=True)
    c = r - mu
    var = jnp.mean(c * c, axis=-1, keepdims=True)
    return c * lax.rsqrt(var + LN_EPS) * g + b


def _rms_norm(y, g):
    ms = jnp.mean(y * y, axis=-1, keepdims=True)
    return y * lax.rsqrt(ms + RMS_EPS) * g


def _silu(g):
    return g * jax.nn.sigmoid(g)


def _post_kernel(ya_ref, yb_ref, x_ref, ga_ref, gb_ref, wo_ref, g1_ref, b1_ref, rw_ref,
                 wsg_ref, wsu_ref, wsd_ref, x1_ref, base_ref, logit_ref, *, alpha):
    d_a = ya_ref.shape[1]
    na = _rms_norm(ya_ref[...], ga_ref[...]).astype(BF16)
    nb = _rms_norm(yb_ref[...], gb_ref[...]).astype(BF16)
    h = _dot(na, wo_ref[:d_a, :]) + _dot(nb, wo_ref[d_a:, :])
    x1 = _layer_norm(alpha * x_ref[...] + h, g1_ref[...], b1_ref[...])
    x1_ref[...] = _pack_bf16_halves(x1)
    xb = x1.astype(BF16)
    logit_ref[...] = _dot_nt(rw_ref[...], xb)
    hs = _silu(_dot(xb, wsg_ref[...])) * _dot(xb, wsu_ref[...])
    base_ref[...] = alpha * x1 + _dot(hs.astype(BF16), wsd_ref[...])


def _post(ya, yb, xt, gain_a, gain_b, wo_b, g1, b1, rw_b, wsg_b, wsu_b, wsd_b, alpha):
    T, D = xt.shape
    d_a, d_b = ya.shape[1], yb.shape[1]
    E = rw_b.shape[0]
    De = wsg_b.shape[1]
    tm = TOK_BLOCK
    row = lambda i: (i, 0)
    fix = lambda i: (0, 0)
    return pl.pallas_call(
        functools.partial(_post_kernel, alpha=alpha),
        grid=(T // tm,),
        in_specs=[pl.BlockSpec((tm, d_a), row), pl.BlockSpec((tm, d_b), row), pl.BlockSpec((tm, D), row),
                  pl.BlockSpec((1, d_a), fix), pl.BlockSpec((1, d_b), fix),
                  pl.BlockSpec((d_a + d_b, D), fix), pl.BlockSpec((1, D), fix), pl.BlockSpec((1, D), fix),
                  pl.BlockSpec((E, D), fix), pl.BlockSpec((D, De), fix), pl.BlockSpec((D, De), fix),
                  pl.BlockSpec((De, D), fix)],
        out_specs=[pl.BlockSpec((tm, D // 2), row), pl.BlockSpec((tm, D), row),
                   pl.BlockSpec((E, tm), lambda i: (0, i))],
        out_shape=[jax.ShapeDtypeStruct((T, D // 2), jnp.uint32), jax.ShapeDtypeStruct((T, D), F32),
                   jax.ShapeDtypeStruct((E, T), F32)],
        compiler_params=_cparams("parallel"),
        name="post_attn",
    )(ya, yb, xt, gain_a, gain_b, wo_b, g1, b1, rw_b, wsg_b, wsu_b, wsd_b)


def _route_kernel(logit_ref, rbias_ref, idx_ref, gate_ref, rank_ref, cnt_ref, carry_ref):
    E, tm = logit_ref.shape
    neg = -jnp.inf

    @pl.when(pl.program_id(0) == 0)
    def _():
        carry_ref[...] = jnp.zeros_like(carry_ref)

    scores = jax.nn.sigmoid(logit_ref[...])
    sel = scores + rbias_ref[...]
    eidx = lax.broadcasted_iota(jnp.int32, (E, tm), 0).astype(F32)

    def first_argmax(v, ids):
        m = jnp.max(v, axis=0, keepdims=True)
        return m, jnp.min(jnp.where(v == m, ids, float(E)), axis=0, keepdims=True)

    grp_scores = []
    ids = lax.broadcasted_iota(jnp.int32, (EXPERTS_PER_GROUP, tm), 0).astype(F32)
    for g in range(N_GROUPS):
        v = sel[g * EXPERTS_PER_GROUP:(g + 1) * EXPERTS_PER_GROUP, :]
        m1, i1 = first_argmax(v, ids)
        m2 = jnp.max(jnp.where(ids == i1, neg, v), axis=0, keepdims=True)
        grp_scores.append(m1 + m2)
    parts = []
    for g in range(N_GROUPS):
        beaten = jnp.zeros((1, tm), jnp.int32)
        for o in range(N_GROUPS):
            if o != g:
                s, t = grp_scores[o], grp_scores[g]
                beaten = beaten + ((s > t) | ((s == t) & (o < g))).astype(jnp.int32)
        rows = slice(g * EXPERTS_PER_GROUP, (g + 1) * EXPERTS_PER_GROUP)
        parts.append(jnp.where(beaten < TOPK_GROUPS, sel[rows, :], neg))
    selm = jnp.concatenate(parts, axis=0)

    hits, gates, ids_k = [], [], []
    chosen = jnp.zeros((E, tm), F32)
    gate_sum = jnp.zeros((1, tm), F32)
    for k in range(TOP_K):
        _, ik = first_argmax(selm, eidx)
        hit = eidx == ik
        gk = jnp.sum(jnp.where(hit, scores, 0.0), axis=0, keepdims=True)
        selm = jnp.where(hit, neg, selm)
        chosen = jnp.where(hit, 1.0, chosen)
        gate_sum = gate_sum + gk
        hits.append(hit)
        gates.append(gk)
        ids_k.append(ik)

    row = lax.broadcasted_iota(jnp.int32, (tm, tm), 0)
    col = lax.broadcasted_iota(jnp.int32, (tm, tm), 1)
    earlier = (row < col).astype(BF16)
    before = _dot(chosen.astype(BF16), earlier) + carry_ref[...]
    carry_ref[...] = carry_ref[...] + jnp.sum(chosen, axis=1, keepdims=True)
    cnt_ref[...] = carry_ref[...].astype(jnp.int32)

    ranks = [jnp.sum(jnp.where(hit, before, 0.0), axis=0, keepdims=True) for hit in hits]
    idx_ref[...] = jnp.concatenate(ids_k, axis=0).astype(jnp.int32)
    rank_ref[...] = jnp.concatenate(ranks, axis=0).astype(jnp.int32)
    gate_ref[...] = jnp.concatenate(gates, axis=0) / gate_sum * ROUTED_SCALE


def _route(logits_t, rbias):
    E, T = logits_t.shape
    tm = TOK_BLOCK
    col = lambda i: (0, i)
    fix = lambda i: (0, 0)
    return pl.pallas_call(
        _route_kernel,
        grid=(T // tm,),
        in_specs=[pl.BlockSpec((E, tm), col), pl.BlockSpec((E, 1), fix)],
        out_specs=[pl.BlockSpec((TOP_K, tm), col), pl.BlockSpec((TOP_K, tm), col),
                   pl.BlockSpec((TOP_K, tm), col), pl.BlockSpec((E, 1), fix)],
        out_shape=[jax.ShapeDtypeStruct((TOP_K, T), jnp.int32), jax.ShapeDtypeStruct((TOP_K, T), F32),
                   jax.ShapeDtypeStruct((TOP_K, T), jnp.int32), jax.ShapeDtypeStruct((E, 1), jnp.int32)],
        scratch_shapes=[pltpu.VMEM((E, 1), F32)],
        compiler_params=_cparams("arbitrary"),
        name="route",
    )(logits_t, rbias)


def _dest_kernel(idx_ref, rank_ref, start_ref, dest_ref):
    K, tm = idx_ref.shape
    E = start_ref.shape[0]
    eidx = lax.broadcasted_iota(jnp.int32, (E, tm), 0)
    idx = idx_ref[...]
    start = start_ref[...]
    rows = [jnp.sum(jnp.where(eidx == idx[k:k + 1, :], start, 0.0), axis=0, keepdims=True) for k in range(K)]
    dest_ref[...] = jnp.concatenate(rows, axis=0).astype(jnp.int32) + rank_ref[...]


def _dest(idx, rank, start_f):
    K, T = idx.shape
    E = start_f.shape[0]
    tm = 2 * TOK_BLOCK
    col = lambda i: (0, i)
    return pl.pallas_call(
        _dest_kernel,
        grid=(T // tm,),
        in_specs=[pl.BlockSpec((K, tm), col), pl.BlockSpec((K, tm), col), pl.BlockSpec((E, 1), lambda i: (0, 0))],
        out_specs=pl.BlockSpec((K, tm), col),
        out_shape=jax.ShapeDtypeStruct((K, T), jnp.int32),
        compiler_params=_cparams("parallel"),
        name="dest_rows",
    )(idx, rank, start_f)


def _expert_kernel(first_ref, nblk_ref, nused_ref, xs_hbm, wg_ref, wu_ref, wd_ref, ys_hbm,
                   xbuf, ybuf, wgu_s, wd_s, in_sem, out_sem):
    ring = xbuf.shape[0]
    e = pl.program_id(0)
    De = wg_ref.shape[2]
    n = nblk_ref[e]
    g0 = first_ref[e]
    n_used = nused_ref[0]

    def rows(g):
        return pl.ds(pl.multiple_of(g * ROW_BLOCK, ROW_BLOCK), ROW_BLOCK)

    def in_copy(g, slot):
        return pltpu.make_async_copy(xs_hbm.at[rows(g), :], xbuf.at[slot], in_sem.at[slot])

    def out_copy(g, slot):
        return pltpu.make_async_copy(ybuf.at[slot], ys_hbm.at[rows(g), :], out_sem.at[slot])

    @pl.when(e == 0)
    def _():
        for g in range(ring - 1):
            @pl.when(g < n_used)
            def _():
                in_copy(g, g).start()

    @pl.when(n > 0)
    def _():
        wgu_s[:, :De] = wg_ref[0].astype(BF16)
        wgu_s[:, De:] = wu_ref[0].astype(BF16)
        wd_s[...] = wd_ref[0].astype(BF16)

        def block(j, c):
            g = g0 + j
            slot = g & (ring - 1)
            in_copy(g, slot).wait()

            @pl.when(g + ring - 1 < n_used)
            def _():
                in_copy(g + ring - 1, (g + ring - 1) & (ring - 1)).start()

            @pl.when(g >= ring)
            def _():
                out_copy(g - ring, slot).wait()

            half = ROW_BLOCK // 2
            dh = xbuf.shape[2]
            gus = []
            for r in range(2):
                lo, hi = _unpack_bf16_halves(xbuf[slot, r * half:(r + 1) * half, :])
                gus.append(_dot(lo.astype(BF16), wgu_s[:dh, :]) + _dot(hi.astype(BF16), wgu_s[dh:, :]))
            hs = [(_silu(gu[:, :De]) * gu[:, De:]).astype(BF16) for gu in gus]
            for r in range(2):
                ybuf[slot, r * half:(r + 1) * half, :] = _pack_bf16_halves(_dot(hs[r], wd_s[...]))
            out_copy(g, slot).start()
            return c

        lax.fori_loop(0, n, block, 0)

    @pl.when(e == pl.num_programs(0) - 1)
    def _():
        for back in range(1, ring + 1):
            @pl.when(n_used >= back)
            def _():
                out_copy(n_used - back, (n_used - back) & (ring - 1)).wait()


def _experts(first_block, n_blocks, n_used, xs, w_gate, w_up, w_down):
    n_rows, Dh = xs.shape
    D = 2 * Dh
    E, _, De = w_gate.shape
    grid_spec = pltpu.PrefetchScalarGridSpec(
        num_scalar_prefetch=3,
        grid=(E,),
        in_specs=[pl.BlockSpec(memory_space=pl.ANY),
                  pl.BlockSpec((1, D, De), lambda e, *_: (e, 0, 0)),
                  pl.BlockSpec((1, D, De), lambda e, *_: (e, 0, 0)),
                  pl.BlockSpec((1, De, D), lambda e, *_: (e, 0, 0))],
        out_specs=pl.BlockSpec(memory_space=pl.ANY),
        scratch_shapes=[pltpu.VMEM((EXPERT_RING, ROW_BLOCK, Dh), jnp.uint32),
                        pltpu.VMEM((EXPERT_RING, ROW_BLOCK, Dh), jnp.uint32),
                        pltpu.VMEM((D, 2 * De), BF16), pltpu.VMEM((De, D), BF16),
                        pltpu.SemaphoreType.DMA((EXPERT_RING,)), pltpu.SemaphoreType.DMA((EXPERT_RING,))],
    )
    return pl.pallas_call(
        _expert_kernel,
        grid_spec=grid_spec,
        out_shape=jax.ShapeDtypeStruct((n_rows, Dh), jnp.uint32),
        compiler_params=_cparams("arbitrary"),
        name="expert_ffn",
    )(first_block, n_blocks, n_used, xs, w_gate, w_up, w_down)


SC_WINDOW = 128


def _sc_mesh():
    return plsc.VectorSubcoreMesh(core_axis_name="core", subcore_axis_name="subcore")


def _sc_workers():
    info = plsc.get_sparse_core_info()
    return info.num_cores, info.num_cores * info.num_subcores


def _sc_worker_id(num_cores):
    return lax.axis_index("subcore") * num_cores + lax.axis_index("core")


def _sc_scatter_rows(x, dest_flat, n_rows):
    T, D = x.shape
    K = dest_flat.shape[0] // T
    nc, nw = _sc_workers()
    per_w = T // nw
    assert T % nw == 0 and per_w % SC_WINDOW == 0

    @functools.partial(
        pl.kernel, out_type=jax.ShapeDtypeStruct((n_rows, D), x.dtype), mesh=_sc_mesh(),
        scratch_types=[pltpu.VMEM((SC_WINDOW,), jnp.int32), pltpu.VMEM((SC_WINDOW, D), x.dtype)],
        name="sc_dispatch")
    def run(x_hbm, i_hbm, o_hbm, idx_v, rows_v):
        first = _sc_worker_id(nc) * per_w

        @pl.loop(0, per_w // SC_WINDOW)
        def _(c):
            base = first + c * SC_WINDOW
            pltpu.sync_copy(x_hbm.at[pl.ds(base, SC_WINDOW)], rows_v)
            for k in range(K):
                pltpu.sync_copy(i_hbm.at[pl.ds(k * T + base, SC_WINDOW)], idx_v)
                pltpu.sync_copy(rows_v, o_hbm.at[idx_v])

    return run(x, dest_flat)


def _sc_gather_rows(ys, dest_flat):
    N = dest_flat.shape[0]
    D = ys.shape[1]
    nc, nw = _sc_workers()
    per_w = N // nw
    assert N % nw == 0 and per_w % SC_WINDOW == 0

    @functools.partial(
        pl.kernel, out_type=jax.ShapeDtypeStruct((N, D), ys.dtype), mesh=_sc_mesh(),
        scratch_types=[pltpu.VMEM((SC_WINDOW,), jnp.int32), pltpu.VMEM((SC_WINDOW, D), ys.dtype)],
        name="sc_combine_gather")
    def run(y_hbm, i_hbm, o_hbm, idx_v, rows_v):
        first = _sc_worker_id(nc) * per_w

        @pl.loop(0, per_w // SC_WINDOW)
        def _(c):
            base = first + c * SC_WINDOW
            pltpu.sync_copy(i_hbm.at[pl.ds(base, SC_WINDOW)], idx_v)
            pltpu.sync_copy(y_hbm.at[idx_v], rows_v)
            pltpu.sync_copy(rows_v, o_hbm.at[pl.ds(base, SC_WINDOW)])

    return run(ys, dest_flat)


def _final_kernel(z_ref, gate_ref, base_ref, g_ref, b_ref, o_ref):
    gate = gate_ref[...]
    acc_lo = acc_hi = None
    for k in range(TOP_K):
        lo, hi = _unpack_bf16_halves(z_ref[k])
        g = gate[:, k:k + 1]
        acc_lo = g * lo if acc_lo is None else acc_lo + g * lo
        acc_hi = g * hi if acc_hi is None else acc_hi + g * hi
    acc = base_ref[...] + jnp.concatenate([acc_lo, acc_hi], axis=-1)
    o_ref[...] = _layer_norm(acc, g_ref[...], b_ref[...])


def _final(z, gate, base, g2, b2):
    T, D = base.shape
    tb = TOK_BLOCK
    return pl.pallas_call(
        _final_kernel,
        grid=(T // tb,),
        in_specs=[pl.BlockSpec((TOP_K, tb, D // 2), lambda i: (0, i, 0)),
                  pl.BlockSpec((tb, TOP_K), lambda i: (i, 0)),
                  pl.BlockSpec((tb, D), lambda i: (i, 0)),
                  pl.BlockSpec((1, D), lambda i: (0, 0)),
                  pl.BlockSpec((1, D), lambda i: (0, 0))],
        out_specs=pl.BlockSpec((tb, D), lambda i: (i, 0)),
        out_shape=jax.ShapeDtypeStruct((T, D), F32),
        compiler_params=_cparams("parallel"),
        name="final_sum_ln",
    )(z, gate, base, g2, b2)


def _layer(x, w_in, rel_bias, gain_a, gain_b, w_out, ln1_g, ln1_b, router_w, router_bias,
           w_gate, w_up, w_down, ws_gate, ws_up, ws_down, ln2_g, ln2_b, alpha):
    B, S, D = x.shape
    T = B * S
    d_a = gain_a.shape[0]
    d_b = gain_b.shape[0]
    assert S % TILE_B == 0 and T % 512 == 0 and d_a % LANES == 0 and d_b % LANES == 0
    xt = x.reshape(T, D)

    scale = HEAD_DIM ** -0.5
    col = jnp.arange(w_in.shape[1])
    is_q = (col < d_a) | ((col >= 3 * d_a) & (col < 3 * d_a + d_b))
    w_in_b = (w_in * jnp.where(is_q, scale, 1.0)[None, :]).astype(BF16)

    qkv = _qkv_proj(xt, w_in_b).reshape(B, S, -1)
    ya = _attn_a(qkv, _rel_bias_by_offset(rel_bias), B, S, d_a).reshape(T, d_a)
    yb = _attn_b(qkv, B, S, d_a, d_b).reshape(T, d_b)

    x1p, base, logits_t = _post(
        ya, yb, xt, gain_a[None], gain_b[None], w_out.astype(BF16), ln1_g[None], ln1_b[None],
        router_w.T.astype(BF16), ws_gate.astype(BF16), ws_up.astype(BF16), ws_down.astype(BF16), alpha)

    idx, gate, rank, cnt = _route(logits_t, router_bias[:, None].astype(F32))
    cnt = cnt[:, 0]
    padded = (cnt + ROW_BLOCK - 1) // ROW_BLOCK * ROW_BLOCK
    ends = jnp.cumsum(padded)
    start = (ends - padded).astype(jnp.int32)
    n_rows = T * TOP_K + N_EXPERTS * ROW_BLOCK
    first_block = start // ROW_BLOCK
    n_blocks = (padded // ROW_BLOCK).astype(jnp.int32)
    n_used = (ends[-1:] // ROW_BLOCK).astype(jnp.int32)
    dest_flat = _dest(idx, rank, start.astype(F32)[:, None]).reshape(-1)

    xs = _sc_scatter_rows(x1p, dest_flat, n_rows)
    ys = _experts(first_block, n_blocks, n_used, xs, w_gate, w_up, w_down)
    z = _sc_gather_rows(ys, dest_flat).reshape(TOP_K, T, D // 2)
    out = _final(z, gate.T, base, ln2_g[None], ln2_b[None])
    return out.reshape(B, S, D)


def kernel(x, w_in, rel_bias, gain_a, gain_b, w_out, ln1_g, ln1_b, router_w, router_bias,
           w_gate, w_up, w_down, ws_gate, ws_up, ws_down, ln2_g, ln2_b):
    depth = w_in.shape[0]
    alpha = (2 * depth) ** 0.25
    for l in range(depth):
        x = _layer(x, w_in[l], rel_bias[l], gain_a[l], gain_b[l], w_out[l], ln1_g[l], ln1_b[l],
                   router_w[l], router_bias[l], w_gate[l], w_up[l], w_down[l],
                   ws_gate[l], ws_up[l], ws_down[l], ln2_g[l], ln2_b[l], alpha)
    return x
```

```python
import functools

import jax
import jax.numpy as jnp
from jax import lax
from jax.experimental import pallas as pl
from jax.experimental.pallas import tpu as pltpu
from jax.experimental.pallas import tpu_sc as plsc

CHUNK = 64
HEAD_DIM = 64
LEFT_CHUNKS = 8
MAX_REL = 128
N_EXPERTS = 256
TOP_K = 8
N_GROUPS = 8
TOPK_GROUPS = 4
EXPERTS_PER_GROUP = N_EXPERTS // N_GROUPS
ROUTED_SCALE = 2.5
LN_EPS = 1e-5
RMS_EPS = 1e-6

LANES = 128
SUBLANES = 8
HEADS_PER_BLOCK = LANES // HEAD_DIM
QBLK_A = 2 * CHUNK
BAND_A = (LEFT_CHUNKS + 2) * CHUNK
BIAS_W = BAND_A + QBLK_A
TILE_B = 256
WIDTH_B = 2 * HEAD_DIM
ROW_BLOCK = 256
EXPERT_RING = 4
TOK_BLOCK = 256
MASK_VALUE = -1e30
VMEM_LIMIT = 48 * 1024 * 1024

F32 = jnp.float32
BF16 = jnp.bfloat16


def _cparams(*sem, flags=None):
    return pltpu.CompilerParams(dimension_semantics=sem, vmem_limit_bytes=VMEM_LIMIT, flags=flags)


def _dot(a, b):
    return jnp.dot(a, b, preferred_element_type=F32)


def _dot_nt(a, b):
    return lax.dot_general(a, b, (((1,), (1,)), ((), ())), preferred_element_type=F32)


def _pack_bf16_halves(x):
    half = x.shape[1] // 2
    bits = lax.bitcast_convert_type(x.astype(BF16).astype(F32), jnp.uint32)
    return (bits[:, :half] >> 16) | (bits[:, half:] & jnp.uint32(0xFFFF0000))


def _unpack_bf16_halves(w):
    lo = lax.bitcast_convert_type(w << 16, F32)
    hi = lax.bitcast_convert_type(w & jnp.uint32(0xFFFF0000), F32)
    return lo, hi


def _qkv_kernel(x_ref, w_ref, o_ref, *, col_chunk):
    xb = x_ref[...].astype(BF16)
    for n in range(w_ref.shape[1] // col_chunk):
        cols = slice(n * col_chunk, (n + 1) * col_chunk)
        o_ref[:, cols] = _dot(xb, w_ref[:, cols]).astype(BF16)


def _qkv_proj(xt, w_b):
    T, D = xt.shape
    N = w_b.shape[1]
    tm = 512
    return pl.pallas_call(
        functools.partial(_qkv_kernel, col_chunk=512),
        grid=(T // tm,),
        in_specs=[pl.BlockSpec((tm, D), lambda i: (i, 0)),
                  pl.BlockSpec((D, N), lambda i: (0, 0))],
        out_specs=pl.BlockSpec((tm, N), lambda i: (i, 0)),
        out_shape=jax.ShapeDtypeStruct((T, N), BF16),
        compiler_params=_cparams("parallel"),
        name="qkv_proj",
    )(xt, w_b)


def _attn_a_kernel(q_ref, k_ref, v_ref, w_ref, o_ref, bias_ref):
    S = q_ref.shape[0]
    nblk = S // QBLK_A
    lead = LEFT_CHUNKS * CHUNK
    n_edge = min(lead // QBLK_A, nblk)

    qc = lax.broadcasted_iota(jnp.int32, (QBLK_A, BAND_A), 0) // CHUNK
    kc = lax.broadcasted_iota(jnp.int32, (QBLK_A, BAND_A), 1) // CHUNK
    allowed = (kc >= qc) & (kc <= qc + LEFT_CHUNKS)
    for h in range(HEADS_PER_BLOCK):
        wb = jnp.broadcast_to(w_ref[h], (QBLK_A, BIAS_W))
        toeplitz = pltpu.roll(wb, BIAS_W - (QBLK_A - 1), 1, stride=1, stride_axis=0)
        bias_ref[h] = jnp.where(allowed, toeplitz[:, :BAND_A], MASK_VALUE)

    def blocks(specs):
        work = []
        for p, kstart, nk, bias_off in specs:
            q = q_ref[pl.ds(p * QBLK_A, QBLK_A), :]
            k = k_ref[pl.ds(kstart, nk), :]
            v = v_ref[pl.ds(kstart, nk), :]
            for h in range(HEADS_PER_BLOCK):
                hs = slice(h * HEAD_DIM, (h + 1) * HEAD_DIM)
                work.append(dict(s=_dot_nt(q[:, hs], k[:, hs]), v=v[:, hs],
                                 bias=bias_ref[h, :, bias_off:bias_off + nk]))
        for w in work:
            s = w["s"] + w["bias"]
            e = jnp.exp(s - jnp.max(s, axis=-1, keepdims=True))
            w["l"] = jnp.sum(e, axis=-1, keepdims=True)
            w["e"] = e.astype(BF16)
        outs = [_dot(w["e"], w["v"]) / w["l"] for w in work]
        for i, spec in enumerate(specs):
            o_ref[pl.ds(spec[0] * QBLK_A, QBLK_A), :] = jnp.concatenate(
                outs[i * HEADS_PER_BLOCK:(i + 1) * HEADS_PER_BLOCK], axis=-1)

    blocks([(p, 0, (p + 1) * QBLK_A, lead - p * QBLK_A) for p in range(n_edge)])

    def full(p):
        return (p, pl.multiple_of(p * QBLK_A - lead, QBLK_A), BAND_A, 0)

    n_full = nblk - n_edge
    if n_full % 2:
        blocks([full(n_edge)])
    if n_full >= 2:
        def body(i, c):
            p = n_edge + n_full % 2 + 2 * i
            blocks([full(p), full(p + 1)])
            return c
        lax.fori_loop(0, n_full // 2, body, 0)


def _attn_a(qkv, bias_w, B, S, d_a):
    n_hb = d_a // LANES
    seg = d_a // LANES
    return pl.pallas_call(
        _attn_a_kernel,
        grid=(B, n_hb),
        in_specs=[pl.BlockSpec((None, S, LANES), lambda b, h: (b, 0, h)),
                  pl.BlockSpec((None, S, LANES), lambda b, h: (b, 0, seg + h)),
                  pl.BlockSpec((None, S, LANES), lambda b, h: (b, 0, 2 * seg + h)),
                  pl.BlockSpec((HEADS_PER_BLOCK, 1, BIAS_W), lambda b, h: (h, 0, 0))],
        out_specs=pl.BlockSpec((None, S, LANES), lambda b, h: (b, 0, h)),
        out_shape=jax.ShapeDtypeStruct((B, S, d_a), F32),
        scratch_shapes=[pltpu.VMEM((HEADS_PER_BLOCK, QBLK_A, BAND_A), F32)],
        compiler_params=_cparams("parallel", "parallel"),
        name="attn_chunked",
    )(qkv, qkv, qkv, bias_w)


def _rel_bias_by_offset(rel_bias):
    dist = jnp.clip(BAND_A - 1 - jnp.arange(BIAS_W), -MAX_REL, MAX_REL) + MAX_REL
    return rel_bias[:, dist].astype(F32)[:, None, :]


def _attn_b_kernel(q_ref, k_ref, v_ref, o_ref):
    qi = pl.program_id(2)
    t = TILE_B
    row = lax.broadcasted_iota(jnp.int32, (t, t), 0)
    col = lax.broadcasted_iota(jnp.int32, (t, t), 1)
    later = (row > col).astype(BF16)
    later2 = jnp.concatenate([later, later], axis=0)
    causal = col < row
    q = q_ref[...]
    heads = [slice(h * HEAD_DIM, (h + 1) * HEAD_DIM) for h in range(q_ref.shape[1] // HEAD_DIM)]
    qs = [q[:, hs] for hs in heads]

    def scores(js):
        zs = []
        for j in js:
            kt = k_ref[pl.ds(pl.multiple_of(j * t, t), t), :]
            zs += [_dot_nt(qs[h], kt[:, hs]) for h, hs in enumerate(heads)]
        return zs

    def tiles(js, zs, state, first_is_diag=False):
        accs, carries = list(state[0::2]), list(state[1::2])
        work = []
        for n, j in enumerate(js):
            vt = v_ref[pl.ds(pl.multiple_of(j * t, t), t), :]
            for h, hs in enumerate(heads):
                work.append(dict(h=h, diag=first_is_diag and n == 0, v=vt[:, hs], z=zs[n * len(heads) + h]))
        for w in work:
            z = w["z"]
            sp = jnp.maximum(z, 0.0) + jnp.log(1.0 + jnp.exp(-jnp.abs(z)))
            w["log_beta"] = z - sp
            if w["diag"]:
                sp = jnp.where(causal, sp, 0.0)
            hi = sp.astype(BF16)
            lo = (sp - hi.astype(F32)).astype(BF16)
            w["suffix"] = _dot(jnp.concatenate([hi, lo], axis=-1), later2)
            w["rowsum"] = jnp.sum(sp, axis=-1, keepdims=True)
        for w in work:
            h = w["h"]
            if w["diag"]:
                a = jnp.where(causal, jnp.exp(w["log_beta"] - w["suffix"]), 0.0)
                accs[h] = _dot(a.astype(BF16), w["v"])
                carries[h] = w["rowsum"]
            else:
                a = jnp.exp(w["log_beta"] - w["suffix"] - carries[h])
                accs[h] = accs[h] + _dot(a.astype(BF16), w["v"])
                carries[h] = carries[h] + w["rowsum"]
        return tuple(x for pair in zip(accs, carries) for x in pair)

    empty = (None,) * (2 * len(heads))
    state = lax.cond(qi % 2 == 1,
                     lambda: tiles([qi, qi - 1], scores([qi, qi - 1]), empty, True),
                     lambda: tiles([qi], scores([qi]), empty, True))
    first = qi - 1 - qi % 2

    def body(it, st):
        j = first - 2 * it
        return tiles([j, j - 1], scores([j, j - 1]), st)

    state = lax.fori_loop(0, qi // 2, body, state)
    o_ref[...] = jnp.concatenate(state[0::2], axis=-1)


def _attn_b(qkv, B, S, d_a, d_b):
    w = WIDTH_B
    n_hb = d_b // w
    base = 3 * d_a // w
    seg = d_b // w
    return pl.pallas_call(
        _attn_b_kernel,
        grid=(B, n_hb, S // TILE_B),
        in_specs=[pl.BlockSpec((None, TILE_B, w), lambda b, h, i: (b, i, base + h)),
                  pl.BlockSpec((None, S, w), lambda b, h, i: (b, 0, base + seg + h)),
                  pl.BlockSpec((None, S, w), lambda b, h, i: (b, 0, base + 2 * seg + h))],
        out_specs=pl.BlockSpec((None, TILE_B, w), lambda b, h, i: (b, i, h)),
        out_shape=jax.ShapeDtypeStruct((B, S, d_b), F32),
        compiler_params=_cparams("parallel", "parallel", "parallel"),
        name="attn_stickbreak",
    )(qkv, qkv, qkv)


def _layer_norm(r, g, b):
    mu = jnp.mean(r, axis=-1, keepdims=True)
    c = r - mu
    var = jnp.mean(c * c, axis=-1, keepdims=True)
    return c * lax.rsqrt(var + LN_EPS) * g + b


def _rms_norm(y, g):
    ms = jnp.mean(y * y, axis=-1, keepdims=True)
    return y * lax.rsqrt(ms + RMS_EPS) * g


def _silu(g):
    return g * jax.nn.sigmoid(g)


def _post_kernel(ya_ref, yb_ref, x_ref, ga_ref, gb_ref, wo_ref, g1_ref, b1_ref, rw_ref,
                 wsg_ref, wsu_ref, wsd_ref, x1_ref, base_ref, logit_ref, *, alpha):
    d_a = ya_ref.shape[1]
    na = _rms_norm(ya_ref[...], ga_ref[...]).astype(BF16)
    nb = _rms_norm(yb_ref[...], gb_ref[...]).astype(BF16)
    h = _dot(na, wo_ref[:d_a, :]) + _dot(nb, wo_ref[d_a:, :])
    x1 = _layer_norm(alpha * x_ref[...] + h, g1_ref[...], b1_ref[...])
    x1_ref[...] = _pack_bf16_halves(x1)
    xb = x1.astype(BF16)
    logit_ref[...] = _dot_nt(rw_ref[...], xb)
    hs = _silu(_dot(xb, wsg_ref[...])) * _dot(xb, wsu_ref[...])
    base_ref[...] = alpha * x1 + _dot(hs.astype(BF16), wsd_ref[...])


def _post(ya, yb, xt, gain_a, gain_b, wo_b, g1, b1, rw_b, wsg_b, wsu_b, wsd_b, alpha):
    T, D = xt.shape
    d_a, d_b = ya.shape[1], yb.shape[1]
    E = rw_b.shape[0]
    De = wsg_b.shape[1]
    tm = TOK_BLOCK
    row = lambda i: (i, 0)
    fix = lambda i: (0, 0)
    return pl.pallas_call(
        functools.partial(_post_kernel, alpha=alpha),
        grid=(T // tm,),
        in_specs=[pl.BlockSpec((tm, d_a), row), pl.BlockSpec((tm, d_b), row), pl.BlockSpec((tm, D), row),
                  pl.BlockSpec((1, d_a), fix), pl.BlockSpec((1, d_b), fix),
                  pl.BlockSpec((d_a + d_b, D), fix), pl.BlockSpec((1, D), fix), pl.BlockSpec((1, D), fix),
                  pl.BlockSpec((E, D), fix), pl.BlockSpec((D, De), fix), pl.BlockSpec((D, De), fix),
                  pl.BlockSpec((De, D), fix)],
        out_specs=[pl.BlockSpec((tm, D // 2), row), pl.BlockSpec((tm, D), row),
                   pl.BlockSpec((E, tm), lambda i: (0, i))],
        out_shape=[jax.ShapeDtypeStruct((T, D // 2), jnp.uint32), jax.ShapeDtypeStruct((T, D), F32),
                   jax.ShapeDtypeStruct((E, T), F32)],
        compiler_params=_cparams("parallel"),
        name="post_attn",
    )(ya, yb, xt, gain_a, gain_b, wo_b, g1, b1, rw_b, wsg_b, wsu_b, wsd_b)


def _route_kernel(logit_ref, rbias_ref, idx_ref, gate_ref, rank_ref, cnt_ref, carry_ref):
    E, tm = logit_ref.shape
    neg = -jnp.inf

    @pl.when(pl.program_id(0) == 0)
    def _():
        carry_ref[...] = jnp.zeros_like(carry_ref)

    scores = jax.nn.sigmoid(logit_ref[...])
    sel = scores + rbias_ref[...]
    eidx = lax.broadcasted_iota(jnp.int32, (E, tm), 0).astype(F32)

    def first_argmax(v, ids):
        m = jnp.max(v, axis=0, keepdims=True)
        return m, jnp.min(jnp.where(v == m, ids, float(E)), axis=0, keepdims=True)

    grp_scores = []
    ids = lax.broadcasted_iota(jnp.int32, (EXPERTS_PER_GROUP, tm), 0).astype(F32)
    for g in range(N_GROUPS):
        v = sel[g * EXPERTS_PER_GROUP:(g + 1) * EXPERTS_PER_GROUP, :]
        m1, i1 = first_argmax(v, ids)
        m2 = jnp.max(jnp.where(ids == i1, neg, v), axis=0, keepdims=True)
        grp_scores.append(m1 + m2)
    parts = []
    for g in range(N_GROUPS):
        beaten = jnp.zeros((1, tm), jnp.int32)
        for o in range(N_GROUPS):
            if o != g:
                s, t = grp_scores[o], grp_scores[g]
                beaten = beaten + ((s > t) | ((s == t) & (o < g))).astype(jnp.int32)
        rows = slice(g * EXPERTS_PER_GROUP, (g + 1) * EXPERTS_PER_GROUP)
        parts.append(jnp.where(beaten < TOPK_GROUPS, sel[rows, :], neg))
    selm = jnp.concatenate(parts, axis=0)

    hits, gates, ids_k = [], [], []
    chosen = jnp.zeros((E, tm), F32)
    gate_sum = jnp.zeros((1, tm), F32)
    for k in range(TOP_K):
        _, ik = first_argmax(selm, eidx)
        hit = eidx == ik
        gk = jnp.sum(jnp.where(hit, scores, 0.0), axis=0, keepdims=True)
        selm = jnp.where(hit, neg, selm)
        chosen = jnp.where(hit, 1.0, chosen)
        gate_sum = gate_sum + gk
        hits.append(hit)
        gates.append(gk)
        ids_k.append(ik)

    row = lax.broadcasted_iota(jnp.int32, (tm, tm), 0)
    col = lax.broadcasted_iota(jnp.int32, (tm, tm), 1)
    earlier = (row < col).astype(BF16)
    before = _dot(chosen.astype(BF16), earlier) + carry_ref[...]
    carry_ref[...] = carry_ref[...] + jnp.sum(chosen, axis=1, keepdims=True)
    cnt_ref[...] = carry_ref[...].astype(jnp.int32)

    ranks = [jnp.sum(jnp.where(hit, before, 0.0), axis=0, keepdims=True) for hit in hits]
    idx_ref[...] = jnp.concatenate(ids_k, axis=0).astype(jnp.int32)
    rank_ref[...] = jnp.concatenate(ranks, axis=0).astype(jnp.int32)
    gate_ref[...] = jnp.concatenate(gates, axis=0) / gate_sum * ROUTED_SCALE


def _route(logits_t, rbias):
    E, T = logits_t.shape
    tm = TOK_BLOCK
    col = lambda i: (0, i)
    fix = lambda i: (0, 0)
    return pl.pallas_call(
        _route_kernel,
        grid=(T // tm,),
        in_specs=[pl.BlockSpec((E, tm), col), pl.BlockSpec((E, 1), fix)],
        out_specs=[pl.BlockSpec((TOP_K, tm), col), pl.BlockSpec((TOP_K, tm), col),
                   pl.BlockSpec((TOP_K, tm), col), pl.BlockSpec((E, 1), fix)],
        out_shape=[jax.ShapeDtypeStruct((TOP_K, T), jnp.int32), jax.ShapeDtypeStruct((TOP_K, T), F32),
                   jax.ShapeDtypeStruct((TOP_K, T), jnp.int32), jax.ShapeDtypeStruct((E, 1), jnp.int32)],
        scratch_shapes=[pltpu.VMEM((E, 1), F32)],
        compiler_params=_cparams("arbitrary"),
        name="route",
    )(logits_t, rbias)


def _dest_kernel(idx_ref, rank_ref, start_ref, dest_ref):
    K, tm = idx_ref.shape
    E = start_ref.shape[0]
    eidx = lax.broadcasted_iota(jnp.int32, (E, tm), 0)
    idx = idx_ref[...]
    start = start_ref[...]
    rows = [jnp.sum(jnp.where(eidx == idx[k:k + 1, :], start, 0.0), axis=0, keepdims=True) for k in range(K)]
    dest_ref[...] = jnp.concatenate(rows, axis=0).astype(jnp.int32) + rank_ref[...]


def _dest(idx, rank, start_f):
    K, T = idx.shape
    E = start_f.shape[0]
    tm = 2 * TOK_BLOCK
    col = lambda i: (0, i)
    return pl.pallas_call(
        _dest_kernel,
        grid=(T // tm,),
        in_specs=[pl.BlockSpec((K, tm), col), pl.BlockSpec((K, tm), col), pl.BlockSpec((E, 1), lambda i: (0, 0))],
        out_specs=pl.BlockSpec((K, tm), col),
        out_shape=jax.ShapeDtypeStruct((K, T), jnp.int32),
        compiler_params=_cparams("parallel"),
        name="dest_rows",
    )(idx, rank, start_f)


def _expert_kernel(first_ref, nblk_ref, nused_ref, xs_hbm, wg_ref, wu_ref, wd_ref, ys_hbm,
                   xbuf, ybuf, wgu_s, wd_s, in_sem, out_sem):
    ring = xbuf.shape[0]
    e = pl.program_id(0)
    De = wg_ref.shape[2]
    n = nblk_ref[e]
    g0 = first_ref[e]
    n_used = nused_ref[0]

    def rows(g):
        return pl.ds(pl.multiple_of(g * ROW_BLOCK, ROW_BLOCK), ROW_BLOCK)

    def in_copy(g, slot):
        return pltpu.make_async_copy(xs_hbm.at[rows(g), :], xbuf.at[slot], in_sem.at[slot])

    def out_copy(g, slot):
        return pltpu.make_async_copy(ybuf.at[slot], ys_hbm.at[rows(g), :], out_sem.at[slot])

    @pl.when(e == 0)
    def _():
        for g in range(ring - 1):
            @pl.when(g < n_used)
            def _():
                in_copy(g, g).start()

    @pl.when(n > 0)
    def _():
        wgu_s[:, :De] = wg_ref[0].astype(BF16)
        wgu_s[:, De:] = wu_ref[0].astype(BF16)
        wd_s[...] = wd_ref[0].astype(BF16)

        def block(j, c):
            g = g0 + j
            slot = g & (ring - 1)
            in_copy(g, slot).wait()

            @pl.when(g + ring - 1 < n_used)
            def _():
                in_copy(g + ring - 1, (g + ring - 1) & (ring - 1)).start()

            @pl.when(g >= ring)
            def _():
                out_copy(g - ring, slot).wait()

            half = ROW_BLOCK // 2
            dh = xbuf.shape[2]
            gus = []
            for r in range(2):
                lo, hi = _unpack_bf16_halves(xbuf[slot, r * half:(r + 1) * half, :])
                gus.append(_dot(lo.astype(BF16), wgu_s[:dh, :]) + _dot(hi.astype(BF16), wgu_s[dh:, :]))
            hs = [(_silu(gu[:, :De]) * gu[:, De:]).astype(BF16) for gu in gus]
            for r in range(2):
                ybuf[slot, r * half:(r + 1) * half, :] = _pack_bf16_halves(_dot(hs[r], wd_s[...]))
            out_copy(g, slot).start()
            return c

        lax.fori_loop(0, n, block, 0)

    @pl.when(e == pl.num_programs(0) - 1)
    def _():
        for back in range(1, ring + 1):
            @pl.when(n_used >= back)
            def _():
                out_copy(n_used - back, (n_used - back) & (ring - 1)).wait()


def _experts(first_block, n_blocks, n_used, xs, w_gate, w_up, w_down):
    n_rows, Dh = xs.shape
    D = 2 * Dh
    E, _, De = w_gate.shape
    grid_spec = pltpu.PrefetchScalarGridSpec(
        num_scalar_prefetch=3,
        grid=(E,),
        in_specs=[pl.BlockSpec(memory_space=pl.ANY),
                  pl.BlockSpec((1, D, De), lambda e, *_: (e, 0, 0)),
                  pl.BlockSpec((1, D, De), lambda e, *_: (e, 0, 0)),
                  pl.BlockSpec((1, De, D), lambda e, *_: (e, 0, 0))],
        out_specs=pl.BlockSpec(memory_space=pl.ANY),
        scratch_shapes=[pltpu.VMEM((EXPERT_RING, ROW_BLOCK, Dh), jnp.uint32),
                        pltpu.VMEM((EXPERT_RING, ROW_BLOCK, Dh), jnp.uint32),
                        pltpu.VMEM((D, 2 * De), BF16), pltpu.VMEM((De, D), BF16),
                        pltpu.SemaphoreType.DMA((EXPERT_RING,)), pltpu.SemaphoreType.DMA((EXPERT_RING,))],
    )
    return pl.pallas_call(
        _expert_kernel,
        grid_spec=grid_spec,
        out_shape=jax.ShapeDtypeStruct((n_rows, Dh), jnp.uint32),
        compiler_params=_cparams("arbitrary"),
        name="expert_ffn",
    )(first_block, n_blocks, n_used, xs, w_gate, w_up, w_down)


SC_WINDOW = 128


def _sc_mesh():
    return plsc.VectorSubcoreMesh(core_axis_name="core", subcore_axis_name="subcore")


def _sc_workers():
    info = plsc.get_sparse_core_info()
    return info.num_cores, info.num_cores * info.num_subcores


def _sc_worker_id(num_cores):
    return lax.axis_index("subcore") * num_cores + lax.axis_index("core")


def _sc_scatter_rows(x, dest_flat, n_rows):
    T, D = x.shape
    K = dest_flat.shape[0] // T
    nc, nw = _sc_workers()
    per_w = T // nw
    assert T % nw == 0 and per_w % SC_WINDOW == 0

    @functools.partial(
        pl.kernel, out_type=jax.ShapeDtypeStruct((n_rows, D), x.dtype), mesh=_sc_mesh(),
        scratch_types=[pltpu.VMEM((SC_WINDOW,), jnp.int32), pltpu.VMEM((SC_WINDOW, D), x.dtype)],
        name="sc_dispatch")
    def run(x_hbm, i_hbm, o_hbm, idx_v, rows_v):
        first = _sc_worker_id(nc) * per_w

        @pl.loop(0, per_w // SC_WINDOW)
        def _(c):
            base = first + c * SC_WINDOW
            pltpu.sync_copy(x_hbm.at[pl.ds(base, SC_WINDOW)], rows_v)
            for k in range(K):
                pltpu.sync_copy(i_hbm.at[pl.ds(k * T + base, SC_WINDOW)], idx_v)
                pltpu.sync_copy(rows_v, o_hbm.at[idx_v])

    return run(x, dest_flat)


def _sc_gather_rows(ys, dest_flat):
    N = dest_flat.shape[0]
    D = ys.shape[1]
    nc, nw = _sc_workers()
    per_w = N // nw
    assert N % nw == 0 and per_w % SC_WINDOW == 0

    @functools.partial(
        pl.kernel, out_type=jax.ShapeDtypeStruct((N, D), ys.dtype), mesh=_sc_mesh(),
        scratch_types=[pltpu.VMEM((SC_WINDOW,), jnp.int32), pltpu.VMEM((SC_WINDOW, D), ys.dtype)],
        name="sc_combine_gather")
    def run(y_hbm, i_hbm, o_hbm, idx_v, rows_v):
        first = _sc_worker_id(nc) * per_w

        @pl.loop(0, per_w // SC_WINDOW)
        def _(c):
            base = first + c * SC_WINDOW
            pltpu.sync_copy(i_hbm.at[pl.ds(base, SC_WINDOW)], idx_v)
            pltpu.sync_copy(y_hbm.at[idx_v], rows_v)
            pltpu.sync_copy(rows_v, o_hbm.at[pl.ds(base, SC_WINDOW)])

    return run(ys, dest_flat)


def _final_kernel(z_ref, gate_ref, base_ref, g_ref, b_ref, o_ref):
    gate = gate_ref[...]
    acc_lo = acc_hi = None
    for k in range(TOP_K):
        lo, hi = _unpack_bf16_halves(z_ref[k])
        g = gate[:, k:k + 1]
        acc_lo = g * lo if acc_lo is None else acc_lo + g * lo
        acc_hi = g * hi if acc_hi is None else acc_hi + g * hi
    acc = base_ref[...] + jnp.concatenate([acc_lo, acc_hi], axis=-1)
    o_ref[...] = _layer_norm(acc, g_ref[...], b_ref[...])


def _final(z, gate, base, g2, b2):
    T, D = base.shape
    tb = TOK_BLOCK
    return pl.pallas_call(
        _final_kernel,
        grid=(T // tb,),
        in_specs=[pl.BlockSpec((TOP_K, tb, D // 2), lambda i: (0, i, 0)),
                  pl.BlockSpec((tb, TOP_K), lambda i: (i, 0)),
                  pl.BlockSpec((tb, D), lambda i: (i, 0)),
                  pl.BlockSpec((1, D), lambda i: (0, 0)),
                  pl.BlockSpec((1, D), lambda i: (0, 0))],
        out_specs=pl.BlockSpec((tb, D), lambda i: (i, 0)),
        out_shape=jax.ShapeDtypeStruct((T, D), F32),
        compiler_params=_cparams("parallel"),
        name="final_sum_ln",
    )(z, gate, base, g2, b2)


def _layer(x, w_in, rel_bias, gain_a, gain_b, w_out, ln1_g, ln1_b, router_w, router_bias,
           w_gate, w_up, w_down, ws_gate, ws_up, ws_down, ln2_g, ln2_b, alpha):
    B, S, D = x.shape
    T = B * S
    d_a = gain_a.shape[0]
    d_b = gain_b.shape[0]
    assert S % TILE_B == 0 and T % 512 == 0 and d_a % LANES == 0 and d_b % WIDTH_B == 0 and (3 * d_a) % WIDTH_B == 0
    xt = x.reshape(T, D)

    scale = HEAD_DIM ** -0.5
    col = jnp.arange(w_in.shape[1])
    is_q = (col < d_a) | ((col >= 3 * d_a) & (col < 3 * d_a + d_b))
    w_in_b = (w_in * jnp.where(is_q, scale, 1.0)[None, :]).astype(BF16)

    qkv = _qkv_proj(xt, w_in_b).reshape(B, S, -1)
    ya = _attn_a(qkv, _rel_bias_by_offset(rel_bias), B, S, d_a).reshape(T, d_a)
    yb = _attn_b(qkv, B, S, d_a, d_b).reshape(T, d_b)

    x1p, base, logits_t = _post(
        ya, yb, xt, gain_a[None], gain_b[None], w_out.astype(BF16), ln1_g[None], ln1_b[None],
        router_w.T.astype(BF16), ws_gate.astype(BF16), ws_up.astype(BF16), ws_down.astype(BF16), alpha)

    idx, gate, rank, cnt = _route(logits_t, router_bias[:, None].astype(F32))
    cnt = cnt[:, 0]
    padded = (cnt + ROW_BLOCK - 1) // ROW_BLOCK * ROW_BLOCK
    ends = jnp.cumsum(padded)
    start = (ends - padded).astype(jnp.int32)
    n_rows = T * TOP_K + N_EXPERTS * ROW_BLOCK
    first_block = start // ROW_BLOCK
    n_blocks = (padded // ROW_BLOCK).astype(jnp.int32)
    n_used = (ends[-1:] // ROW_BLOCK).astype(jnp.int32)
    dest_flat = _dest(idx, rank, start.astype(F32)[:, None]).reshape(-1)

    xs = _sc_scatter_rows(x1p, dest_flat, n_rows)
    ys = _experts(first_block, n_blocks, n_used, xs, w_gate, w_up, w_down)
    z = _sc_gather_rows(ys, dest_flat).reshape(TOP_K, T, D // 2)
    out = _final(z, gate.T, base, ln2_g[None], ln2_b[None])
    return out.reshape(B, S, D)


def kernel(x, w_in, rel_bias, gain_a, gain_b, w_out, ln1_g, ln1_b, router_w, router_bias,
           w_gate, w_up, w_down, ws_gate, ws_up, ws_down, ln2_g, ln2_b):
    depth = w_in.shape[0]
    alpha = (2 * depth) ** 0.25
    for l in range(depth):
        x = _layer(x, w_in[l], rel_bias[l], gain_a[l], gain_b[l], w_out[l], ln1_g[l], ln1_b[l],
                   router_w[l], router_bias[l], w_gate[l], w_up[l], w_down[l],
                   ws_gate[l], ws_up[l], ws_down[l], ln2_g[l], ln2_b[l], alpha)
    return x
```

```python
import functools

import jax
import jax.numpy as jnp
from jax import lax
from jax.experimental import pallas as pl
from jax.experimental.pallas import tpu as pltpu
from jax.experimental.pallas import tpu_sc as plsc

CHUNK = 64
HEAD_DIM = 64
LEFT_CHUNKS = 8
MAX_REL = 128
N_EXPERTS = 256
TOP_K = 8
N_GROUPS = 8
TOPK_GROUPS = 4
EXPERTS_PER_GROUP = N_EXPERTS // N_GROUPS
ROUTED_SCALE = 2.5
LN_EPS = 1e-5
RMS_EPS = 1e-6

LANES = 128
SUBLANES = 8
HEADS_PER_BLOCK = LANES // HEAD_DIM
QBLK_A = 2 * CHUNK
BAND_A = (LEFT_CHUNKS + 2) * CHUNK
BIAS_W = BAND_A + QBLK_A
TILE_B = 256
WIDTH_B = 2 * HEAD_DIM
ROW_BLOCK = 256
EXPERT_RING = 4
TOK_BLOCK = 256
COMBINE_GROUPS = 4
MASK_VALUE = -1e30
VMEM_LIMIT = 48 * 1024 * 1024

F32 = jnp.float32
BF16 = jnp.bfloat16


def _cparams(*sem, flags=None):
    return pltpu.CompilerParams(dimension_semantics=sem, vmem_limit_bytes=VMEM_LIMIT, flags=flags)


def _dot(a, b):
    return jnp.dot(a, b, preferred_element_type=F32)


def _dot_nt(a, b):
    return lax.dot_general(a, b, (((1,), (1,)), ((), ())), preferred_element_type=F32)


def _pack_bf16_halves(x):
    half = x.shape[1] // 2
    bits = lax.bitcast_convert_type(x.astype(BF16).astype(F32), jnp.uint32)
    return (bits[:, :half] >> 16) | (bits[:, half:] & jnp.uint32(0xFFFF0000))


def _unpack_bf16_halves(w):
    lo = lax.bitcast_convert_type(w << 16, F32)
    hi = lax.bitcast_convert_type(w & jnp.uint32(0xFFFF0000), F32)
    return lo, hi


def _qkv_kernel(x_ref, w_ref, o_ref, *, col_chunk):
    xb = x_ref[...].astype(BF16)
    for n in range(w_ref.shape[1] // col_chunk):
        cols = slice(n * col_chunk, (n + 1) * col_chunk)
        o_ref[:, cols] = _dot(xb, w_ref[:, cols]).astype(BF16)


def _qkv_proj(xt, w_b):
    T, D = xt.shape
    N = w_b.shape[1]
    tm = 512
    return pl.pallas_call(
        functools.partial(_qkv_kernel, col_chunk=512),
        grid=(T // tm,),
        in_specs=[pl.BlockSpec((tm, D), lambda i: (i, 0)),
                  pl.BlockSpec((D, N), lambda i: (0, 0))],
        out_specs=pl.BlockSpec((tm, N), lambda i: (i, 0)),
        out_shape=jax.ShapeDtypeStruct((T, N), BF16),
        compiler_params=_cparams("parallel"),
        name="qkv_proj",
    )(xt, w_b)


def _attn_a_kernel(q_ref, k_ref, v_ref, w_ref, o_ref, bias_ref):
    S = q_ref.shape[0]
    nblk = S // QBLK_A
    lead = LEFT_CHUNKS * CHUNK
    n_edge = min(lead // QBLK_A, nblk)

    qc = lax.broadcasted_iota(jnp.int32, (QBLK_A, BAND_A), 0) // CHUNK
    kc = lax.broadcasted_iota(jnp.int32, (QBLK_A, BAND_A), 1) // CHUNK
    allowed = (kc >= qc) & (kc <= qc + LEFT_CHUNKS)
    for h in range(HEADS_PER_BLOCK):
        wb = jnp.broadcast_to(w_ref[h], (QBLK_A, BIAS_W))
        toeplitz = pltpu.roll(wb, BIAS_W - (QBLK_A - 1), 1, stride=1, stride_axis=0)
        bias_ref[h] = jnp.where(allowed, toeplitz[:, :BAND_A], MASK_VALUE)

    def blocks(specs):
        work = []
        for p, kstart, nk, bias_off in specs:
            q = q_ref[pl.ds(p * QBLK_A, QBLK_A), :]
            k = k_ref[pl.ds(kstart, nk), :]
            v = v_ref[pl.ds(kstart, nk), :]
            for h in range(HEADS_PER_BLOCK):
                hs = slice(h * HEAD_DIM, (h + 1) * HEAD_DIM)
                work.append(dict(s=_dot_nt(q[:, hs], k[:, hs]), v=v[:, hs],
                                 bias=bias_ref[h, :, bias_off:bias_off + nk]))
        for w in work:
            s = w["s"] + w["bias"]
            e = jnp.exp(s - jnp.max(s, axis=-1, keepdims=True))
            w["l"] = jnp.sum(e, axis=-1, keepdims=True)
            w["e"] = e.astype(BF16)
        outs = [_dot(w["e"], w["v"]) / w["l"] for w in work]
        for i, spec in enumerate(specs):
            o_ref[pl.ds(spec[0] * QBLK_A, QBLK_A), :] = jnp.concatenate(
                outs[i * HEADS_PER_BLOCK:(i + 1) * HEADS_PER_BLOCK], axis=-1)

    blocks([(p, 0, (p + 1) * QBLK_A, lead - p * QBLK_A) for p in range(n_edge)])

    def full(p):
        return (p, pl.multiple_of(p * QBLK_A - lead, QBLK_A), BAND_A, 0)

    n_full = nblk - n_edge
    if n_full % 2:
        blocks([full(n_edge)])
    if n_full >= 2:
        def body(i, c):
            p = n_edge + n_full % 2 + 2 * i
            blocks([full(p), full(p + 1)])
            return c
        lax.fori_loop(0, n_full // 2, body, 0)


def _attn_a(qkv, bias_w, B, S, d_a):
    n_hb = d_a // LANES
    seg = d_a // LANES
    return pl.pallas_call(
        _attn_a_kernel,
        grid=(B, n_hb),
        in_specs=[pl.BlockSpec((None, S, LANES), lambda b, h: (b, 0, h)),
                  pl.BlockSpec((None, S, LANES), lambda b, h: (b, 0, seg + h)),
                  pl.BlockSpec((None, S, LANES), lambda b, h: (b, 0, 2 * seg + h)),
                  pl.BlockSpec((HEADS_PER_BLOCK, 1, BIAS_W), lambda b, h: (h, 0, 0))],
        out_specs=pl.BlockSpec((None, S, LANES), lambda b, h: (b, 0, h)),
        out_shape=jax.ShapeDtypeStruct((B, S, d_a), F32),
        scratch_shapes=[pltpu.VMEM((HEADS_PER_BLOCK, QBLK_A, BAND_A), F32)],
        compiler_params=_cparams("parallel", "parallel"),
        name="attn_chunked",
    )(qkv, qkv, qkv, bias_w)


def _rel_bias_by_offset(rel_bias):
    dist = jnp.clip(BAND_A - 1 - jnp.arange(BIAS_W), -MAX_REL, MAX_REL) + MAX_REL
    return rel_bias[:, dist].astype(F32)[:, None, :]


def _attn_b_kernel(q_ref, k_ref, v_ref, o_ref):
    qi = pl.program_id(2)
    t = TILE_B
    row = lax.broadcasted_iota(jnp.int32, (t, t), 0)
    col = lax.broadcasted_iota(jnp.int32, (t, t), 1)
    from_s = (row >= col).astype(BF16)
    from_s2 = jnp.concatenate([from_s, from_s], axis=0)
    causal = col < row
    q = q_ref[...]
    heads = [slice(h * HEAD_DIM, (h + 1) * HEAD_DIM) for h in range(q_ref.shape[1] // HEAD_DIM)]
    qs = [q[:, hs] for hs in heads]

    def scores(js):
        zs = []
        for j in js:
            kt = k_ref[pl.ds(pl.multiple_of(j * t, t), t), :]
            zs += [_dot_nt(qs[h], kt[:, hs]) for h, hs in enumerate(heads)]
        return zs

    def tiles(js, zs, state, first_is_diag=False):
        accs, carries = list(state[0::2]), list(state[1::2])
        work = []
        for n, j in enumerate(js):
            vt = v_ref[pl.ds(pl.multiple_of(j * t, t), t), :]
            for h, hs in enumerate(heads):
                work.append(dict(h=h, diag=first_is_diag and n == 0, v=vt[:, hs], z=zs[n * len(heads) + h]))
        for w in work:
            z = w["z"]
            sp = jnp.maximum(z, 0.0) + jnp.log(1.0 + jnp.exp(-jnp.abs(z)))
            if w["diag"]:
                sp = jnp.where(causal, sp, 0.0)
            hi = sp.astype(BF16)
            lo = (sp - hi.astype(F32)).astype(BF16)
            w["suffix"] = _dot(jnp.concatenate([hi, lo], axis=-1), from_s2)
            w["rowsum"] = jnp.sum(sp, axis=-1, keepdims=True)
        for w in work:
            h = w["h"]
            if w["diag"]:
                a = jnp.where(causal, jnp.exp(w["z"] - w["suffix"]), 0.0)
                accs[h] = _dot(a.astype(BF16), w["v"])
                carries[h] = w["rowsum"]
            else:
                a = jnp.exp(w["z"] - w["suffix"] - carries[h])
                accs[h] = accs[h] + _dot(a.astype(BF16), w["v"])
                carries[h] = carries[h] + w["rowsum"]
        return tuple(x for pair in zip(accs, carries) for x in pair)

    empty = (None,) * (2 * len(heads))
    state = lax.cond(qi % 2 == 1,
                     lambda: tiles([qi, qi - 1], scores([qi, qi - 1]), empty, True),
                     lambda: tiles([qi], scores([qi]), empty, True))
    first = qi - 1 - qi % 2

    def body(it, st):
        j = first - 2 * it
        return tiles([j, j - 1], scores([j, j - 1]), st)

    state = lax.fori_loop(0, qi // 2, body, state)
    o_ref[...] = jnp.concatenate(state[0::2], axis=-1)


def _attn_b(qkv, B, S, d_a, d_b):
    w = WIDTH_B
    n_hb = d_b // w
    base = 3 * d_a // w
    seg = d_b // w
    return pl.pallas_call(
        _attn_b_kernel,
        grid=(B, n_hb, S // TILE_B),
        in_specs=[pl.BlockSpec((None, TILE_B, w), lambda b, h, i: (b, i, base + h)),
                  pl.BlockSpec((None, S, w), lambda b, h, i: (b, 0, base + seg + h)),
                  pl.BlockSpec((None, S, w), lambda b, h, i: (b, 0, base + 2 * seg + h))],
        out_specs=pl.BlockSpec((None, TILE_B, w), lambda b, h, i: (b, i, h)),
        out_shape=jax.ShapeDtypeStruct((B, S, d_b), F32),
        compiler_params=_cparams("parallel", "parallel", "parallel"),
        name="attn_stickbreak",
    )(qkv, qkv, qkv)


def _layer_norm(r, g, b):
    mu = jnp.mean(r, axis=-1, keepdims=True)
    c = r - mu
    var = jnp.mean(c * c, axis=-1, keepdims=True)
    return c * lax.rsqrt(var + LN_EPS) * g + b


def _rms_norm(y, g):
    ms = jnp.mean(y * y, axis=-1, keepdims=True)
    return y * lax.rsqrt(ms + RMS_EPS) * g


def _silu(g):
    return g * jax.nn.sigmoid(g)


def _post_kernel(ya_ref, yb_ref, x_ref, ga_ref, gb_ref, wo_ref, g1_ref, b1_ref, rw_ref,
                 wsg_ref, wsu_ref, wsd_ref, x1_ref, base_ref, logit_ref, *, alpha):
    d_a = ya_ref.shape[1]
    na = _rms_norm(ya_ref[...], ga_ref[...]).astype(BF16)
    nb = _rms_norm(yb_ref[...], gb_ref[...]).astype(BF16)
    h = _dot(na, wo_ref[:d_a, :]) + _dot(nb, wo_ref[d_a:, :])
    x1 = _layer_norm(alpha * x_ref[...] + h, g1_ref[...], b1_ref[...])
    x1_ref[...] = _pack_bf16_halves(x1)
    xb = x1.astype(BF16)
    logit_ref[...] = _dot_nt(rw_ref[...], xb)
    hs = _silu(_dot(xb, wsg_ref[...])) * _dot(xb, wsu_ref[...])
    base_ref[...] = alpha * x1 + _dot(hs.astype(BF16), wsd_ref[...])


def _post(ya, yb, xt, gain_a, gain_b, wo_b, g1, b1, rw_b, wsg_b, wsu_b, wsd_b, alpha):
    T, D = xt.shape
    d_a, d_b = ya.shape[1], yb.shape[1]
    E = rw_b.shape[0]
    De = wsg_b.shape[1]
    tm = 2 * TOK_BLOCK
    row = lambda i: (i, 0)
    fix = lambda i: (0, 0)
    return pl.pallas_call(
        functools.partial(_post_kernel, alpha=alpha),
        grid=(T // tm,),
        in_specs=[pl.BlockSpec((tm, d_a), row), pl.BlockSpec((tm, d_b), row), pl.BlockSpec((tm, D), row),
                  pl.BlockSpec((1, d_a), fix), pl.BlockSpec((1, d_b), fix),
                  pl.BlockSpec((d_a + d_b, D), fix), pl.BlockSpec((1, D), fix), pl.BlockSpec((1, D), fix),
                  pl.BlockSpec((E, D), fix), pl.BlockSpec((D, De), fix), pl.BlockSpec((D, De), fix),
                  pl.BlockSpec((De, D), fix)],
        out_specs=[pl.BlockSpec((tm, D // 2), row), pl.BlockSpec((tm, D), row),
                   pl.BlockSpec((E, tm), lambda i: (0, i))],
        out_shape=[jax.ShapeDtypeStruct((T, D // 2), jnp.uint32), jax.ShapeDtypeStruct((T, D), F32),
                   jax.ShapeDtypeStruct((E, T), F32)],
        compiler_params=_cparams("parallel"),
        name="post_attn",
    )(ya, yb, xt, gain_a, gain_b, wo_b, g1, b1, rw_b, wsg_b, wsu_b, wsd_b)


def _route_kernel(logit_ref, rbias_ref, idx_ref, gate_ref, rank_ref, cnt_ref, carry_ref):
    E, tm = logit_ref.shape
    neg = -jnp.inf

    @pl.when(pl.program_id(0) == 0)
    def _():
        carry_ref[...] = jnp.zeros_like(carry_ref)

    scores = jax.nn.sigmoid(logit_ref[...])
    sel = scores + rbias_ref[...]
    eidx = lax.broadcasted_iota(jnp.int32, (E, tm), 0).astype(F32)

    def first_argmax(v, ids):
        m = jnp.max(v, axis=0, keepdims=True)
        return m, jnp.min(jnp.where(v == m, ids, float(E)), axis=0, keepdims=True)

    grp_scores = []
    ids = lax.broadcasted_iota(jnp.int32, (EXPERTS_PER_GROUP, tm), 0).astype(F32)
    for g in range(N_GROUPS):
        v = sel[g * EXPERTS_PER_GROUP:(g + 1) * EXPERTS_PER_GROUP, :]
        m1, i1 = first_argmax(v, ids)
        m2 = jnp.max(jnp.where(ids == i1, neg, v), axis=0, keepdims=True)
        grp_scores.append(m1 + m2)
    parts = []
    for g in range(N_GROUPS):
        beaten = jnp.zeros((1, tm), jnp.int32)
        for o in range(N_GROUPS):
            if o != g:
                s, t = grp_scores[o], grp_scores[g]
                beaten = beaten + ((s > t) | ((s == t) & (o < g))).astype(jnp.int32)
        rows = slice(g * EXPERTS_PER_GROUP, (g + 1) * EXPERTS_PER_GROUP)
        parts.append(jnp.where(beaten < TOPK_GROUPS, sel[rows, :], neg))
    selm = jnp.concatenate(parts, axis=0)

    hits, gates, ids_k = [], [], []
    chosen = jnp.zeros((E, tm), F32)
    gate_sum = jnp.zeros((1, tm), F32)
    for k in range(TOP_K):
        _, ik = first_argmax(selm, eidx)
        hit = eidx == ik
        gk = jnp.sum(jnp.where(hit, scores, 0.0), axis=0, keepdims=True)
        selm = jnp.where(hit, neg, selm)
        chosen = jnp.where(hit, 1.0, chosen)
        gate_sum = gate_sum + gk
        hits.append(hit)
        gates.append(gk)
        ids_k.append(ik)

    row = lax.broadcasted_iota(jnp.int32, (tm, tm), 0)
    col = lax.broadcasted_iota(jnp.int32, (tm, tm), 1)
    earlier = (row < col).astype(BF16)
    before = _dot(chosen.astype(BF16), earlier) + carry_ref[...]
    carry_ref[...] = carry_ref[...] + jnp.sum(chosen, axis=1, keepdims=True)
    cnt_ref[...] = carry_ref[...].astype(jnp.int32)

    ranks = [jnp.sum(jnp.where(hit, before, 0.0), axis=0, keepdims=True) for hit in hits]
    idx_ref[...] = jnp.concatenate(ids_k, axis=0).astype(jnp.int32)
    rank_ref[...] = jnp.concatenate(ranks, axis=0).astype(jnp.int32)
    gate_ref[...] = jnp.concatenate(gates, axis=0) / gate_sum * ROUTED_SCALE


def _route(logits_t, rbias):
    E, T = logits_t.shape
    tm = TOK_BLOCK
    col = lambda i: (0, i)
    fix = lambda i: (0, 0)
    return pl.pallas_call(
        _route_kernel,
        grid=(T // tm,),
        in_specs=[pl.BlockSpec((E, tm), col), pl.BlockSpec((E, 1), fix)],
        out_specs=[pl.BlockSpec((TOP_K, tm), col), pl.BlockSpec((TOP_K, tm), col),
                   pl.BlockSpec((TOP_K, tm), col), pl.BlockSpec((E, 1), fix)],
        out_shape=[jax.ShapeDtypeStruct((TOP_K, T), jnp.int32), jax.ShapeDtypeStruct((TOP_K, T), F32),
                   jax.ShapeDtypeStruct((TOP_K, T), jnp.int32), jax.ShapeDtypeStruct((E, 1), jnp.int32)],
        scratch_shapes=[pltpu.VMEM((E, 1), F32)],
        compiler_params=_cparams("arbitrary"),
        name="route",
    )(logits_t, rbias)


def _dest_kernel(idx_ref, rank_ref, start_ref, dest_ref):
    K, tm = idx_ref.shape
    E = start_ref.shape[0]
    eidx = lax.broadcasted_iota(jnp.int32, (E, tm), 0)
    idx = idx_ref[...]
    start = start_ref[...]
    rows = [jnp.sum(jnp.where(eidx == idx[k:k + 1, :], start, 0.0), axis=0, keepdims=True) for k in range(K)]
    dest_ref[...] = jnp.concatenate(rows, axis=0).astype(jnp.int32) + rank_ref[...]


def _dest(idx, rank, start_f):
    K, T = idx.shape
    E = start_f.shape[0]
    tm = 2 * TOK_BLOCK
    col = lambda i: (0, i)
    return pl.pallas_call(
        _dest_kernel,
        grid=(T // tm,),
        in_specs=[pl.BlockSpec((K, tm), col), pl.BlockSpec((K, tm), col), pl.BlockSpec((E, 1), lambda i: (0, 0))],
        out_specs=pl.BlockSpec((K, tm), col),
        out_shape=jax.ShapeDtypeStruct((K, T), jnp.int32),
        compiler_params=_cparams("parallel"),
        name="dest_rows",
    )(idx, rank, start_f)


def _expert_kernel(first_ref, nblk_ref, nused_ref, xs_hbm, wg_ref, wu_ref, wd_ref, ys_hbm,
                   xbuf, ybuf, wgu_s, wd_s, in_sem, out_sem):
    ring = xbuf.shape[0]
    e = pl.program_id(0)
    De = wg_ref.shape[2]
    n = nblk_ref[e]
    g0 = first_ref[e]
    n_used = nused_ref[0]

    def rows(g):
        return pl.ds(pl.multiple_of(g * ROW_BLOCK, ROW_BLOCK), ROW_BLOCK)

    def in_copy(g, slot):
        return pltpu.make_async_copy(xs_hbm.at[rows(g), :], xbuf.at[slot], in_sem.at[slot])

    def out_copy(g, slot):
        return pltpu.make_async_copy(ybuf.at[slot], ys_hbm.at[rows(g), :], out_sem.at[slot])

    @pl.when(e == 0)
    def _():
        for g in range(ring - 1):
            @pl.when(g < n_used)
            def _():
                in_copy(g, g).start()

    @pl.when(n > 0)
    def _():
        wgu_s[:, :De] = wg_ref[0].astype(BF16)
        wgu_s[:, De:] = wu_ref[0].astype(BF16)
        wd_s[...] = wd_ref[0].astype(BF16)

        def block(j, c):
            g = g0 + j
            slot = g & (ring - 1)
            in_copy(g, slot).wait()

            @pl.when(g + ring - 1 < n_used)
            def _():
                in_copy(g + ring - 1, (g + ring - 1) & (ring - 1)).start()

            @pl.when(g >= ring)
            def _():
                out_copy(g - ring, slot).wait()

            half = ROW_BLOCK // 2
            dh = xbuf.shape[2]
            gus = []
            for r in range(2):
                lo, hi = _unpack_bf16_halves(xbuf[slot, r * half:(r + 1) * half, :])
                gus.append(_dot(lo.astype(BF16), wgu_s[:dh, :]) + _dot(hi.astype(BF16), wgu_s[dh:, :]))
            hs = [(_silu(gu[:, :De]) * gu[:, De:]).astype(BF16) for gu in gus]
            for r in range(2):
                ybuf[slot, r * half:(r + 1) * half, :] = _pack_bf16_halves(_dot(hs[r], wd_s[...]))
            out_copy(g, slot).start()
            return c

        lax.fori_loop(0, n, block, 0)

    @pl.when(e == pl.num_programs(0) - 1)
    def _():
        for back in range(1, ring + 1):
            @pl.when(n_used >= back)
            def _():
                out_copy(n_used - back, (n_used - back) & (ring - 1)).wait()


def _experts(first_block, n_blocks, n_used, xs, w_gate, w_up, w_down):
    n_rows, Dh = xs.shape
    D = 2 * Dh
    E, _, De = w_gate.shape
    grid_spec = pltpu.PrefetchScalarGridSpec(
        num_scalar_prefetch=3,
        grid=(E,),
        in_specs=[pl.BlockSpec(memory_space=pl.ANY),
                  pl.BlockSpec((1, D, De), lambda e, *_: (e, 0, 0)),
                  pl.BlockSpec((1, D, De), lambda e, *_: (e, 0, 0)),
                  pl.BlockSpec((1, De, D), lambda e, *_: (e, 0, 0))],
        out_specs=pl.BlockSpec(memory_space=pl.ANY),
        scratch_shapes=[pltpu.VMEM((EXPERT_RING, ROW_BLOCK, Dh), jnp.uint32),
                        pltpu.VMEM((EXPERT_RING, ROW_BLOCK, Dh), jnp.uint32),
                        pltpu.VMEM((D, 2 * De), BF16), pltpu.VMEM((De, D), BF16),
                        pltpu.SemaphoreType.DMA((EXPERT_RING,)), pltpu.SemaphoreType.DMA((EXPERT_RING,))],
    )
    return pl.pallas_call(
        _expert_kernel,
        grid_spec=grid_spec,
        out_shape=jax.ShapeDtypeStruct((n_rows, Dh), jnp.uint32),
        compiler_params=_cparams("arbitrary"),
        name="expert_ffn",
    )(first_block, n_blocks, n_used, xs, w_gate, w_up, w_down)


SC_WINDOW = 128


def _sc_mesh():
    return plsc.VectorSubcoreMesh(core_axis_name="core", subcore_axis_name="subcore")


def _sc_workers():
    info = plsc.get_sparse_core_info()
    return info.num_cores, info.num_cores * info.num_subcores


def _sc_worker_id(num_cores):
    return lax.axis_index("subcore") * num_cores + lax.axis_index("core")


def _sc_scatter_rows(x, dest_flat, n_rows):
    T, D = x.shape
    K = dest_flat.shape[0] // T
    nc, nw = _sc_workers()
    per_w = T // nw
    assert T % nw == 0 and per_w % SC_WINDOW == 0

    @functools.partial(
        pl.kernel, out_type=jax.ShapeDtypeStruct((n_rows, D), x.dtype), mesh=_sc_mesh(),
        scratch_types=[pltpu.VMEM((SC_WINDOW,), jnp.int32), pltpu.VMEM((SC_WINDOW, D), x.dtype)],
        name="sc_dispatch")
    def run(x_hbm, i_hbm, o_hbm, idx_v, rows_v):
        first = _sc_worker_id(nc) * per_w

        @pl.loop(0, per_w // SC_WINDOW)
        def _(c):
            base = first + c * SC_WINDOW
            pltpu.sync_copy(x_hbm.at[pl.ds(base, SC_WINDOW)], rows_v)
            for k in range(K):
                pltpu.sync_copy(i_hbm.at[pl.ds(k * T + base, SC_WINDOW)], idx_v)
                pltpu.sync_copy(rows_v, o_hbm.at[idx_v])

    return run(x, dest_flat)


def _sc_gather_rows(ys, dest_flat):
    N = dest_flat.shape[0]
    D = ys.shape[1]
    nc, nw = _sc_workers()
    per_w = N // nw
    assert N % nw == 0 and per_w % SC_WINDOW == 0

    @functools.partial(
        pl.kernel, out_type=jax.ShapeDtypeStruct((N, D), ys.dtype), mesh=_sc_mesh(),
        scratch_types=[pltpu.VMEM((SC_WINDOW,), jnp.int32), pltpu.VMEM((SC_WINDOW, D), ys.dtype)],
        name="sc_combine_gather")
    def run(y_hbm, i_hbm, o_hbm, idx_v, rows_v):
        first = _sc_worker_id(nc) * per_w

        @pl.loop(0, per_w // SC_WINDOW)
        def _(c):
            base = first + c * SC_WINDOW
            pltpu.sync_copy(i_hbm.at[pl.ds(base, SC_WINDOW)], idx_v)
            pltpu.sync_copy(y_hbm.at[idx_v], rows_v)
            pltpu.sync_copy(rows_v, o_hbm.at[pl.ds(base, SC_WINDOW)])

    return run(ys, dest_flat)


def _final_kernel(z_ref, gate_ref, base_ref, g_ref, b_ref, o_ref):
    gate = gate_ref[...]
    acc_lo = acc_hi = None
    for k in range(TOP_K):
        lo, hi = _unpack_bf16_halves(z_ref[k])
        g = gate[:, k:k + 1]
        acc_lo = g * lo if acc_lo is None else acc_lo + g * lo
        acc_hi = g * hi if acc_hi is None else acc_hi + g * hi
    acc = base_ref[...] + jnp.concatenate([acc_lo, acc_hi], axis=-1)
    o_ref[...] = _layer_norm(acc, g_ref[...], b_ref[...])


def _final(z, gate, base, g2, b2, first_token):
    T, D = base.shape
    tb = TOK_BLOCK
    off = first_token // tb
    return pl.pallas_call(
        _final_kernel,
        grid=(z.shape[1] // tb,),
        in_specs=[pl.BlockSpec((TOP_K, tb, D // 2), lambda i: (0, i, 0)),
                  pl.BlockSpec((tb, TOP_K), lambda i: (i + off, 0)),
                  pl.BlockSpec((tb, D), lambda i: (i + off, 0)),
                  pl.BlockSpec((1, D), lambda i: (0, 0)),
                  pl.BlockSpec((1, D), lambda i: (0, 0))],
        out_specs=pl.BlockSpec((tb, D), lambda i: (i + off, 0)),
        out_shape=jax.ShapeDtypeStruct((T, D), F32),
        input_output_aliases={2: 0},
        compiler_params=_cparams("parallel"),
        name="final_sum_ln",
    )(z, gate, base, g2, b2)


def _layer(x, w_in, rel_bias, gain_a, gain_b, w_out, ln1_g, ln1_b, router_w, router_bias,
           w_gate, w_up, w_down, ws_gate, ws_up, ws_down, ln2_g, ln2_b, alpha):
    B, S, D = x.shape
    T = B * S
    d_a = gain_a.shape[0]
    d_b = gain_b.shape[0]
    assert S % TILE_B == 0 and T % 512 == 0 and d_a % LANES == 0 and d_b % WIDTH_B == 0 and (3 * d_a) % WIDTH_B == 0
    xt = x.reshape(T, D)

    scale = HEAD_DIM ** -0.5
    col = jnp.arange(w_in.shape[1])
    is_q = (col < d_a) | ((col >= 3 * d_a) & (col < 3 * d_a + d_b))
    w_in_b = (w_in * jnp.where(is_q, scale, 1.0)[None, :]).astype(BF16)

    qkv = _qkv_proj(xt, w_in_b).reshape(B, S, -1)
    ya = _attn_a(qkv, _rel_bias_by_offset(rel_bias), B, S, d_a).reshape(T, d_a)
    yb = _attn_b(qkv, B, S, d_a, d_b).reshape(T, d_b)

    x1p, base, logits_t = _post(
        ya, yb, xt, gain_a[None], gain_b[None], w_out.astype(BF16), ln1_g[None], ln1_b[None],
        router_w.T.astype(BF16), ws_gate.astype(BF16), ws_up.astype(BF16), ws_down.astype(BF16), alpha)

    idx, gate, rank, cnt = _route(logits_t, router_bias[:, None].astype(F32))
    cnt = cnt[:, 0]
    padded = (cnt + ROW_BLOCK - 1) // ROW_BLOCK * ROW_BLOCK
    ends = jnp.cumsum(padded)
    start = (ends - padded).astype(jnp.int32)
    n_rows = T * TOP_K + N_EXPERTS * ROW_BLOCK
    first_block = start // ROW_BLOCK
    n_blocks = (padded // ROW_BLOCK).astype(jnp.int32)
    n_used = (ends[-1:] // ROW_BLOCK).astype(jnp.int32)
    dest = _dest(idx, rank, start.astype(F32)[:, None])

    xs = _sc_scatter_rows(x1p, dest.reshape(-1), n_rows)
    ys = _experts(first_block, n_blocks, n_used, xs, w_gate, w_up, w_down)
    gate_t = gate.T
    out = base
    tg = T // COMBINE_GROUPS
    for q in range(COMBINE_GROUPS):
        z = _sc_gather_rows(ys, dest[:, q * tg:(q + 1) * tg].reshape(-1)).reshape(TOP_K, tg, D // 2)
        out = _final(z, gate_t, out, ln2_g[None], ln2_b[None], q * tg)
    return out.reshape(B, S, D)


def kernel(x, w_in, rel_bias, gain_a, gain_b, w_out, ln1_g, ln1_b, router_w, router_bias,
           w_gate, w_up, w_down, ws_gate, ws_up, ws_down, ln2_g, ln2_b):
    depth = w_in.shape[0]
    alpha = (2 * depth) ** 0.25
    for l in range(depth):
        x = _layer(x, w_in[l], rel_bias[l], gain_a[l], gain_b[l], w_out[l], ln1_g[l], ln1_b[l],
                   router_w[l], router_bias[l], w_gate[l], w_up[l], w_down[l],
                   ws_gate[l], ws_up[l], ws_down[l], ln2_g[l], ln2_b[l], alpha)
    return x
```

```python
import functools

import jax
import jax.numpy as jnp
from jax import lax
from jax.experimental import pallas as pl
from jax.experimental.pallas import tpu as pltpu
from jax.experimental.pallas import tpu_sc as plsc

CHUNK = 64
HEAD_DIM = 64
LEFT_CHUNKS = 8
MAX_REL = 128
N_EXPERTS = 256
TOP_K = 8
N_GROUPS = 8
TOPK_GROUPS = 4
EXPERTS_PER_GROUP = N_EXPERTS // N_GROUPS
ROUTED_SCALE = 2.5
LN_EPS = 1e-5
RMS_EPS = 1e-6

LANES = 128
SUBLANES = 8
HEADS_PER_BLOCK = LANES // HEAD_DIM
QBLK_A = 2 * CHUNK
BAND_A = (LEFT_CHUNKS + 2) * CHUNK
BIAS_W = BAND_A + QBLK_A
TILE_B = 256
WIDTH_B = 2 * HEAD_DIM
ROW_BLOCK = 256
EXPERT_RING = 4
TOK_BLOCK = 256
COMBINE_GROUPS = 4
MASK_VALUE = -1e30
VMEM_LIMIT = 48 * 1024 * 1024

F32 = jnp.float32
BF16 = jnp.bfloat16


def _cparams(*sem, flags=None):
    return pltpu.CompilerParams(dimension_semantics=sem, vmem_limit_bytes=VMEM_LIMIT, flags=flags)


def _dot(a, b):
    return jnp.dot(a, b, preferred_element_type=F32)


def _dot_nt(a, b):
    return lax.dot_general(a, b, (((1,), (1,)), ((), ())), preferred_element_type=F32)


def _pack_bf16_halves(x):
    half = x.shape[1] // 2
    bits = lax.bitcast_convert_type(x.astype(BF16).astype(F32), jnp.uint32)
    return (bits[:, :half] >> 16) | (bits[:, half:] & jnp.uint32(0xFFFF0000))


def _unpack_bf16_halves(w):
    lo = lax.bitcast_convert_type(w << 16, F32)
    hi = lax.bitcast_convert_type(w & jnp.uint32(0xFFFF0000), F32)
    return lo, hi


def _qkv_kernel(x_ref, w_ref, o_ref, *, col_chunk):
    xb = x_ref[...].astype(BF16)
    for n in range(w_ref.shape[1] // col_chunk):
        cols = slice(n * col_chunk, (n + 1) * col_chunk)
        o_ref[:, cols] = _dot(xb, w_ref[:, cols]).astype(BF16)


def _qkv_proj(xt, w_b):
    T, D = xt.shape
    N = w_b.shape[1]
    tm = 512
    return pl.pallas_call(
        functools.partial(_qkv_kernel, col_chunk=512),
        grid=(T // tm,),
        in_specs=[pl.BlockSpec((tm, D), lambda i: (i, 0)),
                  pl.BlockSpec((D, N), lambda i: (0, 0))],
        out_specs=pl.BlockSpec((tm, N), lambda i: (i, 0)),
        out_shape=jax.ShapeDtypeStruct((T, N), BF16),
        compiler_params=_cparams("parallel"),
        name="qkv_proj",
    )(xt, w_b)


def _attn_a_kernel(q_ref, k_ref, v_ref, w_ref, o_ref, bias_ref):
    S = q_ref.shape[0]
    nblk = S // QBLK_A
    lead = LEFT_CHUNKS * CHUNK
    n_edge = min(lead // QBLK_A, nblk)

    qc = lax.broadcasted_iota(jnp.int32, (QBLK_A, BAND_A), 0) // CHUNK
    kc = lax.broadcasted_iota(jnp.int32, (QBLK_A, BAND_A), 1) // CHUNK
    allowed = (kc >= qc) & (kc <= qc + LEFT_CHUNKS)
    for h in range(HEADS_PER_BLOCK):
        wb = jnp.broadcast_to(w_ref[h], (QBLK_A, BIAS_W))
        toeplitz = pltpu.roll(wb, BIAS_W - (QBLK_A - 1), 1, stride=1, stride_axis=0)
        bias_ref[h] = jnp.where(allowed, toeplitz[:, :BAND_A], MASK_VALUE)

    lane = lax.broadcasted_iota(jnp.int32, (QBLK_A, LANES), 1)
    head_of_lane = lane // HEAD_DIM

    def blocks(specs):
        work = []
        for p, kstart, nk, bias_off in specs:
            q = q_ref[pl.ds(p * QBLK_A, QBLK_A), :]
            k = k_ref[pl.ds(kstart, nk), :]
            v = v_ref[pl.ds(kstart, nk), :]
            for h in range(HEADS_PER_BLOCK):
                qh = jnp.where(head_of_lane == h, q, jnp.zeros_like(q))
                work.append(dict(s=_dot_nt(qh, k), v=v, bias=bias_ref[h, :, bias_off:bias_off + nk]))
        for w in work:
            s = w["s"] + w["bias"]
            e = jnp.exp(s - jnp.max(s, axis=-1, keepdims=True))
            w["l"] = jnp.sum(e, axis=-1, keepdims=True)
            w["e"] = e.astype(BF16)
        outs = [_dot(w["e"], w["v"]) / w["l"] for w in work]
        for i, spec in enumerate(specs):
            o = outs[i * HEADS_PER_BLOCK]
            for h in range(1, HEADS_PER_BLOCK):
                o = jnp.where(head_of_lane == h, outs[i * HEADS_PER_BLOCK + h], o)
            o_ref[pl.ds(spec[0] * QBLK_A, QBLK_A), :] = o

    blocks([(p, 0, (p + 1) * QBLK_A, lead - p * QBLK_A) for p in range(n_edge)])

    def full(p):
        return (p, pl.multiple_of(p * QBLK_A - lead, QBLK_A), BAND_A, 0)

    n_full = nblk - n_edge
    if n_full % 2:
        blocks([full(n_edge)])
    if n_full >= 2:
        def body(i, c):
            p = n_edge + n_full % 2 + 2 * i
            blocks([full(p), full(p + 1)])
            return c
        lax.fori_loop(0, n_full // 2, body, 0)


def _attn_a(qkv, bias_w, B, S, d_a):
    n_hb = d_a // LANES
    seg = d_a // LANES
    return pl.pallas_call(
        _attn_a_kernel,
        grid=(B, n_hb),
        in_specs=[pl.BlockSpec((None, S, LANES), lambda b, h: (b, 0, h)),
                  pl.BlockSpec((None, S, LANES), lambda b, h: (b, 0, seg + h)),
                  pl.BlockSpec((None, S, LANES), lambda b, h: (b, 0, 2 * seg + h)),
                  pl.BlockSpec((HEADS_PER_BLOCK, 1, BIAS_W), lambda b, h: (h, 0, 0))],
        out_specs=pl.BlockSpec((None, S, LANES), lambda b, h: (b, 0, h)),
        out_shape=jax.ShapeDtypeStruct((B, S, d_a), F32),
        scratch_shapes=[pltpu.VMEM((HEADS_PER_BLOCK, QBLK_A, BAND_A), F32)],
        compiler_params=_cparams("parallel", "parallel"),
        name="attn_chunked",
    )(qkv, qkv, qkv, bias_w)


def _rel_bias_by_offset(rel_bias):
    dist = jnp.clip(BAND_A - 1 - jnp.arange(BIAS_W), -MAX_REL, MAX_REL) + MAX_REL
    return rel_bias[:, dist].astype(F32)[:, None, :]


def _attn_b_kernel(q_ref, k_ref, v_ref, o_ref):
    qi = pl.program_id(2)
    t = TILE_B
    row = lax.broadcasted_iota(jnp.int32, (t, t), 0)
    col = lax.broadcasted_iota(jnp.int32, (t, t), 1)
    from_s = (row >= col).astype(BF16)
    from_s2 = jnp.concatenate([from_s, from_s], axis=0)
    causal = col < row
    q = q_ref[...]
    n_heads = q_ref.shape[1] // HEAD_DIM
    head_of_lane = lax.broadcasted_iota(jnp.int32, q.shape, 1) // HEAD_DIM
    qs = [jnp.where(head_of_lane == h, q, jnp.zeros_like(q)) for h in range(n_heads)]

    def scores(js):
        zs = []
        for j in js:
            kt = k_ref[pl.ds(pl.multiple_of(j * t, t), t), :]
            zs += [_dot_nt(qh, kt) for qh in qs]
        return zs

    def tiles(js, zs, state, first_is_diag=False):
        accs, carries = list(state[0::2]), list(state[1::2])
        work = []
        for n, j in enumerate(js):
            vt = v_ref[pl.ds(pl.multiple_of(j * t, t), t), :]
            for h in range(n_heads):
                work.append(dict(h=h, diag=first_is_diag and n == 0, v=vt, z=zs[n * n_heads + h]))
        for w in work:
            z = w["z"]
            sp = jnp.maximum(z, 0.0) + jnp.log(1.0 + jnp.exp(-jnp.abs(z)))
            if w["diag"]:
                sp = jnp.where(causal, sp, 0.0)
            hi = sp.astype(BF16)
            lo = (sp - hi.astype(F32)).astype(BF16)
            w["suffix"] = _dot(jnp.concatenate([hi, lo], axis=-1), from_s2)
            w["rowsum"] = jnp.sum(sp, axis=-1, keepdims=True)
        for w in work:
            h = w["h"]
            if w["diag"]:
                a = jnp.where(causal, jnp.exp(w["z"] - w["suffix"]), 0.0)
                accs[h] = _dot(a.astype(BF16), w["v"])
                carries[h] = w["rowsum"]
            else:
                a = jnp.exp(w["z"] - w["suffix"] - carries[h])
                accs[h] = accs[h] + _dot(a.astype(BF16), w["v"])
                carries[h] = carries[h] + w["rowsum"]
        return tuple(x for pair in zip(accs, carries) for x in pair)

    empty = (None,) * (2 * n_heads)
    state = lax.cond(qi % 2 == 1,
                     lambda: tiles([qi, qi - 1], scores([qi, qi - 1]), empty, True),
                     lambda: tiles([qi], scores([qi]), empty, True))
    first = qi - 1 - qi % 2

    def body(it, st):
        j = first - 2 * it
        return tiles([j, j - 1], scores([j, j - 1]), st)

    state = lax.fori_loop(0, qi // 2, body, state)
    out = state[0]
    for h in range(1, n_heads):
        out = jnp.where(head_of_lane == h, state[2 * h], out)
    o_ref[...] = out


def _attn_b(qkv, B, S, d_a, d_b):
    w = WIDTH_B
    n_hb = d_b // w
    base = 3 * d_a // w
    seg = d_b // w
    return pl.pallas_call(
        _attn_b_kernel,
        grid=(B, n_hb, S // TILE_B),
        in_specs=[pl.BlockSpec((None, TILE_B, w), lambda b, h, i: (b, i, base + h)),
                  pl.BlockSpec((None, S, w), lambda b, h, i: (b, 0, base + seg + h)),
                  pl.BlockSpec((None, S, w), lambda b, h, i: (b, 0, base + 2 * seg + h))],
        out_specs=pl.BlockSpec((None, TILE_B, w), lambda b, h, i: (b, i, h)),
        out_shape=jax.ShapeDtypeStruct((B, S, d_b), F32),
        compiler_params=_cparams("parallel", "parallel", "parallel"),
        name="attn_stickbreak",
    )(qkv, qkv, qkv)


def _layer_norm(r, g, b):
    mu = jnp.mean(r, axis=-1, keepdims=True)
    c = r - mu
    var = jnp.mean(c * c, axis=-1, keepdims=True)
    return c * lax.rsqrt(var + LN_EPS) * g + b


def _rms_norm(y, g):
    ms = jnp.mean(y * y, axis=-1, keepdims=True)
    return y * lax.rsqrt(ms + RMS_EPS) * g


def _silu(g):
    return g * jax.nn.sigmoid(g)


def _post_kernel(ya_ref, yb_ref, x_ref, ga_ref, gb_ref, wo_ref, g1_ref, b1_ref, rw_ref,
                 wsg_ref, wsu_ref, wsd_ref, x1_ref, base_ref, logit_ref, *, alpha):
    d_a = ya_ref.shape[1]
    na = _rms_norm(ya_ref[...], ga_ref[...]).astype(BF16)
    nb = _rms_norm(yb_ref[...], gb_ref[...]).astype(BF16)
    h = _dot(na, wo_ref[:d_a, :]) + _dot(nb, wo_ref[d_a:, :])
    x1 = _layer_norm(alpha * x_ref[...] + h, g1_ref[...], b1_ref[...])
    x1_ref[...] = _pack_bf16_halves(x1)
    xb = x1.astype(BF16)
    logit_ref[...] = _dot_nt(rw_ref[...], xb)
    hs = _silu(_dot(xb, wsg_ref[...])) * _dot(xb, wsu_ref[...])
    base_ref[...] = alpha * x1 + _dot(hs.astype(BF16), wsd_ref[...])


def _post(ya, yb, xt, gain_a, gain_b, wo_b, g1, b1, rw_b, wsg_b, wsu_b, wsd_b, alpha):
    T, D = xt.shape
    d_a, d_b = ya.shape[1], yb.shape[1]
    E = rw_b.shape[0]
    De = wsg_b.shape[1]
    tm = 2 * TOK_BLOCK
    row = lambda i: (i, 0)
    fix = lambda i: (0, 0)
    return pl.pallas_call(
        functools.partial(_post_kernel, alpha=alpha),
        grid=(T // tm,),
        in_specs=[pl.BlockSpec((tm, d_a), row), pl.BlockSpec((tm, d_b), row), pl.BlockSpec((tm, D), row),
                  pl.BlockSpec((1, d_a), fix), pl.BlockSpec((1, d_b), fix),
                  pl.BlockSpec((d_a + d_b, D), fix), pl.BlockSpec((1, D), fix), pl.BlockSpec((1, D), fix),
                  pl.BlockSpec((E, D), fix), pl.BlockSpec((D, De), fix), pl.BlockSpec((D, De), fix),
                  pl.BlockSpec((De, D), fix)],
        out_specs=[pl.BlockSpec((tm, D // 2), row), pl.BlockSpec((tm, D), row),
                   pl.BlockSpec((E, tm), lambda i: (0, i))],
        out_shape=[jax.ShapeDtypeStruct((T, D // 2), jnp.uint32), jax.ShapeDtypeStruct((T, D), F32),
                   jax.ShapeDtypeStruct((E, T), F32)],
        compiler_params=_cparams("parallel"),
        name="post_attn",
    )(ya, yb, xt, gain_a, gain_b, wo_b, g1, b1, rw_b, wsg_b, wsu_b, wsd_b)


def _route_kernel(logit_ref, rbias_ref, idx_ref, gate_ref, rank_ref, cnt_ref, carry_ref):
    E, tm = logit_ref.shape
    neg = -jnp.inf

    @pl.when(pl.program_id(0) == 0)
    def _():
        carry_ref[...] = jnp.zeros_like(carry_ref)

    scores = jax.nn.sigmoid(logit_ref[...])
    sel = scores + rbias_ref[...]
    eidx = lax.broadcasted_iota(jnp.int32, (E, tm), 0).astype(F32)

    def first_argmax(v, ids):
        m = jnp.max(v, axis=0, keepdims=True)
        return m, jnp.min(jnp.where(v == m, ids, float(E)), axis=0, keepdims=True)

    grp_scores = []
    ids = lax.broadcasted_iota(jnp.int32, (EXPERTS_PER_GROUP, tm), 0).astype(F32)
    for g in range(N_GROUPS):
        v = sel[g * EXPERTS_PER_GROUP:(g + 1) * EXPERTS_PER_GROUP, :]
        m1, i1 = first_argmax(v, ids)
        m2 = jnp.max(jnp.where(ids == i1, neg, v), axis=0, keepdims=True)
        grp_scores.append(m1 + m2)
    parts = []
    for g in range(N_GROUPS):
        beaten = jnp.zeros((1, tm), jnp.int32)
        for o in range(N_GROUPS):
            if o != g:
                s, t = grp_scores[o], grp_scores[g]
                beaten = beaten + ((s > t) | ((s == t) & (o < g))).astype(jnp.int32)
        rows = slice(g * EXPERTS_PER_GROUP, (g + 1) * EXPERTS_PER_GROUP)
        parts.append(jnp.where(beaten < TOPK_GROUPS, sel[rows, :], neg))
    selm = jnp.concatenate(parts, axis=0)

    hits, gates, ids_k = [], [], []
    chosen = jnp.zeros((E, tm), F32)
    gate_sum = jnp.zeros((1, tm), F32)
    for k in range(TOP_K):
        _, ik = first_argmax(selm, eidx)
        hit = eidx == ik
        gk = jnp.sum(jnp.where(hit, scores, 0.0), axis=0, keepdims=True)
        selm = jnp.where(hit, neg, selm)
        chosen = jnp.where(hit, 1.0, chosen)
        gate_sum = gate_sum + gk
        hits.append(hit)
        gates.append(gk)
        ids_k.append(ik)

    row = lax.broadcasted_iota(jnp.int32, (tm, tm), 0)
    col = lax.broadcasted_iota(jnp.int32, (tm, tm), 1)
    earlier = (row < col).astype(BF16)
    before = _dot(chosen.astype(BF16), earlier) + carry_ref[...]
    carry_ref[...] = carry_ref[...] + jnp.sum(chosen, axis=1, keepdims=True)
    cnt_ref[...] = carry_ref[...].astype(jnp.int32)

    ranks = [jnp.sum(jnp.where(hit, before, 0.0), axis=0, keepdims=True) for hit in hits]
    idx_ref[...] = jnp.concatenate(ids_k, axis=0).astype(jnp.int32)
    rank_ref[...] = jnp.concatenate(ranks, axis=0).astype(jnp.int32)
    gate_ref[...] = jnp.concatenate(gates, axis=0) / gate_sum * ROUTED_SCALE


def _route(logits_t, rbias):
    E, T = logits_t.shape
    tm = TOK_BLOCK
    col = lambda i: (0, i)
    fix = lambda i: (0, 0)
    return pl.pallas_call(
        _route_kernel,
        grid=(T // tm,),
        in_specs=[pl.BlockSpec((E, tm), col), pl.BlockSpec((E, 1), fix)],
        out_specs=[pl.BlockSpec((TOP_K, tm), col), pl.BlockSpec((TOP_K, tm), col),
                   pl.BlockSpec((TOP_K, tm), col), pl.BlockSpec((E, 1), fix)],
        out_shape=[jax.ShapeDtypeStruct((TOP_K, T), jnp.int32), jax.ShapeDtypeStruct((TOP_K, T), F32),
                   jax.ShapeDtypeStruct((TOP_K, T), jnp.int32), jax.ShapeDtypeStruct((E, 1), jnp.int32)],
        scratch_shapes=[pltpu.VMEM((E, 1), F32)],
        compiler_params=_cparams("arbitrary"),
        name="route",
    )(logits_t, rbias)


def _dest_kernel(idx_ref, rank_ref, start_ref, dest_ref):
    K, tm = idx_ref.shape
    E = start_ref.shape[0]
    eidx = lax.broadcasted_iota(jnp.int32, (E, tm), 0)
    idx = idx_ref[...]
    start = start_ref[...]
    rows = [jnp.sum(jnp.where(eidx == idx[k:k + 1, :], start, 0.0), axis=0, keepdims=True) for k in range(K)]
    dest_ref[...] = jnp.concatenate(rows, axis=0).astype(jnp.int32) + rank_ref[...]


def _dest(idx, rank, start_f):
    K, T = idx.shape
    E = start_f.shape[0]
    tm = 2 * TOK_BLOCK
    col = lambda i: (0, i)
    return pl.pallas_call(
        _dest_kernel,
        grid=(T // tm,),
        in_specs=[pl.BlockSpec((K, tm), col), pl.BlockSpec((K, tm), col), pl.BlockSpec((E, 1), lambda i: (0, 0))],
        out_specs=pl.BlockSpec((K, tm), col),
        out_shape=jax.ShapeDtypeStruct((K, T), jnp.int32),
        compiler_params=_cparams("parallel"),
        name="dest_rows",
    )(idx, rank, start_f)


def _expert_kernel(first_ref, nblk_ref, nused_ref, xs_hbm, wg_ref, wu_ref, wd_ref, ys_hbm,
                   xbuf, ybuf, wgu_s, wd_s, in_sem, out_sem):
    ring = xbuf.shape[0]
    e = pl.program_id(0)
    De = wg_ref.shape[2]
    n = nblk_ref[e]
    g0 = first_ref[e]
    n_used = nused_ref[0]

    def rows(g):
        return pl.ds(pl.multiple_of(g * ROW_BLOCK, ROW_BLOCK), ROW_BLOCK)

    def in_copy(g, slot):
        return pltpu.make_async_copy(xs_hbm.at[rows(g), :], xbuf.at[slot], in_sem.at[slot])

    def out_copy(g, slot):
        return pltpu.make_async_copy(ybuf.at[slot], ys_hbm.at[rows(g), :], out_sem.at[slot])

    @pl.when(e == 0)
    def _():
        for g in range(ring - 1):
            @pl.when(g < n_used)
            def _():
                in_copy(g, g).start()

    @pl.when(n > 0)
    def _():
        wgu_s[:, :De] = wg_ref[0].astype(BF16)
        wgu_s[:, De:] = wu_ref[0].astype(BF16)
        wd_s[...] = wd_ref[0].astype(BF16)

        def block(j, c):
            g = g0 + j
            slot = g & (ring - 1)
            in_copy(g, slot).wait()

            @pl.when(g + ring - 1 < n_used)
            def _():
                in_copy(g + ring - 1, (g + ring - 1) & (ring - 1)).start()

            @pl.when(g >= ring)
            def _():
                out_copy(g - ring, slot).wait()

            half = ROW_BLOCK // 2
            dh = xbuf.shape[2]
            gus = []
            for r in range(2):
                lo, hi = _unpack_bf16_halves(xbuf[slot, r * half:(r + 1) * half, :])
                gus.append(_dot(lo.astype(BF16), wgu_s[:dh, :]) + _dot(hi.astype(BF16), wgu_s[dh:, :]))
            hs = [(_silu(gu[:, :De]) * gu[:, De:]).astype(BF16) for gu in gus]
            for r in range(2):
                ybuf[slot, r * half:(r + 1) * half, :] = _pack_bf16_halves(_dot(hs[r], wd_s[...]))
            out_copy(g, slot).start()
            return c

        lax.fori_loop(0, n, block, 0)

    @pl.when(e == pl.num_programs(0) - 1)
    def _():
        for back in range(1, ring + 1):
            @pl.when(n_used >= back)
            def _():
                out_copy(n_used - back, (n_used - back) & (ring - 1)).wait()


def _experts(first_block, n_blocks, n_used, xs, w_gate, w_up, w_down):
    n_rows, Dh = xs.shape
    D = 2 * Dh
    E, _, De = w_gate.shape
    grid_spec = pltpu.PrefetchScalarGridSpec(
        num_scalar_prefetch=3,
        grid=(E,),
        in_specs=[pl.BlockSpec(memory_space=pl.ANY),
                  pl.BlockSpec((1, D, De), lambda e, *_: (e, 0, 0)),
                  pl.BlockSpec((1, D, De), lambda e, *_: (e, 0, 0)),
                  pl.BlockSpec((1, De, D), lambda e, *_: (e, 0, 0))],
        out_specs=pl.BlockSpec(memory_space=pl.ANY),
        scratch_shapes=[pltpu.VMEM((EXPERT_RING, ROW_BLOCK, Dh), jnp.uint32),
                        pltpu.VMEM((EXPERT_RING, ROW_BLOCK, Dh), jnp.uint32),
                        pltpu.VMEM((D, 2 * De), BF16), pltpu.VMEM((De, D), BF16),
                        pltpu.SemaphoreType.DMA((EXPERT_RING,)), pltpu.SemaphoreType.DMA((EXPERT_RING,))],
    )
    return pl.pallas_call(
        _expert_kernel,
        grid_spec=grid_spec,
        out_shape=jax.ShapeDtypeStruct((n_rows, Dh), jnp.uint32),
        compiler_params=_cparams("arbitrary"),
        name="expert_ffn",
    )(first_block, n_blocks, n_used, xs, w_gate, w_up, w_down)


SC_WINDOW = 128


def _sc_mesh():
    return plsc.VectorSubcoreMesh(core_axis_name="core", subcore_axis_name="subcore")


def _sc_workers():
    info = plsc.get_sparse_core_info()
    return info.num_cores, info.num_cores * info.num_subcores


def _sc_worker_id(num_cores):
    return lax.axis_index("subcore") * num_cores + lax.axis_index("core")


def _sc_scatter_rows(x, dest_flat, n_rows):
    T, D = x.shape
    K = dest_flat.shape[0] // T
    nc, nw = _sc_workers()
    per_w = T // nw
    assert T % nw == 0 and per_w % SC_WINDOW == 0

    @functools.partial(
        pl.kernel, out_type=jax.ShapeDtypeStruct((n_rows, D), x.dtype), mesh=_sc_mesh(),
        scratch_types=[pltpu.VMEM((SC_WINDOW,), jnp.int32), pltpu.VMEM((SC_WINDOW, D), x.dtype)],
        name="sc_dispatch")
    def run(x_hbm, i_hbm, o_hbm, idx_v, rows_v):
        first = _sc_worker_id(nc) * per_w

        @pl.loop(0, per_w // SC_WINDOW)
        def _(c):
            base = first + c * SC_WINDOW
            pltpu.sync_copy(x_hbm.at[pl.ds(base, SC_WINDOW)], rows_v)
            for k in range(K):
                pltpu.sync_copy(i_hbm.at[pl.ds(k * T + base, SC_WINDOW)], idx_v)
                pltpu.sync_copy(rows_v, o_hbm.at[idx_v])

    return run(x, dest_flat)


def _sc_gather_rows(ys, dest_flat):
    N = dest_flat.shape[0]
    D = ys.shape[1]
    nc, nw = _sc_workers()
    per_w = N // nw
    assert N % nw == 0 and per_w % SC_WINDOW == 0

    @functools.partial(
        pl.kernel, out_type=jax.ShapeDtypeStruct((N, D), ys.dtype), mesh=_sc_mesh(),
        scratch_types=[pltpu.VMEM((SC_WINDOW,), jnp.int32), pltpu.VMEM((SC_WINDOW, D), ys.dtype)],
        name="sc_combine_gather")
    def run(y_hbm, i_hbm, o_hbm, idx_v, rows_v):
        first = _sc_worker_id(nc) * per_w

        @pl.loop(0, per_w // SC_WINDOW)
        def _(c):
            base = first + c * SC_WINDOW
            pltpu.sync_copy(i_hbm.at[pl.ds(base, SC_WINDOW)], idx_v)
            pltpu.sync_copy(y_hbm.at[idx_v], rows_v)
            pltpu.sync_copy(rows_v, o_hbm.at[pl.ds(base, SC_WINDOW)])

    return run(ys, dest_flat)


def _final_kernel(z_ref, gate_ref, base_ref, g_ref, b_ref, o_ref):
    gate = gate_ref[...]
    acc_lo = acc_hi = None
    for k in range(TOP_K):
        lo, hi = _unpack_bf16_halves(z_ref[k])
        g = gate[:, k:k + 1]
        acc_lo = g * lo if acc_lo is None else acc_lo + g * lo
        acc_hi = g * hi if acc_hi is None else acc_hi + g * hi
    acc = base_ref[...] + jnp.concatenate([acc_lo, acc_hi], axis=-1)
    o_ref[...] = _layer_norm(acc, g_ref[...], b_ref[...])


def _final(z, gate, base, g2, b2, first_token):
    T, D = base.shape
    tb = TOK_BLOCK
    off = first_token // tb
    return pl.pallas_call(
        _final_kernel,
        grid=(z.shape[1] // tb,),
        in_specs=[pl.BlockSpec((TOP_K, tb, D // 2), lambda i: (0, i, 0)),
                  pl.BlockSpec((tb, TOP_K), lambda i: (i + off, 0)),
                  pl.BlockSpec((tb, D), lambda i: (i + off, 0)),
                  pl.BlockSpec((1, D), lambda i: (0, 0)),
                  pl.BlockSpec((1, D), lambda i: (0, 0))],
        out_specs=pl.BlockSpec((tb, D), lambda i: (i + off, 0)),
        out_shape=jax.ShapeDtypeStruct((T, D), F32),
        input_output_aliases={2: 0},
        compiler_params=_cparams("parallel"),
        name="final_sum_ln",
    )(z, gate, base, g2, b2)


def _layer(x, w_in, rel_bias, gain_a, gain_b, w_out, ln1_g, ln1_b, router_w, router_bias,
           w_gate, w_up, w_down, ws_gate, ws_up, ws_down, ln2_g, ln2_b, alpha):
    B, S, D = x.shape
    T = B * S
    d_a = gain_a.shape[0]
    d_b = gain_b.shape[0]
    assert S % TILE_B == 0 and T % 512 == 0 and d_a % LANES == 0 and d_b % WIDTH_B == 0 and (3 * d_a) % WIDTH_B == 0
    xt = x.reshape(T, D)

    scale = HEAD_DIM ** -0.5
    col = jnp.arange(w_in.shape[1])
    is_q = (col < d_a) | ((col >= 3 * d_a) & (col < 3 * d_a + d_b))
    w_in_b = (w_in * jnp.where(is_q, scale, 1.0)[None, :]).astype(BF16)

    qkv = _qkv_proj(xt, w_in_b).reshape(B, S, -1)
    ya = _attn_a(qkv, _rel_bias_by_offset(rel_bias), B, S, d_a).reshape(T, d_a)
    yb = _attn_b(qkv, B, S, d_a, d_b).reshape(T, d_b)

    x1p, base, logits_t = _post(
        ya, yb, xt, gain_a[None], gain_b[None], w_out.astype(BF16), ln1_g[None], ln1_b[None],
        router_w.T.astype(BF16), ws_gate.astype(BF16), ws_up.astype(BF16), ws_down.astype(BF16), alpha)

    idx, gate, rank, cnt = _route(logits_t, router_bias[:, None].astype(F32))
    cnt = cnt[:, 0]
    padded = (cnt + ROW_BLOCK - 1) // ROW_BLOCK * ROW_BLOCK
    ends = jnp.cumsum(padded)
    start = (ends - padded).astype(jnp.int32)
    n_rows = T * TOP_K + N_EXPERTS * ROW_BLOCK
    first_block = start // ROW_BLOCK
    n_blocks = (padded // ROW_BLOCK).astype(jnp.int32)
    n_used = (ends[-1:] // ROW_BLOCK).astype(jnp.int32)
    dest = _dest(idx, rank, start.astype(F32)[:, None])

    xs = _sc_scatter_rows(x1p, dest.reshape(-1), n_rows)
    ys = _experts(first_block, n_blocks, n_used, xs, w_gate, w_up, w_down)
    gate_t = gate.T
    out = base
    tg = T // COMBINE_GROUPS
    for q in range(COMBINE_GROUPS):
        z = _sc_gather_rows(ys, dest[:, q * tg:(q + 1) * tg].reshape(-1)).reshape(TOP_K, tg, D // 2)
        out = _final(z, gate_t, out, ln2_g[None], ln2_b[None], q * tg)
    return out.reshape(B, S, D)


def kernel(x, w_in, rel_bias, gain_a, gain_b, w_out, ln1_g, ln1_b, router_w, router_bias,
           w_gate, w_up, w_down, ws_gate, ws_up, ws_down, ln2_g, ln2_b):
    depth = w_in.shape[0]
    alpha = (2 * depth) ** 0.25
    for l in range(depth):
        x = _layer(x, w_in[l], rel_bias[l], gain_a[l], gain_b[l], w_out[l], ln1_g[l], ln1_b[l],
                   router_w[l], router_bias[l], w_gate[l], w_up[l], w_down[l],
                   ws_gate[l], ws_up[l], ws_down[l], ln2_g[l], ln2_b[l], alpha)
    return x
```

```python
import functools

import jax
import jax.numpy as jnp
from jax import lax
from jax.experimental import pallas as pl
from jax.experimental.pallas import tpu as pltpu
from jax.experimental.pallas import tpu_sc as plsc

CHUNK = 64
HEAD_DIM = 64
LEFT_CHUNKS = 8
MAX_REL = 128
N_EXPERTS = 256
TOP_K = 8
N_GROUPS = 8
TOPK_GROUPS = 4
EXPERTS_PER_GROUP = N_EXPERTS // N_GROUPS
ROUTED_SCALE = 2.5
LN_EPS = 1e-5
RMS_EPS = 1e-6

LANES = 128
SUBLANES = 8
HEADS_PER_BLOCK = LANES // HEAD_DIM
QBLK_A = 2 * CHUNK
BAND_A = (LEFT_CHUNKS + 2) * CHUNK
BIAS_W = BAND_A + QBLK_A
TILE_B = 256
WIDTH_B = 2 * HEAD_DIM
ROW_BLOCK = 256
EXPERT_RING = 8
TOK_BLOCK = 256
COMBINE_GROUPS = 4
MASK_VALUE = -1e30
VMEM_LIMIT = 48 * 1024 * 1024

F32 = jnp.float32
BF16 = jnp.bfloat16


def _cparams(*sem, flags=None):
    return pltpu.CompilerParams(dimension_semantics=sem, vmem_limit_bytes=VMEM_LIMIT, flags=flags)


def _dot(a, b):
    return jnp.dot(a, b, preferred_element_type=F32)


def _dot_nt(a, b):
    return lax.dot_general(a, b, (((1,), (1,)), ((), ())), preferred_element_type=F32)


def _pack_bf16_halves(x):
    half = x.shape[1] // 2
    bits = lax.bitcast_convert_type(x.astype(BF16).astype(F32), jnp.uint32)
    return (bits[:, :half] >> 16) | (bits[:, half:] & jnp.uint32(0xFFFF0000))


def _unpack_bf16_halves(w):
    lo = lax.bitcast_convert_type(w << 16, F32)
    hi = lax.bitcast_convert_type(w & jnp.uint32(0xFFFF0000), F32)
    return lo, hi


def _qkv_kernel(x_ref, w_ref, o_ref, *, col_chunk):
    xb = x_ref[...].astype(BF16)
    for n in range(w_ref.shape[1] // col_chunk):
        cols = slice(n * col_chunk, (n + 1) * col_chunk)
        o_ref[:, cols] = _dot(xb, w_ref[:, cols]).astype(BF16)


def _qkv_proj(xt, w_b):
    T, D = xt.shape
    N = w_b.shape[1]
    tm = 512
    return pl.pallas_call(
        functools.partial(_qkv_kernel, col_chunk=512),
        grid=(T // tm,),
        in_specs=[pl.BlockSpec((tm, D), lambda i: (i, 0)),
                  pl.BlockSpec((D, N), lambda i: (0, 0))],
        out_specs=pl.BlockSpec((tm, N), lambda i: (i, 0)),
        out_shape=jax.ShapeDtypeStruct((T, N), BF16),
        compiler_params=_cparams("parallel"),
        name="qkv_proj",
    )(xt, w_b)


def _attn_a_kernel(q_ref, k_ref, v_ref, w_ref, o_ref, bias_ref):
    S = q_ref.shape[0]
    nblk = S // QBLK_A
    lead = LEFT_CHUNKS * CHUNK
    n_edge = min(lead // QBLK_A, nblk)

    qc = lax.broadcasted_iota(jnp.int32, (QBLK_A, BAND_A), 0) // CHUNK
    kc = lax.broadcasted_iota(jnp.int32, (QBLK_A, BAND_A), 1) // CHUNK
    allowed = (kc >= qc) & (kc <= qc + LEFT_CHUNKS)
    for h in range(HEADS_PER_BLOCK):
        wb = jnp.broadcast_to(w_ref[h], (QBLK_A, BIAS_W))
        toeplitz = pltpu.roll(wb, BIAS_W - (QBLK_A - 1), 1, stride=1, stride_axis=0)
        bias_ref[h] = jnp.where(allowed, toeplitz[:, :BAND_A], MASK_VALUE)

    lane = lax.broadcasted_iota(jnp.int32, (QBLK_A, LANES), 1)
    head_of_lane = lane // HEAD_DIM

    def blocks(specs):
        work = []
        for p, kstart, nk, bias_off in specs:
            q = q_ref[pl.ds(p * QBLK_A, QBLK_A), :]
            k = k_ref[pl.ds(kstart, nk), :]
            v = v_ref[pl.ds(kstart, nk), :]
            for h in range(HEADS_PER_BLOCK):
                qh = jnp.where(head_of_lane == h, q, jnp.zeros_like(q))
                work.append(dict(s=_dot_nt(qh, k), v=v, bias=bias_ref[h, :, bias_off:bias_off + nk]))
        for w in work:
            s = w["s"] + w["bias"]
            e = jnp.exp(s - jnp.max(s, axis=-1, keepdims=True))
            w["l"] = jnp.sum(e, axis=-1, keepdims=True)
            w["e"] = e.astype(BF16)
        outs = [_dot(w["e"], w["v"]) / w["l"] for w in work]
        for i, spec in enumerate(specs):
            o = outs[i * HEADS_PER_BLOCK]
            for h in range(1, HEADS_PER_BLOCK):
                o = jnp.where(head_of_lane == h, outs[i * HEADS_PER_BLOCK + h], o)
            o_ref[pl.ds(spec[0] * QBLK_A, QBLK_A), :] = o

    blocks([(p, 0, (p + 1) * QBLK_A, lead - p * QBLK_A) for p in range(n_edge)])

    def full(p):
        return (p, pl.multiple_of(p * QBLK_A - lead, QBLK_A), BAND_A, 0)

    n_full = nblk - n_edge
    if n_full % 2:
        blocks([full(n_edge)])
    if n_full >= 2:
        def body(i, c):
            p = n_edge + n_full % 2 + 2 * i
            blocks([full(p), full(p + 1)])
            return c
        lax.fori_loop(0, n_full // 2, body, 0)


def _attn_a(qkv, bias_w, B, S, d_a):
    n_hb = d_a // LANES
    seg = d_a // LANES
    return pl.pallas_call(
        _attn_a_kernel,
        grid=(B, n_hb),
        in_specs=[pl.BlockSpec((None, S, LANES), lambda b, h: (b, 0, h)),
                  pl.BlockSpec((None, S, LANES), lambda b, h: (b, 0, seg + h)),
                  pl.BlockSpec((None, S, LANES), lambda b, h: (b, 0, 2 * seg + h)),
                  pl.BlockSpec((HEADS_PER_BLOCK, 1, BIAS_W), lambda b, h: (h, 0, 0))],
        out_specs=pl.BlockSpec((None, S, LANES), lambda b, h: (b, 0, h)),
        out_shape=jax.ShapeDtypeStruct((B, S, d_a), F32),
        scratch_shapes=[pltpu.VMEM((HEADS_PER_BLOCK, QBLK_A, BAND_A), F32)],
        compiler_params=_cparams("parallel", "parallel"),
        name="attn_chunked",
    )(qkv, qkv, qkv, bias_w)


def _rel_bias_by_offset(rel_bias):
    dist = jnp.clip(BAND_A - 1 - jnp.arange(BIAS_W), -MAX_REL, MAX_REL) + MAX_REL
    return rel_bias[:, dist].astype(F32)[:, None, :]


def _attn_b_kernel(q_ref, k_ref, v_ref, o_ref):
    qi = pl.program_id(2)
    t = TILE_B
    row = lax.broadcasted_iota(jnp.int32, (t, t), 0)
    col = lax.broadcasted_iota(jnp.int32, (t, t), 1)
    from_s = (row >= col).astype(BF16)
    from_s2 = jnp.concatenate([from_s, from_s], axis=0)
    causal = col < row
    q = q_ref[...]
    n_heads = q_ref.shape[1] // HEAD_DIM
    head_of_lane = lax.broadcasted_iota(jnp.int32, q.shape, 1) // HEAD_DIM
    qs = [jnp.where(head_of_lane == h, q, jnp.zeros_like(q)) for h in range(n_heads)]

    def scores(js):
        zs = []
        for j in js:
            kt = k_ref[pl.ds(pl.multiple_of(j * t, t), t), :]
            zs += [_dot_nt(qh, kt) for qh in qs]
        return zs

    def tiles(js, zs, state, first_is_diag=False):
        accs, carries = list(state[0::2]), list(state[1::2])
        work = []
        for n, j in enumerate(js):
            vt = v_ref[pl.ds(pl.multiple_of(j * t, t), t), :]
            for h in range(n_heads):
                work.append(dict(h=h, diag=first_is_diag and n == 0, v=vt, z=zs[n * n_heads + h]))
        for w in work:
            z = w["z"]
            sp = jnp.maximum(z, 0.0) + jnp.log(1.0 + jnp.exp(-jnp.abs(z)))
            if w["diag"]:
                sp = jnp.where(causal, sp, 0.0)
            hi = sp.astype(BF16)
            lo = (sp - hi.astype(F32)).astype(BF16)
            w["suffix"] = _dot(jnp.concatenate([hi, lo], axis=-1), from_s2)
            w["rowsum"] = jnp.sum(sp, axis=-1, keepdims=True)
        for w in work:
            h = w["h"]
            if w["diag"]:
                a = jnp.where(causal, jnp.exp(w["z"] - w["suffix"]), 0.0)
                accs[h] = _dot(a.astype(BF16), w["v"])
                carries[h] = w["rowsum"]
            else:
                a = jnp.exp(w["z"] - w["suffix"] - carries[h])
                accs[h] = accs[h] + _dot(a.astype(BF16), w["v"])
                carries[h] = carries[h] + w["rowsum"]
        return tuple(x for pair in zip(accs, carries) for x in pair)

    empty = (None,) * (2 * n_heads)
    state = lax.cond(qi % 2 == 1,
                     lambda: tiles([qi, qi - 1], scores([qi, qi - 1]), empty, True),
                     lambda: tiles([qi], scores([qi]), empty, True))
    first = qi - 1 - qi % 2

    def body(it, st):
        j = first - 2 * it
        return tiles([j, j - 1], scores([j, j - 1]), st)

    state = lax.fori_loop(0, qi // 2, body, state)
    out = state[0]
    for h in range(1, n_heads):
        out = jnp.where(head_of_lane == h, state[2 * h], out)
    o_ref[...] = out


def _attn_b(qkv, B, S, d_a, d_b):
    w = WIDTH_B
    n_hb = d_b // w
    base = 3 * d_a // w
    seg = d_b // w
    return pl.pallas_call(
        _attn_b_kernel,
        grid=(B, n_hb, S // TILE_B),
        in_specs=[pl.BlockSpec((None, TILE_B, w), lambda b, h, i: (b, i, base + h)),
                  pl.BlockSpec((None, S, w), lambda b, h, i: (b, 0, base + seg + h)),
                  pl.BlockSpec((None, S, w), lambda b, h, i: (b, 0, base + 2 * seg + h))],
        out_specs=pl.BlockSpec((None, TILE_B, w), lambda b, h, i: (b, i, h)),
        out_shape=jax.ShapeDtypeStruct((B, S, d_b), F32),
        compiler_params=_cparams("parallel", "parallel", "parallel"),
        name="attn_stickbreak",
    )(qkv, qkv, qkv)


def _layer_norm(r, g, b):
    mu = jnp.mean(r, axis=-1, keepdims=True)
    c = r - mu
    var = jnp.mean(c * c, axis=-1, keepdims=True)
    return c * lax.rsqrt(var + LN_EPS) * g + b


def _rms_norm(y, g):
    ms = jnp.mean(y * y, axis=-1, keepdims=True)
    return y * lax.rsqrt(ms + RMS_EPS) * g


def _silu(g):
    return g * jax.nn.sigmoid(g)


def _post_kernel(ya_ref, yb_ref, x_ref, ga_ref, gb_ref, wo_ref, g1_ref, b1_ref, rw_ref,
                 wsg_ref, wsu_ref, wsd_ref, x1_ref, base_ref, logit_ref, *, alpha):
    d_a = ya_ref.shape[1]
    na = _rms_norm(ya_ref[...], ga_ref[...]).astype(BF16)
    nb = _rms_norm(yb_ref[...], gb_ref[...]).astype(BF16)
    h = _dot(na, wo_ref[:d_a, :]) + _dot(nb, wo_ref[d_a:, :])
    x1 = _layer_norm(alpha * x_ref[...] + h, g1_ref[...], b1_ref[...])
    x1_ref[...] = _pack_bf16_halves(x1)
    xb = x1.astype(BF16)
    logit_ref[...] = _dot_nt(rw_ref[...], xb)
    hs = _silu(_dot(xb, wsg_ref[...])) * _dot(xb, wsu_ref[...])
    base_ref[...] = alpha * x1 + _dot(hs.astype(BF16), wsd_ref[...])


def _post(ya, yb, xt, gain_a, gain_b, wo_b, g1, b1, rw_b, wsg_b, wsu_b, wsd_b, alpha):
    T, D = xt.shape
    d_a, d_b = ya.shape[1], yb.shape[1]
    E = rw_b.shape[0]
    De = wsg_b.shape[1]
    tm = 4 * TOK_BLOCK
    row = lambda i: (i, 0)
    fix = lambda i: (0, 0)
    return pl.pallas_call(
        functools.partial(_post_kernel, alpha=alpha),
        grid=(T // tm,),
        in_specs=[pl.BlockSpec((tm, d_a), row), pl.BlockSpec((tm, d_b), row), pl.BlockSpec((tm, D), row),
                  pl.BlockSpec((1, d_a), fix), pl.BlockSpec((1, d_b), fix),
                  pl.BlockSpec((d_a + d_b, D), fix), pl.BlockSpec((1, D), fix), pl.BlockSpec((1, D), fix),
                  pl.BlockSpec((E, D), fix), pl.BlockSpec((D, De), fix), pl.BlockSpec((D, De), fix),
                  pl.BlockSpec((De, D), fix)],
        out_specs=[pl.BlockSpec((tm, D // 2), row), pl.BlockSpec((tm, D), row),
                   pl.BlockSpec((E, tm), lambda i: (0, i))],
        out_shape=[jax.ShapeDtypeStruct((T, D // 2), jnp.uint32), jax.ShapeDtypeStruct((T, D), F32),
                   jax.ShapeDtypeStruct((E, T), F32)],
        compiler_params=_cparams("parallel"),
        name="post_attn",
    )(ya, yb, xt, gain_a, gain_b, wo_b, g1, b1, rw_b, wsg_b, wsu_b, wsd_b)


def _route_kernel(logit_ref, rbias_ref, idx_ref, gate_ref, rank_ref, cnt_ref, carry_ref):
    E, tm = logit_ref.shape
    neg = -jnp.inf

    @pl.when(pl.program_id(0) == 0)
    def _():
        carry_ref[...] = jnp.zeros_like(carry_ref)

    scores = jax.nn.sigmoid(logit_ref[...])
    sel = scores + rbias_ref[...]
    eidx = lax.broadcasted_iota(jnp.int32, (E, tm), 0).astype(F32)

    def first_argmax(v, ids):
        m = jnp.max(v, axis=0, keepdims=True)
        return m, jnp.min(jnp.where(v == m, ids, float(E)), axis=0, keepdims=True)

    grp_scores = []
    ids = lax.broadcasted_iota(jnp.int32, (EXPERTS_PER_GROUP, tm), 0).astype(F32)
    for g in range(N_GROUPS):
        v = sel[g * EXPERTS_PER_GROUP:(g + 1) * EXPERTS_PER_GROUP, :]
        m1, i1 = first_argmax(v, ids)
        m2 = jnp.max(jnp.where(ids == i1, neg, v), axis=0, keepdims=True)
        grp_scores.append(m1 + m2)
    parts = []
    for g in range(N_GROUPS):
        beaten = jnp.zeros((1, tm), jnp.int32)
        for o in range(N_GROUPS):
            if o != g:
                s, t = grp_scores[o], grp_scores[g]
                beaten = beaten + ((s > t) | ((s == t) & (o < g))).astype(jnp.int32)
        rows = slice(g * EXPERTS_PER_GROUP, (g + 1) * EXPERTS_PER_GROUP)
        parts.append(jnp.where(beaten < TOPK_GROUPS, sel[rows, :], neg))
    selm = jnp.concatenate(parts, axis=0)

    hits, gates, ids_k = [], [], []
    chosen = jnp.zeros((E, tm), F32)
    gate_sum = jnp.zeros((1, tm), F32)
    for k in range(TOP_K):
        _, ik = first_argmax(selm, eidx)
        hit = eidx == ik
        gk = jnp.sum(jnp.where(hit, scores, 0.0), axis=0, keepdims=True)
        selm = jnp.where(hit, neg, selm)
        chosen = jnp.where(hit, 1.0, chosen)
        gate_sum = gate_sum + gk
        hits.append(hit)
        gates.append(gk)
        ids_k.append(ik)

    row = lax.broadcasted_iota(jnp.int32, (tm, tm), 0)
    col = lax.broadcasted_iota(jnp.int32, (tm, tm), 1)
    earlier = (row < col).astype(BF16)
    before = _dot(chosen.astype(BF16), earlier) + carry_ref[...]
    carry_ref[...] = carry_ref[...] + jnp.sum(chosen, axis=1, keepdims=True)
    cnt_ref[...] = carry_ref[...].astype(jnp.int32)

    ranks = [jnp.sum(jnp.where(hit, before, 0.0), axis=0, keepdims=True) for hit in hits]
    idx_ref[...] = jnp.concatenate(ids_k, axis=0).astype(jnp.int32)
    rank_ref[...] = jnp.concatenate(ranks, axis=0).astype(jnp.int32)
    gate_ref[...] = jnp.concatenate(gates, axis=0) / gate_sum * ROUTED_SCALE


def _route(logits_t, rbias):
    E, T = logits_t.shape
    tm = TOK_BLOCK
    col = lambda i: (0, i)
    fix = lambda i: (0, 0)
    return pl.pallas_call(
        _route_kernel,
        grid=(T // tm,),
        in_specs=[pl.BlockSpec((E, tm), col), pl.BlockSpec((E, 1), fix)],
        out_specs=[pl.BlockSpec((TOP_K, tm), col), pl.BlockSpec((TOP_K, tm), col),
                   pl.BlockSpec((TOP_K, tm), col), pl.BlockSpec((E, 1), fix)],
        out_shape=[jax.ShapeDtypeStruct((TOP_K, T), jnp.int32), jax.ShapeDtypeStruct((TOP_K, T), F32),
                   jax.ShapeDtypeStruct((TOP_K, T), jnp.int32), jax.ShapeDtypeStruct((E, 1), jnp.int32)],
        scratch_shapes=[pltpu.VMEM((E, 1), F32)],
        compiler_params=_cparams("arbitrary"),
        name="route",
    )(logits_t, rbias)


def _dest_kernel(idx_ref, rank_ref, start_ref, dest_ref):
    K, tm = idx_ref.shape
    E = start_ref.shape[0]
    eidx = lax.broadcasted_iota(jnp.int32, (E, tm), 0)
    idx = idx_ref[...]
    start = start_ref[...]
    rows = [jnp.sum(jnp.where(eidx == idx[k:k + 1, :], start, 0.0), axis=0, keepdims=True) for k in range(K)]
    dest_ref[...] = jnp.concatenate(rows, axis=0).astype(jnp.int32) + rank_ref[...]


def _dest(idx, rank, start_f):
    K, T = idx.shape
    E = start_f.shape[0]
    tm = 2 * TOK_BLOCK
    col = lambda i: (0, i)
    return pl.pallas_call(
        _dest_kernel,
        grid=(T // tm,),
        in_specs=[pl.BlockSpec((K, tm), col), pl.BlockSpec((K, tm), col), pl.BlockSpec((E, 1), lambda i: (0, 0))],
        out_specs=pl.BlockSpec((K, tm), col),
        out_shape=jax.ShapeDtypeStruct((K, T), jnp.int32),
        compiler_params=_cparams("parallel"),
        name="dest_rows",
    )(idx, rank, start_f)


def _expert_kernel(first_ref, nblk_ref, nused_ref, xs_hbm, wg_ref, wu_ref, wd_ref, ys_hbm,
                   xbuf, ybuf, wgu_s, wd_s, in_sem, out_sem):
    ring = xbuf.shape[0]
    e = pl.program_id(0)
    De = wg_ref.shape[2]
    n = nblk_ref[e]
    g0 = first_ref[e]
    n_used = nused_ref[0]

    def rows(g):
        return pl.ds(pl.multiple_of(g * ROW_BLOCK, ROW_BLOCK), ROW_BLOCK)

    def in_copy(g, slot):
        return pltpu.make_async_copy(xs_hbm.at[rows(g), :], xbuf.at[slot], in_sem.at[slot])

    def out_copy(g, slot):
        return pltpu.make_async_copy(ybuf.at[slot], ys_hbm.at[rows(g), :], out_sem.at[slot])

    @pl.when(e == 0)
    def _():
        for g in range(ring - 1):
            @pl.when(g < n_used)
            def _():
                in_copy(g, g).start()

    @pl.when(n > 0)
    def _():
        wgu_s[:, :De] = wg_ref[0].astype(BF16)
        wgu_s[:, De:] = wu_ref[0].astype(BF16)
        wd_s[...] = wd_ref[0].astype(BF16)

        def block(j, c):
            g = g0 + j
            slot = g & (ring - 1)
            in_copy(g, slot).wait()

            @pl.when(g + ring - 1 < n_used)
            def _():
                in_copy(g + ring - 1, (g + ring - 1) & (ring - 1)).start()

            @pl.when(g >= ring)
            def _():
                out_copy(g - ring, slot).wait()

            half = ROW_BLOCK // 2
            dh = xbuf.shape[2]
            gus = []
            for r in range(2):
                lo, hi = _unpack_bf16_halves(xbuf[slot, r * half:(r + 1) * half, :])
                gus.append(_dot(lo.astype(BF16), wgu_s[:dh, :]) + _dot(hi.astype(BF16), wgu_s[dh:, :]))
            hs = [(_silu(gu[:, :De]) * gu[:, De:]).astype(BF16) for gu in gus]
            for r in range(2):
                ybuf[slot, r * half:(r + 1) * half, :] = _pack_bf16_halves(_dot(hs[r], wd_s[...]))
            out_copy(g, slot).start()
            return c

        lax.fori_loop(0, n, block, 0)

    @pl.when(e == pl.num_programs(0) - 1)
    def _():
        for back in range(1, ring + 1):
            @pl.when(n_used >= back)
            def _():
                out_copy(n_used - back, (n_used - back) & (ring - 1)).wait()


def _experts(first_block, n_blocks, n_used, xs, w_gate, w_up, w_down):
    n_rows, Dh = xs.shape
    D = 2 * Dh
    E, _, De = w_gate.shape
    grid_spec = pltpu.PrefetchScalarGridSpec(
        num_scalar_prefetch=3,
        grid=(E,),
        in_specs=[pl.BlockSpec(memory_space=pl.ANY),
                  pl.BlockSpec((1, D, De), lambda e, *_: (e, 0, 0)),
                  pl.BlockSpec((1, D, De), lambda e, *_: (e, 0, 0)),
                  pl.BlockSpec((1, De, D), lambda e, *_: (e, 0, 0))],
        out_specs=pl.BlockSpec(memory_space=pl.ANY),
        scratch_shapes=[pltpu.VMEM((EXPERT_RING, ROW_BLOCK, Dh), jnp.uint32),
                        pltpu.VMEM((EXPERT_RING, ROW_BLOCK, Dh), jnp.uint32),
                        pltpu.VMEM((D, 2 * De), BF16), pltpu.VMEM((De, D), BF16),
                        pltpu.SemaphoreType.DMA((EXPERT_RING,)), pltpu.SemaphoreType.DMA((EXPERT_RING,))],
    )
    return pl.pallas_call(
        _expert_kernel,
        grid_spec=grid_spec,
        out_shape=jax.ShapeDtypeStruct((n_rows, Dh), jnp.uint32),
        compiler_params=_cparams("arbitrary"),
        name="expert_ffn",
    )(first_block, n_blocks, n_used, xs, w_gate, w_up, w_down)


SC_WINDOW = 128


def _sc_mesh():
    return plsc.VectorSubcoreMesh(core_axis_name="core", subcore_axis_name="subcore")


def _sc_workers():
    info = plsc.get_sparse_core_info()
    return info.num_cores, info.num_cores * info.num_subcores


def _sc_worker_id(num_cores):
    return lax.axis_index("subcore") * num_cores + lax.axis_index("core")


def _sc_scatter_rows(x, dest_flat, n_rows):
    T, D = x.shape
    K = dest_flat.shape[0] // T
    nc, nw = _sc_workers()
    per_w = T // nw
    assert T % nw == 0 and per_w % SC_WINDOW == 0

    @functools.partial(
        pl.kernel, out_type=jax.ShapeDtypeStruct((n_rows, D), x.dtype), mesh=_sc_mesh(),
        scratch_types=[pltpu.VMEM((SC_WINDOW,), jnp.int32), pltpu.VMEM((SC_WINDOW, D), x.dtype)],
        name="sc_dispatch")
    def run(x_hbm, i_hbm, o_hbm, idx_v, rows_v):
        first = _sc_worker_id(nc) * per_w

        @pl.loop(0, per_w // SC_WINDOW)
        def _(c):
            base = first + c * SC_WINDOW
            pltpu.sync_copy(x_hbm.at[pl.ds(base, SC_WINDOW)], rows_v)
            for k in range(K):
                pltpu.sync_copy(i_hbm.at[pl.ds(k * T + base, SC_WINDOW)], idx_v)
                pltpu.sync_copy(rows_v, o_hbm.at[idx_v])

    return run(x, dest_flat)


def _sc_gather_rows(ys, dest_flat):
    N = dest_flat.shape[0]
    D = ys.shape[1]
    nc, nw = _sc_workers()
    per_w = N // nw
    assert N % nw == 0 and per_w % SC_WINDOW == 0

    @functools.partial(
        pl.kernel, out_type=jax.ShapeDtypeStruct((N, D), ys.dtype), mesh=_sc_mesh(),
        scratch_types=[pltpu.VMEM((SC_WINDOW,), jnp.int32), pltpu.VMEM((SC_WINDOW, D), ys.dtype)],
        name="sc_combine_gather")
    def run(y_hbm, i_hbm, o_hbm, idx_v, rows_v):
        first = _sc_worker_id(nc) * per_w

        @pl.loop(0, per_w // SC_WINDOW)
        def _(c):
            base = first + c * SC_WINDOW
            pltpu.sync_copy(i_hbm.at[pl.ds(base, SC_WINDOW)], idx_v)
            pltpu.sync_copy(y_hbm.at[idx_v], rows_v)
            pltpu.sync_copy(rows_v, o_hbm.at[pl.ds(base, SC_WINDOW)])

    return run(ys, dest_flat)


def _final_kernel(z_ref, gate_ref, base_ref, g_ref, b_ref, o_ref):
    gate = gate_ref[...]
    acc_lo = acc_hi = None
    for k in range(TOP_K):
        lo, hi = _unpack_bf16_halves(z_ref[k])
        g = gate[:, k:k + 1]
        acc_lo = g * lo if acc_lo is None else acc_lo + g * lo
        acc_hi = g * hi if acc_hi is None else acc_hi + g * hi
    acc = base_ref[...] + jnp.concatenate([acc_lo, acc_hi], axis=-1)
    o_ref[...] = _layer_norm(acc, g_ref[...], b_ref[...])


def _final(z, gate, base, g2, b2, first_token):
    T, D = base.shape
    tb = TOK_BLOCK
    off = first_token // tb
    return pl.pallas_call(
        _final_kernel,
        grid=(z.shape[1] // tb,),
        in_specs=[pl.BlockSpec((TOP_K, tb, D // 2), lambda i: (0, i, 0)),
                  pl.BlockSpec((tb, TOP_K), lambda i: (i + off, 0)),
                  pl.BlockSpec((tb, D), lambda i: (i + off, 0)),
                  pl.BlockSpec((1, D), lambda i: (0, 0)),
                  pl.BlockSpec((1, D), lambda i: (0, 0))],
        out_specs=pl.BlockSpec((tb, D), lambda i: (i + off, 0)),
        out_shape=jax.ShapeDtypeStruct((T, D), F32),
        input_output_aliases={2: 0},
        compiler_params=_cparams("parallel"),
        name="final_sum_ln",
    )(z, gate, base, g2, b2)


def _layer(x, w_in, rel_bias, gain_a, gain_b, w_out, ln1_g, ln1_b, router_w, router_bias,
           w_gate, w_up, w_down, ws_gate, ws_up, ws_down, ln2_g, ln2_b, alpha):
    B, S, D = x.shape
    T = B * S
    d_a = gain_a.shape[0]
    d_b = gain_b.shape[0]
    assert S % TILE_B == 0 and T % 512 == 0 and d_a % LANES == 0 and d_b % WIDTH_B == 0 and (3 * d_a) % WIDTH_B == 0
    xt = x.reshape(T, D)

    scale = HEAD_DIM ** -0.5
    col = jnp.arange(w_in.shape[1])
    is_q = (col < d_a) | ((col >= 3 * d_a) & (col < 3 * d_a + d_b))
    w_in_b = (w_in * jnp.where(is_q, scale, 1.0)[None, :]).astype(BF16)

    qkv = _qkv_proj(xt, w_in_b).reshape(B, S, -1)
    ya = _attn_a(qkv, _rel_bias_by_offset(rel_bias), B, S, d_a).reshape(T, d_a)
    yb = _attn_b(qkv, B, S, d_a, d_b).reshape(T, d_b)

    x1p, base, logits_t = _post(
        ya, yb, xt, gain_a[None], gain_b[None], w_out.astype(BF16), ln1_g[None], ln1_b[None],
        router_w.T.astype(BF16), ws_gate.astype(BF16), ws_up.astype(BF16), ws_down.astype(BF16), alpha)

    idx, gate, rank, cnt = _route(logits_t, router_bias[:, None].astype(F32))
    cnt = cnt[:, 0]
    padded = (cnt + ROW_BLOCK - 1) // ROW_BLOCK * ROW_BLOCK
    ends = jnp.cumsum(padded)
    start = (ends - padded).astype(jnp.int32)
    n_rows = T * TOP_K + N_EXPERTS * ROW_BLOCK
    first_block = start // ROW_BLOCK
    n_blocks = (padded // ROW_BLOCK).astype(jnp.int32)
    n_used = (ends[-1:] // ROW_BLOCK).astype(jnp.int32)
    dest = _dest(idx, rank, start.astype(F32)[:, None])

    xs = _sc_scatter_rows(x1p, dest.reshape(-1), n_rows)
    ys = _experts(first_block, n_blocks, n_used, xs, w_gate, w_up, w_down)
    gate_t = gate.T
    out = base
    tg = T // COMBINE_GROUPS
    for q in range(COMBINE_GROUPS):
        z = _sc_gather_rows(ys, dest[:, q * tg:(q + 1) * tg].reshape(-1)).reshape(TOP_K, tg, D // 2)
        out = _final(z, gate_t, out, ln2_g[None], ln2_b[None], q * tg)
    return out.reshape(B, S, D)


def kernel(x, w_in, rel_bias, gain_a, gain_b, w_out, ln1_g, ln1_b, router_w, router_bias,
           w_gate, w_up, w_down, ws_gate, ws_up, ws_down, ln2_g, ln2_b):
    depth = w_in.shape[0]
    alpha = (2 * depth) ** 0.25
    for l in range(depth):
        x = _layer(x, w_in[l], rel_bias[l], gain_a[l], gain_b[l], w_out[l], ln1_g[l], ln1_b[l],
                   router_w[l], router_bias[l], w_gate[l], w_up[l], w_down[l],
                   ws_gate[l], ws_up[l], ws_down[l], ln2_g[l], ln2_b[l], alpha)
    return x
```

```python
import functools

import jax
import jax.numpy as jnp
from jax import lax
from jax.experimental import pallas as pl
from jax.experimental.pallas import tpu as pltpu
from jax.experimental.pallas import tpu_sc as plsc

CHUNK = 64
HEAD_DIM = 64
LEFT_CHUNKS = 8
MAX_REL = 128
N_EXPERTS = 256
TOP_K = 8
N_GROUPS = 8
TOPK_GROUPS = 4
EXPERTS_PER_GROUP = N_EXPERTS // N_GROUPS
ROUTED_SCALE = 2.5
LN_EPS = 1e-5
RMS_EPS = 1e-6

LANES = 128
SUBLANES = 8
HEADS_PER_BLOCK = LANES // HEAD_DIM
QBLK_A = 2 * CHUNK
BAND_A = (LEFT_CHUNKS + 2) * CHUNK
BIAS_W = BAND_A + QBLK_A
TILE_B = 256
WIDTH_B = 2 * HEAD_DIM
ROW_BLOCK = 256
EXPERT_RING = 8
TOK_BLOCK = 256
COMBINE_GROUPS = 4
MASK_VALUE = -1e30
VMEM_LIMIT = 48 * 1024 * 1024

F32 = jnp.float32
BF16 = jnp.bfloat16


def _cparams(*sem, flags=None):
    return pltpu.CompilerParams(dimension_semantics=sem, vmem_limit_bytes=VMEM_LIMIT, flags=flags)


def _dot(a, b):
    return jnp.dot(a, b, preferred_element_type=F32)


def _dot_nt(a, b):
    return lax.dot_general(a, b, (((1,), (1,)), ((), ())), preferred_element_type=F32)


def _pack_bf16_halves(x):
    half = x.shape[1] // 2
    bits = lax.bitcast_convert_type(x.astype(BF16).astype(F32), jnp.uint32)
    return (bits[:, :half] >> 16) | (bits[:, half:] & jnp.uint32(0xFFFF0000))


def _unpack_bf16_halves(w):
    lo = lax.bitcast_convert_type(w << 16, F32)
    hi = lax.bitcast_convert_type(w & jnp.uint32(0xFFFF0000), F32)
    return lo, hi


def _qkv_kernel(x_ref, w_ref, o_ref, *, col_chunk):
    xb = x_ref[...].astype(BF16)
    for n in range(w_ref.shape[1] // col_chunk):
        cols = slice(n * col_chunk, (n + 1) * col_chunk)
        o_ref[:, cols] = _dot(xb, w_ref[:, cols]).astype(BF16)


def _qkv_proj(xt, w_b):
    T, D = xt.shape
    N = w_b.shape[1]
    tm = 512
    return pl.pallas_call(
        functools.partial(_qkv_kernel, col_chunk=512),
        grid=(T // tm,),
        in_specs=[pl.BlockSpec((tm, D), lambda i: (i, 0)),
                  pl.BlockSpec((D, N), lambda i: (0, 0))],
        out_specs=pl.BlockSpec((tm, N), lambda i: (i, 0)),
        out_shape=jax.ShapeDtypeStruct((T, N), BF16),
        compiler_params=_cparams("parallel"),
        name="qkv_proj",
    )(xt, w_b)


def _attn_a_kernel(q_ref, k_ref, v_ref, w_ref, o_ref, bias_ref):
    S = q_ref.shape[0]
    nblk = S // QBLK_A
    lead = LEFT_CHUNKS * CHUNK
    n_edge = min(lead // QBLK_A, nblk)

    qc = lax.broadcasted_iota(jnp.int32, (QBLK_A, BAND_A), 0) // CHUNK
    kc = lax.broadcasted_iota(jnp.int32, (QBLK_A, BAND_A), 1) // CHUNK
    allowed = (kc >= qc) & (kc <= qc + LEFT_CHUNKS)
    for h in range(HEADS_PER_BLOCK):
        wb = jnp.broadcast_to(w_ref[h], (QBLK_A, BIAS_W))
        toeplitz = pltpu.roll(wb, BIAS_W - (QBLK_A - 1), 1, stride=1, stride_axis=0)
        bias_ref[h] = jnp.where(allowed, toeplitz[:, :BAND_A], MASK_VALUE)

    lane = lax.broadcasted_iota(jnp.int32, (QBLK_A, LANES), 1)
    head_of_lane = lane // HEAD_DIM

    def blocks(specs):
        work = []
        for p, kstart, nk, bias_off in specs:
            q = q_ref[pl.ds(p * QBLK_A, QBLK_A), :]
            k = k_ref[pl.ds(kstart, nk), :]
            v = v_ref[pl.ds(kstart, nk), :]
            for h in range(HEADS_PER_BLOCK):
                qh = jnp.where(head_of_lane == h, q, jnp.zeros_like(q))
                work.append(dict(s=_dot_nt(qh, k), v=v, bias=bias_ref[h, :, bias_off:bias_off + nk]))
        for w in work:
            s = w["s"] + w["bias"]
            e = jnp.exp(s - jnp.max(s, axis=-1, keepdims=True))
            w["l"] = jnp.sum(e, axis=-1, keepdims=True)
            w["e"] = e.astype(BF16)
        outs = [_dot(w["e"], w["v"]) / w["l"] for w in work]
        for i, spec in enumerate(specs):
            o = outs[i * HEADS_PER_BLOCK]
            for h in range(1, HEADS_PER_BLOCK):
                o = jnp.where(head_of_lane == h, outs[i * HEADS_PER_BLOCK + h], o)
            o_ref[pl.ds(spec[0] * QBLK_A, QBLK_A), :] = o

    blocks([(p, 0, (p + 1) * QBLK_A, lead - p * QBLK_A) for p in range(n_edge)])

    def full(p):
        return (p, pl.multiple_of(p * QBLK_A - lead, QBLK_A), BAND_A, 0)

    n_full = nblk - n_edge
    if n_full % 2:
        blocks([full(n_edge)])
    if n_full >= 2:
        def body(i, c):
            p = n_edge + n_full % 2 + 2 * i
            blocks([full(p), full(p + 1)])
            return c
        lax.fori_loop(0, n_full // 2, body, 0)


def _attn_a(qkv, bias_w, B, S, d_a):
    n_hb = d_a // LANES
    seg = d_a // LANES
    return pl.pallas_call(
        _attn_a_kernel,
        grid=(B, n_hb),
        in_specs=[pl.BlockSpec((None, S, LANES), lambda b, h: (b, 0, h)),
                  pl.BlockSpec((None, S, LANES), lambda b, h: (b, 0, seg + h)),
                  pl.BlockSpec((None, S, LANES), lambda b, h: (b, 0, 2 * seg + h)),
                  pl.BlockSpec((HEADS_PER_BLOCK, 1, BIAS_W), lambda b, h: (h, 0, 0))],
        out_specs=pl.BlockSpec((None, S, LANES), lambda b, h: (b, 0, h)),
        out_shape=jax.ShapeDtypeStruct((B, S, d_a), F32),
        scratch_shapes=[pltpu.VMEM((HEADS_PER_BLOCK, QBLK_A, BAND_A), F32)],
        compiler_params=_cparams("parallel", "parallel"),
        name="attn_chunked",
    )(qkv, qkv, qkv, bias_w)


def _rel_bias_by_offset(rel_bias):
    dist = jnp.clip(BAND_A - 1 - jnp.arange(BIAS_W), -MAX_REL, MAX_REL) + MAX_REL
    return rel_bias[:, dist].astype(F32)[:, None, :]


def _attn_b_kernel(q_ref, k_ref, v_ref, o_ref):
    qi = pl.program_id(2)
    t = TILE_B
    row = lax.broadcasted_iota(jnp.int32, (t, t), 0)
    col = lax.broadcasted_iota(jnp.int32, (t, t), 1)
    from_s = (row >= col).astype(BF16)
    from_s2 = jnp.concatenate([from_s, from_s], axis=0)
    causal = col < row
    q = q_ref[...]
    n_heads = q_ref.shape[1] // HEAD_DIM
    head_of_lane = lax.broadcasted_iota(jnp.int32, q.shape, 1) // HEAD_DIM
    qs = [jnp.where(head_of_lane == h, q, jnp.zeros_like(q)) for h in range(n_heads)]

    def scores(js):
        zs = []
        for j in js:
            kt = k_ref[pl.ds(pl.multiple_of(j * t, t), t), :]
            zs += [_dot_nt(qh, kt) for qh in qs]
        return zs

    def tiles(js, zs, state, first_is_diag=False):
        accs, carries = list(state[0::2]), list(state[1::2])
        work = []
        for n, j in enumerate(js):
            vt = v_ref[pl.ds(pl.multiple_of(j * t, t), t), :]
            for h in range(n_heads):
                work.append(dict(h=h, diag=first_is_diag and n == 0, v=vt, z=zs[n * n_heads + h]))
        for w in work:
            z = w["z"]
            sp = jnp.maximum(z, 0.0) + jnp.log(1.0 + jnp.exp(-jnp.abs(z)))
            if w["diag"]:
                sp = jnp.where(causal, sp, 0.0)
            hi = sp.astype(BF16)
            lo = (sp - hi.astype(F32)).astype(BF16)
            w["suffix"] = _dot(jnp.concatenate([hi, lo], axis=-1), from_s2)
            w["rowsum"] = jnp.sum(sp, axis=-1, keepdims=True)
        for w in work:
            h = w["h"]
            if w["diag"]:
                a = jnp.where(causal, jnp.exp(w["z"] - w["suffix"]), 0.0)
                accs[h] = _dot(a.astype(BF16), w["v"])
                carries[h] = w["rowsum"]
            else:
                a = jnp.exp(w["z"] - w["suffix"] - carries[h])
                accs[h] = accs[h] + _dot(a.astype(BF16), w["v"])
                carries[h] = carries[h] + w["rowsum"]
        return tuple(x for pair in zip(accs, carries) for x in pair)

    empty = (None,) * (2 * n_heads)
    state = lax.cond(qi % 2 == 1,
                     lambda: tiles([qi, qi - 1], scores([qi, qi - 1]), empty, True),
                     lambda: tiles([qi], scores([qi]), empty, True))
    first = qi - 1 - qi % 2

    def body(it, st):
        j = first - 2 * it
        return tiles([j, j - 1], scores([j, j - 1]), st)

    state = lax.fori_loop(0, qi // 2, body, state)
    out = state[0]
    for h in range(1, n_heads):
        out = jnp.where(head_of_lane == h, state[2 * h], out)
    o_ref[...] = out


def _attn_b(qkv, B, S, d_a, d_b):
    w = WIDTH_B
    n_hb = d_b // w
    base = 3 * d_a // w
    seg = d_b // w
    return pl.pallas_call(
        _attn_b_kernel,
        grid=(B, n_hb, S // TILE_B),
        in_specs=[pl.BlockSpec((None, TILE_B, w), lambda b, h, i: (b, i, base + h)),
                  pl.BlockSpec((None, S, w), lambda b, h, i: (b, 0, base + seg + h)),
                  pl.BlockSpec((None, S, w), lambda b, h, i: (b, 0, base + 2 * seg + h))],
        out_specs=pl.BlockSpec((None, TILE_B, w), lambda b, h, i: (b, i, h)),
        out_shape=jax.ShapeDtypeStruct((B, S, d_b), F32),
        compiler_params=_cparams("parallel", "parallel", "parallel"),
        name="attn_stickbreak",
    )(qkv, qkv, qkv)


def _layer_norm(r, g, b):
    mu = jnp.mean(r, axis=-1, keepdims=True)
    c = r - mu
    var = jnp.mean(c * c, axis=-1, keepdims=True)
    return c * lax.rsqrt(var + LN_EPS) * g + b


def _rms_norm(y, g):
    ms = jnp.mean(y * y, axis=-1, keepdims=True)
    return y * lax.rsqrt(ms + RMS_EPS) * g


def _silu(g):
    return g * jax.nn.sigmoid(g)


def _post_kernel(ya_ref, yb_ref, x_ref, ga_ref, gb_ref, wo_ref, g1_ref, b1_ref, rw_ref,
                 wsg_ref, wsu_ref, wsd_ref, x1_ref, base_ref, logit_ref, *, alpha):
    d_a = ya_ref.shape[1]
    na = _rms_norm(ya_ref[...], ga_ref[...]).astype(BF16)
    nb = _rms_norm(yb_ref[...], gb_ref[...]).astype(BF16)
    h = _dot(na, wo_ref[:d_a, :]) + _dot(nb, wo_ref[d_a:, :])
    x1 = _layer_norm(alpha * x_ref[...] + h, g1_ref[...], b1_ref[...])
    x1_ref[...] = _pack_bf16_halves(x1)
    xb = x1.astype(BF16)
    logit_ref[...] = _dot_nt(rw_ref[...], xb)
    hs = _silu(_dot(xb, wsg_ref[...])) * _dot(xb, wsu_ref[...])
    base_ref[...] = alpha * x1 + _dot(hs.astype(BF16), wsd_ref[...])


def _post(ya, yb, xt, gain_a, gain_b, wo_b, g1, b1, rw_b, wsg_b, wsu_b, wsd_b, alpha):
    T, D = xt.shape
    d_a, d_b = ya.shape[1], yb.shape[1]
    E = rw_b.shape[0]
    De = wsg_b.shape[1]
    tm = 4 * TOK_BLOCK
    row = lambda i: (i, 0)
    fix = lambda i: (0, 0)
    return pl.pallas_call(
        functools.partial(_post_kernel, alpha=alpha),
        grid=(T // tm,),
        in_specs=[pl.BlockSpec((tm, d_a), row), pl.BlockSpec((tm, d_b), row), pl.BlockSpec((tm, D), row),
                  pl.BlockSpec((1, d_a), fix), pl.BlockSpec((1, d_b), fix),
                  pl.BlockSpec((d_a + d_b, D), fix), pl.BlockSpec((1, D), fix), pl.BlockSpec((1, D), fix),
                  pl.BlockSpec((E, D), fix), pl.BlockSpec((D, De), fix), pl.BlockSpec((D, De), fix),
                  pl.BlockSpec((De, D), fix)],
        out_specs=[pl.BlockSpec((tm, D // 2), row), pl.BlockSpec((tm, D), row),
                   pl.BlockSpec((E, tm), lambda i: (0, i))],
        out_shape=[jax.ShapeDtypeStruct((T, D // 2), jnp.uint32), jax.ShapeDtypeStruct((T, D), F32),
                   jax.ShapeDtypeStruct((E, T), F32)],
        compiler_params=_cparams("parallel"),
        name="post_attn",
    )(ya, yb, xt, gain_a, gain_b, wo_b, g1, b1, rw_b, wsg_b, wsu_b, wsd_b)


def _route_kernel(logit_ref, rbias_ref, idx_ref, gate_ref, rank_ref, cnt_ref, carry_ref):
    E, tm = logit_ref.shape
    neg = -jnp.inf

    @pl.when(pl.program_id(0) == 0)
    def _():
        carry_ref[...] = jnp.zeros_like(carry_ref)

    scores = jax.nn.sigmoid(logit_ref[...])
    sel = scores + rbias_ref[...]
    eidx = lax.broadcasted_iota(jnp.int32, (E, tm), 0).astype(F32)

    def first_argmax(v, ids):
        m = jnp.max(v, axis=0, keepdims=True)
        return m, jnp.min(jnp.where(v == m, ids, float(E)), axis=0, keepdims=True)

    grp_scores = []
    ids = lax.broadcasted_iota(jnp.int32, (EXPERTS_PER_GROUP, tm), 0).astype(F32)
    for g in range(N_GROUPS):
        v = sel[g * EXPERTS_PER_GROUP:(g + 1) * EXPERTS_PER_GROUP, :]
        m1, i1 = first_argmax(v, ids)
        m2 = jnp.max(jnp.where(ids == i1, neg, v), axis=0, keepdims=True)
        grp_scores.append(m1 + m2)
    parts = []
    for g in range(N_GROUPS):
        beaten = jnp.zeros((1, tm), jnp.int32)
        for o in range(N_GROUPS):
            if o != g:
                s, t = grp_scores[o], grp_scores[g]
                beaten = beaten + ((s > t) | ((s == t) & (o < g))).astype(jnp.int32)
        rows = slice(g * EXPERTS_PER_GROUP, (g + 1) * EXPERTS_PER_GROUP)
        parts.append(jnp.where(beaten < TOPK_GROUPS, sel[rows, :], neg))
    selm = jnp.concatenate(parts, axis=0)

    hits, gates, ids_k = [], [], []
    chosen = jnp.zeros((E, tm), F32)
    gate_sum = jnp.zeros((1, tm), F32)
    for k in range(TOP_K):
        _, ik = first_argmax(selm, eidx)
        hit = eidx == ik
        gk = jnp.sum(jnp.where(hit, scores, 0.0), axis=0, keepdims=True)
        selm = jnp.where(hit, neg, selm)
        chosen = jnp.where(hit, 1.0, chosen)
        gate_sum = gate_sum + gk
        hits.append(hit)
        gates.append(gk)
        ids_k.append(ik)

    row = lax.broadcasted_iota(jnp.int32, (tm, tm), 0)
    col = lax.broadcasted_iota(jnp.int32, (tm, tm), 1)
    earlier = (row < col).astype(BF16)
    before = _dot(chosen.astype(BF16), earlier) + carry_ref[...]
    carry_ref[...] = carry_ref[...] + jnp.sum(chosen, axis=1, keepdims=True)
    cnt_ref[...] = carry_ref[...].astype(jnp.int32)

    ranks = [jnp.sum(jnp.where(hit, before, 0.0), axis=0, keepdims=True) for hit in hits]
    idx_ref[...] = jnp.concatenate(ids_k, axis=0).astype(jnp.int32)
    rank_ref[...] = jnp.concatenate(ranks, axis=0).astype(jnp.int32)
    gate_ref[...] = jnp.concatenate(gates, axis=0) / gate_sum * ROUTED_SCALE


def _route(logits_t, rbias):
    E, T = logits_t.shape
    tm = TOK_BLOCK
    col = lambda i: (0, i)
    fix = lambda i: (0, 0)
    return pl.pallas_call(
        _route_kernel,
        grid=(T // tm,),
        in_specs=[pl.BlockSpec((E, tm), col), pl.BlockSpec((E, 1), fix)],
        out_specs=[pl.BlockSpec((TOP_K, tm), col), pl.BlockSpec((TOP_K, tm), col),
                   pl.BlockSpec((TOP_K, tm), col), pl.BlockSpec((E, 1), fix)],
        out_shape=[jax.ShapeDtypeStruct((TOP_K, T), jnp.int32), jax.ShapeDtypeStruct((TOP_K, T), F32),
                   jax.ShapeDtypeStruct((TOP_K, T), jnp.int32), jax.ShapeDtypeStruct((E, 1), jnp.int32)],
        scratch_shapes=[pltpu.VMEM((E, 1), F32)],
        compiler_params=_cparams("arbitrary"),
        name="route",
    )(logits_t, rbias)


def _dest_kernel(idx_ref, rank_ref, start_ref, dest_ref):
    K, tm = idx_ref.shape
    E = start_ref.shape[0]
    eidx = lax.broadcasted_iota(jnp.int32, (E, tm), 0)
    idx = idx_ref[...]
    start = start_ref[...]
    rows = [jnp.sum(jnp.where(eidx == idx[k:k + 1, :], start, 0.0), axis=0, keepdims=True) for k in range(K)]
    dest_ref[...] = jnp.concatenate(rows, axis=0).astype(jnp.int32) + rank_ref[...]


def _dest(idx, rank, start_f):
    K, T = idx.shape
    E = start_f.shape[0]
    tm = 2 * TOK_BLOCK
    col = lambda i: (0, i)
    return pl.pallas_call(
        _dest_kernel,
        grid=(T // tm,),
        in_specs=[pl.BlockSpec((K, tm), col), pl.BlockSpec((K, tm), col), pl.BlockSpec((E, 1), lambda i: (0, 0))],
        out_specs=pl.BlockSpec((K, tm), col),
        out_shape=jax.ShapeDtypeStruct((K, T), jnp.int32),
        compiler_params=_cparams("parallel"),
        name="dest_rows",
    )(idx, rank, start_f)


def _expert_kernel(first_ref, nblk_ref, nused_ref, xs_hbm, wg_ref, wu_ref, wd_ref, ys_hbm,
                   xbuf, ybuf, wgu_s, wd_s, in_sem, out_sem):
    ring = xbuf.shape[0]
    e = pl.program_id(0)
    De = wg_ref.shape[2]
    n = nblk_ref[e]
    g0 = first_ref[e]
    n_used = nused_ref[0]

    def rows(g):
        return pl.ds(pl.multiple_of(g * ROW_BLOCK, ROW_BLOCK), ROW_BLOCK)

    def in_copy(g, slot):
        return pltpu.make_async_copy(xs_hbm.at[rows(g), :], xbuf.at[slot], in_sem.at[slot])

    def out_copy(g, slot):
        return pltpu.make_async_copy(ybuf.at[slot], ys_hbm.at[rows(g), :], out_sem.at[slot])

    @pl.when(e == 0)
    def _():
        for g in range(ring - 2):
            @pl.when(g < n_used)
            def _():
                in_copy(g, g).start()

    def acquire(g, ahead):
        slot = g & (ring - 1)
        in_copy(g, slot).wait()

        @pl.when(g + ahead < n_used)
        def _():
            in_copy(g + ahead, (g + ahead) & (ring - 1)).start()

        @pl.when(g >= ring)
        def _():
            out_copy(g - ring, slot).wait()
        return slot

    def ffn(pieces):
        dh = xbuf.shape[2]
        gus = []
        for slot, rs in pieces:
            lo, hi = _unpack_bf16_halves(xbuf[slot, rs, :])
            gus.append(_dot(lo.astype(BF16), wgu_s[:dh, :]) + _dot(hi.astype(BF16), wgu_s[dh:, :]))
        hs = [(_silu(gu[:, :De]) * gu[:, De:]).astype(BF16) for gu in gus]
        for (slot, rs), h in zip(pieces, hs):
            ybuf[slot, rs, :] = _pack_bf16_halves(_dot(h, wd_s[...]))

    @pl.when(n > 0)
    def _():
        wgu_s[:, :De] = wg_ref[0].astype(BF16)
        wgu_s[:, De:] = wu_ref[0].astype(BF16)
        wd_s[...] = wd_ref[0].astype(BF16)

        def pair(j, c):
            g = g0 + 2 * j
            slots = [acquire(g + b, ring - 2) for b in range(2)]
            ffn([(slot, slice(None)) for slot in slots])
            for b, slot in enumerate(slots):
                out_copy(g + b, slot).start()
            return c

        lax.fori_loop(0, n // 2, pair, 0)

        @pl.when(n % 2 == 1)
        def _():
            g = g0 + n - 1
            slot = acquire(g, ring - 2)
            half = ROW_BLOCK // 2
            ffn([(slot, slice(0, half)), (slot, slice(half, ROW_BLOCK))])
            out_copy(g, slot).start()

    @pl.when(e == pl.num_programs(0) - 1)
    def _():
        for back in range(1, ring + 1):
            @pl.when(n_used >= back)
            def _():
                out_copy(n_used - back, (n_used - back) & (ring - 1)).wait()


def _experts(first_block, n_blocks, n_used, xs, w_gate, w_up, w_down):
    n_rows, Dh = xs.shape
    D = 2 * Dh
    E, _, De = w_gate.shape
    grid_spec = pltpu.PrefetchScalarGridSpec(
        num_scalar_prefetch=3,
        grid=(E,),
        in_specs=[pl.BlockSpec(memory_space=pl.ANY),
                  pl.BlockSpec((1, D, De), lambda e, *_: (e, 0, 0)),
                  pl.BlockSpec((1, D, De), lambda e, *_: (e, 0, 0)),
                  pl.BlockSpec((1, De, D), lambda e, *_: (e, 0, 0))],
        out_specs=pl.BlockSpec(memory_space=pl.ANY),
        scratch_shapes=[pltpu.VMEM((EXPERT_RING, ROW_BLOCK, Dh), jnp.uint32),
                        pltpu.VMEM((EXPERT_RING, ROW_BLOCK, Dh), jnp.uint32),
                        pltpu.VMEM((D, 2 * De), BF16), pltpu.VMEM((De, D), BF16),
                        pltpu.SemaphoreType.DMA((EXPERT_RING,)), pltpu.SemaphoreType.DMA((EXPERT_RING,))],
    )
    return pl.pallas_call(
        _expert_kernel,
        grid_spec=grid_spec,
        out_shape=jax.ShapeDtypeStruct((n_rows, Dh), jnp.uint32),
        compiler_params=_cparams("arbitrary"),
        name="expert_ffn",
    )(first_block, n_blocks, n_used, xs, w_gate, w_up, w_down)


SC_WINDOW = 128


def _sc_mesh():
    return plsc.VectorSubcoreMesh(core_axis_name="core", subcore_axis_name="subcore")


def _sc_workers():
    info = plsc.get_sparse_core_info()
    return info.num_cores, info.num_cores * info.num_subcores


def _sc_worker_id(num_cores):
    return lax.axis_index("subcore") * num_cores + lax.axis_index("core")


def _sc_scatter_rows(x, dest_flat, n_rows):
    T, D = x.shape
    K = dest_flat.shape[0] // T
    nc, nw = _sc_workers()
    per_w = T // nw
    assert T % nw == 0 and per_w % SC_WINDOW == 0

    @functools.partial(
        pl.kernel, out_type=jax.ShapeDtypeStruct((n_rows, D), x.dtype), mesh=_sc_mesh(),
        scratch_types=[pltpu.VMEM((SC_WINDOW,), jnp.int32), pltpu.VMEM((SC_WINDOW, D), x.dtype)],
        name="sc_dispatch")
    def run(x_hbm, i_hbm, o_hbm, idx_v, rows_v):
        first = _sc_worker_id(nc) * per_w

        @pl.loop(0, per_w // SC_WINDOW)
        def _(c):
            base = first + c * SC_WINDOW
            pltpu.sync_copy(x_hbm.at[pl.ds(base, SC_WINDOW)], rows_v)
            for k in range(K):
                pltpu.sync_copy(i_hbm.at[pl.ds(k * T + base, SC_WINDOW)], idx_v)
                pltpu.sync_copy(rows_v, o_hbm.at[idx_v])

    return run(x, dest_flat)


def _sc_gather_rows(ys, dest_flat):
    N = dest_flat.shape[0]
    D = ys.shape[1]
    nc, nw = _sc_workers()
    per_w = N // nw
    assert N % nw == 0 and per_w % SC_WINDOW == 0

    @functools.partial(
        pl.kernel, out_type=jax.ShapeDtypeStruct((N, D), ys.dtype), mesh=_sc_mesh(),
        scratch_types=[pltpu.VMEM((SC_WINDOW,), jnp.int32), pltpu.VMEM((SC_WINDOW, D), ys.dtype)],
        name="sc_combine_gather")
    def run(y_hbm, i_hbm, o_hbm, idx_v, rows_v):
        first = _sc_worker_id(nc) * per_w

        @pl.loop(0, per_w // SC_WINDOW)
        def _(c):
            base = first + c * SC_WINDOW
            pltpu.sync_copy(i_hbm.at[pl.ds(base, SC_WINDOW)], idx_v)
            pltpu.sync_copy(y_hbm.at[idx_v], rows_v)
            pltpu.sync_copy(rows_v, o_hbm.at[pl.ds(base, SC_WINDOW)])

    return run(ys, dest_flat)


def _final_kernel(z_ref, gate_ref, base_ref, g_ref, b_ref, o_ref):
    gate = gate_ref[...]
    acc_lo = acc_hi = None
    for k in range(TOP_K):
        lo, hi = _unpack_bf16_halves(z_ref[k])
        g = gate[:, k:k + 1]
        acc_lo = g * lo if acc_lo is None else acc_lo + g * lo
        acc_hi = g * hi if acc_hi is None else acc_hi + g * hi
    acc = base_ref[...] + jnp.concatenate([acc_lo, acc_hi], axis=-1)
    o_ref[...] = _layer_norm(acc, g_ref[...], b_ref[...])


def _final(z, gate, base, g2, b2, first_token):
    T, D = base.shape
    tb = TOK_BLOCK
    off = first_token // tb
    return pl.pallas_call(
        _final_kernel,
        grid=(z.shape[1] // tb,),
        in_specs=[pl.BlockSpec((TOP_K, tb, D // 2), lambda i: (0, i, 0)),
                  pl.BlockSpec((tb, TOP_K), lambda i: (i + off, 0)),
                  pl.BlockSpec((tb, D), lambda i: (i + off, 0)),
                  pl.BlockSpec((1, D), lambda i: (0, 0)),
                  pl.BlockSpec((1, D), lambda i: (0, 0))],
        out_specs=pl.BlockSpec((tb, D), lambda i: (i + off, 0)),
        out_shape=jax.ShapeDtypeStruct((T, D), F32),
        input_output_aliases={2: 0},
        compiler_params=_cparams("parallel"),
        name="final_sum_ln",
    )(z, gate, base, g2, b2)


def _layer(x, w_in, rel_bias, gain_a, gain_b, w_out, ln1_g, ln1_b, router_w, router_bias,
           w_gate, w_up, w_down, ws_gate, ws_up, ws_down, ln2_g, ln2_b, alpha):
    B, S, D = x.shape
    T = B * S
    d_a = gain_a.shape[0]
    d_b = gain_b.shape[0]
    assert S % TILE_B == 0 and T % 512 == 0 and d_a % LANES == 0 and d_b % WIDTH_B == 0 and (3 * d_a) % WIDTH_B == 0
    xt = x.reshape(T, D)

    scale = HEAD_DIM ** -0.5
    col = jnp.arange(w_in.shape[1])
    is_q = (col < d_a) | ((col >= 3 * d_a) & (col < 3 * d_a + d_b))
    w_in_b = (w_in * jnp.where(is_q, scale, 1.0)[None, :]).astype(BF16)

    qkv = _qkv_proj(xt, w_in_b).reshape(B, S, -1)
    ya = _attn_a(qkv, _rel_bias_by_offset(rel_bias), B, S, d_a).reshape(T, d_a)
    yb = _attn_b(qkv, B, S, d_a, d_b).reshape(T, d_b)

    x1p, base, logits_t = _post(
        ya, yb, xt, gain_a[None], gain_b[None], w_out.astype(BF16), ln1_g[None], ln1_b[None],
        router_w.T.astype(BF16), ws_gate.astype(BF16), ws_up.astype(BF16), ws_down.astype(BF16), alpha)

    idx, gate, rank, cnt = _route(logits_t, router_bias[:, None].astype(F32))
    cnt = cnt[:, 0]
    padded = (cnt + ROW_BLOCK - 1) // ROW_BLOCK * ROW_BLOCK
    ends = jnp.cumsum(padded)
    start = (ends - padded).astype(jnp.int32)
    n_rows = T * TOP_K + N_EXPERTS * ROW_BLOCK
    first_block = start // ROW_BLOCK
    n_blocks = (padded // ROW_BLOCK).astype(jnp.int32)
    n_used = (ends[-1:] // ROW_BLOCK).astype(jnp.int32)
    dest = _dest(idx, rank, start.astype(F32)[:, None])

    xs = _sc_scatter_rows(x1p, dest.reshape(-1), n_rows)
    ys = _experts(first_block, n_blocks, n_used, xs, w_gate, w_up, w_down)
    gate_t = gate.T
    out = base
    tg = T // COMBINE_GROUPS
    for q in range(COMBINE_GROUPS):
        z = _sc_gather_rows(ys, dest[:, q * tg:(q + 1) * tg].reshape(-1)).reshape(TOP_K, tg, D // 2)
        out = _final(z, gate_t, out, ln2_g[None], ln2_b[None], q * tg)
    return out.reshape(B, S, D)


def kernel(x, w_in, rel_bias, gain_a, gain_b, w_out, ln1_g, ln1_b, router_w, router_bias,
           w_gate, w_up, w_down, ws_gate, ws_up, ws_down, ln2_g, ln2_b):
    depth = w_in.shape[0]
    alpha = (2 * depth) ** 0.25
    for l in range(depth):
        x = _layer(x, w_in[l], rel_bias[l], gain_a[l], gain_b[l], w_out[l], ln1_g[l], ln1_b[l],
                   router_w[l], router_bias[l], w_gate[l], w_up[l], w_down[l],
                   ws_gate[l], ws_up[l], ws_down[l], ln2_g[l], ln2_b[l], alpha)
    return x
```

```python
import functools

import jax
import jax.numpy as jnp
from jax import lax
from jax.experimental import pallas as pl
from jax.experimental.pallas import tpu as pltpu
from jax.experimental.pallas import tpu_sc as plsc

CHUNK = 64
HEAD_DIM = 64
LEFT_CHUNKS = 8
MAX_REL = 128
N_EXPERTS = 256
TOP_K = 8
N_GROUPS = 8
TOPK_GROUPS = 4
EXPERTS_PER_GROUP = N_EXPERTS // N_GROUPS
ROUTED_SCALE = 2.5
LN_EPS = 1e-5
RMS_EPS = 1e-6

LANES = 128
SUBLANES = 8
HEADS_PER_BLOCK = LANES // HEAD_DIM
QBLK_A = 2 * CHUNK
BAND_A = (LEFT_CHUNKS + 2) * CHUNK
BIAS_W = BAND_A + QBLK_A
BLOCKS_PER_TRIP_A = 4
TILE_B = 256
WIDTH_B = 2 * HEAD_DIM
TILES_PER_TRIP_B = 3
ROW_BLOCK = 256
EXPERT_RING = 8
TOK_BLOCK = 256
COMBINE_GROUPS = 4
MASK_VALUE = -1e30
VMEM_LIMIT = 48 * 1024 * 1024

F32 = jnp.float32
BF16 = jnp.bfloat16


def _cparams(*sem, flags=None):
    return pltpu.CompilerParams(dimension_semantics=sem, vmem_limit_bytes=VMEM_LIMIT, flags=flags)


def _dot(a, b):
    return jnp.dot(a, b, preferred_element_type=F32)


def _dot_nt(a, b):
    return lax.dot_general(a, b, (((1,), (1,)), ((), ())), preferred_element_type=F32)


def _pack_bf16_halves(x):
    half = x.shape[1] // 2
    bits = lax.bitcast_convert_type(x.astype(BF16).astype(F32), jnp.uint32)
    return (bits[:, :half] >> 16) | (bits[:, half:] & jnp.uint32(0xFFFF0000))


def _unpack_bf16_halves(w):
    lo = lax.bitcast_convert_type(w << 16, F32)
    hi = lax.bitcast_convert_type(w & jnp.uint32(0xFFFF0000), F32)
    return lo, hi


def _qkv_kernel(x_ref, w_ref, o_ref, *, col_chunk):
    xb = x_ref[...].astype(BF16)
    for n in range(w_ref.shape[1] // col_chunk):
        cols = slice(n * col_chunk, (n + 1) * col_chunk)
        o_ref[:, cols] = _dot(xb, w_ref[:, cols]).astype(BF16)


def _qkv_proj(xt, w_b):
    T, D = xt.shape
    N = w_b.shape[1]
    tm = 512
    return pl.pallas_call(
        functools.partial(_qkv_kernel, col_chunk=512),
        grid=(T // tm,),
        in_specs=[pl.BlockSpec((tm, D), lambda i: (i, 0)),
                  pl.BlockSpec((D, N), lambda i: (0, 0))],
        out_specs=pl.BlockSpec((tm, N), lambda i: (i, 0)),
        out_shape=jax.ShapeDtypeStruct((T, N), BF16),
        compiler_params=_cparams("parallel"),
        name="qkv_proj",
    )(xt, w_b)


def _attn_a_kernel(q_ref, k_ref, v_ref, w_ref, o_ref, bias_ref):
    S = q_ref.shape[0]
    nblk = S // QBLK_A
    lead = LEFT_CHUNKS * CHUNK
    n_edge = min(lead // QBLK_A, nblk)

    qc = lax.broadcasted_iota(jnp.int32, (QBLK_A, BAND_A), 0) // CHUNK
    kc = lax.broadcasted_iota(jnp.int32, (QBLK_A, BAND_A), 1) // CHUNK
    allowed = (kc >= qc) & (kc <= qc + LEFT_CHUNKS)
    for h in range(HEADS_PER_BLOCK):
        wb = jnp.broadcast_to(w_ref[h], (QBLK_A, BIAS_W))
        toeplitz = pltpu.roll(wb, BIAS_W - (QBLK_A - 1), 1, stride=1, stride_axis=0)
        bias_ref[h] = jnp.where(allowed, toeplitz[:, :BAND_A], MASK_VALUE)

    lane = lax.broadcasted_iota(jnp.int32, (QBLK_A, LANES), 1)
    head_of_lane = lane // HEAD_DIM

    def blocks(specs):
        work = []
        for p, kstart, nk, bias_off in specs:
            q = q_ref[pl.ds(p * QBLK_A, QBLK_A), :]
            k = k_ref[pl.ds(kstart, nk), :]
            v = v_ref[pl.ds(kstart, nk), :]
            for h in range(HEADS_PER_BLOCK):
                qh = jnp.where(head_of_lane == h, q, jnp.zeros_like(q))
                work.append(dict(s=_dot_nt(qh, k), v=v, bias=bias_ref[h, :, bias_off:bias_off + nk]))
        for w in work:
            s = w["s"] + w["bias"]
            e = jnp.exp(s - jnp.max(s, axis=-1, keepdims=True))
            w["l"] = jnp.sum(e, axis=-1, keepdims=True)
            w["e"] = e.astype(BF16)
        outs = [_dot(w["e"], w["v"]) / w["l"] for w in work]
        for i, spec in enumerate(specs):
            o = outs[i * HEADS_PER_BLOCK]
            for h in range(1, HEADS_PER_BLOCK):
                o = jnp.where(head_of_lane == h, outs[i * HEADS_PER_BLOCK + h], o)
            o_ref[pl.ds(spec[0] * QBLK_A, QBLK_A), :] = o

    blocks([(p, 0, (p + 1) * QBLK_A, lead - p * QBLK_A) for p in range(n_edge)])

    def full(p):
        return (p, pl.multiple_of(p * QBLK_A - lead, QBLK_A), BAND_A, 0)

    n_full = nblk - n_edge
    per_trip = BLOCKS_PER_TRIP_A
    rem = n_full % per_trip
    if rem:
        blocks([full(n_edge + r) for r in range(rem)])
    if n_full >= per_trip:
        def body(i, c):
            p = n_edge + rem + per_trip * i
            blocks([full(p + r) for r in range(per_trip)])
            return c
        lax.fori_loop(0, n_full // per_trip, body, 0)


def _attn_a(qkv, bias_w, B, S, d_a):
    n_hb = d_a // LANES
    seg = d_a // LANES
    return pl.pallas_call(
        _attn_a_kernel,
        grid=(B, n_hb),
        in_specs=[pl.BlockSpec((None, S, LANES), lambda b, h: (b, 0, h)),
                  pl.BlockSpec((None, S, LANES), lambda b, h: (b, 0, seg + h)),
                  pl.BlockSpec((None, S, LANES), lambda b, h: (b, 0, 2 * seg + h)),
                  pl.BlockSpec((HEADS_PER_BLOCK, 1, BIAS_W), lambda b, h: (h, 0, 0))],
        out_specs=pl.BlockSpec((None, S, LANES), lambda b, h: (b, 0, h)),
        out_shape=jax.ShapeDtypeStruct((B, S, d_a), F32),
        scratch_shapes=[pltpu.VMEM((HEADS_PER_BLOCK, QBLK_A, BAND_A), F32)],
        compiler_params=_cparams("parallel", "parallel"),
        name="attn_chunked",
    )(qkv, qkv, qkv, bias_w)


def _rel_bias_by_offset(rel_bias):
    dist = jnp.clip(BAND_A - 1 - jnp.arange(BIAS_W), -MAX_REL, MAX_REL) + MAX_REL
    return rel_bias[:, dist].astype(F32)[:, None, :]


def _attn_b_kernel(q_ref, k_ref, v_ref, o_ref):
    qi = pl.program_id(2)
    t = TILE_B
    row = lax.broadcasted_iota(jnp.int32, (t, t), 0)
    col = lax.broadcasted_iota(jnp.int32, (t, t), 1)
    from_s = (row >= col).astype(BF16)
    from_s2 = jnp.concatenate([from_s, from_s], axis=0)
    causal = col < row
    q = q_ref[...]
    n_heads = q_ref.shape[1] // HEAD_DIM
    head_of_lane = lax.broadcasted_iota(jnp.int32, q.shape, 1) // HEAD_DIM
    qs = [jnp.where(head_of_lane == h, q, jnp.zeros_like(q)) for h in range(n_heads)]

    def scores(js):
        zs = []
        for j in js:
            kt = k_ref[pl.ds(pl.multiple_of(j * t, t), t), :]
            zs += [_dot_nt(qh, kt) for qh in qs]
        return zs

    def tiles(js, zs, state, first_is_diag=False):
        accs, carries = list(state[0::2]), list(state[1::2])
        work = []
        for n, j in enumerate(js):
            vt = v_ref[pl.ds(pl.multiple_of(j * t, t), t), :]
            for h in range(n_heads):
                work.append(dict(h=h, diag=first_is_diag and n == 0, v=vt, z=zs[n * n_heads + h]))
        for w in work:
            z = w["z"]
            sp = jnp.maximum(z, 0.0) + jnp.log(1.0 + jnp.exp(-jnp.abs(z)))
            if w["diag"]:
                sp = jnp.where(causal, sp, 0.0)
            hi = sp.astype(BF16)
            lo = (sp - hi.astype(F32)).astype(BF16)
            w["suffix"] = _dot(jnp.concatenate([hi, lo], axis=-1), from_s2)
            w["rowsum"] = jnp.sum(sp, axis=-1, keepdims=True)
        for w in work:
            h = w["h"]
            if w["diag"]:
                a = jnp.where(causal, jnp.exp(w["z"] - w["suffix"]), 0.0)
                accs[h] = _dot(a.astype(BF16), w["v"])
                carries[h] = w["rowsum"]
            else:
                a = jnp.exp(w["z"] - w["suffix"] - carries[h])
                accs[h] = accs[h] + _dot(a.astype(BF16), w["v"])
                carries[h] = carries[h] + w["rowsum"]
        return tuple(x for pair in zip(accs, carries) for x in pair)

    per_trip = TILES_PER_TRIP_B
    empty = (None,) * (2 * n_heads)

    def opening(extra):
        js = [qi - r for r in range(extra + 1)]
        return lambda: tiles(js, scores(js), empty, True)

    state = lax.switch(qi % per_trip, [opening(r) for r in range(per_trip)])
    first = qi - 1 - qi % per_trip

    def body(it, st):
        js = [first - per_trip * it - r for r in range(per_trip)]
        return tiles(js, scores(js), st)

    state = lax.fori_loop(0, qi // per_trip, body, state)
    out = state[0]
    for h in range(1, n_heads):
        out = jnp.where(head_of_lane == h, state[2 * h], out)
    o_ref[...] = out


def _attn_b(qkv, B, S, d_a, d_b):
    w = WIDTH_B
    n_hb = d_b // w
    base = 3 * d_a // w
    seg = d_b // w
    return pl.pallas_call(
        _attn_b_kernel,
        grid=(B, n_hb, S // TILE_B),
        in_specs=[pl.BlockSpec((None, TILE_B, w), lambda b, h, i: (b, i, base + h)),
                  pl.BlockSpec((None, S, w), lambda b, h, i: (b, 0, base + seg + h)),
                  pl.BlockSpec((None, S, w), lambda b, h, i: (b, 0, base + 2 * seg + h))],
        out_specs=pl.BlockSpec((None, TILE_B, w), lambda b, h, i: (b, i, h)),
        out_shape=jax.ShapeDtypeStruct((B, S, d_b), F32),
        compiler_params=_cparams("parallel", "parallel", "parallel"),
        name="attn_stickbreak",
    )(qkv, qkv, qkv)


def _layer_norm(r, g, b):
    mu = jnp.mean(r, axis=-1, keepdims=True)
    c = r - mu
    var = jnp.mean(c * c, axis=-1, keepdims=True)
    return c * lax.rsqrt(var + LN_EPS) * g + b


def _rms_norm(y, g):
    ms = jnp.mean(y * y, axis=-1, keepdims=True)
    return y * lax.rsqrt(ms + RMS_EPS) * g


def _silu(g):
    return g * jax.nn.sigmoid(g)


def _post_kernel(ya_ref, yb_ref, x_ref, ga_ref, gb_ref, wo_ref, g1_ref, b1_ref, rw_ref,
                 wsg_ref, wsu_ref, wsd_ref, x1_ref, base_ref, logit_ref, *, alpha):
    d_a = ya_ref.shape[1]
    na = _rms_norm(ya_ref[...], ga_ref[...]).astype(BF16)
    nb = _rms_norm(yb_ref[...], gb_ref[...]).astype(BF16)
    h = _dot(na, wo_ref[:d_a, :]) + _dot(nb, wo_ref[d_a:, :])
    x1 = _layer_norm(alpha * x_ref[...] + h, g1_ref[...], b1_ref[...])
    x1_ref[...] = _pack_bf16_halves(x1)
    xb = x1.astype(BF16)
    logit_ref[...] = _dot_nt(rw_ref[...], xb)
    hs = _silu(_dot(xb, wsg_ref[...])) * _dot(xb, wsu_ref[...])
    base_ref[...] = alpha * x1 + _dot(hs.astype(BF16), wsd_ref[...])


def _post(ya, yb, xt, gain_a, gain_b, wo_b, g1, b1, rw_b, wsg_b, wsu_b, wsd_b, alpha):
    T, D = xt.shape
    d_a, d_b = ya.shape[1], yb.shape[1]
    E = rw_b.shape[0]
    De = wsg_b.shape[1]
    tm = 4 * TOK_BLOCK
    row = lambda i: (i, 0)
    fix = lambda i: (0, 0)
    return pl.pallas_call(
        functools.partial(_post_kernel, alpha=alpha),
        grid=(T // tm,),
        in_specs=[pl.BlockSpec((tm, d_a), row), pl.BlockSpec((tm, d_b), row), pl.BlockSpec((tm, D), row),
                  pl.BlockSpec((1, d_a), fix), pl.BlockSpec((1, d_b), fix),
                  pl.BlockSpec((d_a + d_b, D), fix), pl.BlockSpec((1, D), fix), pl.BlockSpec((1, D), fix),
                  pl.BlockSpec((E, D), fix), pl.BlockSpec((D, De), fix), pl.BlockSpec((D, De), fix),
                  pl.BlockSpec((De, D), fix)],
        out_specs=[pl.BlockSpec((tm, D // 2), row), pl.BlockSpec((tm, D), row),
                   pl.BlockSpec((E, tm), lambda i: (0, i))],
        out_shape=[jax.ShapeDtypeStruct((T, D // 2), jnp.uint32), jax.ShapeDtypeStruct((T, D), F32),
                   jax.ShapeDtypeStruct((E, T), F32)],
        compiler_params=_cparams("parallel"),
        name="post_attn",
    )(ya, yb, xt, gain_a, gain_b, wo_b, g1, b1, rw_b, wsg_b, wsu_b, wsd_b)


def _route_kernel(logit_ref, rbias_ref, idx_ref, gate_ref, rank_ref, cnt_ref, carry_ref):
    E, tm = logit_ref.shape
    neg = -jnp.inf

    @pl.when(pl.program_id(0) == 0)
    def _():
        carry_ref[...] = jnp.zeros_like(carry_ref)

    scores = jax.nn.sigmoid(logit_ref[...])
    sel = scores + rbias_ref[...]
    eidx = lax.broadcasted_iota(jnp.int32, (E, tm), 0).astype(F32)

    def first_argmax(v, ids):
        m = jnp.max(v, axis=0, keepdims=True)
        return m, jnp.min(jnp.where(v == m, ids, float(E)), axis=0, keepdims=True)

    grp_scores = []
    ids = lax.broadcasted_iota(jnp.int32, (EXPERTS_PER_GROUP, tm), 0).astype(F32)
    for g in range(N_GROUPS):
        v = sel[g * EXPERTS_PER_GROUP:(g + 1) * EXPERTS_PER_GROUP, :]
        m1, i1 = first_argmax(v, ids)
        m2 = jnp.max(jnp.where(ids == i1, neg, v), axis=0, keepdims=True)
        grp_scores.append(m1 + m2)
    parts = []
    for g in range(N_GROUPS):
        beaten = jnp.zeros((1, tm), jnp.int32)
        for o in range(N_GROUPS):
            if o != g:
                s, t = grp_scores[o], grp_scores[g]
                beaten = beaten + ((s > t) | ((s == t) & (o < g))).astype(jnp.int32)
        rows = slice(g * EXPERTS_PER_GROUP, (g + 1) * EXPERTS_PER_GROUP)
        parts.append(jnp.where(beaten < TOPK_GROUPS, sel[rows, :], neg))
    selm = jnp.concatenate(parts, axis=0)

    hits, gates, ids_k = [], [], []
    chosen = jnp.zeros((E, tm), F32)
    gate_sum = jnp.zeros((1, tm), F32)
    for k in range(TOP_K):
        _, ik = first_argmax(selm, eidx)
        hit = eidx == ik
        gk = jnp.sum(jnp.where(hit, scores, 0.0), axis=0, keepdims=True)
        selm = jnp.where(hit, neg, selm)
        chosen = jnp.where(hit, 1.0, chosen)
        gate_sum = gate_sum + gk
        hits.append(hit)
        gates.append(gk)
        ids_k.append(ik)

    row = lax.broadcasted_iota(jnp.int32, (tm, tm), 0)
    col = lax.broadcasted_iota(jnp.int32, (tm, tm), 1)
    earlier = (row < col).astype(BF16)
    before = _dot(chosen.astype(BF16), earlier) + carry_ref[...]
    carry_ref[...] = carry_ref[...] + jnp.sum(chosen, axis=1, keepdims=True)
    cnt_ref[...] = carry_ref[...].astype(jnp.int32)

    ranks = [jnp.sum(jnp.where(hit, before, 0.0), axis=0, keepdims=True) for hit in hits]
    idx_ref[...] = jnp.concatenate(ids_k, axis=0).astype(jnp.int32)
    rank_ref[...] = jnp.concatenate(ranks, axis=0).astype(jnp.int32)
    gate_ref[...] = jnp.concatenate(gates, axis=0) / gate_sum * ROUTED_SCALE


def _route(logits_t, rbias):
    E, T = logits_t.shape
    tm = TOK_BLOCK
    col = lambda i: (0, i)
    fix = lambda i: (0, 0)
    return pl.pallas_call(
        _route_kernel,
        grid=(T // tm,),
        in_specs=[pl.BlockSpec((E, tm), col), pl.BlockSpec((E, 1), fix)],
        out_specs=[pl.BlockSpec((TOP_K, tm), col), pl.BlockSpec((TOP_K, tm), col),
                   pl.BlockSpec((TOP_K, tm), col), pl.BlockSpec((E, 1), fix)],
        out_shape=[jax.ShapeDtypeStruct((TOP_K, T), jnp.int32), jax.ShapeDtypeStruct((TOP_K, T), F32),
                   jax.ShapeDtypeStruct((TOP_K, T), jnp.int32), jax.ShapeDtypeStruct((E, 1), jnp.int32)],
        scratch_shapes=[pltpu.VMEM((E, 1), F32)],
        compiler_params=_cparams("arbitrary"),
        name="route",
    )(logits_t, rbias)


def _dest_kernel(idx_ref, rank_ref, start_ref, dest_ref):
    K, tm = idx_ref.shape
    E = start_ref.shape[0]
    eidx = lax.broadcasted_iota(jnp.int32, (E, tm), 0)
    idx = idx_ref[...]
    start = start_ref[...]
    rows = [jnp.sum(jnp.where(eidx == idx[k:k + 1, :], start, 0.0), axis=0, keepdims=True) for k in range(K)]
    dest_ref[...] = jnp.concatenate(rows, axis=0).astype(jnp.int32) + rank_ref[...]


def _dest(idx, rank, start_f):
    K, T = idx.shape
    E = start_f.shape[0]
    tm = 2 * TOK_BLOCK
    col = lambda i: (0, i)
    return pl.pallas_call(
        _dest_kernel,
        grid=(T // tm,),
        in_specs=[pl.BlockSpec((K, tm), col), pl.BlockSpec((K, tm), col), pl.BlockSpec((E, 1), lambda i: (0, 0))],
        out_specs=pl.BlockSpec((K, tm), col),
        out_shape=jax.ShapeDtypeStruct((K, T), jnp.int32),
        compiler_params=_cparams("parallel"),
        name="dest_rows",
    )(idx, rank, start_f)


def _expert_kernel(first_ref, nblk_ref, nused_ref, xs_hbm, wg_ref, wu_ref, wd_ref, ys_hbm,
                   xbuf, ybuf, wgu_s, wd_s, in_sem, out_sem):
    ring = xbuf.shape[0]
    e = pl.program_id(0)
    De = wg_ref.shape[2]
    n = nblk_ref[e]
    g0 = first_ref[e]
    n_used = nused_ref[0]

    def rows(g):
        return pl.ds(pl.multiple_of(g * ROW_BLOCK, ROW_BLOCK), ROW_BLOCK)

    def in_copy(g, slot):
        return pltpu.make_async_copy(xs_hbm.at[rows(g), :], xbuf.at[slot], in_sem.at[slot])

    def out_copy(g, slot):
        return pltpu.make_async_copy(ybuf.at[slot], ys_hbm.at[rows(g), :], out_sem.at[slot])

    @pl.when(e == 0)
    def _():
        for g in range(ring - 2):
            @pl.when(g < n_used)
            def _():
                in_copy(g, g).start()

    def acquire(g, ahead):
        slot = g & (ring - 1)
        in_copy(g, slot).wait()

        @pl.when(g + ahead < n_used)
        def _():
            in_copy(g + ahead, (g + ahead) & (ring - 1)).start()

        @pl.when(g >= ring)
        def _():
            out_copy(g - ring, slot).wait()
        return slot

    def ffn(pieces):
        dh = xbuf.shape[2]
        gus = []
        for slot, rs in pieces:
            lo, hi = _unpack_bf16_halves(xbuf[slot, rs, :])
            gus.append(_dot(lo.astype(BF16), wgu_s[:dh, :]) + _dot(hi.astype(BF16), wgu_s[dh:, :]))
        hs = [(_silu(gu[:, :De]) * gu[:, De:]).astype(BF16) for gu in gus]
        for (slot, rs), h in zip(pieces, hs):
            ybuf[slot, rs, :] = _pack_bf16_halves(_dot(h, wd_s[...]))

    @pl.when(n > 0)
    def _():
        wgu_s[:, :De] = wg_ref[0].astype(BF16)
        wgu_s[:, De:] = wu_ref[0].astype(BF16)
        wd_s[...] = wd_ref[0].astype(BF16)

        def pair(j, c):
            g = g0 + 2 * j
            slots = [acquire(g + b, ring - 2) for b in range(2)]
            ffn([(slot, slice(None)) for slot in slots])
            for b, slot in enumerate(slots):
                out_copy(g + b, slot).start()
            return c

        lax.fori_loop(0, n // 2, pair, 0)

        @pl.when(n % 2 == 1)
        def _():
            g = g0 + n - 1
            slot = acquire(g, ring - 2)
            half = ROW_BLOCK // 2
            ffn([(slot, slice(0, half)), (slot, slice(half, ROW_BLOCK))])
            out_copy(g, slot).start()

    @pl.when(e == pl.num_programs(0) - 1)
    def _():
        for back in range(1, ring + 1):
            @pl.when(n_used >= back)
            def _():
                out_copy(n_used - back, (n_used - back) & (ring - 1)).wait()


def _experts(first_block, n_blocks, n_used, xs, w_gate, w_up, w_down):
    n_rows, Dh = xs.shape
    D = 2 * Dh
    E, _, De = w_gate.shape
    grid_spec = pltpu.PrefetchScalarGridSpec(
        num_scalar_prefetch=3,
        grid=(E,),
        in_specs=[pl.BlockSpec(memory_space=pl.ANY),
                  pl.BlockSpec((1, D, De), lambda e, *_: (e, 0, 0)),
                  pl.BlockSpec((1, D, De), lambda e, *_: (e, 0, 0)),
                  pl.BlockSpec((1, De, D), lambda e, *_: (e, 0, 0))],
        out_specs=pl.BlockSpec(memory_space=pl.ANY),
        scratch_shapes=[pltpu.VMEM((EXPERT_RING, ROW_BLOCK, Dh), jnp.uint32),
                        pltpu.VMEM((EXPERT_RING, ROW_BLOCK, Dh), jnp.uint32),
                        pltpu.VMEM((D, 2 * De), BF16), pltpu.VMEM((De, D), BF16),
                        pltpu.SemaphoreType.DMA((EXPERT_RING,)), pltpu.SemaphoreType.DMA((EXPERT_RING,))],
    )
    return pl.pallas_call(
        _expert_kernel,
        grid_spec=grid_spec,
        out_shape=jax.ShapeDtypeStruct((n_rows, Dh), jnp.uint32),
        compiler_params=_cparams("arbitrary"),
        name="expert_ffn",
    )(first_block, n_blocks, n_used, xs, w_gate, w_up, w_down)


SC_WINDOW = 128


def _sc_mesh():
    return plsc.VectorSubcoreMesh(core_axis_name="core", subcore_axis_name="subcore")


def _sc_workers():
    info = plsc.get_sparse_core_info()
    return info.num_cores, info.num_cores * info.num_subcores


def _sc_worker_id(num_cores):
    return lax.axis_index("subcore") * num_cores + lax.axis_index("core")


def _sc_scatter_rows(x, dest_flat, n_rows):
    T, D = x.shape
    K = dest_flat.shape[0] // T
    nc, nw = _sc_workers()
    per_w = T // nw
    assert T % nw == 0 and per_w % SC_WINDOW == 0

    @functools.partial(
        pl.kernel, out_type=jax.ShapeDtypeStruct((n_rows, D), x.dtype), mesh=_sc_mesh(),
        scratch_types=[pltpu.VMEM((SC_WINDOW,), jnp.int32), pltpu.VMEM((SC_WINDOW, D), x.dtype)],
        name="sc_dispatch")
    def run(x_hbm, i_hbm, o_hbm, idx_v, rows_v):
        first = _sc_worker_id(nc) * per_w

        @pl.loop(0, per_w // SC_WINDOW)
        def _(c):
            base = first + c * SC_WINDOW
            pltpu.sync_copy(x_hbm.at[pl.ds(base, SC_WINDOW)], rows_v)
            for k in range(K):
                pltpu.sync_copy(i_hbm.at[pl.ds(k * T + base, SC_WINDOW)], idx_v)
                pltpu.sync_copy(rows_v, o_hbm.at[idx_v])

    return run(x, dest_flat)


def _sc_gather_rows(ys, dest_flat):
    N = dest_flat.shape[0]
    D = ys.shape[1]
    nc, nw = _sc_workers()
    per_w = N // nw
    assert N % nw == 0 and per_w % SC_WINDOW == 0

    @functools.partial(
        pl.kernel, out_type=jax.ShapeDtypeStruct((N, D), ys.dtype), mesh=_sc_mesh(),
        scratch_types=[pltpu.VMEM((SC_WINDOW,), jnp.int32), pltpu.VMEM((SC_WINDOW, D), ys.dtype)],
        name="sc_combine_gather")
    def run(y_hbm, i_hbm, o_hbm, idx_v, rows_v):
        first = _sc_worker_id(nc) * per_w

        @pl.loop(0, per_w // SC_WINDOW)
        def _(c):
            base = first + c * SC_WINDOW
            pltpu.sync_copy(i_hbm.at[pl.ds(base, SC_WINDOW)], idx_v)
            pltpu.sync_copy(y_hbm.at[idx_v], rows_v)
            pltpu.sync_copy(rows_v, o_hbm.at[pl.ds(base, SC_WINDOW)])

    return run(ys, dest_flat)


def _final_kernel(z_ref, gate_ref, base_ref, g_ref, b_ref, o_ref):
    gate = gate_ref[...]
    acc_lo = acc_hi = None
    for k in range(TOP_K):
        lo, hi = _unpack_bf16_halves(z_ref[k])
        g = gate[:, k:k + 1]
        acc_lo = g * lo if acc_lo is None else acc_lo + g * lo
        acc_hi = g * hi if acc_hi is None else acc_hi + g * hi
    acc = base_ref[...] + jnp.concatenate([acc_lo, acc_hi], axis=-1)
    o_ref[...] = _layer_norm(acc, g_ref[...], b_ref[...])


def _final(z, gate, base, g2, b2, first_token):
    T, D = base.shape
    tb = TOK_BLOCK
    off = first_token // tb
    return pl.pallas_call(
        _final_kernel,
        grid=(z.shape[1] // tb,),
        in_specs=[pl.BlockSpec((TOP_K, tb, D // 2), lambda i: (0, i, 0)),
                  pl.BlockSpec((tb, TOP_K), lambda i: (i + off, 0)),
                  pl.BlockSpec((tb, D), lambda i: (i + off, 0)),
                  pl.BlockSpec((1, D), lambda i: (0, 0)),
                  pl.BlockSpec((1, D), lambda i: (0, 0))],
        out_specs=pl.BlockSpec((tb, D), lambda i: (i + off, 0)),
        out_shape=jax.ShapeDtypeStruct((T, D), F32),
        input_output_aliases={2: 0},
        compiler_params=_cparams("parallel"),
        name="final_sum_ln",
    )(z, gate, base, g2, b2)


def _layer(x, w_in, rel_bias, gain_a, gain_b, w_out, ln1_g, ln1_b, router_w, router_bias,
           w_gate, w_up, w_down, ws_gate, ws_up, ws_down, ln2_g, ln2_b, alpha):
    B, S, D = x.shape
    T = B * S
    d_a = gain_a.shape[0]
    d_b = gain_b.shape[0]
    assert S % TILE_B == 0 and T % 512 == 0 and d_a % LANES == 0 and d_b % WIDTH_B == 0 and (3 * d_a) % WIDTH_B == 0
    xt = x.reshape(T, D)

    scale = HEAD_DIM ** -0.5
    col = jnp.arange(w_in.shape[1])
    is_q = (col < d_a) | ((col >= 3 * d_a) & (col < 3 * d_a + d_b))
    w_in_b = (w_in * jnp.where(is_q, scale, 1.0)[None, :]).astype(BF16)

    qkv = _qkv_proj(xt, w_in_b).reshape(B, S, -1)
    ya = _attn_a(qkv, _rel_bias_by_offset(rel_bias), B, S, d_a).reshape(T, d_a)
    yb = _attn_b(qkv, B, S, d_a, d_b).reshape(T, d_b)

    x1p, base, logits_t = _post(
        ya, yb, xt, gain_a[None], gain_b[None], w_out.astype(BF16), ln1_g[None], ln1_b[None],
        router_w.T.astype(BF16), ws_gate.astype(BF16), ws_up.astype(BF16), ws_down.astype(BF16), alpha)

    idx, gate, rank, cnt = _route(logits_t, router_bias[:, None].astype(F32))
    cnt = cnt[:, 0]
    padded = (cnt + ROW_BLOCK - 1) // ROW_BLOCK * ROW_BLOCK
    ends = jnp.cumsum(padded)
    start = (ends - padded).astype(jnp.int32)
    n_rows = T * TOP_K + N_EXPERTS * ROW_BLOCK
    first_block = start // ROW_BLOCK
    n_blocks = (padded // ROW_BLOCK).astype(jnp.int32)
    n_used = (ends[-1:] // ROW_BLOCK).astype(jnp.int32)
    dest = _dest(idx, rank, start.astype(F32)[:, None])

    xs = _sc_scatter_rows(x1p, dest.reshape(-1), n_rows)
    ys = _experts(first_block, n_blocks, n_used, xs, w_gate, w_up, w_down)
    gate_t = gate.T
    out = base
    tg = T // COMBINE_GROUPS
    for q in range(COMBINE_GROUPS):
        z = _sc_gather_rows(ys, dest[:, q * tg:(q + 1) * tg].reshape(-1)).reshape(TOP_K, tg, D // 2)
        out = _final(z, gate_t, out, ln2_g[None], ln2_b[None], q * tg)
    return out.reshape(B, S, D)


def kernel(x, w_in, rel_bias, gain_a, gain_b, w_out, ln1_g, ln1_b, router_w, router_bias,
           w_gate, w_up, w_down, ws_gate, ws_up, ws_down, ln2_g, ln2_b):
    depth = w_in.shape[0]
    alpha = (2 * depth) ** 0.25
    for l in range(depth):
        x = _layer(x, w_in[l], rel_bias[l], gain_a[l], gain_b[l], w_out[l], ln1_g[l], ln1_b[l],
                   router_w[l], router_bias[l], w_gate[l], w_up[l], w_down[l],
                   ws_gate[l], ws_up[l], ws_down[l], ln2_g[l], ln2_b[l], alpha)
    return x
```

```python
import functools

import jax
import jax.numpy as jnp
from jax import lax
from jax.experimental import pallas as pl
from jax.experimental.pallas import tpu as pltpu
from jax.experimental.pallas import tpu_sc as plsc

CHUNK = 64
HEAD_DIM = 64
LEFT_CHUNKS = 8
MAX_REL = 128
N_EXPERTS = 256
TOP_K = 8
N_GROUPS = 8
TOPK_GROUPS = 4
EXPERTS_PER_GROUP = N_EXPERTS // N_GROUPS
ROUTED_SCALE = 2.5
LN_EPS = 1e-5
RMS_EPS = 1e-6

LANES = 128
SUBLANES = 8
HEADS_PER_BLOCK = LANES // HEAD_DIM
QBLK_A = 2 * CHUNK
BAND_A = (LEFT_CHUNKS + 2) * CHUNK
BIAS_W = BAND_A + QBLK_A
BLOCKS_PER_TRIP_A = 4
TILE_B = 256
WIDTH_B = 2 * HEAD_DIM
TILES_PER_TRIP_B = 4
ROW_BLOCK = 256
EXPERT_RING = 8
TOK_BLOCK = 256
COMBINE_GROUPS = 4
MASK_VALUE = -1e30
VMEM_LIMIT = 48 * 1024 * 1024

F32 = jnp.float32
BF16 = jnp.bfloat16


def _cparams(*sem, flags=None):
    return pltpu.CompilerParams(dimension_semantics=sem, vmem_limit_bytes=VMEM_LIMIT, flags=flags)


def _dot(a, b):
    return jnp.dot(a, b, preferred_element_type=F32)


def _dot_nt(a, b):
    return lax.dot_general(a, b, (((1,), (1,)), ((), ())), preferred_element_type=F32)


def _pack_bf16_halves(x):
    half = x.shape[1] // 2
    bits = lax.bitcast_convert_type(x.astype(BF16).astype(F32), jnp.uint32)
    return (bits[:, :half] >> 16) | (bits[:, half:] & jnp.uint32(0xFFFF0000))


def _unpack_bf16_halves(w):
    lo = lax.bitcast_convert_type(w << 16, F32)
    hi = lax.bitcast_convert_type(w & jnp.uint32(0xFFFF0000), F32)
    return lo, hi


def _qkv_kernel(x_ref, w_ref, o_ref, *, col_chunk):
    xb = x_ref[...].astype(BF16)
    for n in range(w_ref.shape[1] // col_chunk):
        cols = slice(n * col_chunk, (n + 1) * col_chunk)
        o_ref[:, cols] = _dot(xb, w_ref[:, cols]).astype(BF16)


def _qkv_proj(xt, w_b):
    T, D = xt.shape
    N = w_b.shape[1]
    tm = 512
    return pl.pallas_call(
        functools.partial(_qkv_kernel, col_chunk=512),
        grid=(T // tm,),
        in_specs=[pl.BlockSpec((tm, D), lambda i: (i, 0)),
                  pl.BlockSpec((D, N), lambda i: (0, 0))],
        out_specs=pl.BlockSpec((tm, N), lambda i: (i, 0)),
        out_shape=jax.ShapeDtypeStruct((T, N), BF16),
        compiler_params=_cparams("parallel"),
        name="qkv_proj",
    )(xt, w_b)


def _attn_a_kernel(q_ref, k_ref, v_ref, w_ref, o_ref, bias_ref):
    S = q_ref.shape[0]
    nblk = S // QBLK_A
    lead = LEFT_CHUNKS * CHUNK
    n_edge = min(lead // QBLK_A, nblk)

    qc = lax.broadcasted_iota(jnp.int32, (QBLK_A, BAND_A), 0) // CHUNK
    kc = lax.broadcasted_iota(jnp.int32, (QBLK_A, BAND_A), 1) // CHUNK
    allowed = (kc >= qc) & (kc <= qc + LEFT_CHUNKS)
    for h in range(HEADS_PER_BLOCK):
        wb = jnp.broadcast_to(w_ref[h], (QBLK_A, BIAS_W))
        toeplitz = pltpu.roll(wb, BIAS_W - (QBLK_A - 1), 1, stride=1, stride_axis=0)
        bias_ref[h] = jnp.where(allowed, toeplitz[:, :BAND_A], MASK_VALUE)

    lane = lax.broadcasted_iota(jnp.int32, (QBLK_A, LANES), 1)
    head_of_lane = lane // HEAD_DIM

    def blocks(specs):
        work = []
        for p, kstart, nk, bias_off in specs:
            q = q_ref[pl.ds(p * QBLK_A, QBLK_A), :]
            k = k_ref[pl.ds(kstart, nk), :]
            v = v_ref[pl.ds(kstart, nk), :]
            for h in range(HEADS_PER_BLOCK):
                qh = jnp.where(head_of_lane == h, q, jnp.zeros_like(q))
                work.append(dict(s=_dot_nt(qh, k), v=v, bias=bias_ref[h, :, bias_off:bias_off + nk]))
        for w in work:
            s = w["s"] + w["bias"]
            e = jnp.exp(s - jnp.max(s, axis=-1, keepdims=True))
            w["l"] = jnp.sum(e, axis=-1, keepdims=True)
            w["e"] = e.astype(BF16)
        outs = [_dot(w["e"], w["v"]) / w["l"] for w in work]
        for i, spec in enumerate(specs):
            o = outs[i * HEADS_PER_BLOCK]
            for h in range(1, HEADS_PER_BLOCK):
                o = jnp.where(head_of_lane == h, outs[i * HEADS_PER_BLOCK + h], o)
            o_ref[pl.ds(spec[0] * QBLK_A, QBLK_A), :] = o

    blocks([(p, 0, (p + 1) * QBLK_A, lead - p * QBLK_A) for p in range(n_edge)])

    def full(p):
        return (p, pl.multiple_of(p * QBLK_A - lead, QBLK_A), BAND_A, 0)

    n_full = nblk - n_edge
    per_trip = BLOCKS_PER_TRIP_A
    rem = n_full % per_trip
    if rem:
        blocks([full(n_edge + r) for r in range(rem)])
    if n_full >= per_trip:
        def body(i, c):
            p = n_edge + rem + per_trip * i
            blocks([full(p + r) for r in range(per_trip)])
            return c
        lax.fori_loop(0, n_full // per_trip, body, 0)


def _attn_a(qkv, bias_w, B, S, d_a):
    n_hb = d_a // LANES
    seg = d_a // LANES
    return pl.pallas_call(
        _attn_a_kernel,
        grid=(B, n_hb),
        in_specs=[pl.BlockSpec((None, S, LANES), lambda b, h: (b, 0, h)),
                  pl.BlockSpec((None, S, LANES), lambda b, h: (b, 0, seg + h)),
                  pl.BlockSpec((None, S, LANES), lambda b, h: (b, 0, 2 * seg + h)),
                  pl.BlockSpec((HEADS_PER_BLOCK, 1, BIAS_W), lambda b, h: (h, 0, 0))],
        out_specs=pl.BlockSpec((None, S, LANES), lambda b, h: (b, 0, h)),
        out_shape=jax.ShapeDtypeStruct((B, S, d_a), F32),
        scratch_shapes=[pltpu.VMEM((HEADS_PER_BLOCK, QBLK_A, BAND_A), F32)],
        compiler_params=_cparams("parallel", "parallel"),
        name="attn_chunked",
    )(qkv, qkv, qkv, bias_w)


def _rel_bias_by_offset(rel_bias):
    dist = jnp.clip(BAND_A - 1 - jnp.arange(BIAS_W), -MAX_REL, MAX_REL) + MAX_REL
    return rel_bias[:, dist].astype(F32)[:, None, :]


def _attn_b_kernel(q_ref, k_ref, v_ref, o_ref):
    qi = pl.program_id(2)
    t = TILE_B
    row = lax.broadcasted_iota(jnp.int32, (t, t), 0)
    col = lax.broadcasted_iota(jnp.int32, (t, t), 1)
    from_s = (row >= col).astype(BF16)
    from_s2 = jnp.concatenate([from_s, from_s], axis=0)
    causal = col < row
    q = q_ref[...]
    n_heads = q_ref.shape[1] // HEAD_DIM
    head_of_lane = lax.broadcasted_iota(jnp.int32, q.shape, 1) // HEAD_DIM
    qs = [jnp.where(head_of_lane == h, q, jnp.zeros_like(q)) for h in range(n_heads)]

    def scores(js):
        zs = []
        for j in js:
            kt = k_ref[pl.ds(pl.multiple_of(j * t, t), t), :]
            zs += [_dot_nt(qh, kt) for qh in qs]
        return zs

    def tiles(js, zs, state, first_is_diag=False):
        accs, carries = list(state[0::2]), list(state[1::2])
        work = []
        for n, j in enumerate(js):
            vt = v_ref[pl.ds(pl.multiple_of(j * t, t), t), :]
            for h in range(n_heads):
                work.append(dict(h=h, diag=first_is_diag and n == 0, v=vt, z=zs[n * n_heads + h]))
        for w in work:
            z = w["z"]
            sp = jnp.maximum(z, 0.0) + jnp.log(1.0 + jnp.exp(-jnp.abs(z)))
            if w["diag"]:
                sp = jnp.where(causal, sp, 0.0)
            hi = sp.astype(BF16)
            lo = (sp - hi.astype(F32)).astype(BF16)
            w["suffix"] = _dot(jnp.concatenate([hi, lo], axis=-1), from_s2)
            w["rowsum"] = jnp.sum(sp, axis=-1, keepdims=True)
        for w in work:
            h = w["h"]
            if w["diag"]:
                a = jnp.where(causal, jnp.exp(w["z"] - w["suffix"]), 0.0)
                accs[h] = _dot(a.astype(BF16), w["v"])
                carries[h] = w["rowsum"]
            else:
                a = jnp.exp(w["z"] - w["suffix"] - carries[h])
                accs[h] = accs[h] + _dot(a.astype(BF16), w["v"])
                carries[h] = carries[h] + w["rowsum"]
        return tuple(x for pair in zip(accs, carries) for x in pair)

    per_trip = TILES_PER_TRIP_B
    empty = (None,) * (2 * n_heads)

    def opening(extra):
        js = [qi - r for r in range(extra + 1)]
        return lambda: tiles(js, scores(js), empty, True)

    state = lax.switch(qi % per_trip, [opening(r) for r in range(per_trip)])
    first = qi - 1 - qi % per_trip

    def body(it, st):
        js = [first - per_trip * it - r for r in range(per_trip)]
        return tiles(js, scores(js), st)

    state = lax.fori_loop(0, qi // per_trip, body, state)
    out = state[0]
    for h in range(1, n_heads):
        out = jnp.where(head_of_lane == h, state[2 * h], out)
    o_ref[...] = out


def _attn_b(qkv, B, S, d_a, d_b):
    w = WIDTH_B
    n_hb = d_b // w
    base = 3 * d_a // w
    seg = d_b // w
    return pl.pallas_call(
        _attn_b_kernel,
        grid=(B, n_hb, S // TILE_B),
        in_specs=[pl.BlockSpec((None, TILE_B, w), lambda b, h, i: (b, i, base + h)),
                  pl.BlockSpec((None, S, w), lambda b, h, i: (b, 0, base + seg + h)),
                  pl.BlockSpec((None, S, w), lambda b, h, i: (b, 0, base + 2 * seg + h))],
        out_specs=pl.BlockSpec((None, TILE_B, w), lambda b, h, i: (b, i, h)),
        out_shape=jax.ShapeDtypeStruct((B, S, d_b), F32),
        compiler_params=_cparams("parallel", "parallel", "parallel"),
        name="attn_stickbreak",
    )(qkv, qkv, qkv)


def _layer_norm(r, g, b):
    mu = jnp.mean(r, axis=-1, keepdims=True)
    c = r - mu
    var = jnp.mean(c * c, axis=-1, keepdims=True)
    return c * lax.rsqrt(var + LN_EPS) * g + b


def _rms_norm(y, g):
    ms = jnp.mean(y * y, axis=-1, keepdims=True)
    return y * lax.rsqrt(ms + RMS_EPS) * g


def _silu(g):
    return g * jax.nn.sigmoid(g)


def _post_kernel(ya_ref, yb_ref, x_ref, ga_ref, gb_ref, wo_ref, g1_ref, b1_ref, rw_ref,
                 wsg_ref, wsu_ref, wsd_ref, x1_ref, base_ref, logit_ref, *, alpha):
    d_a = ya_ref.shape[1]
    na = _rms_norm(ya_ref[...], ga_ref[...]).astype(BF16)
    nb = _rms_norm(yb_ref[...], gb_ref[...]).astype(BF16)
    h = _dot(na, wo_ref[:d_a, :]) + _dot(nb, wo_ref[d_a:, :])
    x1 = _layer_norm(alpha * x_ref[...] + h, g1_ref[...], b1_ref[...])
    x1_ref[...] = _pack_bf16_halves(x1)
    xb = x1.astype(BF16)
    logit_ref[...] = _dot_nt(rw_ref[...], xb)
    hs = _silu(_dot(xb, wsg_ref[...])) * _dot(xb, wsu_ref[...])
    base_ref[...] = alpha * x1 + _dot(hs.astype(BF16), wsd_ref[...])


def _post(ya, yb, xt, gain_a, gain_b, wo_b, g1, b1, rw_b, wsg_b, wsu_b, wsd_b, alpha):
    T, D = xt.shape
    d_a, d_b = ya.shape[1], yb.shape[1]
    E = rw_b.shape[0]
    De = wsg_b.shape[1]
    tm = 4 * TOK_BLOCK
    row = lambda i: (i, 0)
    fix = lambda i: (0, 0)
    return pl.pallas_call(
        functools.partial(_post_kernel, alpha=alpha),
        grid=(T // tm,),
        in_specs=[pl.BlockSpec((tm, d_a), row), pl.BlockSpec((tm, d_b), row), pl.BlockSpec((tm, D), row),
                  pl.BlockSpec((1, d_a), fix), pl.BlockSpec((1, d_b), fix),
                  pl.BlockSpec((d_a + d_b, D), fix), pl.BlockSpec((1, D), fix), pl.BlockSpec((1, D), fix),
                  pl.BlockSpec((E, D), fix), pl.BlockSpec((D, De), fix), pl.BlockSpec((D, De), fix),
                  pl.BlockSpec((De, D), fix)],
        out_specs=[pl.BlockSpec((tm, D // 2), row), pl.BlockSpec((tm, D), row),
                   pl.BlockSpec((E, tm), lambda i: (0, i))],
        out_shape=[jax.ShapeDtypeStruct((T, D // 2), jnp.uint32), jax.ShapeDtypeStruct((T, D), F32),
                   jax.ShapeDtypeStruct((E, T), F32)],
        compiler_params=_cparams("parallel"),
        name="post_attn",
    )(ya, yb, xt, gain_a, gain_b, wo_b, g1, b1, rw_b, wsg_b, wsu_b, wsd_b)


def _route_kernel(logit_ref, rbias_ref, idx_ref, gate_ref, rank_ref, cnt_ref, carry_ref):
    E, tm = logit_ref.shape
    neg = -jnp.inf

    @pl.when(pl.program_id(0) == 0)
    def _():
        carry_ref[...] = jnp.zeros_like(carry_ref)

    scores = jax.nn.sigmoid(logit_ref[...])
    sel = scores + rbias_ref[...]
    eidx = lax.broadcasted_iota(jnp.int32, (E, tm), 0).astype(F32)

    def first_argmax(v, ids):
        m = jnp.max(v, axis=0, keepdims=True)
        return m, jnp.min(jnp.where(v == m, ids, float(E)), axis=0, keepdims=True)

    grp_scores = []
    ids = lax.broadcasted_iota(jnp.int32, (EXPERTS_PER_GROUP, tm), 0).astype(F32)
    for g in range(N_GROUPS):
        v = sel[g * EXPERTS_PER_GROUP:(g + 1) * EXPERTS_PER_GROUP, :]
        m1, i1 = first_argmax(v, ids)
        m2 = jnp.max(jnp.where(ids == i1, neg, v), axis=0, keepdims=True)
        grp_scores.append(m1 + m2)
    parts = []
    for g in range(N_GROUPS):
        beaten = jnp.zeros((1, tm), jnp.int32)
        for o in range(N_GROUPS):
            if o != g:
                s, t = grp_scores[o], grp_scores[g]
                beaten = beaten + ((s > t) | ((s == t) & (o < g))).astype(jnp.int32)
        rows = slice(g * EXPERTS_PER_GROUP, (g + 1) * EXPERTS_PER_GROUP)
        parts.append(jnp.where(beaten < TOPK_GROUPS, sel[rows, :], neg))
    selm = jnp.concatenate(parts, axis=0)

    hits, gates, ids_k = [], [], []
    chosen = jnp.zeros((E, tm), F32)
    gate_sum = jnp.zeros((1, tm), F32)
    for k in range(TOP_K):
        _, ik = first_argmax(selm, eidx)
        hit = eidx == ik
        gk = jnp.sum(jnp.where(hit, scores, 0.0), axis=0, keepdims=True)
        selm = jnp.where(hit, neg, selm)
        chosen = jnp.where(hit, 1.0, chosen)
        gate_sum = gate_sum + gk
        hits.append(hit)
        gates.append(gk)
        ids_k.append(ik)

    row = lax.broadcasted_iota(jnp.int32, (tm, tm), 0)
    col = lax.broadcasted_iota(jnp.int32, (tm, tm), 1)
    earlier = (row < col).astype(BF16)
    before = _dot(chosen.astype(BF16), earlier) + carry_ref[...]
    carry_ref[...] = carry_ref[...] + jnp.sum(chosen, axis=1, keepdims=True)
    cnt_ref[...] = carry_ref[...].astype(jnp.int32)

    ranks = [jnp.sum(jnp.where(hit, before, 0.0), axis=0, keepdims=True) for hit in hits]
    idx_ref[...] = jnp.concatenate(ids_k, axis=0).astype(jnp.int32)
    rank_ref[...] = jnp.concatenate(ranks, axis=0).astype(jnp.int32)
    gate_ref[...] = jnp.concatenate(gates, axis=0) / gate_sum * ROUTED_SCALE


def _route(logits_t, rbias):
    E, T = logits_t.shape
    tm = TOK_BLOCK
    col = lambda i: (0, i)
    fix = lambda i: (0, 0)
    return pl.pallas_call(
        _route_kernel,
        grid=(T // tm,),
        in_specs=[pl.BlockSpec((E, tm), col), pl.BlockSpec((E, 1), fix)],
        out_specs=[pl.BlockSpec((TOP_K, tm), col), pl.BlockSpec((TOP_K, tm), col),
                   pl.BlockSpec((TOP_K, tm), col), pl.BlockSpec((E, 1), fix)],
        out_shape=[jax.ShapeDtypeStruct((TOP_K, T), jnp.int32), jax.ShapeDtypeStruct((TOP_K, T), F32),
                   jax.ShapeDtypeStruct((TOP_K, T), jnp.int32), jax.ShapeDtypeStruct((E, 1), jnp.int32)],
        scratch_shapes=[pltpu.VMEM((E, 1), F32)],
        compiler_params=_cparams("arbitrary"),
        name="route",
    )(logits_t, rbias)


def _dest_kernel(idx_ref, rank_ref, start_ref, dest_ref):
    K, tm = idx_ref.shape
    E = start_ref.shape[0]
    eidx = lax.broadcasted_iota(jnp.int32, (E, tm), 0)
    idx = idx_ref[...]
    start = start_ref[...]
    rows = [jnp.sum(jnp.where(eidx == idx[k:k + 1, :], start, 0.0), axis=0, keepdims=True) for k in range(K)]
    dest_ref[...] = jnp.concatenate(rows, axis=0).astype(jnp.int32) + rank_ref[...]


def _dest(idx, rank, start_f):
    K, T = idx.shape
    E = start_f.shape[0]
    tm = 2 * TOK_BLOCK
    col = lambda i: (0, i)
    return pl.pallas_call(
        _dest_kernel,
        grid=(T // tm,),
        in_specs=[pl.BlockSpec((K, tm), col), pl.BlockSpec((K, tm), col), pl.BlockSpec((E, 1), lambda i: (0, 0))],
        out_specs=pl.BlockSpec((K, tm), col),
        out_shape=jax.ShapeDtypeStruct((K, T), jnp.int32),
        compiler_params=_cparams("parallel"),
        name="dest_rows",
    )(idx, rank, start_f)


def _expert_kernel(first_ref, nblk_ref, nused_ref, xs_hbm, wg_ref, wu_ref, wd_ref, ys_hbm,
                   xbuf, ybuf, wgu_s, wd_s, in_sem, out_sem):
    ring = xbuf.shape[0]
    e = pl.program_id(0)
    De = wg_ref.shape[2]
    n = nblk_ref[e]
    g0 = first_ref[e]
    n_used = nused_ref[0]

    def rows(g):
        return pl.ds(pl.multiple_of(g * ROW_BLOCK, ROW_BLOCK), ROW_BLOCK)

    def in_copy(g, slot):
        return pltpu.make_async_copy(xs_hbm.at[rows(g), :], xbuf.at[slot], in_sem.at[slot])

    def out_copy(g, slot):
        return pltpu.make_async_copy(ybuf.at[slot], ys_hbm.at[rows(g), :], out_sem.at[slot])

    @pl.when(e == 0)
    def _():
        for g in range(ring - 2):
            @pl.when(g < n_used)
            def _():
                in_copy(g, g).start()

    def acquire(g, ahead):
        slot = g & (ring - 1)
        in_copy(g, slot).wait()

        @pl.when(g + ahead < n_used)
        def _():
            in_copy(g + ahead, (g + ahead) & (ring - 1)).start()

        @pl.when(g >= ring)
        def _():
            out_copy(g - ring, slot).wait()
        return slot

    def ffn(pieces):
        dh = xbuf.shape[2]
        gus = []
        for slot, rs in pieces:
            lo, hi = _unpack_bf16_halves(xbuf[slot, rs, :])
            gus.append(_dot(lo.astype(BF16), wgu_s[:dh, :]) + _dot(hi.astype(BF16), wgu_s[dh:, :]))
        hs = [(_silu(gu[:, :De]) * gu[:, De:]).astype(BF16) for gu in gus]
        for (slot, rs), h in zip(pieces, hs):
            ybuf[slot, rs, :] = _pack_bf16_halves(_dot(h, wd_s[...]))

    @pl.when(n > 0)
    def _():
        wgu_s[:, :De] = wg_ref[0].astype(BF16)
        wgu_s[:, De:] = wu_ref[0].astype(BF16)
        wd_s[...] = wd_ref[0].astype(BF16)

        def pair(j, c):
            g = g0 + 2 * j
            slots = [acquire(g + b, ring - 2) for b in range(2)]
            ffn([(slot, slice(None)) for slot in slots])
            for b, slot in enumerate(slots):
                out_copy(g + b, slot).start()
            return c

        lax.fori_loop(0, n // 2, pair, 0)

        @pl.when(n % 2 == 1)
        def _():
            g = g0 + n - 1
            slot = acquire(g, ring - 2)
            half = ROW_BLOCK // 2
            ffn([(slot, slice(0, half)), (slot, slice(half, ROW_BLOCK))])
            out_copy(g, slot).start()

    @pl.when(e == pl.num_programs(0) - 1)
    def _():
        for back in range(1, ring + 1):
            @pl.when(n_used >= back)
            def _():
                out_copy(n_used - back, (n_used - back) & (ring - 1)).wait()


def _experts(first_block, n_blocks, n_used, xs, w_gate, w_up, w_down):
    n_rows, Dh = xs.shape
    D = 2 * Dh
    E, _, De = w_gate.shape
    grid_spec = pltpu.PrefetchScalarGridSpec(
        num_scalar_prefetch=3,
        grid=(E,),
        in_specs=[pl.BlockSpec(memory_space=pl.ANY),
                  pl.BlockSpec((1, D, De), lambda e, *_: (e, 0, 0)),
                  pl.BlockSpec((1, D, De), lambda e, *_: (e, 0, 0)),
                  pl.BlockSpec((1, De, D), lambda e, *_: (e, 0, 0))],
        out_specs=pl.BlockSpec(memory_space=pl.ANY),
        scratch_shapes=[pltpu.VMEM((EXPERT_RING, ROW_BLOCK, Dh), jnp.uint32),
                        pltpu.VMEM((EXPERT_RING, ROW_BLOCK, Dh), jnp.uint32),
                        pltpu.VMEM((D, 2 * De), BF16), pltpu.VMEM((De, D), BF16),
                        pltpu.SemaphoreType.DMA((EXPERT_RING,)), pltpu.SemaphoreType.DMA((EXPERT_RING,))],
    )
    return pl.pallas_call(
        _expert_kernel,
        grid_spec=grid_spec,
        out_shape=jax.ShapeDtypeStruct((n_rows, Dh), jnp.uint32),
        compiler_params=_cparams("arbitrary"),
        name="expert_ffn",
    )(first_block, n_blocks, n_used, xs, w_gate, w_up, w_down)


SC_WINDOW = 128


def _sc_mesh():
    return plsc.VectorSubcoreMesh(core_axis_name="core", subcore_axis_name="subcore")


def _sc_workers():
    info = plsc.get_sparse_core_info()
    return info.num_cores, info.num_cores * info.num_subcores


def _sc_worker_id(num_cores):
    return lax.axis_index("subcore") * num_cores + lax.axis_index("core")


def _sc_scatter_rows(x, dest_flat, n_rows):
    T, D = x.shape
    K = dest_flat.shape[0] // T
    nc, nw = _sc_workers()
    per_w = T // nw
    assert T % nw == 0 and per_w % SC_WINDOW == 0

    @functools.partial(
        pl.kernel, out_type=jax.ShapeDtypeStruct((n_rows, D), x.dtype), mesh=_sc_mesh(),
        scratch_types=[pltpu.VMEM((SC_WINDOW,), jnp.int32), pltpu.VMEM((SC_WINDOW, D), x.dtype)],
        name="sc_dispatch")
    def run(x_hbm, i_hbm, o_hbm, idx_v, rows_v):
        first = _sc_worker_id(nc) * per_w

        @pl.loop(0, per_w // SC_WINDOW)
        def _(c):
            base = first + c * SC_WINDOW
            pltpu.sync_copy(x_hbm.at[pl.ds(base, SC_WINDOW)], rows_v)
            for k in range(K):
                pltpu.sync_copy(i_hbm.at[pl.ds(k * T + base, SC_WINDOW)], idx_v)
                pltpu.sync_copy(rows_v, o_hbm.at[idx_v])

    return run(x, dest_flat)


def _sc_gather_rows(ys, dest_flat):
    N = dest_flat.shape[0]
    D = ys.shape[1]
    nc, nw = _sc_workers()
    per_w = N // nw
    assert N % nw == 0 and per_w % SC_WINDOW == 0

    @functools.partial(
        pl.kernel, out_type=jax.ShapeDtypeStruct((N, D), ys.dtype), mesh=_sc_mesh(),
        scratch_types=[pltpu.VMEM((SC_WINDOW,), jnp.int32), pltpu.VMEM((SC_WINDOW, D), ys.dtype)],
        name="sc_combine_gather")
    def run(y_hbm, i_hbm, o_hbm, idx_v, rows_v):
        first = _sc_worker_id(nc) * per_w

        @pl.loop(0, per_w // SC_WINDOW)
        def _(c):
            base = first + c * SC_WINDOW
            pltpu.sync_copy(i_hbm.at[pl.ds(base, SC_WINDOW)], idx_v)
            pltpu.sync_copy(y_hbm.at[idx_v], rows_v)
            pltpu.sync_copy(rows_v, o_hbm.at[pl.ds(base, SC_WINDOW)])

    return run(ys, dest_flat)


def _final_kernel(z_ref, gate_ref, base_ref, g_ref, b_ref, o_ref):
    gate = gate_ref[...]
    acc_lo = acc_hi = None
    for k in range(TOP_K):
        lo, hi = _unpack_bf16_halves(z_ref[k])
        g = gate[:, k:k + 1]
        acc_lo = g * lo if acc_lo is None else acc_lo + g * lo
        acc_hi = g * hi if acc_hi is None else acc_hi + g * hi
    acc = base_ref[...] + jnp.concatenate([acc_lo, acc_hi], axis=-1)
    o_ref[...] = _layer_norm(acc, g_ref[...], b_ref[...])


def _final(z, gate, base, g2, b2, first_token):
    T, D = base.shape
    tb = TOK_BLOCK
    off = first_token // tb
    return pl.pallas_call(
        _final_kernel,
        grid=(z.shape[1] // tb,),
        in_specs=[pl.BlockSpec((TOP_K, tb, D // 2), lambda i: (0, i, 0)),
                  pl.BlockSpec((tb, TOP_K), lambda i: (i + off, 0)),
                  pl.BlockSpec((tb, D), lambda i: (i + off, 0)),
                  pl.BlockSpec((1, D), lambda i: (0, 0)),
                  pl.BlockSpec((1, D), lambda i: (0, 0))],
        out_specs=pl.BlockSpec((tb, D), lambda i: (i + off, 0)),
        out_shape=jax.ShapeDtypeStruct((T, D), F32),
        input_output_aliases={2: 0},
        compiler_params=_cparams("parallel"),
        name="final_sum_ln",
    )(z, gate, base, g2, b2)


def _layer(x, w_in, rel_bias, gain_a, gain_b, w_out, ln1_g, ln1_b, router_w, router_bias,
           w_gate, w_up, w_down, ws_gate, ws_up, ws_down, ln2_g, ln2_b, alpha):
    B, S, D = x.shape
    T = B * S
    d_a = gain_a.shape[0]
    d_b = gain_b.shape[0]
    assert S % TILE_B == 0 and T % 512 == 0 and d_a % LANES == 0 and d_b % WIDTH_B == 0 and (3 * d_a) % WIDTH_B == 0
    xt = x.reshape(T, D)

    scale = HEAD_DIM ** -0.5
    col = jnp.arange(w_in.shape[1])
    is_q = (col < d_a) | ((col >= 3 * d_a) & (col < 3 * d_a + d_b))
    w_in_b = (w_in * jnp.where(is_q, scale, 1.0)[None, :]).astype(BF16)

    qkv = _qkv_proj(xt, w_in_b).reshape(B, S, -1)
    ya = _attn_a(qkv, _rel_bias_by_offset(rel_bias), B, S, d_a).reshape(T, d_a)
    yb = _attn_b(qkv, B, S, d_a, d_b).reshape(T, d_b)

    x1p, base, logits_t = _post(
        ya, yb, xt, gain_a[None], gain_b[None], w_out.astype(BF16), ln1_g[None], ln1_b[None],
        router_w.T.astype(BF16), ws_gate.astype(BF16), ws_up.astype(BF16), ws_down.astype(BF16), alpha)

    idx, gate, rank, cnt = _route(logits_t, router_bias[:, None].astype(F32))
    cnt = cnt[:, 0]
    padded = (cnt + ROW_BLOCK - 1) // ROW_BLOCK * ROW_BLOCK
    ends = jnp.cumsum(padded)
    start = (ends - padded).astype(jnp.int32)
    n_rows = T * TOP_K + N_EXPERTS * ROW_BLOCK
    first_block = start // ROW_BLOCK
    n_blocks = (padded // ROW_BLOCK).astype(jnp.int32)
    n_used = (ends[-1:] // ROW_BLOCK).astype(jnp.int32)
    dest = _dest(idx, rank, start.astype(F32)[:, None])

    xs = _sc_scatter_rows(x1p, dest.reshape(-1), n_rows)
    ys = _experts(first_block, n_blocks, n_used, xs, w_gate, w_up, w_down)
    gate_t = gate.T
    out = base
    tg = T // COMBINE_GROUPS
    for q in range(COMBINE_GROUPS):
        z = _sc_gather_rows(ys, dest[:, q * tg:(q + 1) * tg].reshape(-1)).reshape(TOP_K, tg, D // 2)
        out = _final(z, gate_t, out, ln2_g[None], ln2_b[None], q * tg)
    return out.reshape(B, S, D)


def kernel(x, w_in, rel_bias, gain_a, gain_b, w_out, ln1_g, ln1_b, router_w, router_bias,
           w_gate, w_up, w_down, ws_gate, ws_up, ws_down, ln2_g, ln2_b):
    depth = w_in.shape[0]
    alpha = (2 * depth) ** 0.25
    for l in range(depth):
        x = _layer(x, w_in[l], rel_bias[l], gain_a[l], gain_b[l], w_out[l], ln1_g[l], ln1_b[l],
                   router_w[l], router_bias[l], w_gate[l], w_up[l], w_down[l],
                   ws_gate[l], ws_up[l], ws_down[l], ln2_g[l], ln2_b[l], alpha)
    return x
```

```python
import functools

import jax
import jax.numpy as jnp
from jax import lax
from jax.experimental import pallas as pl
from jax.experimental.pallas import tpu as pltpu
from jax.experimental.pallas import tpu_sc as plsc

CHUNK = 64
HEAD_DIM = 64
LEFT_CHUNKS = 8
MAX_REL = 128
N_EXPERTS = 256
TOP_K = 8
N_GROUPS = 8
TOPK_GROUPS = 4
EXPERTS_PER_GROUP = N_EXPERTS // N_GROUPS
ROUTED_SCALE = 2.5
LN_EPS = 1e-5
RMS_EPS = 1e-6

LANES = 128
HEADS_PER_BLOCK = LANES // HEAD_DIM
QBLK_A = 2 * CHUNK
BAND_A = (LEFT_CHUNKS + 2) * CHUNK
BIAS_W = BAND_A + QBLK_A
BLOCKS_PER_TRIP_A = 4
TILE_B = 256
WIDTH_B = 2 * HEAD_DIM
TILES_PER_TRIP_B = 3
ROW_BLOCK = 256
EXPERT_RING = 8
TOK_BLOCK = 256
COMBINE_GROUPS = 4
MASK_VALUE = -1e30
V7X_VMEM_BYTES = 64 * 1024 * 1024
VMEM_LIMIT = V7X_VMEM_BYTES * 3 // 4

F32 = jnp.float32
BF16 = jnp.bfloat16


def _cparams(*sem):
    return pltpu.CompilerParams(dimension_semantics=sem, vmem_limit_bytes=VMEM_LIMIT)


def _dot(a, b):
    return jnp.dot(a, b, preferred_element_type=F32)


def _dot_nt(a, b):
    return lax.dot_general(a, b, (((1,), (1,)), ((), ())), preferred_element_type=F32)


def _pack_bf16_halves(x):
    half = x.shape[1] // 2
    bits = lax.bitcast_convert_type(x.astype(BF16).astype(F32), jnp.uint32)
    return (bits[:, :half] >> 16) | (bits[:, half:] & jnp.uint32(0xFFFF0000))


def _unpack_bf16_halves(w):
    lo = lax.bitcast_convert_type(w << 16, F32)
    hi = lax.bitcast_convert_type(w & jnp.uint32(0xFFFF0000), F32)
    return lo, hi


def _qkv_kernel(x_ref, w_ref, o_ref, *, col_chunk):
    xb = x_ref[...].astype(BF16)
    for n in range(w_ref.shape[1] // col_chunk):
        cols = slice(n * col_chunk, (n + 1) * col_chunk)
        o_ref[:, cols] = _dot(xb, w_ref[:, cols]).astype(BF16)


def _qkv_proj(xt, w_b):
    T, D = xt.shape
    N = w_b.shape[1]
    tm = 512
    return pl.pallas_call(
        functools.partial(_qkv_kernel, col_chunk=512),
        grid=(T // tm,),
        in_specs=[pl.BlockSpec((tm, D), lambda i: (i, 0)),
                  pl.BlockSpec((D, N), lambda i: (0, 0))],
        out_specs=pl.BlockSpec((tm, N), lambda i: (i, 0)),
        out_shape=jax.ShapeDtypeStruct((T, N), BF16),
        compiler_params=_cparams("parallel"),
        name="qkv_proj",
    )(xt, w_b)


def _attn_a_kernel(q_ref, k_ref, v_ref, w_ref, o_ref, bias_ref):
    S = q_ref.shape[0]
    nblk = S // QBLK_A
    lead = LEFT_CHUNKS * CHUNK
    n_edge = min(lead // QBLK_A, nblk)

    qc = lax.broadcasted_iota(jnp.int32, (QBLK_A, BAND_A), 0) // CHUNK
    kc = lax.broadcasted_iota(jnp.int32, (QBLK_A, BAND_A), 1) // CHUNK
    allowed = (kc >= qc) & (kc <= qc + LEFT_CHUNKS)
    for h in range(HEADS_PER_BLOCK):
        wb = jnp.broadcast_to(w_ref[h], (QBLK_A, BIAS_W))
        toeplitz = pltpu.roll(wb, BIAS_W - (QBLK_A - 1), 1, stride=1, stride_axis=0)
        bias_ref[h] = jnp.where(allowed, toeplitz[:, :BAND_A], MASK_VALUE)

    lane = lax.broadcasted_iota(jnp.int32, (QBLK_A, LANES), 1)
    head_of_lane = lane // HEAD_DIM

    def blocks(specs):
        work = []
        for p, kstart, nk, bias_off in specs:
            q = q_ref[pl.ds(p * QBLK_A, QBLK_A), :]
            k = k_ref[pl.ds(kstart, nk), :]
            v = v_ref[pl.ds(kstart, nk), :]
            for h in range(HEADS_PER_BLOCK):
                qh = jnp.where(head_of_lane == h, q, jnp.zeros_like(q))
                work.append(dict(s=_dot_nt(qh, k), v=v, bias=bias_ref[h, :, bias_off:bias_off + nk]))
        for w in work:
            s = w["s"] + w["bias"]
            e = jnp.exp(s - jnp.max(s, axis=-1, keepdims=True))
            w["l"] = jnp.sum(e, axis=-1, keepdims=True)
            w["e"] = e.astype(BF16)
        outs = [_dot(w["e"], w["v"]) / w["l"] for w in work]
        for i, spec in enumerate(specs):
            o = outs[i * HEADS_PER_BLOCK]
            for h in range(1, HEADS_PER_BLOCK):
                o = jnp.where(head_of_lane == h, outs[i * HEADS_PER_BLOCK + h], o)
            o_ref[pl.ds(spec[0] * QBLK_A, QBLK_A), :] = o

    blocks([(p, 0, (p + 1) * QBLK_A, lead - p * QBLK_A) for p in range(n_edge)])

    def full(p):
        return (p, pl.multiple_of(p * QBLK_A - lead, QBLK_A), BAND_A, 0)

    n_full = nblk - n_edge
    per_trip = BLOCKS_PER_TRIP_A
    rem = n_full % per_trip
    if rem:
        blocks([full(n_edge + r) for r in range(rem)])
    if n_full >= per_trip:
        def body(i, c):
            p = n_edge + rem + per_trip * i
            blocks([full(p + r) for r in range(per_trip)])
            return c
        lax.fori_loop(0, n_full // per_trip, body, 0)


def _attn_a(qkv, bias_w, B, S, d_a):
    n_hb = d_a // LANES
    seg = d_a // LANES
    return pl.pallas_call(
        _attn_a_kernel,
        grid=(B, n_hb),
        in_specs=[pl.BlockSpec((None, S, LANES), lambda b, h: (b, 0, h)),
                  pl.BlockSpec((None, S, LANES), lambda b, h: (b, 0, seg + h)),
                  pl.BlockSpec((None, S, LANES), lambda b, h: (b, 0, 2 * seg + h)),
                  pl.BlockSpec((HEADS_PER_BLOCK, 1, BIAS_W), lambda b, h: (h, 0, 0))],
        out_specs=pl.BlockSpec((None, S, LANES), lambda b, h: (b, 0, h)),
        out_shape=jax.ShapeDtypeStruct((B, S, d_a), F32),
        scratch_shapes=[pltpu.VMEM((HEADS_PER_BLOCK, QBLK_A, BAND_A), F32)],
        compiler_params=_cparams("parallel", "parallel"),
        name="attn_chunked",
    )(qkv, qkv, qkv, bias_w)


def _rel_bias_by_offset(rel_bias):
    dist = jnp.clip(BAND_A - 1 - jnp.arange(BIAS_W), -MAX_REL, MAX_REL) + MAX_REL
    return rel_bias[:, dist].astype(F32)[:, None, :]


def _attn_b_kernel(q_ref, k_ref, v_ref, o_ref):
    qi = pl.program_id(2)
    t = TILE_B
    row = lax.broadcasted_iota(jnp.int32, (t, t), 0)
    col = lax.broadcasted_iota(jnp.int32, (t, t), 1)
    from_s = (row >= col).astype(BF16)
    from_s2 = jnp.concatenate([from_s, from_s], axis=0)
    causal = col < row
    q = q_ref[...]
    n_heads = q_ref.shape[1] // HEAD_DIM
    head_of_lane = lax.broadcasted_iota(jnp.int32, q.shape, 1) // HEAD_DIM
    qs = [jnp.where(head_of_lane == h, q, jnp.zeros_like(q)) for h in range(n_heads)]

    def scores(js):
        zs = []
        for j in js:
            kt = k_ref[pl.ds(pl.multiple_of(j * t, t), t), :]
            zs += [_dot_nt(qh, kt) for qh in qs]
        return zs

    def tiles(js, zs, state, first_is_diag=False):
        accs, carries = list(state[0::2]), list(state[1::2])
        work = []
        for n, j in enumerate(js):
            vt = v_ref[pl.ds(pl.multiple_of(j * t, t), t), :]
            for h in range(n_heads):
                work.append(dict(h=h, diag=first_is_diag and n == 0, v=vt, z=zs[n * n_heads + h]))
        for w in work:
            z = w["z"]
            sp = jnp.maximum(z, 0.0) + jnp.log(1.0 + jnp.exp(-jnp.abs(z)))
            if w["diag"]:
                sp = jnp.where(causal, sp, 0.0)
            hi = sp.astype(BF16)
            lo = (sp - hi.astype(F32)).astype(BF16)
            w["suffix"] = _dot(jnp.concatenate([hi, lo], axis=-1), from_s2)
            w["rowsum"] = jnp.sum(sp, axis=-1, keepdims=True)
        for w in work:
            h = w["h"]
            if w["diag"]:
                a = jnp.where(causal, jnp.exp(w["z"] - w["suffix"]), 0.0)
                accs[h] = _dot(a.astype(BF16), w["v"])
                carries[h] = w["rowsum"]
            else:
                a = jnp.exp(w["z"] - w["suffix"] - carries[h])
                accs[h] = accs[h] + _dot(a.astype(BF16), w["v"])
                carries[h] = carries[h] + w["rowsum"]
        return tuple(x for pair in zip(accs, carries) for x in pair)

    per_trip = TILES_PER_TRIP_B
    empty = (None,) * (2 * n_heads)

    def opening(extra):
        js = [qi - r for r in range(extra + 1)]
        return lambda: tiles(js, scores(js), empty, True)

    state = lax.switch(qi % per_trip, [opening(r) for r in range(per_trip)])
    first = qi - 1 - qi % per_trip

    def body(it, st):
        js = [first - per_trip * it - r for r in range(per_trip)]
        return tiles(js, scores(js), st)

    state = lax.fori_loop(0, qi // per_trip, body, state)
    out = state[0]
    for h in range(1, n_heads):
        out = jnp.where(head_of_lane == h, state[2 * h], out)
    o_ref[...] = out


def _attn_b(qkv, B, S, d_a, d_b):
    w = WIDTH_B
    n_hb = d_b // w
    base = 3 * d_a // w
    seg = d_b // w
    return pl.pallas_call(
        _attn_b_kernel,
        grid=(B, n_hb, S // TILE_B),
        in_specs=[pl.BlockSpec((None, TILE_B, w), lambda b, h, i: (b, i, base + h)),
                  pl.BlockSpec((None, S, w), lambda b, h, i: (b, 0, base + seg + h)),
                  pl.BlockSpec((None, S, w), lambda b, h, i: (b, 0, base + 2 * seg + h))],
        out_specs=pl.BlockSpec((None, TILE_B, w), lambda b, h, i: (b, i, h)),
        out_shape=jax.ShapeDtypeStruct((B, S, d_b), F32),
        compiler_params=_cparams("parallel", "parallel", "parallel"),
        name="attn_stickbreak",
    )(qkv, qkv, qkv)


def _layer_norm(r, g, b):
    mu = jnp.mean(r, axis=-1, keepdims=True)
    c = r - mu
    var = jnp.mean(c * c, axis=-1, keepdims=True)
    return c * lax.rsqrt(var + LN_EPS) * g + b


def _rms_norm(y, g):
    ms = jnp.mean(y * y, axis=-1, keepdims=True)
    return y * lax.rsqrt(ms + RMS_EPS) * g


def _silu(g):
    return g * jax.nn.sigmoid(g)


def _post_kernel(ya_ref, yb_ref, x_ref, ga_ref, gb_ref, wo_ref, g1_ref, b1_ref, rw_ref,
                 wsg_ref, wsu_ref, wsd_ref, x1_ref, base_ref, logit_ref, *, alpha):
    d_a = ya_ref.shape[1]
    na = _rms_norm(ya_ref[...], ga_ref[...]).astype(BF16)
    nb = _rms_norm(yb_ref[...], gb_ref[...]).astype(BF16)
    h = _dot(na, wo_ref[:d_a, :]) + _dot(nb, wo_ref[d_a:, :])
    x1 = _layer_norm(alpha * x_ref[...] + h, g1_ref[...], b1_ref[...])
    x1_ref[...] = _pack_bf16_halves(x1)
    xb = x1.astype(BF16)
    logit_ref[...] = _dot_nt(rw_ref[...], xb)
    hs = _silu(_dot(xb, wsg_ref[...])) * _dot(xb, wsu_ref[...])
    base_ref[...] = alpha * x1 + _dot(hs.astype(BF16), wsd_ref[...])


def _post(ya, yb, xt, gain_a, gain_b, wo_b, g1, b1, rw_b, wsg_b, wsu_b, wsd_b, alpha):
    T, D = xt.shape
    d_a, d_b = ya.shape[1], yb.shape[1]
    E = rw_b.shape[0]
    De = wsg_b.shape[1]
    tm = 4 * TOK_BLOCK
    row = lambda i: (i, 0)
    fix = lambda i: (0, 0)
    return pl.pallas_call(
        functools.partial(_post_kernel, alpha=alpha),
        grid=(T // tm,),
        in_specs=[pl.BlockSpec((tm, d_a), row), pl.BlockSpec((tm, d_b), row), pl.BlockSpec((tm, D), row),
                  pl.BlockSpec((1, d_a), fix), pl.BlockSpec((1, d_b), fix),
                  pl.BlockSpec((d_a + d_b, D), fix), pl.BlockSpec((1, D), fix), pl.BlockSpec((1, D), fix),
                  pl.BlockSpec((E, D), fix), pl.BlockSpec((D, De), fix), pl.BlockSpec((D, De), fix),
                  pl.BlockSpec((De, D), fix)],
        out_specs=[pl.BlockSpec((tm, D // 2), row), pl.BlockSpec((tm, D), row),
                   pl.BlockSpec((E, tm), lambda i: (0, i))],
        out_shape=[jax.ShapeDtypeStruct((T, D // 2), jnp.uint32), jax.ShapeDtypeStruct((T, D), F32),
                   jax.ShapeDtypeStruct((E, T), F32)],
        compiler_params=_cparams("parallel"),
        name="post_attn",
    )(ya, yb, xt, gain_a, gain_b, wo_b, g1, b1, rw_b, wsg_b, wsu_b, wsd_b)


def _route_kernel(logit_ref, rbias_ref, idx_ref, gate_ref, rank_ref, cnt_ref, carry_ref):
    E, tm = logit_ref.shape
    neg = -jnp.inf

    @pl.when(pl.program_id(0) == 0)
    def _():
        carry_ref[...] = jnp.zeros_like(carry_ref)

    scores = jax.nn.sigmoid(logit_ref[...])
    sel = scores + rbias_ref[...]
    eidx = lax.broadcasted_iota(jnp.int32, (E, tm), 0).astype(F32)

    def first_argmax(v, ids):
        m = jnp.max(v, axis=0, keepdims=True)
        return m, jnp.min(jnp.where(v == m, ids, float(E)), axis=0, keepdims=True)

    grp_scores = []
    ids = lax.broadcasted_iota(jnp.int32, (EXPERTS_PER_GROUP, tm), 0).astype(F32)
    for g in range(N_GROUPS):
        v = sel[g * EXPERTS_PER_GROUP:(g + 1) * EXPERTS_PER_GROUP, :]
        m1, i1 = first_argmax(v, ids)
        m2 = jnp.max(jnp.where(ids == i1, neg, v), axis=0, keepdims=True)
        grp_scores.append(m1 + m2)
    parts = []
    for g in range(N_GROUPS):
        beaten = jnp.zeros((1, tm), jnp.int32)
        for o in range(N_GROUPS):
            if o != g:
                s, t = grp_scores[o], grp_scores[g]
                beaten = beaten + ((s > t) | ((s == t) & (o < g))).astype(jnp.int32)
        rows = slice(g * EXPERTS_PER_GROUP, (g + 1) * EXPERTS_PER_GROUP)
        parts.append(jnp.where(beaten < TOPK_GROUPS, sel[rows, :], neg))
    selm = jnp.concatenate(parts, axis=0)

    hits, gates, ids_k = [], [], []
    chosen = jnp.zeros((E, tm), F32)
    gate_sum = jnp.zeros((1, tm), F32)
    for k in range(TOP_K):
        _, ik = first_argmax(selm, eidx)
        hit = eidx == ik
        gk = jnp.sum(jnp.where(hit, scores, 0.0), axis=0, keepdims=True)
        selm = jnp.where(hit, neg, selm)
        chosen = jnp.where(hit, 1.0, chosen)
        gate_sum = gate_sum + gk
        hits.append(hit)
        gates.append(gk)
        ids_k.append(ik)

    row = lax.broadcasted_iota(jnp.int32, (tm, tm), 0)
    col = lax.broadcasted_iota(jnp.int32, (tm, tm), 1)
    earlier = (row < col).astype(BF16)
    before = _dot(chosen.astype(BF16), earlier) + carry_ref[...]
    carry_ref[...] = carry_ref[...] + jnp.sum(chosen, axis=1, keepdims=True)
    cnt_ref[...] = carry_ref[...].astype(jnp.int32)

    ranks = [jnp.sum(jnp.where(hit, before, 0.0), axis=0, keepdims=True) for hit in hits]
    idx_ref[...] = jnp.concatenate(ids_k, axis=0).astype(jnp.int32)
    rank_ref[...] = jnp.concatenate(ranks, axis=0).astype(jnp.int32)
    gate_ref[...] = jnp.concatenate(gates, axis=0) / gate_sum * ROUTED_SCALE


def _route(logits_t, rbias):
    E, T = logits_t.shape
    tm = TOK_BLOCK
    col = lambda i: (0, i)
    fix = lambda i: (0, 0)
    return pl.pallas_call(
        _route_kernel,
        grid=(T // tm,),
        in_specs=[pl.BlockSpec((E, tm), col), pl.BlockSpec((E, 1), fix)],
        out_specs=[pl.BlockSpec((TOP_K, tm), col), pl.BlockSpec((TOP_K, tm), col),
                   pl.BlockSpec((TOP_K, tm), col), pl.BlockSpec((E, 1), fix)],
        out_shape=[jax.ShapeDtypeStruct((TOP_K, T), jnp.int32), jax.ShapeDtypeStruct((TOP_K, T), F32),
                   jax.ShapeDtypeStruct((TOP_K, T), jnp.int32), jax.ShapeDtypeStruct((E, 1), jnp.int32)],
        scratch_shapes=[pltpu.VMEM((E, 1), F32)],
        compiler_params=_cparams("arbitrary"),
        name="route",
    )(logits_t, rbias)


def _dest_kernel(idx_ref, rank_ref, start_ref, dest_ref):
    K, tm = idx_ref.shape
    E = start_ref.shape[0]
    eidx = lax.broadcasted_iota(jnp.int32, (E, tm), 0)
    idx = idx_ref[...]
    start = start_ref[...]
    rows = [jnp.sum(jnp.where(eidx == idx[k:k + 1, :], start, 0.0), axis=0, keepdims=True) for k in range(K)]
    dest_ref[...] = jnp.concatenate(rows, axis=0).astype(jnp.int32) + rank_ref[...]


def _dest(idx, rank, start_f):
    K, T = idx.shape
    E = start_f.shape[0]
    tm = 2 * TOK_BLOCK
    col = lambda i: (0, i)
    return pl.pallas_call(
        _dest_kernel,
        grid=(T // tm,),
        in_specs=[pl.BlockSpec((K, tm), col), pl.BlockSpec((K, tm), col), pl.BlockSpec((E, 1), lambda i: (0, 0))],
        out_specs=pl.BlockSpec((K, tm), col),
        out_shape=jax.ShapeDtypeStruct((K, T), jnp.int32),
        compiler_params=_cparams("parallel"),
        name="dest_rows",
    )(idx, rank, start_f)


def _expert_kernel(first_ref, nblk_ref, nused_ref, xs_hbm, wg_ref, wu_ref, wd_ref, ys_hbm,
                   xbuf, ybuf, wgu_s, wd_s, in_sem, out_sem):
    ring = xbuf.shape[0]
    e = pl.program_id(0)
    De = wg_ref.shape[2]
    n = nblk_ref[e]
    g0 = first_ref[e]
    n_used = nused_ref[0]

    def rows(g):
        return pl.ds(pl.multiple_of(g * ROW_BLOCK, ROW_BLOCK), ROW_BLOCK)

    def in_copy(g, slot):
        return pltpu.make_async_copy(xs_hbm.at[rows(g), :], xbuf.at[slot], in_sem.at[slot])

    def out_copy(g, slot):
        return pltpu.make_async_copy(ybuf.at[slot], ys_hbm.at[rows(g), :], out_sem.at[slot])

    @pl.when(e == 0)
    def _():
        for g in range(ring - 2):
            @pl.when(g < n_used)
            def _():
                in_copy(g, g).start()

    def acquire(g, ahead):
        slot = g & (ring - 1)
        in_copy(g, slot).wait()

        @pl.when(g + ahead < n_used)
        def _():
            in_copy(g + ahead, (g + ahead) & (ring - 1)).start()

        @pl.when(g >= ring)
        def _():
            out_copy(g - ring, slot).wait()
        return slot

    def ffn(pieces):
        dh = xbuf.shape[2]
        gus = []
        for slot, rs in pieces:
            lo, hi = _unpack_bf16_halves(xbuf[slot, rs, :])
            gus.append(_dot(lo.astype(BF16), wgu_s[:dh, :]) + _dot(hi.astype(BF16), wgu_s[dh:, :]))
        hs = [(_silu(gu[:, :De]) * gu[:, De:]).astype(BF16) for gu in gus]
        for (slot, rs), h in zip(pieces, hs):
            ybuf[slot, rs, :] = _pack_bf16_halves(_dot(h, wd_s[...]))

    @pl.when(n > 0)
    def _():
        wgu_s[:, :De] = wg_ref[0].astype(BF16)
        wgu_s[:, De:] = wu_ref[0].astype(BF16)
        wd_s[...] = wd_ref[0].astype(BF16)

        def pair(j, c):
            g = g0 + 2 * j
            slots = [acquire(g + b, ring - 2) for b in range(2)]
            ffn([(slot, slice(None)) for slot in slots])
            for b, slot in enumerate(slots):
                out_copy(g + b, slot).start()
            return c

        lax.fori_loop(0, n // 2, pair, 0)

        @pl.when(n % 2 == 1)
        def _():
            g = g0 + n - 1
            slot = acquire(g, ring - 2)
            half = ROW_BLOCK // 2
            ffn([(slot, slice(0, half)), (slot, slice(half, ROW_BLOCK))])
            out_copy(g, slot).start()

    @pl.when(e == pl.num_programs(0) - 1)
    def _():
        for back in range(1, ring + 1):
            @pl.when(n_used >= back)
            def _():
                out_copy(n_used - back, (n_used - back) & (ring - 1)).wait()


def _experts(first_block, n_blocks, n_used, xs, w_gate, w_up, w_down):
    n_rows, Dh = xs.shape
    D = 2 * Dh
    E, _, De = w_gate.shape
    grid_spec = pltpu.PrefetchScalarGridSpec(
        num_scalar_prefetch=3,
        grid=(E,),
        in_specs=[pl.BlockSpec(memory_space=pl.ANY),
                  pl.BlockSpec((1, D, De), lambda e, *_: (e, 0, 0)),
                  pl.BlockSpec((1, D, De), lambda e, *_: (e, 0, 0)),
                  pl.BlockSpec((1, De, D), lambda e, *_: (e, 0, 0))],
        out_specs=pl.BlockSpec(memory_space=pl.ANY),
        scratch_shapes=[pltpu.VMEM((EXPERT_RING, ROW_BLOCK, Dh), jnp.uint32),
                        pltpu.VMEM((EXPERT_RING, ROW_BLOCK, Dh), jnp.uint32),
                        pltpu.VMEM((D, 2 * De), BF16), pltpu.VMEM((De, D), BF16),
                        pltpu.SemaphoreType.DMA((EXPERT_RING,)), pltpu.SemaphoreType.DMA((EXPERT_RING,))],
    )
    return pl.pallas_call(
        _expert_kernel,
        grid_spec=grid_spec,
        out_shape=jax.ShapeDtypeStruct((n_rows, Dh), jnp.uint32),
        compiler_params=_cparams("arbitrary"),
        name="expert_ffn",
    )(first_block, n_blocks, n_used, xs, w_gate, w_up, w_down)


SC_WINDOW = 128


def _sc_mesh():
    return plsc.VectorSubcoreMesh(core_axis_name="core", subcore_axis_name="subcore")


def _sc_workers():
    info = plsc.get_sparse_core_info()
    return info.num_cores, info.num_cores * info.num_subcores


def _sc_worker_id(num_cores):
    return lax.axis_index("subcore") * num_cores + lax.axis_index("core")


def _sc_scatter_rows(x, dest_flat, n_rows):
    T, D = x.shape
    K = dest_flat.shape[0] // T
    nc, nw = _sc_workers()
    per_w = T // nw
    assert T % nw == 0 and per_w % SC_WINDOW == 0

    @functools.partial(
        pl.kernel, out_type=jax.ShapeDtypeStruct((n_rows, D), x.dtype), mesh=_sc_mesh(),
        scratch_types=[pltpu.VMEM((SC_WINDOW,), jnp.int32), pltpu.VMEM((SC_WINDOW, D), x.dtype)],
        name="sc_dispatch")
    def run(x_hbm, i_hbm, o_hbm, idx_v, rows_v):
        first = _sc_worker_id(nc) * per_w

        @pl.loop(0, per_w // SC_WINDOW)
        def _(c):
            base = first + c * SC_WINDOW
            pltpu.sync_copy(x_hbm.at[pl.ds(base, SC_WINDOW)], rows_v)
            for k in range(K):
                pltpu.sync_copy(i_hbm.at[pl.ds(k * T + base, SC_WINDOW)], idx_v)
                pltpu.sync_copy(rows_v, o_hbm.at[idx_v])

    return run(x, dest_flat)


def _sc_gather_rows(ys, dest_flat):
    N = dest_flat.shape[0]
    D = ys.shape[1]
    nc, nw = _sc_workers()
    per_w = N // nw
    assert N % nw == 0 and per_w % SC_WINDOW == 0

    @functools.partial(
        pl.kernel, out_type=jax.ShapeDtypeStruct((N, D), ys.dtype), mesh=_sc_mesh(),
        scratch_types=[pltpu.VMEM((SC_WINDOW,), jnp.int32), pltpu.VMEM((SC_WINDOW, D), ys.dtype)],
        name="sc_combine_gather")
    def run(y_hbm, i_hbm, o_hbm, idx_v, rows_v):
        first = _sc_worker_id(nc) * per_w

        @pl.loop(0, per_w // SC_WINDOW)
        def _(c):
            base = first + c * SC_WINDOW
            pltpu.sync_copy(i_hbm.at[pl.ds(base, SC_WINDOW)], idx_v)
            pltpu.sync_copy(y_hbm.at[idx_v], rows_v)
            pltpu.sync_copy(rows_v, o_hbm.at[pl.ds(base, SC_WINDOW)])

    return run(ys, dest_flat)


def _final_kernel(z_ref, gate_ref, base_ref, g_ref, b_ref, o_ref):
    gate = gate_ref[...]
    acc_lo = acc_hi = None
    for k in range(TOP_K):
        lo, hi = _unpack_bf16_halves(z_ref[k])
        g = gate[:, k:k + 1]
        acc_lo = g * lo if acc_lo is None else acc_lo + g * lo
        acc_hi = g * hi if acc_hi is None else acc_hi + g * hi
    acc = base_ref[...] + jnp.concatenate([acc_lo, acc_hi], axis=-1)
    o_ref[...] = _layer_norm(acc, g_ref[...], b_ref[...])


def _final(z, gate, base, g2, b2, first_token):
    T, D = base.shape
    tb = TOK_BLOCK
    off = first_token // tb
    return pl.pallas_call(
        _final_kernel,
        grid=(z.shape[1] // tb,),
        in_specs=[pl.BlockSpec((TOP_K, tb, D // 2), lambda i: (0, i, 0)),
                  pl.BlockSpec((tb, TOP_K), lambda i: (i + off, 0)),
                  pl.BlockSpec((tb, D), lambda i: (i + off, 0)),
                  pl.BlockSpec((1, D), lambda i: (0, 0)),
                  pl.BlockSpec((1, D), lambda i: (0, 0))],
        out_specs=pl.BlockSpec((tb, D), lambda i: (i + off, 0)),
        out_shape=jax.ShapeDtypeStruct((T, D), F32),
        input_output_aliases={2: 0},
        compiler_params=_cparams("parallel"),
        name="final_sum_ln",
    )(z, gate, base, g2, b2)


def _layer(x, w_in, rel_bias, gain_a, gain_b, w_out, ln1_g, ln1_b, router_w, router_bias,
           w_gate, w_up, w_down, ws_gate, ws_up, ws_down, ln2_g, ln2_b, alpha):
    B, S, D = x.shape
    T = B * S
    d_a = gain_a.shape[0]
    d_b = gain_b.shape[0]
    assert S % TILE_B == 0 and T % 512 == 0 and d_a % LANES == 0 and d_b % WIDTH_B == 0 and (3 * d_a) % WIDTH_B == 0
    xt = x.reshape(T, D)

    scale = HEAD_DIM ** -0.5
    col = jnp.arange(w_in.shape[1])
    is_q = (col < d_a) | ((col >= 3 * d_a) & (col < 3 * d_a + d_b))
    w_in_b = (w_in * jnp.where(is_q, scale, 1.0)[None, :]).astype(BF16)

    qkv = _qkv_proj(xt, w_in_b).reshape(B, S, -1)
    ya = _attn_a(qkv, _rel_bias_by_offset(rel_bias), B, S, d_a).reshape(T, d_a)
    yb = _attn_b(qkv, B, S, d_a, d_b).reshape(T, d_b)

    x1p, base, logits_t = _post(
        ya, yb, xt, gain_a[None], gain_b[None], w_out.astype(BF16), ln1_g[None], ln1_b[None],
        router_w.T.astype(BF16), ws_gate.astype(BF16), ws_up.astype(BF16), ws_down.astype(BF16), alpha)

    idx, gate, rank, cnt = _route(logits_t, router_bias[:, None].astype(F32))
    cnt = cnt[:, 0]
    padded = (cnt + ROW_BLOCK - 1) // ROW_BLOCK * ROW_BLOCK
    ends = jnp.cumsum(padded)
    start = (ends - padded).astype(jnp.int32)
    n_rows = T * TOP_K + N_EXPERTS * ROW_BLOCK
    first_block = start // ROW_BLOCK
    n_blocks = (padded // ROW_BLOCK).astype(jnp.int32)
    n_used = (ends[-1:] // ROW_BLOCK).astype(jnp.int32)
    dest = _dest(idx, rank, start.astype(F32)[:, None])

    xs = _sc_scatter_rows(x1p, dest.reshape(-1), n_rows)
    ys = _experts(first_block, n_blocks, n_used, xs, w_gate, w_up, w_down)
    gate_t = gate.T
    out = base
    tg = T // COMBINE_GROUPS
    for q in range(COMBINE_GROUPS):
        z = _sc_gather_rows(ys, dest[:, q * tg:(q + 1) * tg].reshape(-1)).reshape(TOP_K, tg, D // 2)
        out = _final(z, gate_t, out, ln2_g[None], ln2_b[None], q * tg)
    return out.reshape(B, S, D)


def kernel(x, w_in, rel_bias, gain_a, gain_b, w_out, ln1_g, ln1_b, router_w, router_bias,
           w_gate, w_up, w_down, ws_gate, ws_up, ws_down, ln2_g, ln2_b):
    depth = w_in.shape[0]
    alpha = (2 * depth) ** 0.25
    for l in range(depth):
        x = _layer(x, w_in[l], rel_bias[l], gain_a[l], gain_b[l], w_out[l], ln1_g[l], ln1_b[l],
                   router_w[l], router_bias[l], w_gate[l], w_up[l], w_down[l],
                   ws_gate[l], ws_up[l], ws_down[l], ln2_g[l], ln2_b[l], alpha)
    return x
```

```python
import functools

import jax
import jax.numpy as jnp
from jax import lax
from jax.experimental import pallas as pl
from jax.experimental.pallas import tpu as pltpu
from jax.experimental.pallas import tpu_sc as plsc

CHUNK = 64
HEAD_DIM = 64
LEFT_CHUNKS = 8
MAX_REL = 128
N_EXPERTS = 256
TOP_K = 8
N_GROUPS = 8
TOPK_GROUPS = 4
EXPERTS_PER_GROUP = N_EXPERTS // N_GROUPS
ROUTED_SCALE = 2.5
LN_EPS = 1e-5
RMS_EPS = 1e-6

LANES = 128
HEADS_PER_BLOCK = LANES // HEAD_DIM
QBLK_A = 2 * CHUNK
BAND_A = (LEFT_CHUNKS + 2) * CHUNK
BIAS_W = BAND_A + QBLK_A
BLOCKS_PER_TRIP_A = 4
TILE_B = 256
WIDTH_B = 2 * HEAD_DIM
TILES_PER_TRIP_B = 3
ROW_BLOCK = 256
EXPERT_RING = 8
TOK_BLOCK = 256
COMBINE_GROUPS = 4
MASK_VALUE = -1e30
V7X_VMEM_BYTES = 64 * 1024 * 1024
VMEM_LIMIT = V7X_VMEM_BYTES * 3 // 4

F32 = jnp.float32
BF16 = jnp.bfloat16


def _cparams(*sem):
    return pltpu.CompilerParams(dimension_semantics=sem, vmem_limit_bytes=VMEM_LIMIT)


def _dot(a, b):
    return jnp.dot(a, b, preferred_element_type=F32)


def _dot_nt(a, b):
    return lax.dot_general(a, b, (((1,), (1,)), ((), ())), preferred_element_type=F32)


def _pack_bf16_halves(x):
    half = x.shape[1] // 2
    bits = lax.bitcast_convert_type(x.astype(BF16).astype(F32), jnp.uint32)
    return (bits[:, :half] >> 16) | (bits[:, half:] & jnp.uint32(0xFFFF0000))


def _unpack_bf16_halves(w):
    lo = lax.bitcast_convert_type(w << 16, F32)
    hi = lax.bitcast_convert_type(w & jnp.uint32(0xFFFF0000), F32)
    return lo, hi


def _qkv_kernel(x_ref, w_ref, o_ref, *, col_chunk):
    xb = x_ref[...].astype(BF16)
    for n in range(w_ref.shape[1] // col_chunk):
        cols = slice(n * col_chunk, (n + 1) * col_chunk)
        o_ref[:, cols] = _dot(xb, w_ref[:, cols]).astype(BF16)


def _qkv_proj(xt, w_b):
    T, D = xt.shape
    N = w_b.shape[1]
    tm = 512
    return pl.pallas_call(
        functools.partial(_qkv_kernel, col_chunk=512),
        grid=(T // tm,),
        in_specs=[pl.BlockSpec((tm, D), lambda i: (i, 0)),
                  pl.BlockSpec((D, N), lambda i: (0, 0))],
        out_specs=pl.BlockSpec((tm, N), lambda i: (i, 0)),
        out_shape=jax.ShapeDtypeStruct((T, N), BF16),
        compiler_params=_cparams("parallel"),
        name="qkv_proj",
    )(xt, w_b)


def _attn_a_kernel(q_ref, k_ref, v_ref, w_ref, o_ref, bias_ref):
    S = q_ref.shape[0]
    nblk = S // QBLK_A
    lead = LEFT_CHUNKS * CHUNK
    n_edge = min(lead // QBLK_A, nblk)

    qc = lax.broadcasted_iota(jnp.int32, (QBLK_A, BAND_A), 0) // CHUNK
    kc = lax.broadcasted_iota(jnp.int32, (QBLK_A, BAND_A), 1) // CHUNK
    allowed = (kc >= qc) & (kc <= qc + LEFT_CHUNKS)
    for h in range(HEADS_PER_BLOCK):
        wb = jnp.broadcast_to(w_ref[h], (QBLK_A, BIAS_W))
        toeplitz = pltpu.roll(wb, BIAS_W - (QBLK_A - 1), 1, stride=1, stride_axis=0)
        bias_ref[h] = jnp.where(allowed, toeplitz[:, :BAND_A], MASK_VALUE)

    lane = lax.broadcasted_iota(jnp.int32, (QBLK_A, LANES), 1)
    head_of_lane = lane // HEAD_DIM

    def blocks(specs):
        work = []
        for p, kstart, nk, bias_off in specs:
            q = q_ref[pl.ds(p * QBLK_A, QBLK_A), :]
            k = k_ref[pl.ds(kstart, nk), :]
            v = v_ref[pl.ds(kstart, nk), :]
            for h in range(HEADS_PER_BLOCK):
                qh = jnp.where(head_of_lane == h, q, jnp.zeros_like(q))
                work.append(dict(s=_dot_nt(qh, k), v=v, bias=bias_ref[h, :, bias_off:bias_off + nk]))
        for w in work:
            s = w["s"] + w["bias"]
            e = jnp.exp(s - jnp.max(s, axis=-1, keepdims=True))
            w["l"] = jnp.sum(e, axis=-1, keepdims=True)
            w["e"] = e.astype(BF16)
        outs = [_dot(w["e"], w["v"]) / w["l"] for w in work]
        for i, spec in enumerate(specs):
            o = outs[i * HEADS_PER_BLOCK]
            for h in range(1, HEADS_PER_BLOCK):
                o = jnp.where(head_of_lane == h, outs[i * HEADS_PER_BLOCK + h], o)
            o_ref[pl.ds(spec[0] * QBLK_A, QBLK_A), :] = o

    blocks([(p, 0, (p + 1) * QBLK_A, lead - p * QBLK_A) for p in range(n_edge)])

    def full(p):
        return (p, pl.multiple_of(p * QBLK_A - lead, QBLK_A), BAND_A, 0)

    n_full = nblk - n_edge
    per_trip = BLOCKS_PER_TRIP_A
    rem = n_full % per_trip
    if rem:
        blocks([full(n_edge + r) for r in range(rem)])
    if n_full >= per_trip:
        def body(i, c):
            p = n_edge + rem + per_trip * i
            blocks([full(p + r) for r in range(per_trip)])
            return c
        lax.fori_loop(0, n_full // per_trip, body, 0)


def _attn_a(qkv, bias_w, B, S, d_a):
    n_hb = d_a // LANES
    seg = d_a // LANES
    return pl.pallas_call(
        _attn_a_kernel,
        grid=(B, n_hb),
        in_specs=[pl.BlockSpec((None, S, LANES), lambda b, h: (b, 0, h)),
                  pl.BlockSpec((None, S, LANES), lambda b, h: (b, 0, seg + h)),
                  pl.BlockSpec((None, S, LANES), lambda b, h: (b, 0, 2 * seg + h)),
                  pl.BlockSpec((HEADS_PER_BLOCK, 1, BIAS_W), lambda b, h: (h, 0, 0))],
        out_specs=pl.BlockSpec((None, S, LANES), lambda b, h: (b, 0, h)),
        out_shape=jax.ShapeDtypeStruct((B, S, d_a), F32),
        scratch_shapes=[pltpu.VMEM((HEADS_PER_BLOCK, QBLK_A, BAND_A), F32)],
        compiler_params=_cparams("parallel", "parallel"),
        name="attn_chunked",
    )(qkv, qkv, qkv, bias_w)


def _rel_bias_by_offset(rel_bias):
    dist = jnp.clip(BAND_A - 1 - jnp.arange(BIAS_W), -MAX_REL, MAX_REL) + MAX_REL
    return rel_bias[:, dist].astype(F32)[:, None, :]


def _attn_b_kernel(q_ref, k_ref, v_ref, o_ref):
    qi = pl.program_id(2)
    t = TILE_B
    row = lax.broadcasted_iota(jnp.int32, (t, t), 0)
    col = lax.broadcasted_iota(jnp.int32, (t, t), 1)
    from_s = (row >= col).astype(BF16)
    from_s2 = jnp.concatenate([from_s, from_s], axis=0)
    causal = col < row
    q = q_ref[...]
    n_heads = q_ref.shape[1] // HEAD_DIM
    head_of_lane = lax.broadcasted_iota(jnp.int32, q.shape, 1) // HEAD_DIM
    qs = [jnp.where(head_of_lane == h, q, jnp.zeros_like(q)) for h in range(n_heads)]

    def scores(js):
        zs = []
        for j in js:
            kt = k_ref[pl.ds(pl.multiple_of(j * t, t), t), :]
            zs += [_dot_nt(qh, kt) for qh in qs]
        return zs

    def tiles(js, zs, state, first_is_diag=False):
        accs, carries = list(state[0::2]), list(state[1::2])
        work = []
        for n, j in enumerate(js):
            vt = v_ref[pl.ds(pl.multiple_of(j * t, t), t), :]
            for h in range(n_heads):
                work.append(dict(h=h, diag=first_is_diag and n == 0, v=vt, z=zs[n * n_heads + h]))
        for w in work:
            z = w["z"]
            sp = jnp.maximum(z, 0.0) + jnp.log(1.0 + jnp.exp(-jnp.abs(z)))
            if w["diag"]:
                sp = jnp.where(causal, sp, 0.0)
            hi = sp.astype(BF16)
            lo = (sp - hi.astype(F32)).astype(BF16)
            w["suffix"] = _dot(jnp.concatenate([hi, lo], axis=-1), from_s2)
            w["rowsum"] = jnp.sum(sp, axis=-1, keepdims=True)
        for w in work:
            h = w["h"]
            if w["diag"]:
                a = jnp.where(causal, jnp.exp(w["z"] - w["suffix"]), 0.0)
                accs[h] = _dot(a.astype(BF16), w["v"])
                carries[h] = w["rowsum"]
            else:
                a = jnp.exp(w["z"] - w["suffix"] - carries[h])
                accs[h] = accs[h] + _dot(a.astype(BF16), w["v"])
                carries[h] = carries[h] + w["rowsum"]
        return tuple(x for pair in zip(accs, carries) for x in pair)

    per_trip = TILES_PER_TRIP_B
    empty = (None,) * (2 * n_heads)

    def opening(extra):
        js = [qi - r for r in range(extra + 1)]
        return lambda: tiles(js, scores(js), empty, True)

    state = lax.switch(qi % per_trip, [opening(r) for r in range(per_trip)])
    first = qi - 1 - qi % per_trip

    def body(it, st):
        js = [first - per_trip * it - r for r in range(per_trip)]
        return tiles(js, scores(js), st)

    state = lax.fori_loop(0, qi // per_trip, body, state)
    out = state[0]
    for h in range(1, n_heads):
        out = jnp.where(head_of_lane == h, state[2 * h], out)
    o_ref[...] = out


def _attn_b(qkv, B, S, d_a, d_b):
    w = WIDTH_B
    n_hb = d_b // w
    base = 3 * d_a // w
    seg = d_b // w
    return pl.pallas_call(
        _attn_b_kernel,
        grid=(B, n_hb, S // TILE_B),
        in_specs=[pl.BlockSpec((None, TILE_B, w), lambda b, h, i: (b, i, base + h)),
                  pl.BlockSpec((None, S, w), lambda b, h, i: (b, 0, base + seg + h)),
                  pl.BlockSpec((None, S, w), lambda b, h, i: (b, 0, base + 2 * seg + h))],
        out_specs=pl.BlockSpec((None, TILE_B, w), lambda b, h, i: (b, i, h)),
        out_shape=jax.ShapeDtypeStruct((B, S, d_b), F32),
        compiler_params=_cparams("parallel", "parallel", "parallel"),
        name="attn_stickbreak",
    )(qkv, qkv, qkv)


def _layer_norm(r, g, b):
    mu = jnp.mean(r, axis=-1, keepdims=True)
    c = r - mu
    var = jnp.mean(c * c, axis=-1, keepdims=True)
    return c * lax.rsqrt(var + LN_EPS) * g + b


def _rms_norm(y, g):
    ms = jnp.mean(y * y, axis=-1, keepdims=True)
    return y * lax.rsqrt(ms + RMS_EPS) * g


def _silu(g):
    return g * jax.nn.sigmoid(g)


def _post_kernel(ya_ref, yb_ref, x_ref, ga_ref, gb_ref, wo_ref, g1_ref, b1_ref, rw_ref, rbias_ref,
                 wsg_ref, wsu_ref, wsd_ref, x1_ref, base_ref, idx_ref, gate_ref, rank_ref, cnt_ref, carry_ref,
                 *, alpha):
    @pl.when(pl.program_id(0) == 0)
    def _():
        carry_ref[...] = jnp.zeros_like(carry_ref)

    d_a = ya_ref.shape[1]
    na = _rms_norm(ya_ref[...], ga_ref[...]).astype(BF16)
    nb = _rms_norm(yb_ref[...], gb_ref[...]).astype(BF16)
    h = _dot(na, wo_ref[:d_a, :]) + _dot(nb, wo_ref[d_a:, :])
    x1 = _layer_norm(alpha * x_ref[...] + h, g1_ref[...], b1_ref[...])
    x1_ref[...] = _pack_bf16_halves(x1)
    xb = x1.astype(BF16)
    logits = _dot_nt(rw_ref[...], xb)
    hs = _silu(_dot(xb, wsg_ref[...])) * _dot(xb, wsu_ref[...])
    base_ref[...] = alpha * x1 + _dot(hs.astype(BF16), wsd_ref[...])

    tb = TOK_BLOCK
    row = lax.broadcasted_iota(jnp.int32, (tb, tb), 0)
    col = lax.broadcasted_iota(jnp.int32, (tb, tb), 1)
    earlier = (row < col).astype(BF16)
    carry = carry_ref[...]
    for n in range(logits.shape[1] // tb):
        cols = slice(n * tb, (n + 1) * tb)
        idx, gate, rank, carry = _route_block(logits[:, cols], rbias_ref[...], carry, earlier)
        idx_ref[:, cols] = idx
        gate_ref[:, cols] = gate
        rank_ref[:, cols] = rank
    carry_ref[...] = carry
    cnt_ref[...] = carry.astype(jnp.int32)


def _post(ya, yb, xt, gain_a, gain_b, wo_b, g1, b1, rw_b, rbias, wsg_b, wsu_b, wsd_b, alpha):
    T, D = xt.shape
    d_a, d_b = ya.shape[1], yb.shape[1]
    E = rw_b.shape[0]
    De = wsg_b.shape[1]
    tm = 4 * TOK_BLOCK
    row = lambda i: (i, 0)
    col = lambda i: (0, i)
    fix = lambda i: (0, 0)
    return pl.pallas_call(
        functools.partial(_post_kernel, alpha=alpha),
        grid=(T // tm,),
        in_specs=[pl.BlockSpec((tm, d_a), row), pl.BlockSpec((tm, d_b), row), pl.BlockSpec((tm, D), row),
                  pl.BlockSpec((1, d_a), fix), pl.BlockSpec((1, d_b), fix),
                  pl.BlockSpec((d_a + d_b, D), fix), pl.BlockSpec((1, D), fix), pl.BlockSpec((1, D), fix),
                  pl.BlockSpec((E, D), fix), pl.BlockSpec((E, 1), fix),
                  pl.BlockSpec((D, De), fix), pl.BlockSpec((D, De), fix), pl.BlockSpec((De, D), fix)],
        out_specs=[pl.BlockSpec((tm, D // 2), row), pl.BlockSpec((tm, D), row),
                   pl.BlockSpec((TOP_K, tm), col), pl.BlockSpec((TOP_K, tm), col), pl.BlockSpec((TOP_K, tm), col),
                   pl.BlockSpec((E, 1), fix)],
        out_shape=[jax.ShapeDtypeStruct((T, D // 2), jnp.uint32), jax.ShapeDtypeStruct((T, D), F32),
                   jax.ShapeDtypeStruct((TOP_K, T), jnp.int32), jax.ShapeDtypeStruct((TOP_K, T), F32),
                   jax.ShapeDtypeStruct((TOP_K, T), jnp.int32), jax.ShapeDtypeStruct((E, 1), jnp.int32)],
        scratch_shapes=[pltpu.VMEM((E, 1), F32)],
        compiler_params=_cparams("arbitrary"),
        name="post_attn_route",
    )(ya, yb, xt, gain_a, gain_b, wo_b, g1, b1, rw_b, rbias, wsg_b, wsu_b, wsd_b)


def _route_block(logits, rbias, carry, earlier):
    E, tm = logits.shape
    neg = -jnp.inf
    scores = jax.nn.sigmoid(logits)
    sel = scores + rbias
    eidx = lax.broadcasted_iota(jnp.int32, (E, tm), 0).astype(F32)

    def first_argmax(v, ids):
        m = jnp.max(v, axis=0, keepdims=True)
        return m, jnp.min(jnp.where(v == m, ids, float(E)), axis=0, keepdims=True)

    grp_scores = []
    ids = lax.broadcasted_iota(jnp.int32, (EXPERTS_PER_GROUP, tm), 0).astype(F32)
    for g in range(N_GROUPS):
        v = sel[g * EXPERTS_PER_GROUP:(g + 1) * EXPERTS_PER_GROUP, :]
        m1, i1 = first_argmax(v, ids)
        m2 = jnp.max(jnp.where(ids == i1, neg, v), axis=0, keepdims=True)
        grp_scores.append(m1 + m2)
    parts = []
    for g in range(N_GROUPS):
        beaten = jnp.zeros((1, tm), jnp.int32)
        for o in range(N_GROUPS):
            if o != g:
                s, t = grp_scores[o], grp_scores[g]
                beaten = beaten + ((s > t) | ((s == t) & (o < g))).astype(jnp.int32)
        rows = slice(g * EXPERTS_PER_GROUP, (g + 1) * EXPERTS_PER_GROUP)
        parts.append(jnp.where(beaten < TOPK_GROUPS, sel[rows, :], neg))
    selm = jnp.concatenate(parts, axis=0)

    hits, gates, ids_k = [], [], []
    chosen = jnp.zeros((E, tm), F32)
    gate_sum = jnp.zeros((1, tm), F32)
    for k in range(TOP_K):
        _, ik = first_argmax(selm, eidx)
        hit = eidx == ik
        gk = jnp.sum(jnp.where(hit, scores, 0.0), axis=0, keepdims=True)
        selm = jnp.where(hit, neg, selm)
        chosen = jnp.where(hit, 1.0, chosen)
        gate_sum = gate_sum + gk
        hits.append(hit)
        gates.append(gk)
        ids_k.append(ik)

    before = _dot(chosen.astype(BF16), earlier) + carry
    ranks = [jnp.sum(jnp.where(hit, before, 0.0), axis=0, keepdims=True) for hit in hits]
    return (jnp.concatenate(ids_k, axis=0).astype(jnp.int32),
            jnp.concatenate(gates, axis=0) / gate_sum * ROUTED_SCALE,
            jnp.concatenate(ranks, axis=0).astype(jnp.int32),
            carry + jnp.sum(chosen, axis=1, keepdims=True))


def _dest_kernel(idx_ref, rank_ref, start_ref, dest_ref):
    K, tm = idx_ref.shape
    E = start_ref.shape[0]
    eidx = lax.broadcasted_iota(jnp.int32, (E, tm), 0)
    idx = idx_ref[...]
    start = start_ref[...]
    rows = [jnp.sum(jnp.where(eidx == idx[k:k + 1, :], start, 0.0), axis=0, keepdims=True) for k in range(K)]
    dest_ref[...] = jnp.concatenate(rows, axis=0).astype(jnp.int32) + rank_ref[...]


def _dest(idx, rank, start_f):
    K, T = idx.shape
    E = start_f.shape[0]
    tm = 2 * TOK_BLOCK
    col = lambda i: (0, i)
    return pl.pallas_call(
        _dest_kernel,
        grid=(T // tm,),
        in_specs=[pl.BlockSpec((K, tm), col), pl.BlockSpec((K, tm), col), pl.BlockSpec((E, 1), lambda i: (0, 0))],
        out_specs=pl.BlockSpec((K, tm), col),
        out_shape=jax.ShapeDtypeStruct((K, T), jnp.int32),
        compiler_params=_cparams("parallel"),
        name="dest_rows",
    )(idx, rank, start_f)


def _expert_kernel(first_ref, nblk_ref, nused_ref, xs_hbm, wg_ref, wu_ref, wd_ref, ys_hbm,
                   xbuf, ybuf, wgu_s, wd_s, in_sem, out_sem):
    ring = xbuf.shape[0]
    e = pl.program_id(0)
    De = wg_ref.shape[2]
    n = nblk_ref[e]
    g0 = first_ref[e]
    n_used = nused_ref[0]

    def rows(g):
        return pl.ds(pl.multiple_of(g * ROW_BLOCK, ROW_BLOCK), ROW_BLOCK)

    def in_copy(g, slot):
        return pltpu.make_async_copy(xs_hbm.at[rows(g), :], xbuf.at[slot], in_sem.at[slot])

    def out_copy(g, slot):
        return pltpu.make_async_copy(ybuf.at[slot], ys_hbm.at[rows(g), :], out_sem.at[slot])

    @pl.when(e == 0)
    def _():
        for g in range(ring - 2):
            @pl.when(g < n_used)
            def _():
                in_copy(g, g).start()

    def acquire(g, ahead):
        slot = g & (ring - 1)
        in_copy(g, slot).wait()

        @pl.when(g + ahead < n_used)
        def _():
            in_copy(g + ahead, (g + ahead) & (ring - 1)).start()

        @pl.when(g >= ring)
        def _():
            out_copy(g - ring, slot).wait()
        return slot

    def ffn(pieces):
        dh = xbuf.shape[2]
        gus = []
        for slot, rs in pieces:
            lo, hi = _unpack_bf16_halves(xbuf[slot, rs, :])
            gus.append(_dot(lo.astype(BF16), wgu_s[:dh, :]) + _dot(hi.astype(BF16), wgu_s[dh:, :]))
        hs = [(_silu(gu[:, :De]) * gu[:, De:]).astype(BF16) for gu in gus]
        for (slot, rs), h in zip(pieces, hs):
            ybuf[slot, rs, :] = _pack_bf16_halves(_dot(h, wd_s[...]))

    @pl.when(n > 0)
    def _():
        wgu_s[:, :De] = wg_ref[0].astype(BF16)
        wgu_s[:, De:] = wu_ref[0].astype(BF16)
        wd_s[...] = wd_ref[0].astype(BF16)

        def pair(j, c):
            g = g0 + 2 * j
            slots = [acquire(g + b, ring - 2) for b in range(2)]
            ffn([(slot, slice(None)) for slot in slots])
            for b, slot in enumerate(slots):
                out_copy(g + b, slot).start()
            return c

        lax.fori_loop(0, n // 2, pair, 0)

        @pl.when(n % 2 == 1)
        def _():
            g = g0 + n - 1
            slot = acquire(g, ring - 2)
            half = ROW_BLOCK // 2
            ffn([(slot, slice(0, half)), (slot, slice(half, ROW_BLOCK))])
            out_copy(g, slot).start()

    @pl.when(e == pl.num_programs(0) - 1)
    def _():
        for back in range(1, ring + 1):
            @pl.when(n_used >= back)
            def _():
                out_copy(n_used - back, (n_used - back) & (ring - 1)).wait()


def _experts(first_block, n_blocks, n_used, xs, w_gate, w_up, w_down):
    n_rows, Dh = xs.shape
    D = 2 * Dh
    E, _, De = w_gate.shape
    grid_spec = pltpu.PrefetchScalarGridSpec(
        num_scalar_prefetch=3,
        grid=(E,),
        in_specs=[pl.BlockSpec(memory_space=pl.ANY),
                  pl.BlockSpec((1, D, De), lambda e, *_: (e, 0, 0)),
                  pl.BlockSpec((1, D, De), lambda e, *_: (e, 0, 0)),
                  pl.BlockSpec((1, De, D), lambda e, *_: (e, 0, 0))],
        out_specs=pl.BlockSpec(memory_space=pl.ANY),
        scratch_shapes=[pltpu.VMEM((EXPERT_RING, ROW_BLOCK, Dh), jnp.uint32),
                        pltpu.VMEM((EXPERT_RING, ROW_BLOCK, Dh), jnp.uint32),
                        pltpu.VMEM((D, 2 * De), BF16), pltpu.VMEM((De, D), BF16),
                        pltpu.SemaphoreType.DMA((EXPERT_RING,)), pltpu.SemaphoreType.DMA((EXPERT_RING,))],
    )
    return pl.pallas_call(
        _expert_kernel,
        grid_spec=grid_spec,
        out_shape=jax.ShapeDtypeStruct((n_rows, Dh), jnp.uint32),
        compiler_params=_cparams("arbitrary"),
        name="expert_ffn",
    )(first_block, n_blocks, n_used, xs, w_gate, w_up, w_down)


SC_WINDOW = 128


def _sc_mesh():
    return plsc.VectorSubcoreMesh(core_axis_name="core", subcore_axis_name="subcore")


def _sc_workers():
    info = plsc.get_sparse_core_info()
    return info.num_cores, info.num_cores * info.num_subcores


def _sc_worker_id(num_cores):
    return lax.axis_index("subcore") * num_cores + lax.axis_index("core")


def _sc_scatter_rows(x, dest_flat, n_rows):
    T, D = x.shape
    K = dest_flat.shape[0] // T
    nc, nw = _sc_workers()
    per_w = T // nw
    assert T % nw == 0 and per_w % SC_WINDOW == 0

    @functools.partial(
        pl.kernel, out_type=jax.ShapeDtypeStruct((n_rows, D), x.dtype), mesh=_sc_mesh(),
        scratch_types=[pltpu.VMEM((SC_WINDOW,), jnp.int32), pltpu.VMEM((SC_WINDOW, D), x.dtype)],
        name="sc_dispatch")
    def run(x_hbm, i_hbm, o_hbm, idx_v, rows_v):
        first = _sc_worker_id(nc) * per_w

        @pl.loop(0, per_w // SC_WINDOW)
        def _(c):
            base = first + c * SC_WINDOW
            pltpu.sync_copy(x_hbm.at[pl.ds(base, SC_WINDOW)], rows_v)
            for k in range(K):
                pltpu.sync_copy(i_hbm.at[pl.ds(k * T + base, SC_WINDOW)], idx_v)
                pltpu.sync_copy(rows_v, o_hbm.at[idx_v])

    return run(x, dest_flat)


def _sc_gather_rows(ys, dest_flat):
    N = dest_flat.shape[0]
    D = ys.shape[1]
    nc, nw = _sc_workers()
    per_w = N // nw
    assert N % nw == 0 and per_w % SC_WINDOW == 0

    @functools.partial(
        pl.kernel, out_type=jax.ShapeDtypeStruct((N, D), ys.dtype), mesh=_sc_mesh(),
        scratch_types=[pltpu.VMEM((SC_WINDOW,), jnp.int32), pltpu.VMEM((SC_WINDOW, D), ys.dtype)],
        name="sc_combine_gather")
    def run(y_hbm, i_hbm, o_hbm, idx_v, rows_v):
        first = _sc_worker_id(nc) * per_w

        @pl.loop(0, per_w // SC_WINDOW)
        def _(c):
            base = first + c * SC_WINDOW
            pltpu.sync_copy(i_hbm.at[pl.ds(base, SC_WINDOW)], idx_v)
            pltpu.sync_copy(y_hbm.at[idx_v], rows_v)
            pltpu.sync_copy(rows_v, o_hbm.at[pl.ds(base, SC_WINDOW)])

    return run(ys, dest_flat)


def _final_kernel(z_ref, gate_ref, base_ref, g_ref, b_ref, o_ref):
    gate = gate_ref[...]
    acc_lo = acc_hi = None
    for k in range(TOP_K):
        lo, hi = _unpack_bf16_halves(z_ref[k])
        g = gate[:, k:k + 1]
        acc_lo = g * lo if acc_lo is None else acc_lo + g * lo
        acc_hi = g * hi if acc_hi is None else acc_hi + g * hi
    acc = base_ref[...] + jnp.concatenate([acc_lo, acc_hi], axis=-1)
    o_ref[...] = _layer_norm(acc, g_ref[...], b_ref[...])


def _final(z, gate, base, g2, b2, first_token):
    T, D = base.shape
    tb = TOK_BLOCK
    off = first_token // tb
    return pl.pallas_call(
        _final_kernel,
        grid=(z.shape[1] // tb,),
        in_specs=[pl.BlockSpec((TOP_K, tb, D // 2), lambda i: (0, i, 0)),
                  pl.BlockSpec((tb, TOP_K), lambda i: (i + off, 0)),
                  pl.BlockSpec((tb, D), lambda i: (i + off, 0)),
                  pl.BlockSpec((1, D), lambda i: (0, 0)),
                  pl.BlockSpec((1, D), lambda i: (0, 0))],
        out_specs=pl.BlockSpec((tb, D), lambda i: (i + off, 0)),
        out_shape=jax.ShapeDtypeStruct((T, D), F32),
        input_output_aliases={2: 0},
        compiler_params=_cparams("parallel"),
        name="final_sum_ln",
    )(z, gate, base, g2, b2)


def _layer(x, w_in, rel_bias, gain_a, gain_b, w_out, ln1_g, ln1_b, router_w, router_bias,
           w_gate, w_up, w_down, ws_gate, ws_up, ws_down, ln2_g, ln2_b, alpha):
    B, S, D = x.shape
    T = B * S
    d_a = gain_a.shape[0]
    d_b = gain_b.shape[0]
    assert S % TILE_B == 0 and T % 512 == 0 and d_a % LANES == 0 and d_b % WIDTH_B == 0 and (3 * d_a) % WIDTH_B == 0
    xt = x.reshape(T, D)

    scale = HEAD_DIM ** -0.5
    col = jnp.arange(w_in.shape[1])
    is_q = (col < d_a) | ((col >= 3 * d_a) & (col < 3 * d_a + d_b))
    w_in_b = (w_in * jnp.where(is_q, scale, 1.0)[None, :]).astype(BF16)

    qkv = _qkv_proj(xt, w_in_b).reshape(B, S, -1)
    ya = _attn_a(qkv, _rel_bias_by_offset(rel_bias), B, S, d_a).reshape(T, d_a)
    yb = _attn_b(qkv, B, S, d_a, d_b).reshape(T, d_b)

    x1p, base, idx, gate, rank, cnt = _post(
        ya, yb, xt, gain_a[None], gain_b[None], w_out.astype(BF16), ln1_g[None], ln1_b[None],
        router_w.T.astype(BF16), router_bias[:, None].astype(F32),
        ws_gate.astype(BF16), ws_up.astype(BF16), ws_down.astype(BF16), alpha)
    cnt = cnt[:, 0]
    padded = (cnt + ROW_BLOCK - 1) // ROW_BLOCK * ROW_BLOCK
    ends = jnp.cumsum(padded)
    start = (ends - padded).astype(jnp.int32)
    n_rows = T * TOP_K + N_EXPERTS * ROW_BLOCK
    first_block = start // ROW_BLOCK
    n_blocks = (padded // ROW_BLOCK).astype(jnp.int32)
    n_used = (ends[-1:] // ROW_BLOCK).astype(jnp.int32)
    dest = _dest(idx, rank, start.astype(F32)[:, None])

    xs = _sc_scatter_rows(x1p, dest.reshape(-1), n_rows)
    ys = _experts(first_block, n_blocks, n_used, xs, w_gate, w_up, w_down)
    gate_t = gate.T
    out = base
    tg = T // COMBINE_GROUPS
    for q in range(COMBINE_GROUPS):
        z = _sc_gather_rows(ys, dest[:, q * tg:(q + 1) * tg].reshape(-1)).reshape(TOP_K, tg, D // 2)
        out = _final(z, gate_t, out, ln2_g[None], ln2_b[None], q * tg)
    return out.reshape(B, S, D)


def kernel(x, w_in, rel_bias, gain_a, gain_b, w_out, ln1_g, ln1_b, router_w, router_bias,
           w_gate, w_up, w_down, ws_gate, ws_up, ws_down, ln2_g, ln2_b):
    depth = w_in.shape[0]
    alpha = (2 * depth) ** 0.25
    for l in range(depth):
        x = _layer(x, w_in[l], rel_bias[l], gain_a[l], gain_b[l], w_out[l], ln1_g[l], ln1_b[l],
                   router_w[l], router_bias[l], w_gate[l], w_up[l], w_down[l],
                   ws_gate[l], ws_up[l], ws_down[l], ln2_g[l], ln2_b[l], alpha)
    return x
```

```python
import functools

import jax
import jax.numpy as jnp
from jax import lax
from jax.experimental import pallas as pl
from jax.experimental.pallas import tpu as pltpu
from jax.experimental.pallas import tpu_sc as plsc

CHUNK = 64
HEAD_DIM = 64
LEFT_CHUNKS = 8
MAX_REL = 128
N_EXPERTS = 256
TOP_K = 8
N_GROUPS = 8
TOPK_GROUPS = 4
EXPERTS_PER_GROUP = N_EXPERTS // N_GROUPS
ROUTED_SCALE = 2.5
LN_EPS = 1e-5
RMS_EPS = 1e-6

LANES = 128
HEADS_PER_BLOCK = LANES // HEAD_DIM
QBLK_A = 2 * CHUNK
BAND_A = (LEFT_CHUNKS + 2) * CHUNK
BIAS_W = BAND_A + QBLK_A
BLOCKS_PER_TRIP_A = 4
TILE_B = 256
WIDTH_B = 2 * HEAD_DIM
TILES_PER_TRIP_B = 3
ROW_BLOCK = 256
EXPERT_RING = 8
TOK_BLOCK = 256
COMBINE_GROUPS = 4
MASK_VALUE = -1e30
V7X_VMEM_BYTES = 64 * 1024 * 1024
VMEM_LIMIT = V7X_VMEM_BYTES * 3 // 4

F32 = jnp.float32
BF16 = jnp.bfloat16


def _cparams(*sem):
    return pltpu.CompilerParams(dimension_semantics=sem, vmem_limit_bytes=VMEM_LIMIT)


def _dot(a, b):
    return jnp.dot(a, b, preferred_element_type=F32)


def _dot_nt(a, b):
    return lax.dot_general(a, b, (((1,), (1,)), ((), ())), preferred_element_type=F32)


def _pack_bf16_halves(x):
    half = x.shape[1] // 2
    bits = lax.bitcast_convert_type(x.astype(BF16).astype(F32), jnp.uint32)
    return (bits[:, :half] >> 16) | (bits[:, half:] & jnp.uint32(0xFFFF0000))


def _unpack_bf16_halves(w):
    lo = lax.bitcast_convert_type(w << 16, F32)
    hi = lax.bitcast_convert_type(w & jnp.uint32(0xFFFF0000), F32)
    return lo, hi


def _qkv_kernel(x_ref, w_ref, o_ref, *, col_chunk):
    xb = x_ref[...].astype(BF16)
    for n in range(w_ref.shape[1] // col_chunk):
        cols = slice(n * col_chunk, (n + 1) * col_chunk)
        o_ref[:, cols] = _dot(xb, w_ref[:, cols]).astype(BF16)


def _qkv_proj(xt, w_b):
    T, D = xt.shape
    N = w_b.shape[1]
    tm = 512
    return pl.pallas_call(
        functools.partial(_qkv_kernel, col_chunk=512),
        grid=(T // tm,),
        in_specs=[pl.BlockSpec((tm, D), lambda i: (i, 0)),
                  pl.BlockSpec((D, N), lambda i: (0, 0))],
        out_specs=pl.BlockSpec((tm, N), lambda i: (i, 0)),
        out_shape=jax.ShapeDtypeStruct((T, N), BF16),
        compiler_params=_cparams("parallel"),
        name="qkv_proj",
    )(xt, w_b)


def _attn_a_kernel(q_ref, k_ref, v_ref, w_ref, o_ref, bias_ref):
    S = q_ref.shape[0]
    nblk = S // QBLK_A
    lead = LEFT_CHUNKS * CHUNK
    n_edge = min(lead // QBLK_A, nblk)

    qc = lax.broadcasted_iota(jnp.int32, (QBLK_A, BAND_A), 0) // CHUNK
    kc = lax.broadcasted_iota(jnp.int32, (QBLK_A, BAND_A), 1) // CHUNK
    allowed = (kc >= qc) & (kc <= qc + LEFT_CHUNKS)
    for h in range(HEADS_PER_BLOCK):
        wb = jnp.broadcast_to(w_ref[h], (QBLK_A, BIAS_W))
        toeplitz = pltpu.roll(wb, BIAS_W - (QBLK_A - 1), 1, stride=1, stride_axis=0)
        bias_ref[h] = jnp.where(allowed, toeplitz[:, :BAND_A], MASK_VALUE)

    lane = lax.broadcasted_iota(jnp.int32, (QBLK_A, LANES), 1)
    head_of_lane = lane // HEAD_DIM

    def blocks(specs):
        work = []
        for p, kstart, nk, bias_off in specs:
            q = q_ref[pl.ds(p * QBLK_A, QBLK_A), :]
            k = k_ref[pl.ds(kstart, nk), :]
            v = v_ref[pl.ds(kstart, nk), :]
            for h in range(HEADS_PER_BLOCK):
                qh = jnp.where(head_of_lane == h, q, jnp.zeros_like(q))
                work.append(dict(s=_dot_nt(qh, k), v=v, bias=bias_ref[h, :, bias_off:bias_off + nk]))
        for w in work:
            s = w["s"] + w["bias"]
            e = jnp.exp(s - jnp.max(s, axis=-1, keepdims=True))
            w["l"] = jnp.sum(e, axis=-1, keepdims=True)
            w["e"] = e.astype(BF16)
        outs = [_dot(w["e"], w["v"]) / w["l"] for w in work]
        for i, spec in enumerate(specs):
            o = outs[i * HEADS_PER_BLOCK]
            for h in range(1, HEADS_PER_BLOCK):
                o = jnp.where(head_of_lane == h, outs[i * HEADS_PER_BLOCK + h], o)
            o_ref[pl.ds(spec[0] * QBLK_A, QBLK_A), :] = o

    blocks([(p, 0, (p + 1) * QBLK_A, lead - p * QBLK_A) for p in range(n_edge)])

    def full(p):
        return (p, pl.multiple_of(p * QBLK_A - lead, QBLK_A), BAND_A, 0)

    n_full = nblk - n_edge
    per_trip = BLOCKS_PER_TRIP_A
    rem = n_full % per_trip
    if rem:
        blocks([full(n_edge + r) for r in range(rem)])
    if n_full >= per_trip:
        def body(i, c):
            p = n_edge + rem + per_trip * i
            blocks([full(p + r) for r in range(per_trip)])
            return c
        lax.fori_loop(0, n_full // per_trip, body, 0)


def _attn_a(qkv, bias_w, B, S, d_a):
    n_hb = d_a // LANES
    seg = d_a // LANES
    return pl.pallas_call(
        _attn_a_kernel,
        grid=(B, n_hb),
        in_specs=[pl.BlockSpec((None, S, LANES), lambda b, h: (b, 0, h)),
                  pl.BlockSpec((None, S, LANES), lambda b, h: (b, 0, seg + h)),
                  pl.BlockSpec((None, S, LANES), lambda b, h: (b, 0, 2 * seg + h)),
                  pl.BlockSpec((HEADS_PER_BLOCK, 1, BIAS_W), lambda b, h: (h, 0, 0))],
        out_specs=pl.BlockSpec((None, S, LANES), lambda b, h: (b, 0, h)),
        out_shape=jax.ShapeDtypeStruct((B, S, d_a), F32),
        scratch_shapes=[pltpu.VMEM((HEADS_PER_BLOCK, QBLK_A, BAND_A), F32)],
        compiler_params=_cparams("parallel", "parallel"),
        name="attn_chunked",
    )(qkv, qkv, qkv, bias_w)


def _rel_bias_by_offset(rel_bias):
    dist = jnp.clip(BAND_A - 1 - jnp.arange(BIAS_W), -MAX_REL, MAX_REL) + MAX_REL
    return rel_bias[:, dist].astype(F32)[:, None, :]


def _attn_b_kernel(q_ref, k_ref, v_ref, o_ref):
    qi = pl.program_id(2)
    t = TILE_B
    row = lax.broadcasted_iota(jnp.int32, (t, t), 0)
    col = lax.broadcasted_iota(jnp.int32, (t, t), 1)
    from_s = (row >= col).astype(BF16)
    from_s2 = jnp.concatenate([from_s, from_s], axis=0)
    causal = col < row
    q = q_ref[...]
    n_heads = q_ref.shape[1] // HEAD_DIM
    head_of_lane = lax.broadcasted_iota(jnp.int32, q.shape, 1) // HEAD_DIM
    qs = [jnp.where(head_of_lane == h, q, jnp.zeros_like(q)) for h in range(n_heads)]

    def scores(js):
        zs = []
        for j in js:
            kt = k_ref[pl.ds(pl.multiple_of(j * t, t), t), :]
            zs += [_dot_nt(qh, kt) for qh in qs]
        return zs

    def tiles(js, zs, state, first_is_diag=False):
        accs, carries = list(state[0::2]), list(state[1::2])
        work = []
        for n, j in enumerate(js):
            vt = v_ref[pl.ds(pl.multiple_of(j * t, t), t), :]
            for h in range(n_heads):
                work.append(dict(h=h, diag=first_is_diag and n == 0, v=vt, z=zs[n * n_heads + h]))
        for w in work:
            z = w["z"]
            sp = jnp.maximum(z, 0.0) + jnp.log(1.0 + jnp.exp(-jnp.abs(z)))
            if w["diag"]:
                sp = jnp.where(causal, sp, 0.0)
            hi = sp.astype(BF16)
            lo = (sp - hi.astype(F32)).astype(BF16)
            w["suffix"] = _dot(jnp.concatenate([hi, lo], axis=-1), from_s2)
            w["rowsum"] = jnp.sum(sp, axis=-1, keepdims=True)
        for w in work:
            h = w["h"]
            if w["diag"]:
                a = jnp.where(causal, jnp.exp(w["z"] - w["suffix"]), 0.0)
                accs[h] = _dot(a.astype(BF16), w["v"])
                carries[h] = w["rowsum"]
            else:
                a = jnp.exp(w["z"] - w["suffix"] - carries[h])
                accs[h] = accs[h] + _dot(a.astype(BF16), w["v"])
                carries[h] = carries[h] + w["rowsum"]
        return tuple(x for pair in zip(accs, carries) for x in pair)

    per_trip = TILES_PER_TRIP_B
    empty = (None,) * (2 * n_heads)

    def opening(extra):
        js = [qi - r for r in range(extra + 1)]
        return lambda: tiles(js, scores(js), empty, True)

    extra = qi % per_trip
    extra = jnp.where((extra == 0) & (qi >= per_trip), per_trip, extra)
    state = lax.switch(extra, [opening(r) for r in range(per_trip + 1)])
    first = qi - 1 - extra

    def body(it, st):
        js = [first - per_trip * it - r for r in range(per_trip)]
        return tiles(js, scores(js), st)

    state = lax.fori_loop(0, (qi - extra) // per_trip, body, state)
    out = state[0]
    for h in range(1, n_heads):
        out = jnp.where(head_of_lane == h, state[2 * h], out)
    o_ref[...] = out


def _attn_b(qkv, B, S, d_a, d_b):
    w = WIDTH_B
    n_hb = d_b // w
    base = 3 * d_a // w
    seg = d_b // w
    return pl.pallas_call(
        _attn_b_kernel,
        grid=(B, n_hb, S // TILE_B),
        in_specs=[pl.BlockSpec((None, TILE_B, w), lambda b, h, i: (b, i, base + h)),
                  pl.BlockSpec((None, S, w), lambda b, h, i: (b, 0, base + seg + h)),
                  pl.BlockSpec((None, S, w), lambda b, h, i: (b, 0, base + 2 * seg + h))],
        out_specs=pl.BlockSpec((None, TILE_B, w), lambda b, h, i: (b, i, h)),
        out_shape=jax.ShapeDtypeStruct((B, S, d_b), F32),
        compiler_params=_cparams("parallel", "parallel", "parallel"),
        name="attn_stickbreak",
    )(qkv, qkv, qkv)


def _layer_norm(r, g, b):
    mu = jnp.mean(r, axis=-1, keepdims=True)
    c = r - mu
    var = jnp.mean(c * c, axis=-1, keepdims=True)
    return c * lax.rsqrt(var + LN_EPS) * g + b


def _rms_norm(y, g):
    ms = jnp.mean(y * y, axis=-1, keepdims=True)
    return y * lax.rsqrt(ms + RMS_EPS) * g


def _silu(g):
    return g * jax.nn.sigmoid(g)


def _post_kernel(ya_ref, yb_ref, x_ref, ga_ref, gb_ref, wo_ref, g1_ref, b1_ref, rw_ref, rbias_ref,
                 wsg_ref, wsu_ref, wsd_ref, x1_ref, base_ref, idx_ref, gate_ref, rank_ref, cnt_ref, carry_ref,
                 *, alpha):
    @pl.when(pl.program_id(0) == 0)
    def _():
        carry_ref[...] = jnp.zeros_like(carry_ref)

    d_a = ya_ref.shape[1]
    na = _rms_norm(ya_ref[...], ga_ref[...]).astype(BF16)
    nb = _rms_norm(yb_ref[...], gb_ref[...]).astype(BF16)
    h = _dot(na, wo_ref[:d_a, :]) + _dot(nb, wo_ref[d_a:, :])
    x1 = _layer_norm(alpha * x_ref[...] + h, g1_ref[...], b1_ref[...])
    x1_ref[...] = _pack_bf16_halves(x1)
    xb = x1.astype(BF16)
    logits = _dot_nt(rw_ref[...], xb)
    hs = _silu(_dot(xb, wsg_ref[...])) * _dot(xb, wsu_ref[...])
    base_ref[...] = alpha * x1 + _dot(hs.astype(BF16), wsd_ref[...])

    tb = TOK_BLOCK
    row = lax.broadcasted_iota(jnp.int32, (tb, tb), 0)
    col = lax.broadcasted_iota(jnp.int32, (tb, tb), 1)
    earlier = (row < col).astype(BF16)
    carry = carry_ref[...]
    for n in range(logits.shape[1] // tb):
        cols = slice(n * tb, (n + 1) * tb)
        idx, gate, rank, carry = _route_block(logits[:, cols], rbias_ref[...], carry, earlier)
        idx_ref[:, cols] = idx
        gate_ref[:, cols] = gate
        rank_ref[:, cols] = rank
    carry_ref[...] = carry
    cnt_ref[...] = carry.astype(jnp.int32)


def _post(ya, yb, xt, gain_a, gain_b, wo_b, g1, b1, rw_b, rbias, wsg_b, wsu_b, wsd_b, alpha):
    T, D = xt.shape
    d_a, d_b = ya.shape[1], yb.shape[1]
    E = rw_b.shape[0]
    De = wsg_b.shape[1]
    tm = 4 * TOK_BLOCK
    row = lambda i: (i, 0)
    col = lambda i: (0, i)
    fix = lambda i: (0, 0)
    return pl.pallas_call(
        functools.partial(_post_kernel, alpha=alpha),
        grid=(T // tm,),
        in_specs=[pl.BlockSpec((tm, d_a), row), pl.BlockSpec((tm, d_b), row), pl.BlockSpec((tm, D), row),
                  pl.BlockSpec((1, d_a), fix), pl.BlockSpec((1, d_b), fix),
                  pl.BlockSpec((d_a + d_b, D), fix), pl.BlockSpec((1, D), fix), pl.BlockSpec((1, D), fix),
                  pl.BlockSpec((E, D), fix), pl.BlockSpec((E, 1), fix),
                  pl.BlockSpec((D, De), fix), pl.BlockSpec((D, De), fix), pl.BlockSpec((De, D), fix)],
        out_specs=[pl.BlockSpec((tm, D // 2), row), pl.BlockSpec((tm, D), row),
                   pl.BlockSpec((TOP_K, tm), col), pl.BlockSpec((TOP_K, tm), col), pl.BlockSpec((TOP_K, tm), col),
                   pl.BlockSpec((E, 1), fix)],
        out_shape=[jax.ShapeDtypeStruct((T, D // 2), jnp.uint32), jax.ShapeDtypeStruct((T, D), F32),
                   jax.ShapeDtypeStruct((TOP_K, T), jnp.int32), jax.ShapeDtypeStruct((TOP_K, T), F32),
                   jax.ShapeDtypeStruct((TOP_K, T), jnp.int32), jax.ShapeDtypeStruct((E, 1), jnp.int32)],
        scratch_shapes=[pltpu.VMEM((E, 1), F32)],
        compiler_params=_cparams("arbitrary"),
        name="post_attn_route",
    )(ya, yb, xt, gain_a, gain_b, wo_b, g1, b1, rw_b, rbias, wsg_b, wsu_b, wsd_b)


def _route_block(logits, rbias, carry, earlier):
    E, tm = logits.shape
    neg = -jnp.inf
    scores = jax.nn.sigmoid(logits)
    sel = scores + rbias
    eidx = lax.broadcasted_iota(jnp.int32, (E, tm), 0).astype(F32)

    def first_argmax(v, ids):
        m = jnp.max(v, axis=0, keepdims=True)
        return m, jnp.min(jnp.where(v == m, ids, float(E)), axis=0, keepdims=True)

    grp_scores = []
    ids = lax.broadcasted_iota(jnp.int32, (EXPERTS_PER_GROUP, tm), 0).astype(F32)
    for g in range(N_GROUPS):
        v = sel[g * EXPERTS_PER_GROUP:(g + 1) * EXPERTS_PER_GROUP, :]
        m1, i1 = first_argmax(v, ids)
        m2 = jnp.max(jnp.where(ids == i1, neg, v), axis=0, keepdims=True)
        grp_scores.append(m1 + m2)
    parts = []
    for g in range(N_GROUPS):
        beaten = jnp.zeros((1, tm), jnp.int32)
        for o in range(N_GROUPS):
            if o != g:
                s, t = grp_scores[o], grp_scores[g]
                beaten = beaten + ((s > t) | ((s == t) & (o < g))).astype(jnp.int32)
        rows = slice(g * EXPERTS_PER_GROUP, (g + 1) * EXPERTS_PER_GROUP)
        parts.append(jnp.where(beaten < TOPK_GROUPS, sel[rows, :], neg))
    selm = jnp.concatenate(parts, axis=0)

    hits, gates, ids_k = [], [], []
    chosen = jnp.zeros((E, tm), F32)
    gate_sum = jnp.zeros((1, tm), F32)
    for k in range(TOP_K):
        _, ik = first_argmax(selm, eidx)
        hit = eidx == ik
        gk = jnp.sum(jnp.where(hit, scores, 0.0), axis=0, keepdims=True)
        selm = jnp.where(hit, neg, selm)
        chosen = jnp.where(hit, 1.0, chosen)
        gate_sum = gate_sum + gk
        hits.append(hit)
        gates.append(gk)
        ids_k.append(ik)

    before = _dot(chosen.astype(BF16), earlier) + carry
    ranks = [jnp.sum(jnp.where(hit, before, 0.0), axis=0, keepdims=True) for hit in hits]
    return (jnp.concatenate(ids_k, axis=0).astype(jnp.int32),
            jnp.concatenate(gates, axis=0) / gate_sum * ROUTED_SCALE,
            jnp.concatenate(ranks, axis=0).astype(jnp.int32),
            carry + jnp.sum(chosen, axis=1, keepdims=True))


def _dest_kernel(idx_ref, rank_ref, start_ref, dest_ref):
    K, tm = idx_ref.shape
    E = start_ref.shape[0]
    eidx = lax.broadcasted_iota(jnp.int32, (E, tm), 0)
    idx = idx_ref[...]
    start = start_ref[...]
    rows = [jnp.sum(jnp.where(eidx == idx[k:k + 1, :], start, 0.0), axis=0, keepdims=True) for k in range(K)]
    dest_ref[...] = jnp.concatenate(rows, axis=0).astype(jnp.int32) + rank_ref[...]


def _dest(idx, rank, start_f):
    K, T = idx.shape
    E = start_f.shape[0]
    tm = 2 * TOK_BLOCK
    col = lambda i: (0, i)
    return pl.pallas_call(
        _dest_kernel,
        grid=(T // tm,),
        in_specs=[pl.BlockSpec((K, tm), col), pl.BlockSpec((K, tm), col), pl.BlockSpec((E, 1), lambda i: (0, 0))],
        out_specs=pl.BlockSpec((K, tm), col),
        out_shape=jax.ShapeDtypeStruct((K, T), jnp.int32),
        compiler_params=_cparams("parallel"),
        name="dest_rows",
    )(idx, rank, start_f)


def _expert_kernel(first_ref, nblk_ref, nused_ref, xs_hbm, wg_ref, wu_ref, wd_ref, ys_hbm,
                   xbuf, ybuf, wgu_s, wd_s, in_sem, out_sem):
    ring = xbuf.shape[0]
    e = pl.program_id(0)
    De = wg_ref.shape[2]
    n = nblk_ref[e]
    g0 = first_ref[e]
    n_used = nused_ref[0]

    def rows(g):
        return pl.ds(pl.multiple_of(g * ROW_BLOCK, ROW_BLOCK), ROW_BLOCK)

    def in_copy(g, slot):
        return pltpu.make_async_copy(xs_hbm.at[rows(g), :], xbuf.at[slot], in_sem.at[slot])

    def out_copy(g, slot):
        return pltpu.make_async_copy(ybuf.at[slot], ys_hbm.at[rows(g), :], out_sem.at[slot])

    @pl.when(e == 0)
    def _():
        for g in range(ring - 2):
            @pl.when(g < n_used)
            def _():
                in_copy(g, g).start()

    def acquire(g, ahead):
        slot = g & (ring - 1)
        in_copy(g, slot).wait()

        @pl.when(g + ahead < n_used)
        def _():
            in_copy(g + ahead, (g + ahead) & (ring - 1)).start()

        @pl.when(g >= ring)
        def _():
            out_copy(g - ring, slot).wait()
        return slot

    def ffn(pieces):
        dh = xbuf.shape[2]
        gus = []
        for slot, rs in pieces:
            lo, hi = _unpack_bf16_halves(xbuf[slot, rs, :])
            gus.append(_dot(lo.astype(BF16), wgu_s[:dh, :]) + _dot(hi.astype(BF16), wgu_s[dh:, :]))
        hs = [(_silu(gu[:, :De]) * gu[:, De:]).astype(BF16) for gu in gus]
        for (slot, rs), h in zip(pieces, hs):
            ybuf[slot, rs, :] = _pack_bf16_halves(_dot(h, wd_s[...]))

    @pl.when(n > 0)
    def _():
        wgu_s[:, :De] = wg_ref[0].astype(BF16)
        wgu_s[:, De:] = wu_ref[0].astype(BF16)
        wd_s[...] = wd_ref[0].astype(BF16)

        def pair(j, c):
            g = g0 + 2 * j
            slots = [acquire(g + b, ring - 2) for b in range(2)]
            ffn([(slot, slice(None)) for slot in slots])
            for b, slot in enumerate(slots):
                out_copy(g + b, slot).start()
            return c

        lax.fori_loop(0, n // 2, pair, 0)

        @pl.when(n % 2 == 1)
        def _():
            g = g0 + n - 1
            slot = acquire(g, ring - 2)
            half = ROW_BLOCK // 2
            ffn([(slot, slice(0, half)), (slot, slice(half, ROW_BLOCK))])
            out_copy(g, slot).start()

    @pl.when(e == pl.num_programs(0) - 1)
    def _():
        for back in range(1, ring + 1):
            @pl.when(n_used >= back)
            def _():
                out_copy(n_used - back, (n_used - back) & (ring - 1)).wait()


def _experts(first_block, n_blocks, n_used, xs, w_gate, w_up, w_down):
    n_rows, Dh = xs.shape
    D = 2 * Dh
    E, _, De = w_gate.shape
    grid_spec = pltpu.PrefetchScalarGridSpec(
        num_scalar_prefetch=3,
        grid=(E,),
        in_specs=[pl.BlockSpec(memory_space=pl.ANY),
                  pl.BlockSpec((1, D, De), lambda e, *_: (e, 0, 0)),
                  pl.BlockSpec((1, D, De), lambda e, *_: (e, 0, 0)),
                  pl.BlockSpec((1, De, D), lambda e, *_: (e, 0, 0))],
        out_specs=pl.BlockSpec(memory_space=pl.ANY),
        scratch_shapes=[pltpu.VMEM((EXPERT_RING, ROW_BLOCK, Dh), jnp.uint32),
                        pltpu.VMEM((EXPERT_RING, ROW_BLOCK, Dh), jnp.uint32),
                        pltpu.VMEM((D, 2 * De), BF16), pltpu.VMEM((De, D), BF16),
                        pltpu.SemaphoreType.DMA((EXPERT_RING,)), pltpu.SemaphoreType.DMA((EXPERT_RING,))],
    )
    return pl.pallas_call(
        _expert_kernel,
        grid_spec=grid_spec,
        out_shape=jax.ShapeDtypeStruct((n_rows, Dh), jnp.uint32),
        compiler_params=_cparams("arbitrary"),
        name="expert_ffn",
    )(first_block, n_blocks, n_used, xs, w_gate, w_up, w_down)


SC_WINDOW = 128


def _sc_mesh():
    return plsc.VectorSubcoreMesh(core_axis_name="core", subcore_axis_name="subcore")


def _sc_workers():
    info = plsc.get_sparse_core_info()
    return info.num_cores, info.num_cores * info.num_subcores


def _sc_worker_id(num_cores):
    return lax.axis_index("subcore") * num_cores + lax.axis_index("core")


def _sc_scatter_rows(x, dest_flat, n_rows):
    T, D = x.shape
    K = dest_flat.shape[0] // T
    nc, nw = _sc_workers()
    per_w = T // nw
    assert T % nw == 0 and per_w % SC_WINDOW == 0

    @functools.partial(
        pl.kernel, out_type=jax.ShapeDtypeStruct((n_rows, D), x.dtype), mesh=_sc_mesh(),
        scratch_types=[pltpu.VMEM((SC_WINDOW,), jnp.int32), pltpu.VMEM((SC_WINDOW, D), x.dtype)],
        name="sc_dispatch")
    def run(x_hbm, i_hbm, o_hbm, idx_v, rows_v):
        first = _sc_worker_id(nc) * per_w

        @pl.loop(0, per_w // SC_WINDOW)
        def _(c):
            base = first + c * SC_WINDOW
            pltpu.sync_copy(x_hbm.at[pl.ds(base, SC_WINDOW)], rows_v)
            for k in range(K):
                pltpu.sync_copy(i_hbm.at[pl.ds(k * T + base, SC_WINDOW)], idx_v)
                pltpu.sync_copy(rows_v, o_hbm.at[idx_v])

    return run(x, dest_flat)


def _sc_gather_rows(ys, dest_flat):
    N = dest_flat.shape[0]
    D = ys.shape[1]
    nc, nw = _sc_workers()
    per_w = N // nw
    assert N % nw == 0 and per_w % SC_WINDOW == 0

    @functools.partial(
        pl.kernel, out_type=jax.ShapeDtypeStruct((N, D), ys.dtype), mesh=_sc_mesh(),
        scratch_types=[pltpu.VMEM((SC_WINDOW,), jnp.int32), pltpu.VMEM((SC_WINDOW, D), ys.dtype)],
        name="sc_combine_gather")
    def run(y_hbm, i_hbm, o_hbm, idx_v, rows_v):
        first = _sc_worker_id(nc) * per_w

        @pl.loop(0, per_w // SC_WINDOW)
        def _(c):
            base = first + c * SC_WINDOW
            pltpu.sync_copy(i_hbm.at[pl.ds(base, SC_WINDOW)], idx_v)
            pltpu.sync_copy(y_hbm.at[idx_v], rows_v)
            pltpu.sync_copy(rows_v, o_hbm.at[pl.ds(base, SC_WINDOW)])

    return run(ys, dest_flat)


def _final_kernel(z_ref, gate_ref, base_ref, g_ref, b_ref, o_ref):
    gate = gate_ref[...]
    acc_lo = acc_hi = None
    for k in range(TOP_K):
        lo, hi = _unpack_bf16_halves(z_ref[k])
        g = gate[:, k:k + 1]
        acc_lo = g * lo if acc_lo is None else acc_lo + g * lo
        acc_hi = g * hi if acc_hi is None else acc_hi + g * hi
    acc = base_ref[...] + jnp.concatenate([acc_lo, acc_hi], axis=-1)
    o_ref[...] = _layer_norm(acc, g_ref[...], b_ref[...])


def _final(z, gate, base, g2, b2, first_token):
    T, D = base.shape
    tb = TOK_BLOCK
    off = first_token // tb
    return pl.pallas_call(
        _final_kernel,
        grid=(z.shape[1] // tb,),
        in_specs=[pl.BlockSpec((TOP_K, tb, D // 2), lambda i: (0, i, 0)),
                  pl.BlockSpec((tb, TOP_K), lambda i: (i + off, 0)),
                  pl.BlockSpec((tb, D), lambda i: (i + off, 0)),
                  pl.BlockSpec((1, D), lambda i: (0, 0)),
                  pl.BlockSpec((1, D), lambda i: (0, 0))],
        out_specs=pl.BlockSpec((tb, D), lambda i: (i + off, 0)),
        out_shape=jax.ShapeDtypeStruct((T, D), F32),
        input_output_aliases={2: 0},
        compiler_params=_cparams("parallel"),
        name="final_sum_ln",
    )(z, gate, base, g2, b2)


def _layer(x, w_in, rel_bias, gain_a, gain_b, w_out, ln1_g, ln1_b, router_w, router_bias,
           w_gate, w_up, w_down, ws_gate, ws_up, ws_down, ln2_g, ln2_b, alpha):
    B, S, D = x.shape
    T = B * S
    d_a = gain_a.shape[0]
    d_b = gain_b.shape[0]
    assert S % TILE_B == 0 and T % 512 == 0 and d_a % LANES == 0 and d_b % WIDTH_B == 0 and (3 * d_a) % WIDTH_B == 0
    xt = x.reshape(T, D)

    scale = HEAD_DIM ** -0.5
    col = jnp.arange(w_in.shape[1])
    is_q = (col < d_a) | ((col >= 3 * d_a) & (col < 3 * d_a + d_b))
    w_in_b = (w_in * jnp.where(is_q, scale, 1.0)[None, :]).astype(BF16)

    qkv = _qkv_proj(xt, w_in_b).reshape(B, S, -1)
    ya = _attn_a(qkv, _rel_bias_by_offset(rel_bias), B, S, d_a).reshape(T, d_a)
    yb = _attn_b(qkv, B, S, d_a, d_b).reshape(T, d_b)

    x1p, base, idx, gate, rank, cnt = _post(
        ya, yb, xt, gain_a[None], gain_b[None], w_out.astype(BF16), ln1_g[None], ln1_b[None],
        router_w.T.astype(BF16), router_bias[:, None].astype(F32),
        ws_gate.astype(BF16), ws_up.astype(BF16), ws_down.astype(BF16), alpha)
    cnt = cnt[:, 0]
    padded = (cnt + ROW_BLOCK - 1) // ROW_BLOCK * ROW_BLOCK
    ends = jnp.cumsum(padded)
    start = (ends - padded).astype(jnp.int32)
    n_rows = T * TOP_K + N_EXPERTS * ROW_BLOCK
    first_block = start // ROW_BLOCK
    n_blocks = (padded // ROW_BLOCK).astype(jnp.int32)
    n_used = (ends[-1:] // ROW_BLOCK).astype(jnp.int32)
    dest = _dest(idx, rank, start.astype(F32)[:, None])

    xs = _sc_scatter_rows(x1p, dest.reshape(-1), n_rows)
    ys = _experts(first_block, n_blocks, n_used, xs, w_gate, w_up, w_down)
    gate_t = gate.T
    out = base
    tg = T // COMBINE_GROUPS
    for q in range(COMBINE_GROUPS):
        z = _sc_gather_rows(ys, dest[:, q * tg:(q + 1) * tg].reshape(-1)).reshape(TOP_K, tg, D // 2)
        out = _final(z, gate_t, out, ln2_g[None], ln2_b[None], q * tg)
    return out.reshape(B, S, D)


def kernel(x, w_in, rel_bias, gain_a, gain_b, w_out, ln1_g, ln1_b, router_w, router_bias,
           w_gate, w_up, w_down, ws_gate, ws_up, ws_down, ln2_g, ln2_b):
    depth = w_in.shape[0]
    alpha = (2 * depth) ** 0.25
    for l in range(depth):
        x = _layer(x, w_in[l], rel_bias[l], gain_a[l], gain_b[l], w_out[l], ln1_g[l], ln1_b[l],
                   router_w[l], router_bias[l], w_gate[l], w_up[l], w_down[l],
                   ws_gate[l], ws_up[l], ws_down[l], ln2_g[l], ln2_b[l], alpha)
    return x
```

```python
import functools

import jax
import jax.numpy as jnp
from jax import lax
from jax.experimental import pallas as pl
from jax.experimental.pallas import tpu as pltpu
from jax.experimental.pallas import tpu_sc as plsc

CHUNK = 64
HEAD_DIM = 64
LEFT_CHUNKS = 8
MAX_REL = 128
N_EXPERTS = 256
TOP_K = 8
N_GROUPS = 8
TOPK_GROUPS = 4
EXPERTS_PER_GROUP = N_EXPERTS // N_GROUPS
ROUTED_SCALE = 2.5
LN_EPS = 1e-5
RMS_EPS = 1e-6

LANES = 128
HEADS_PER_BLOCK = LANES // HEAD_DIM
QBLK_A = 2 * CHUNK
BAND_A = (LEFT_CHUNKS + 2) * CHUNK
BIAS_W = BAND_A + QBLK_A
BLOCKS_PER_TRIP_A = 4
TILE_B = 256
WIDTH_B = 2 * HEAD_DIM
TILES_PER_TRIP_B = 3
ROW_BLOCK = 256
EXPERT_RING = 8
TOK_BLOCK = 256
COMBINE_GROUPS = 4
MASK_VALUE = -1e30
V7X_VMEM_BYTES = 64 * 1024 * 1024
VMEM_LIMIT = V7X_VMEM_BYTES * 3 // 4

F32 = jnp.float32
BF16 = jnp.bfloat16


def _cparams(*sem):
    return pltpu.CompilerParams(dimension_semantics=sem, vmem_limit_bytes=VMEM_LIMIT)


def _dot(a, b):
    return jnp.dot(a, b, preferred_element_type=F32)


def _dot_nt(a, b):
    return lax.dot_general(a, b, (((1,), (1,)), ((), ())), preferred_element_type=F32)


def _pack_bf16_halves(x):
    half = x.shape[1] // 2
    bits = lax.bitcast_convert_type(x.astype(BF16).astype(F32), jnp.uint32)
    return (bits[:, :half] >> 16) | (bits[:, half:] & jnp.uint32(0xFFFF0000))


def _unpack_bf16_halves(w):
    lo = lax.bitcast_convert_type(w << 16, F32)
    hi = lax.bitcast_convert_type(w & jnp.uint32(0xFFFF0000), F32)
    return lo, hi


def _qkv_kernel(x_ref, w_ref, o_ref, *, col_chunk):
    xb = x_ref[...].astype(BF16)
    for n in range(w_ref.shape[1] // col_chunk):
        cols = slice(n * col_chunk, (n + 1) * col_chunk)
        o_ref[:, cols] = _dot(xb, w_ref[:, cols]).astype(BF16)


def _qkv_proj(xt, w_b):
    T, D = xt.shape
    N = w_b.shape[1]
    tm = 512
    return pl.pallas_call(
        functools.partial(_qkv_kernel, col_chunk=512),
        grid=(T // tm,),
        in_specs=[pl.BlockSpec((tm, D), lambda i: (i, 0)),
                  pl.BlockSpec((D, N), lambda i: (0, 0))],
        out_specs=pl.BlockSpec((tm, N), lambda i: (i, 0)),
        out_shape=jax.ShapeDtypeStruct((T, N), BF16),
        compiler_params=_cparams("parallel"),
        name="qkv_proj",
    )(xt, w_b)


def _attn_a_kernel(q_ref, k_ref, v_ref, w_ref, o_ref, bias_ref):
    S = q_ref.shape[0]
    nblk = S // QBLK_A
    lead = LEFT_CHUNKS * CHUNK
    n_edge = min(lead // QBLK_A, nblk)

    qc = lax.broadcasted_iota(jnp.int32, (QBLK_A, BAND_A), 0) // CHUNK
    kc = lax.broadcasted_iota(jnp.int32, (QBLK_A, BAND_A), 1) // CHUNK
    allowed = (kc >= qc) & (kc <= qc + LEFT_CHUNKS)
    for h in range(HEADS_PER_BLOCK):
        wb = jnp.broadcast_to(w_ref[h], (QBLK_A, BIAS_W))
        toeplitz = pltpu.roll(wb, BIAS_W - (QBLK_A - 1), 1, stride=1, stride_axis=0)
        bias_ref[h] = jnp.where(allowed, toeplitz[:, :BAND_A], MASK_VALUE)

    lane = lax.broadcasted_iota(jnp.int32, (QBLK_A, LANES), 1)
    head_of_lane = lane // HEAD_DIM

    def blocks(specs):
        work = []
        for p, kstart, nk, bias_off in specs:
            q = q_ref[pl.ds(p * QBLK_A, QBLK_A), :]
            k = k_ref[pl.ds(kstart, nk), :]
            v = v_ref[pl.ds(kstart, nk), :]
            for h in range(HEADS_PER_BLOCK):
                qh = jnp.where(head_of_lane == h, q, jnp.zeros_like(q))
                work.append(dict(s=_dot_nt(qh, k), v=v, bias=bias_ref[h, :, bias_off:bias_off + nk]))
        for w in work:
            s = w["s"] + w["bias"]
            e = jnp.exp(s - jnp.max(s, axis=-1, keepdims=True))
            w["l"] = jnp.sum(e, axis=-1, keepdims=True)
            w["e"] = e.astype(BF16)
        outs = [_dot(w["e"], w["v"]) / w["l"] for w in work]
        for i, spec in enumerate(specs):
            o = outs[i * HEADS_PER_BLOCK]
            for h in range(1, HEADS_PER_BLOCK):
                o = jnp.where(head_of_lane == h, outs[i * HEADS_PER_BLOCK + h], o)
            o_ref[pl.ds(spec[0] * QBLK_A, QBLK_A), :] = o

    blocks([(p, 0, (p + 1) * QBLK_A, lead - p * QBLK_A) for p in range(n_edge)])

    def full(p):
        return (p, pl.multiple_of(p * QBLK_A - lead, QBLK_A), BAND_A, 0)

    n_full = nblk - n_edge
    per_trip = BLOCKS_PER_TRIP_A
    rem = n_full % per_trip
    if rem:
        blocks([full(n_edge + r) for r in range(rem)])
    if n_full >= per_trip:
        def body(i, c):
            p = n_edge + rem + per_trip * i
            blocks([full(p + r) for r in range(per_trip)])
            return c
        lax.fori_loop(0, n_full // per_trip, body, 0)


def _attn_a(qkv, bias_w, B, S, d_a):
    n_hb = d_a // LANES
    seg = d_a // LANES
    return pl.pallas_call(
        _attn_a_kernel,
        grid=(B, n_hb),
        in_specs=[pl.BlockSpec((None, S, LANES), lambda b, h: (b, 0, h)),
                  pl.BlockSpec((None, S, LANES), lambda b, h: (b, 0, seg + h)),
                  pl.BlockSpec((None, S, LANES), lambda b, h: (b, 0, 2 * seg + h)),
                  pl.BlockSpec((HEADS_PER_BLOCK, 1, BIAS_W), lambda b, h: (h, 0, 0))],
        out_specs=pl.BlockSpec((None, S, LANES), lambda b, h: (b, 0, h)),
        out_shape=jax.ShapeDtypeStruct((B, S, d_a), F32),
        scratch_shapes=[pltpu.VMEM((HEADS_PER_BLOCK, QBLK_A, BAND_A), F32)],
        compiler_params=_cparams("parallel", "parallel"),
        name="attn_chunked",
    )(qkv, qkv, qkv, bias_w)


def _rel_bias_by_offset(rel_bias):
    dist = jnp.clip(BAND_A - 1 - jnp.arange(BIAS_W), -MAX_REL, MAX_REL) + MAX_REL
    return rel_bias[:, dist].astype(F32)[:, None, :]


def _attn_b_kernel(q_ref, k_ref, v_ref, o_ref):
    t = TILE_B
    row = lax.broadcasted_iota(jnp.int32, (t, t), 0)
    col = lax.broadcasted_iota(jnp.int32, (t, t), 1)
    from_s = (row >= col).astype(BF16)
    from_s2 = jnp.concatenate([from_s, from_s], axis=0)
    causal = col < row

    def q_tile(qi, c):
        q = q_ref[pl.ds(pl.multiple_of(qi * t, t), t), :]
        n_heads = q_ref.shape[1] // HEAD_DIM
        head_of_lane = lax.broadcasted_iota(jnp.int32, q.shape, 1) // HEAD_DIM
        qs = [jnp.where(head_of_lane == h, q, jnp.zeros_like(q)) for h in range(n_heads)]

        def scores(js):
            zs = []
            for j in js:
                kt = k_ref[pl.ds(pl.multiple_of(j * t, t), t), :]
                zs += [_dot_nt(qh, kt) for qh in qs]
            return zs

        def tiles(js, zs, state, first_is_diag=False):
            accs, carries = list(state[0::2]), list(state[1::2])
            work = []
            for n, j in enumerate(js):
                vt = v_ref[pl.ds(pl.multiple_of(j * t, t), t), :]
                for h in range(n_heads):
                    work.append(dict(h=h, diag=first_is_diag and n == 0, v=vt, z=zs[n * n_heads + h]))
            for w in work:
                z = w["z"]
                sp = jnp.maximum(z, 0.0) + jnp.log(1.0 + jnp.exp(-jnp.abs(z)))
                if w["diag"]:
                    sp = jnp.where(causal, sp, 0.0)
                hi = sp.astype(BF16)
                lo = (sp - hi.astype(F32)).astype(BF16)
                w["suffix"] = _dot(jnp.concatenate([hi, lo], axis=-1), from_s2)
                w["rowsum"] = jnp.sum(sp, axis=-1, keepdims=True)
            for w in work:
                h = w["h"]
                if w["diag"]:
                    a = jnp.where(causal, jnp.exp(w["z"] - w["suffix"]), 0.0)
                    accs[h] = _dot(a.astype(BF16), w["v"])
                    carries[h] = w["rowsum"]
                else:
                    a = jnp.exp(w["z"] - w["suffix"] - carries[h])
                    accs[h] = accs[h] + _dot(a.astype(BF16), w["v"])
                    carries[h] = carries[h] + w["rowsum"]
            return tuple(x for pair in zip(accs, carries) for x in pair)

        per_trip = TILES_PER_TRIP_B
        empty = (None,) * (2 * n_heads)

        def opening(extra):
            js = [qi - r for r in range(extra + 1)]
            return lambda: tiles(js, scores(js), empty, True)

        state = lax.switch(qi % per_trip, [opening(r) for r in range(per_trip)])
        first = qi - 1 - qi % per_trip

        def body(it, st):
            js = [first - per_trip * it - r for r in range(per_trip)]
            return tiles(js, scores(js), st)

        state = lax.fori_loop(0, qi // per_trip, body, state)
        out = state[0]
        for h in range(1, n_heads):
            out = jnp.where(head_of_lane == h, state[2 * h], out)
        o_ref[pl.ds(pl.multiple_of(qi * t, t), t), :] = out
        return c

    lax.fori_loop(0, q_ref.shape[0] // t, q_tile, 0)


def _attn_b(qkv, B, S, d_a, d_b):
    w = WIDTH_B
    n_hb = d_b // w
    base = 3 * d_a // w
    seg = d_b // w
    return pl.pallas_call(
        _attn_b_kernel,
        grid=(B, n_hb),
        in_specs=[pl.BlockSpec((None, S, w), lambda b, h: (b, 0, base + h)),
                  pl.BlockSpec((None, S, w), lambda b, h: (b, 0, base + seg + h)),
                  pl.BlockSpec((None, S, w), lambda b, h: (b, 0, base + 2 * seg + h))],
        out_specs=pl.BlockSpec((None, S, w), lambda b, h: (b, 0, h)),
        out_shape=jax.ShapeDtypeStruct((B, S, d_b), F32),
        compiler_params=_cparams("parallel", "parallel"),
        name="attn_stickbreak",
    )(qkv, qkv, qkv)


def _layer_norm(r, g, b):
    mu = jnp.mean(r, axis=-1, keepdims=True)
    c = r - mu
    var = jnp.mean(c * c, axis=-1, keepdims=True)
    return c * lax.rsqrt(var + LN_EPS) * g + b


def _rms_norm(y, g):
    ms = jnp.mean(y * y, axis=-1, keepdims=True)
    return y * lax.rsqrt(ms + RMS_EPS) * g


def _silu(g):
    return g * jax.nn.sigmoid(g)


def _post_kernel(ya_ref, yb_ref, x_ref, ga_ref, gb_ref, wo_ref, g1_ref, b1_ref, rw_ref, rbias_ref,
                 wsg_ref, wsu_ref, wsd_ref, x1_ref, base_ref, idx_ref, gate_ref, rank_ref, cnt_ref, carry_ref,
                 *, alpha):
    @pl.when(pl.program_id(0) == 0)
    def _():
        carry_ref[...] = jnp.zeros_like(carry_ref)

    d_a = ya_ref.shape[1]
    na = _rms_norm(ya_ref[...], ga_ref[...]).astype(BF16)
    nb = _rms_norm(yb_ref[...], gb_ref[...]).astype(BF16)
    h = _dot(na, wo_ref[:d_a, :]) + _dot(nb, wo_ref[d_a:, :])
    x1 = _layer_norm(alpha * x_ref[...] + h, g1_ref[...], b1_ref[...])
    x1_ref[...] = _pack_bf16_halves(x1)
    xb = x1.astype(BF16)
    logits = _dot_nt(rw_ref[...], xb)
    hs = _silu(_dot(xb, wsg_ref[...])) * _dot(xb, wsu_ref[...])
    base_ref[...] = alpha * x1 + _dot(hs.astype(BF16), wsd_ref[...])

    tb = TOK_BLOCK
    row = lax.broadcasted_iota(jnp.int32, (tb, tb), 0)
    col = lax.broadcasted_iota(jnp.int32, (tb, tb), 1)
    earlier = (row < col).astype(BF16)
    carry = carry_ref[...]
    for n in range(logits.shape[1] // tb):
        cols = slice(n * tb, (n + 1) * tb)
        idx, gate, rank, carry = _route_block(logits[:, cols], rbias_ref[...], carry, earlier)
        idx_ref[:, cols] = idx
        gate_ref[:, cols] = gate
        rank_ref[:, cols] = rank
    carry_ref[...] = carry
    cnt_ref[...] = carry.astype(jnp.int32)


def _post(ya, yb, xt, gain_a, gain_b, wo_b, g1, b1, rw_b, rbias, wsg_b, wsu_b, wsd_b, alpha):
    T, D = xt.shape
    d_a, d_b = ya.shape[1], yb.shape[1]
    E = rw_b.shape[0]
    De = wsg_b.shape[1]
    tm = 4 * TOK_BLOCK
    row = lambda i: (i, 0)
    col = lambda i: (0, i)
    fix = lambda i: (0, 0)
    return pl.pallas_call(
        functools.partial(_post_kernel, alpha=alpha),
        grid=(T // tm,),
        in_specs=[pl.BlockSpec((tm, d_a), row), pl.BlockSpec((tm, d_b), row), pl.BlockSpec((tm, D), row),
                  pl.BlockSpec((1, d_a), fix), pl.BlockSpec((1, d_b), fix),
                  pl.BlockSpec((d_a + d_b, D), fix), pl.BlockSpec((1, D), fix), pl.BlockSpec((1, D), fix),
                  pl.BlockSpec((E, D), fix), pl.BlockSpec((E, 1), fix),
                  pl.BlockSpec((D, De), fix), pl.BlockSpec((D, De), fix), pl.BlockSpec((De, D), fix)],
        out_specs=[pl.BlockSpec((tm, D // 2), row), pl.BlockSpec((tm, D), row),
                   pl.BlockSpec((TOP_K, tm), col), pl.BlockSpec((TOP_K, tm), col), pl.BlockSpec((TOP_K, tm), col),
                   pl.BlockSpec((E, 1), fix)],
        out_shape=[jax.ShapeDtypeStruct((T, D // 2), jnp.uint32), jax.ShapeDtypeStruct((T, D), F32),
                   jax.ShapeDtypeStruct((TOP_K, T), jnp.int32), jax.ShapeDtypeStruct((TOP_K, T), F32),
                   jax.ShapeDtypeStruct((TOP_K, T), jnp.int32), jax.ShapeDtypeStruct((E, 1), jnp.int32)],
        scratch_shapes=[pltpu.VMEM((E, 1), F32)],
        compiler_params=_cparams("arbitrary"),
        name="post_attn_route",
    )(ya, yb, xt, gain_a, gain_b, wo_b, g1, b1, rw_b, rbias, wsg_b, wsu_b, wsd_b)


def _route_block(logits, rbias, carry, earlier):
    E, tm = logits.shape
    neg = -jnp.inf
    scores = jax.nn.sigmoid(logits)
    sel = scores + rbias
    eidx = lax.broadcasted_iota(jnp.int32, (E, tm), 0).astype(F32)

    def first_argmax(v, ids):
        m = jnp.max(v, axis=0, keepdims=True)
        return m, jnp.min(jnp.where(v == m, ids, float(E)), axis=0, keepdims=True)

    grp_scores = []
    ids = lax.broadcasted_iota(jnp.int32, (EXPERTS_PER_GROUP, tm), 0).astype(F32)
    for g in range(N_GROUPS):
        v = sel[g * EXPERTS_PER_GROUP:(g + 1) * EXPERTS_PER_GROUP, :]
        m1, i1 = first_argmax(v, ids)
        m2 = jnp.max(jnp.where(ids == i1, neg, v), axis=0, keepdims=True)
        grp_scores.append(m1 + m2)
    parts = []
    for g in range(N_GROUPS):
        beaten = jnp.zeros((1, tm), jnp.int32)
        for o in range(N_GROUPS):
            if o != g:
                s, t = grp_scores[o], grp_scores[g]
                beaten = beaten + ((s > t) | ((s == t) & (o < g))).astype(jnp.int32)
        rows = slice(g * EXPERTS_PER_GROUP, (g + 1) * EXPERTS_PER_GROUP)
        parts.append(jnp.where(beaten < TOPK_GROUPS, sel[rows, :], neg))
    selm = jnp.concatenate(parts, axis=0)

    hits, gates, ids_k = [], [], []
    chosen = jnp.zeros((E, tm), F32)
    gate_sum = jnp.zeros((1, tm), F32)
    for k in range(TOP_K):
        _, ik = first_argmax(selm, eidx)
        hit = eidx == ik
        gk = jnp.sum(jnp.where(hit, scores, 0.0), axis=0, keepdims=True)
        selm = jnp.where(hit, neg, selm)
        chosen = jnp.where(hit, 1.0, chosen)
        gate_sum = gate_sum + gk
        hits.append(hit)
        gates.append(gk)
        ids_k.append(ik)

    before = _dot(chosen.astype(BF16), earlier) + carry
    ranks = [jnp.sum(jnp.where(hit, before, 0.0), axis=0, keepdims=True) for hit in hits]
    return (jnp.concatenate(ids_k, axis=0).astype(jnp.int32),
            jnp.concatenate(gates, axis=0) / gate_sum * ROUTED_SCALE,
            jnp.concatenate(ranks, axis=0).astype(jnp.int32),
            carry + jnp.sum(chosen, axis=1, keepdims=True))


def _dest_kernel(idx_ref, rank_ref, start_ref, dest_ref):
    K, tm = idx_ref.shape
    E = start_ref.shape[0]
    eidx = lax.broadcasted_iota(jnp.int32, (E, tm), 0)
    idx = idx_ref[...]
    start = start_ref[...]
    rows = [jnp.sum(jnp.where(eidx == idx[k:k + 1, :], start, 0.0), axis=0, keepdims=True) for k in range(K)]
    dest_ref[...] = jnp.concatenate(rows, axis=0).astype(jnp.int32) + rank_ref[...]


def _dest(idx, rank, start_f):
    K, T = idx.shape
    E = start_f.shape[0]
    tm = 2 * TOK_BLOCK
    col = lambda i: (0, i)
    return pl.pallas_call(
        _dest_kernel,
        grid=(T // tm,),
        in_specs=[pl.BlockSpec((K, tm), col), pl.BlockSpec((K, tm), col), pl.BlockSpec((E, 1), lambda i: (0, 0))],
        out_specs=pl.BlockSpec((K, tm), col),
        out_shape=jax.ShapeDtypeStruct((K, T), jnp.int32),
        compiler_params=_cparams("parallel"),
        name="dest_rows",
    )(idx, rank, start_f)


def _expert_kernel(first_ref, nblk_ref, nused_ref, xs_hbm, wg_ref, wu_ref, wd_ref, ys_hbm,
                   xbuf, ybuf, wgu_s, wd_s, in_sem, out_sem):
    ring = xbuf.shape[0]
    e = pl.program_id(0)
    De = wg_ref.shape[2]
    n = nblk_ref[e]
    g0 = first_ref[e]
    n_used = nused_ref[0]

    def rows(g):
        return pl.ds(pl.multiple_of(g * ROW_BLOCK, ROW_BLOCK), ROW_BLOCK)

    def in_copy(g, slot):
        return pltpu.make_async_copy(xs_hbm.at[rows(g), :], xbuf.at[slot], in_sem.at[slot])

    def out_copy(g, slot):
        return pltpu.make_async_copy(ybuf.at[slot], ys_hbm.at[rows(g), :], out_sem.at[slot])

    @pl.when(e == 0)
    def _():
        for g in range(ring - 2):
            @pl.when(g < n_used)
            def _():
                in_copy(g, g).start()

    def acquire(g, ahead):
        slot = g & (ring - 1)
        in_copy(g, slot).wait()

        @pl.when(g + ahead < n_used)
        def _():
            in_copy(g + ahead, (g + ahead) & (ring - 1)).start()

        @pl.when(g >= ring)
        def _():
            out_copy(g - ring, slot).wait()
        return slot

    def ffn(pieces):
        dh = xbuf.shape[2]
        gus = []
        for slot, rs in pieces:
            lo, hi = _unpack_bf16_halves(xbuf[slot, rs, :])
            gus.append(_dot(lo.astype(BF16), wgu_s[:dh, :]) + _dot(hi.astype(BF16), wgu_s[dh:, :]))
        hs = [(_silu(gu[:, :De]) * gu[:, De:]).astype(BF16) for gu in gus]
        for (slot, rs), h in zip(pieces, hs):
            ybuf[slot, rs, :] = _pack_bf16_halves(_dot(h, wd_s[...]))

    @pl.when(n > 0)
    def _():
        wgu_s[:, :De] = wg_ref[0].astype(BF16)
        wgu_s[:, De:] = wu_ref[0].astype(BF16)
        wd_s[...] = wd_ref[0].astype(BF16)

        def pair(j, c):
            g = g0 + 2 * j
            slots = [acquire(g + b, ring - 2) for b in range(2)]
            ffn([(slot, slice(None)) for slot in slots])
            for b, slot in enumerate(slots):
                out_copy(g + b, slot).start()
            return c

        lax.fori_loop(0, n // 2, pair, 0)

        @pl.when(n % 2 == 1)
        def _():
            g = g0 + n - 1
            slot = acquire(g, ring - 2)
            half = ROW_BLOCK // 2
            ffn([(slot, slice(0, half)), (slot, slice(half, ROW_BLOCK))])
            out_copy(g, slot).start()

    @pl.when(e == pl.num_programs(0) - 1)
    def _():
        for back in range(1, ring + 1):
            @pl.when(n_used >= back)
            def _():
                out_copy(n_used - back, (n_used - back) & (ring - 1)).wait()


def _experts(first_block, n_blocks, n_used, xs, w_gate, w_up, w_down):
    n_rows, Dh = xs.shape
    D = 2 * Dh
    E, _, De = w_gate.shape
    grid_spec = pltpu.PrefetchScalarGridSpec(
        num_scalar_prefetch=3,
        grid=(E,),
        in_specs=[pl.BlockSpec(memory_space=pl.ANY),
                  pl.BlockSpec((1, D, De), lambda e, *_: (e, 0, 0)),
                  pl.BlockSpec((1, D, De), lambda e, *_: (e, 0, 0)),
                  pl.BlockSpec((1, De, D), lambda e, *_: (e, 0, 0))],
        out_specs=pl.BlockSpec(memory_space=pl.ANY),
        scratch_shapes=[pltpu.VMEM((EXPERT_RING, ROW_BLOCK, Dh), jnp.uint32),
                        pltpu.VMEM((EXPERT_RING, ROW_BLOCK, Dh), jnp.uint32),
                        pltpu.VMEM((D, 2 * De), BF16), pltpu.VMEM((De, D), BF16),
                        pltpu.SemaphoreType.DMA((EXPERT_RING,)), pltpu.SemaphoreType.DMA((EXPERT_RING,))],
    )
    return pl.pallas_call(
        _expert_kernel,
        grid_spec=grid_spec,
        out_shape=jax.ShapeDtypeStruct((n_rows, Dh), jnp.uint32),
        compiler_params=_cparams("arbitrary"),
        name="expert_ffn",
    )(first_block, n_blocks, n_used, xs, w_gate, w_up, w_down)


SC_WINDOW = 128


def _sc_mesh():
    return plsc.VectorSubcoreMesh(core_axis_name="core", subcore_axis_name="subcore")


def _sc_workers():
    info = plsc.get_sparse_core_info()
    return info.num_cores, info.num_cores * info.num_subcores


def _sc_worker_id(num_cores):
    return lax.axis_index("subcore") * num_cores + lax.axis_index("core")


def _sc_scatter_rows(x, dest_flat, n_rows):
    T, D = x.shape
    K = dest_flat.shape[0] // T
    nc, nw = _sc_workers()
    per_w = T // nw
    assert T % nw == 0 and per_w % SC_WINDOW == 0

    @functools.partial(
        pl.kernel, out_type=jax.ShapeDtypeStruct((n_rows, D), x.dtype), mesh=_sc_mesh(),
        scratch_types=[pltpu.VMEM((SC_WINDOW,), jnp.int32), pltpu.VMEM((SC_WINDOW, D), x.dtype)],
        name="sc_dispatch")
    def run(x_hbm, i_hbm, o_hbm, idx_v, rows_v):
        first = _sc_worker_id(nc) * per_w

        @pl.loop(0, per_w // SC_WINDOW)
        def _(c):
            base = first + c * SC_WINDOW
            pltpu.sync_copy(x_hbm.at[pl.ds(base, SC_WINDOW)], rows_v)
            for k in range(K):
                pltpu.sync_copy(i_hbm.at[pl.ds(k * T + base, SC_WINDOW)], idx_v)
                pltpu.sync_copy(rows_v, o_hbm.at[idx_v])

    return run(x, dest_flat)


def _sc_gather_rows(ys, dest_flat):
    N = dest_flat.shape[0]
    D = ys.shape[1]
    nc, nw = _sc_workers()
    per_w = N // nw
    assert N % nw == 0 and per_w % SC_WINDOW == 0

    @functools.partial(
        pl.kernel, out_type=jax.ShapeDtypeStruct((N, D), ys.dtype), mesh=_sc_mesh(),
        scratch_types=[pltpu.VMEM((SC_WINDOW,), jnp.int32), pltpu.VMEM((SC_WINDOW, D), ys.dtype)],
        name="sc_combine_gather")
    def run(y_hbm, i_hbm, o_hbm, idx_v, rows_v):
        first = _sc_worker_id(nc) * per_w

        @pl.loop(0, per_w // SC_WINDOW)
        def _(c):
            base = first + c * SC_WINDOW
            pltpu.sync_copy(i_hbm.at[pl.ds(base, SC_WINDOW)], idx_v)
            pltpu.sync_copy(y_hbm.at[idx_v], rows_v)
            pltpu.sync_copy(rows_v, o_hbm.at[pl.ds(base, SC_WINDOW)])

    return run(ys, dest_flat)


def _final_kernel(z_ref, gate_ref, base_ref, g_ref, b_ref, o_ref):
    gate = gate_ref[...]
    acc_lo = acc_hi = None
    for k in range(TOP_K):
        lo, hi = _unpack_bf16_halves(z_ref[k])
        g = gate[:, k:k + 1]
        acc_lo = g * lo if acc_lo is None else acc_lo + g * lo
        acc_hi = g * hi if acc_hi is None else acc_hi + g * hi
    acc = base_ref[...] + jnp.concatenate([acc_lo, acc_hi], axis=-1)
    o_ref[...] = _layer_norm(acc, g_ref[...], b_ref[...])


def _final(z, gate, base, g2, b2, first_token):
    T, D = base.shape
    tb = TOK_BLOCK
    off = first_token // tb
    return pl.pallas_call(
        _final_kernel,
        grid=(z.shape[1] // tb,),
        in_specs=[pl.BlockSpec((TOP_K, tb, D // 2), lambda i: (0, i, 0)),
                  pl.BlockSpec((tb, TOP_K), lambda i: (i + off, 0)),
                  pl.BlockSpec((tb, D), lambda i: (i + off, 0)),
                  pl.BlockSpec((1, D), lambda i: (0, 0)),
                  pl.BlockSpec((1, D), lambda i: (0, 0))],
        out_specs=pl.BlockSpec((tb, D), lambda i: (i + off, 0)),
        out_shape=jax.ShapeDtypeStruct((T, D), F32),
        input_output_aliases={2: 0},
        compiler_params=_cparams("parallel"),
        name="final_sum_ln",
    )(z, gate, base, g2, b2)


def _layer(x, w_in, rel_bias, gain_a, gain_b, w_out, ln1_g, ln1_b, router_w, router_bias,
           w_gate, w_up, w_down, ws_gate, ws_up, ws_down, ln2_g, ln2_b, alpha):
    B, S, D = x.shape
    T = B * S
    d_a = gain_a.shape[0]
    d_b = gain_b.shape[0]
    assert S % TILE_B == 0 and T % 512 == 0 and d_a % LANES == 0 and d_b % WIDTH_B == 0 and (3 * d_a) % WIDTH_B == 0
    xt = x.reshape(T, D)

    scale = HEAD_DIM ** -0.5
    col = jnp.arange(w_in.shape[1])
    is_q = (col < d_a) | ((col >= 3 * d_a) & (col < 3 * d_a + d_b))
    w_in_b = (w_in * jnp.where(is_q, scale, 1.0)[None, :]).astype(BF16)

    qkv = _qkv_proj(xt, w_in_b).reshape(B, S, -1)
    ya = _attn_a(qkv, _rel_bias_by_offset(rel_bias), B, S, d_a).reshape(T, d_a)
    yb = _attn_b(qkv, B, S, d_a, d_b).reshape(T, d_b)

    x1p, base, idx, gate, rank, cnt = _post(
        ya, yb, xt, gain_a[None], gain_b[None], w_out.astype(BF16), ln1_g[None], ln1_b[None],
        router_w.T.astype(BF16), router_bias[:, None].astype(F32),
        ws_gate.astype(BF16), ws_up.astype(BF16), ws_down.astype(BF16), alpha)
    cnt = cnt[:, 0]
    padded = (cnt + ROW_BLOCK - 1) // ROW_BLOCK * ROW_BLOCK
    ends = jnp.cumsum(padded)
    start = (ends - padded).astype(jnp.int32)
    n_rows = T * TOP_K + N_EXPERTS * ROW_BLOCK
    first_block = start // ROW_BLOCK
    n_blocks = (padded // ROW_BLOCK).astype(jnp.int32)
    n_used = (ends[-1:] // ROW_BLOCK).astype(jnp.int32)
    dest = _dest(idx, rank, start.astype(F32)[:, None])

    xs = _sc_scatter_rows(x1p, dest.reshape(-1), n_rows)
    ys = _experts(first_block, n_blocks, n_used, xs, w_gate, w_up, w_down)
    gate_t = gate.T
    out = base
    tg = T // COMBINE_GROUPS
    for q in range(COMBINE_GROUPS):
        z = _sc_gather_rows(ys, dest[:, q * tg:(q + 1) * tg].reshape(-1)).reshape(TOP_K, tg, D // 2)
        out = _final(z, gate_t, out, ln2_g[None], ln2_b[None], q * tg)
    return out.reshape(B, S, D)


def kernel(x, w_in, rel_bias, gain_a, gain_b, w_out, ln1_g, ln1_b, router_w, router_bias,
           w_gate, w_up, w_down, ws_gate, ws_up, ws_down, ln2_g, ln2_b):
    depth = w_in.shape[0]
    alpha = (2 * depth) ** 0.25
    for l in range(depth):
        x = _layer(x, w_in[l], rel_bias[l], gain_a[l], gain_b[l], w_out[l], ln1_g[l], ln1_b[l],
                   router_w[l], router_bias[l], w_gate[l], w_up[l], w_down[l],
                   ws_gate[l], ws_up[l], ws_down[l], ln2_g[l], ln2_b[l], alpha)
    return x
```

```python
import functools

import jax
import jax.numpy as jnp
from jax import lax
from jax.experimental import pallas as pl
from jax.experimental.pallas import tpu as pltpu
from jax.experimental.pallas import tpu_sc as plsc

CHUNK = 64
HEAD_DIM = 64
LEFT_CHUNKS = 8
MAX_REL = 128
N_EXPERTS = 256
TOP_K = 8
N_GROUPS = 8
TOPK_GROUPS = 4
EXPERTS_PER_GROUP = N_EXPERTS // N_GROUPS
ROUTED_SCALE = 2.5
LN_EPS = 1e-5
RMS_EPS = 1e-6

LANES = 128
HEADS_PER_BLOCK = LANES // HEAD_DIM
QBLK_A = 2 * CHUNK
BAND_A = (LEFT_CHUNKS + 2) * CHUNK
BIAS_W = BAND_A + QBLK_A
BLOCKS_PER_TRIP_A = 4
TILE_B = 256
WIDTH_B = 2 * HEAD_DIM
TILES_PER_TRIP_B = 3
ROW_BLOCK = 256
EXPERT_RING = 8
TOK_BLOCK = 256
COMBINE_GROUPS = 4
MASK_VALUE = -1e30
V7X_VMEM_BYTES = 64 * 1024 * 1024
VMEM_LIMIT = V7X_VMEM_BYTES * 3 // 4

F32 = jnp.float32
BF16 = jnp.bfloat16


def _cparams(*sem):
    return pltpu.CompilerParams(dimension_semantics=sem, vmem_limit_bytes=VMEM_LIMIT)


def _dot(a, b):
    return jnp.dot(a, b, preferred_element_type=F32)


def _dot_nt(a, b):
    return lax.dot_general(a, b, (((1,), (1,)), ((), ())), preferred_element_type=F32)


def _pack_bf16_halves(x):
    half = x.shape[1] // 2
    bits = lax.bitcast_convert_type(x.astype(BF16).astype(F32), jnp.uint32)
    return (bits[:, :half] >> 16) | (bits[:, half:] & jnp.uint32(0xFFFF0000))


def _unpack_bf16_halves(w):
    lo = lax.bitcast_convert_type(w << 16, F32)
    hi = lax.bitcast_convert_type(w & jnp.uint32(0xFFFF0000), F32)
    return lo, hi


def _qkv_kernel(x_ref, w_ref, o_ref, *, col_chunk):
    xb = x_ref[...].astype(BF16)
    for n in range(w_ref.shape[1] // col_chunk):
        cols = slice(n * col_chunk, (n + 1) * col_chunk)
        o_ref[:, cols] = _dot(xb, w_ref[:, cols]).astype(BF16)


def _qkv_proj(xt, w_b):
    T, D = xt.shape
    N = w_b.shape[1]
    tm = 512
    return pl.pallas_call(
        functools.partial(_qkv_kernel, col_chunk=512),
        grid=(T // tm,),
        in_specs=[pl.BlockSpec((tm, D), lambda i: (i, 0)),
                  pl.BlockSpec((D, N), lambda i: (0, 0))],
        out_specs=pl.BlockSpec((tm, N), lambda i: (i, 0)),
        out_shape=jax.ShapeDtypeStruct((T, N), BF16),
        compiler_params=_cparams("parallel"),
        name="qkv_proj",
    )(xt, w_b)


def _attn_a_kernel(q_ref, k_ref, v_ref, w_ref, o_ref, bias_ref):
    S = q_ref.shape[0]
    nblk = S // QBLK_A
    lead = LEFT_CHUNKS * CHUNK
    n_edge = min(lead // QBLK_A, nblk)

    qc = lax.broadcasted_iota(jnp.int32, (QBLK_A, BAND_A), 0) // CHUNK
    kc = lax.broadcasted_iota(jnp.int32, (QBLK_A, BAND_A), 1) // CHUNK
    allowed = (kc >= qc) & (kc <= qc + LEFT_CHUNKS)
    for h in range(HEADS_PER_BLOCK):
        wb = jnp.broadcast_to(w_ref[h], (QBLK_A, BIAS_W))
        toeplitz = pltpu.roll(wb, BIAS_W - (QBLK_A - 1), 1, stride=1, stride_axis=0)
        bias_ref[h] = jnp.where(allowed, toeplitz[:, :BAND_A], MASK_VALUE)

    lane = lax.broadcasted_iota(jnp.int32, (QBLK_A, LANES), 1)
    head_of_lane = lane // HEAD_DIM

    def blocks(specs):
        work = []
        for p, kstart, nk, bias_off in specs:
            q = q_ref[pl.ds(p * QBLK_A, QBLK_A), :]
            k = k_ref[pl.ds(kstart, nk), :]
            v = v_ref[pl.ds(kstart, nk), :]
            for h in range(HEADS_PER_BLOCK):
                qh = jnp.where(head_of_lane == h, q, jnp.zeros_like(q))
                work.append(dict(s=_dot_nt(qh, k), v=v, bias=bias_ref[h, :, bias_off:bias_off + nk]))
        for w in work:
            s = w["s"] + w["bias"]
            e = jnp.exp(s - jnp.max(s, axis=-1, keepdims=True))
            w["l"] = jnp.sum(e, axis=-1, keepdims=True)
            w["e"] = e.astype(BF16)
        outs = [_dot(w["e"], w["v"]) / w["l"] for w in work]
        for i, spec in enumerate(specs):
            o = outs[i * HEADS_PER_BLOCK]
            for h in range(1, HEADS_PER_BLOCK):
                o = jnp.where(head_of_lane == h, outs[i * HEADS_PER_BLOCK + h], o)
            o_ref[pl.ds(spec[0] * QBLK_A, QBLK_A), :] = o

    blocks([(p, 0, (p + 1) * QBLK_A, lead - p * QBLK_A) for p in range(n_edge)])

    def full(p):
        return (p, pl.multiple_of(p * QBLK_A - lead, QBLK_A), BAND_A, 0)

    n_full = nblk - n_edge
    per_trip = BLOCKS_PER_TRIP_A
    rem = n_full % per_trip
    if rem:
        blocks([full(n_edge + r) for r in range(rem)])
    if n_full >= per_trip:
        def body(i, c):
            p = n_edge + rem + per_trip * i
            blocks([full(p + r) for r in range(per_trip)])
            return c
        lax.fori_loop(0, n_full // per_trip, body, 0)


def _attn_a(qkv, bias_w, B, S, d_a):
    n_hb = d_a // LANES
    seg = d_a // LANES
    return pl.pallas_call(
        _attn_a_kernel,
        grid=(B, n_hb),
        in_specs=[pl.BlockSpec((None, S, LANES), lambda b, h: (b, 0, h)),
                  pl.BlockSpec((None, S, LANES), lambda b, h: (b, 0, seg + h)),
                  pl.BlockSpec((None, S, LANES), lambda b, h: (b, 0, 2 * seg + h)),
                  pl.BlockSpec((HEADS_PER_BLOCK, 1, BIAS_W), lambda b, h: (h, 0, 0))],
        out_specs=pl.BlockSpec((None, S, LANES), lambda b, h: (b, 0, h)),
        out_shape=jax.ShapeDtypeStruct((B, S, d_a), F32),
        scratch_shapes=[pltpu.VMEM((HEADS_PER_BLOCK, QBLK_A, BAND_A), F32)],
        compiler_params=_cparams("parallel", "parallel"),
        name="attn_chunked",
    )(qkv, qkv, qkv, bias_w)


def _rel_bias_by_offset(rel_bias):
    dist = jnp.clip(BAND_A - 1 - jnp.arange(BIAS_W), -MAX_REL, MAX_REL) + MAX_REL
    return rel_bias[:, dist].astype(F32)[:, None, :]


def _attn_b_kernel(q_ref, k_ref, v_ref, o_ref):
    t = TILE_B
    row = lax.broadcasted_iota(jnp.int32, (t, t), 0)
    col = lax.broadcasted_iota(jnp.int32, (t, t), 1)
    from_s = (row >= col).astype(BF16)
    from_s2 = jnp.concatenate([from_s, from_s], axis=0)
    causal = col < row

    def q_tile(qi, c):
        q = q_ref[pl.ds(pl.multiple_of(qi * t, t), t), :]
        n_heads = q_ref.shape[1] // HEAD_DIM
        head_of_lane = lax.broadcasted_iota(jnp.int32, q.shape, 1) // HEAD_DIM
        qs = [jnp.where(head_of_lane == h, q, jnp.zeros_like(q)) for h in range(n_heads)]

        def scores(js):
            zs = []
            for j in js:
                kt = k_ref[pl.ds(pl.multiple_of(j * t, t), t), :]
                zs += [_dot_nt(qh, kt) for qh in qs]
            return zs

        def tiles(js, zs, state, first_is_diag=False):
            accs, carries = list(state[0::2]), list(state[1::2])
            work = []
            for n, j in enumerate(js):
                vt = v_ref[pl.ds(pl.multiple_of(j * t, t), t), :]
                for h in range(n_heads):
                    work.append(dict(h=h, diag=first_is_diag and n == 0, v=vt, z=zs[n * n_heads + h]))
            for w in work:
                z = w["z"]
                sp = jnp.maximum(z, 0.0) + jnp.log(1.0 + jnp.exp(-jnp.abs(z)))
                if w["diag"]:
                    sp = jnp.where(causal, sp, 0.0)
                hi = sp.astype(BF16)
                lo = (sp - hi.astype(F32)).astype(BF16)
                w["suffix"] = _dot(jnp.concatenate([hi, lo], axis=-1), from_s2)
                w["rowsum"] = jnp.sum(sp, axis=-1, keepdims=True)
            for w in work:
                h = w["h"]
                if w["diag"]:
                    a = jnp.where(causal, jnp.exp(w["z"] - w["suffix"]), 0.0)
                    accs[h] = _dot(a.astype(BF16), w["v"])
                    carries[h] = w["rowsum"]
                else:
                    a = jnp.exp(w["z"] - w["suffix"] - carries[h])
                    accs[h] = accs[h] + _dot(a.astype(BF16), w["v"])
                    carries[h] = carries[h] + w["rowsum"]
            return tuple(x for pair in zip(accs, carries) for x in pair)

        per_trip = TILES_PER_TRIP_B
        empty = (None,) * (2 * n_heads)

        def opening(extra):
            js = [qi - r for r in range(extra + 1)]
            return lambda: tiles(js, scores(js), empty, True)

        state = lax.switch(qi % per_trip, [opening(r) for r in range(per_trip)])
        first = qi - 1 - qi % per_trip

        def body(it, st):
            js = [first - per_trip * it - r for r in range(per_trip)]
            return tiles(js, scores(js), st)

        state = lax.fori_loop(0, qi // per_trip, body, state)
        out = state[0]
        for h in range(1, n_heads):
            out = jnp.where(head_of_lane == h, state[2 * h], out)
        o_ref[pl.ds(pl.multiple_of(qi * t, t), t), :] = out
        return c

    lax.fori_loop(0, q_ref.shape[0] // t, q_tile, 0)


def _attn_b(qkv, B, S, d_a, d_b):
    w = WIDTH_B
    n_hb = d_b // w
    base = 3 * d_a // w
    seg = d_b // w
    return pl.pallas_call(
        _attn_b_kernel,
        grid=(B, n_hb),
        in_specs=[pl.BlockSpec((None, S, w), lambda b, h: (b, 0, base + h)),
                  pl.BlockSpec((None, S, w), lambda b, h: (b, 0, base + seg + h)),
                  pl.BlockSpec((None, S, w), lambda b, h: (b, 0, base + 2 * seg + h))],
        out_specs=pl.BlockSpec((None, S, w), lambda b, h: (b, 0, h)),
        out_shape=jax.ShapeDtypeStruct((B, S, d_b), F32),
        compiler_params=_cparams("parallel", "parallel"),
        name="attn_stickbreak",
    )(qkv, qkv, qkv)


def _layer_norm(r, g, b):
    mu = jnp.mean(r, axis=-1, keepdims=True)
    c = r - mu
    var = jnp.mean(c * c, axis=-1, keepdims=True)
    return c * lax.rsqrt(var + LN_EPS) * g + b


def _rms_norm(y, g):
    ms = jnp.mean(y * y, axis=-1, keepdims=True)
    return y * lax.rsqrt(ms + RMS_EPS) * g


def _silu(g):
    return g * jax.nn.sigmoid(g)


def _post_kernel(ya_ref, yb_ref, x_ref, ga_ref, gb_ref, wo_ref, g1_ref, b1_ref, rw_ref, rbias_ref,
                 wsg_ref, wsu_ref, wsd_ref, x1_ref, base_ref, idx_ref, gate_ref, rank_ref, cnt_ref, carry_ref,
                 *, alpha):
    @pl.when(pl.program_id(0) == 0)
    def _():
        carry_ref[...] = jnp.zeros_like(carry_ref)

    d_a = ya_ref.shape[1]
    na = _rms_norm(ya_ref[...], ga_ref[...]).astype(BF16)
    nb = _rms_norm(yb_ref[...], gb_ref[...]).astype(BF16)
    h = _dot(na, wo_ref[:d_a, :]) + _dot(nb, wo_ref[d_a:, :])
    x1 = _layer_norm(alpha * x_ref[...] + h, g1_ref[...], b1_ref[...])
    x1_ref[...] = _pack_bf16_halves(x1)
    xb = x1.astype(BF16)
    logits = _dot_nt(rw_ref[...], xb)
    hs = _silu(_dot(xb, wsg_ref[...])) * _dot(xb, wsu_ref[...])
    base_ref[...] = alpha * x1 + _dot(hs.astype(BF16), wsd_ref[...])

    tb = TOK_BLOCK
    row = lax.broadcasted_iota(jnp.int32, (tb, tb), 0)
    col = lax.broadcasted_iota(jnp.int32, (tb, tb), 1)
    earlier = (row < col).astype(BF16)
    carry = carry_ref[...]
    for n in range(logits.shape[1] // tb):
        cols = slice(n * tb, (n + 1) * tb)
        idx, gate, rank, carry = _route_block(logits[:, cols], rbias_ref[...], carry, earlier)
        idx_ref[:, cols] = idx
        gate_ref[:, cols] = gate
        rank_ref[:, cols] = rank
    carry_ref[...] = carry
    cnt_ref[...] = carry.astype(jnp.int32)


def _post(ya, yb, xt, gain_a, gain_b, wo_b, g1, b1, rw_b, rbias, wsg_b, wsu_b, wsd_b, alpha):
    T, D = xt.shape
    d_a, d_b = ya.shape[1], yb.shape[1]
    E = rw_b.shape[0]
    De = wsg_b.shape[1]
    tm = 4 * TOK_BLOCK
    row = lambda i: (i, 0)
    col = lambda i: (0, i)
    fix = lambda i: (0, 0)
    return pl.pallas_call(
        functools.partial(_post_kernel, alpha=alpha),
        grid=(T // tm,),
        in_specs=[pl.BlockSpec((tm, d_a), row), pl.BlockSpec((tm, d_b), row), pl.BlockSpec((tm, D), row),
                  pl.BlockSpec((1, d_a), fix), pl.BlockSpec((1, d_b), fix),
                  pl.BlockSpec((d_a + d_b, D), fix), pl.BlockSpec((1, D), fix), pl.BlockSpec((1, D), fix),
                  pl.BlockSpec((E, D), fix), pl.BlockSpec((E, 1), fix),
                  pl.BlockSpec((D, De), fix), pl.BlockSpec((D, De), fix), pl.BlockSpec((De, D), fix)],
        out_specs=[pl.BlockSpec((tm, D // 2), row), pl.BlockSpec((tm, D), row),
                   pl.BlockSpec((TOP_K, tm), col), pl.BlockSpec((TOP_K, tm), col), pl.BlockSpec((TOP_K, tm), col),
                   pl.BlockSpec((E, 1), fix)],
        out_shape=[jax.ShapeDtypeStruct((T, D // 2), jnp.uint32), jax.ShapeDtypeStruct((T, D), F32),
                   jax.ShapeDtypeStruct((TOP_K, T), jnp.int32), jax.ShapeDtypeStruct((TOP_K, T), F32),
                   jax.ShapeDtypeStruct((TOP_K, T), jnp.int32), jax.ShapeDtypeStruct((E, 1), jnp.int32)],
        scratch_shapes=[pltpu.VMEM((E, 1), F32)],
        compiler_params=_cparams("arbitrary"),
        name="post_attn_route",
    )(ya, yb, xt, gain_a, gain_b, wo_b, g1, b1, rw_b, rbias, wsg_b, wsu_b, wsd_b)


def _route_block(logits, rbias, carry, earlier):
    E, tm = logits.shape
    neg = -jnp.inf
    scores = jax.nn.sigmoid(logits)
    sel = scores + rbias
    eidx = lax.broadcasted_iota(jnp.int32, (E, tm), 0).astype(F32)

    def first_argmax(v, ids):
        m = jnp.max(v, axis=0, keepdims=True)
        return m, jnp.min(jnp.where(v == m, ids, float(E)), axis=0, keepdims=True)

    grp_scores = []
    ids = lax.broadcasted_iota(jnp.int32, (EXPERTS_PER_GROUP, tm), 0).astype(F32)
    for g in range(N_GROUPS):
        v = sel[g * EXPERTS_PER_GROUP:(g + 1) * EXPERTS_PER_GROUP, :]
        m1, i1 = first_argmax(v, ids)
        m2 = jnp.max(jnp.where(ids == i1, neg, v), axis=0, keepdims=True)
        grp_scores.append(m1 + m2)
    parts = []
    for g in range(N_GROUPS):
        beaten = jnp.zeros((1, tm), jnp.int32)
        for o in range(N_GROUPS):
            if o != g:
                s, t = grp_scores[o], grp_scores[g]
                beaten = beaten + ((s > t) | ((s == t) & (o < g))).astype(jnp.int32)
        rows = slice(g * EXPERTS_PER_GROUP, (g + 1) * EXPERTS_PER_GROUP)
        parts.append(jnp.where(beaten < TOPK_GROUPS, sel[rows, :], neg))
    selm = jnp.concatenate(parts, axis=0)

    hits, gates, ids_k = [], [], []
    chosen = jnp.zeros((E, tm), F32)
    gate_sum = jnp.zeros((1, tm), F32)
    for k in range(TOP_K):
        _, ik = first_argmax(selm, eidx)
        hit = eidx == ik
        gk = jnp.sum(jnp.where(hit, scores, 0.0), axis=0, keepdims=True)
        selm = jnp.where(hit, neg, selm)
        chosen = jnp.where(hit, 1.0, chosen)
        gate_sum = gate_sum + gk
        hits.append(hit)
        gates.append(gk)
        ids_k.append(ik)

    before = _dot(chosen.astype(BF16), earlier) + carry
    ranks = [jnp.sum(jnp.where(hit, before, 0.0), axis=0, keepdims=True) for hit in hits]
    return (jnp.concatenate(ids_k, axis=0).astype(jnp.int32),
            jnp.concatenate(gates, axis=0) / gate_sum * ROUTED_SCALE,
            jnp.concatenate(ranks, axis=0).astype(jnp.int32),
            carry + jnp.sum(chosen, axis=1, keepdims=True))


def _expert_kernel(first_ref, nblk_ref, nused_ref, xs_hbm, wg_ref, wu_ref, wd_ref, ys_hbm,
                   xbuf, ybuf, wgu_s, wd_s, in_sem, out_sem):
    ring = xbuf.shape[0]
    e = pl.program_id(0)
    De = wg_ref.shape[2]
    n = nblk_ref[e]
    g0 = first_ref[e]
    n_used = nused_ref[0]

    def rows(g):
        return pl.ds(pl.multiple_of(g * ROW_BLOCK, ROW_BLOCK), ROW_BLOCK)

    def in_copy(g, slot):
        return pltpu.make_async_copy(xs_hbm.at[rows(g), :], xbuf.at[slot], in_sem.at[slot])

    def out_copy(g, slot):
        return pltpu.make_async_copy(ybuf.at[slot], ys_hbm.at[rows(g), :], out_sem.at[slot])

    @pl.when(e == 0)
    def _():
        for g in range(ring - 2):
            @pl.when(g < n_used)
            def _():
                in_copy(g, g).start()

    def acquire(g, ahead):
        slot = g & (ring - 1)
        in_copy(g, slot).wait()

        @pl.when(g + ahead < n_used)
        def _():
            in_copy(g + ahead, (g + ahead) & (ring - 1)).start()

        @pl.when(g >= ring)
        def _():
            out_copy(g - ring, slot).wait()
        return slot

    def ffn(pieces):
        dh = xbuf.shape[2]
        gus = []
        for slot, rs in pieces:
            lo, hi = _unpack_bf16_halves(xbuf[slot, rs, :])
            gus.append(_dot(lo.astype(BF16), wgu_s[:dh, :]) + _dot(hi.astype(BF16), wgu_s[dh:, :]))
        hs = [(_silu(gu[:, :De]) * gu[:, De:]).astype(BF16) for gu in gus]
        for (slot, rs), h in zip(pieces, hs):
            ybuf[slot, rs, :] = _pack_bf16_halves(_dot(h, wd_s[...]))

    @pl.when(n > 0)
    def _():
        wgu_s[:, :De] = wg_ref[0].astype(BF16)
        wgu_s[:, De:] = wu_ref[0].astype(BF16)
        wd_s[...] = wd_ref[0].astype(BF16)

        def pair(j, c):
            g = g0 + 2 * j
            slots = [acquire(g + b, ring - 2) for b in range(2)]
            ffn([(slot, slice(None)) for slot in slots])
            for b, slot in enumerate(slots):
                out_copy(g + b, slot).start()
            return c

        lax.fori_loop(0, n // 2, pair, 0)

        @pl.when(n % 2 == 1)
        def _():
            g = g0 + n - 1
            slot = acquire(g, ring - 2)
            half = ROW_BLOCK // 2
            ffn([(slot, slice(0, half)), (slot, slice(half, ROW_BLOCK))])
            out_copy(g, slot).start()

    @pl.when(e == pl.num_programs(0) - 1)
    def _():
        for back in range(1, ring + 1):
            @pl.when(n_used >= back)
            def _():
                out_copy(n_used - back, (n_used - back) & (ring - 1)).wait()


def _experts(first_block, n_blocks, n_used, xs, w_gate, w_up, w_down):
    n_rows, Dh = xs.shape
    D = 2 * Dh
    E, _, De = w_gate.shape
    grid_spec = pltpu.PrefetchScalarGridSpec(
        num_scalar_prefetch=3,
        grid=(E,),
        in_specs=[pl.BlockSpec(memory_space=pl.ANY),
                  pl.BlockSpec((1, D, De), lambda e, *_: (e, 0, 0)),
                  pl.BlockSpec((1, D, De), lambda e, *_: (e, 0, 0)),
                  pl.BlockSpec((1, De, D), lambda e, *_: (e, 0, 0))],
        out_specs=pl.BlockSpec(memory_space=pl.ANY),
        scratch_shapes=[pltpu.VMEM((EXPERT_RING, ROW_BLOCK, Dh), jnp.uint32),
                        pltpu.VMEM((EXPERT_RING, ROW_BLOCK, Dh), jnp.uint32),
                        pltpu.VMEM((D, 2 * De), BF16), pltpu.VMEM((De, D), BF16),
                        pltpu.SemaphoreType.DMA((EXPERT_RING,)), pltpu.SemaphoreType.DMA((EXPERT_RING,))],
    )
    return pl.pallas_call(
        _expert_kernel,
        grid_spec=grid_spec,
        out_shape=jax.ShapeDtypeStruct((n_rows, Dh), jnp.uint32),
        compiler_params=_cparams("arbitrary"),
        name="expert_ffn",
    )(first_block, n_blocks, n_used, xs, w_gate, w_up, w_down)


SC_WINDOW = 128


def _sc_mesh():
    return plsc.VectorSubcoreMesh(core_axis_name="core", subcore_axis_name="subcore")


def _sc_workers():
    info = plsc.get_sparse_core_info()
    return info.num_cores, info.num_cores * info.num_subcores


def _sc_worker_id(num_cores):
    return lax.axis_index("subcore") * num_cores + lax.axis_index("core")


def _sc_scatter_rows(x, idx_flat, rank_flat, start, n_rows):
    T, D = x.shape
    K = idx_flat.shape[0] // T
    E = start.shape[0]
    nc, nw = _sc_workers()
    lanes = plsc.get_sparse_core_info().num_lanes
    per_w = T // nw
    assert T % nw == 0 and per_w % SC_WINDOW == 0 and SC_WINDOW % lanes == 0

    @functools.partial(
        pl.kernel,
        out_type=(jax.ShapeDtypeStruct((n_rows, D), x.dtype), jax.ShapeDtypeStruct((K * T,), jnp.int32)),
        mesh=_sc_mesh(),
        scratch_types=[pltpu.VMEM((SC_WINDOW,), jnp.int32), pltpu.VMEM((SC_WINDOW,), jnp.int32),
                       pltpu.VMEM((E,), jnp.int32), pltpu.VMEM((SC_WINDOW, D), x.dtype)],
        compiler_params=pltpu.CompilerParams(needs_layout_passes=False),
        name="sc_dispatch")
    def run(x_hbm, i_hbm, r_hbm, s_hbm, o_hbm, d_hbm, idx_v, rank_v, start_v, rows_v):
        first = _sc_worker_id(nc) * per_w
        pltpu.sync_copy(s_hbm, start_v)

        @pl.loop(0, per_w // SC_WINDOW)
        def _(c):
            base = first + c * SC_WINDOW
            pltpu.sync_copy(x_hbm.at[pl.ds(base, SC_WINDOW)], rows_v)
            for k in range(K):
                window = pl.ds(k * T + base, SC_WINDOW)
                pltpu.sync_copy(i_hbm.at[window], idx_v)
                pltpu.sync_copy(r_hbm.at[window], rank_v)
                for s in range(SC_WINDOW // lanes):
                    part = pl.ds(s * lanes, lanes)
                    idx_v[part] = plsc.load_gather(start_v, [idx_v[part]]) + rank_v[part]
                pltpu.sync_copy(idx_v, d_hbm.at[window])
                pltpu.sync_copy(rows_v, o_hbm.at[idx_v])

    return run(x, idx_flat, rank_flat, start)


def _sc_gather_rows(ys, dest_flat):
    N = dest_flat.shape[0]
    D = ys.shape[1]
    nc, nw = _sc_workers()
    per_w = N // nw
    assert N % nw == 0 and per_w % SC_WINDOW == 0

    @functools.partial(
        pl.kernel, out_type=jax.ShapeDtypeStruct((N, D), ys.dtype), mesh=_sc_mesh(),
        scratch_types=[pltpu.VMEM((SC_WINDOW,), jnp.int32), pltpu.VMEM((SC_WINDOW, D), ys.dtype)],
        name="sc_combine_gather")
    def run(y_hbm, i_hbm, o_hbm, idx_v, rows_v):
        first = _sc_worker_id(nc) * per_w

        @pl.loop(0, per_w // SC_WINDOW)
        def _(c):
            base = first + c * SC_WINDOW
            pltpu.sync_copy(i_hbm.at[pl.ds(base, SC_WINDOW)], idx_v)
            pltpu.sync_copy(y_hbm.at[idx_v], rows_v)
            pltpu.sync_copy(rows_v, o_hbm.at[pl.ds(base, SC_WINDOW)])

    return run(ys, dest_flat)


def _final_kernel(z_ref, gate_ref, base_ref, g_ref, b_ref, o_ref):
    gate = gate_ref[...]
    acc_lo = acc_hi = None
    for k in range(TOP_K):
        lo, hi = _unpack_bf16_halves(z_ref[k])
        g = gate[:, k:k + 1]
        acc_lo = g * lo if acc_lo is None else acc_lo + g * lo
        acc_hi = g * hi if acc_hi is None else acc_hi + g * hi
    acc = base_ref[...] + jnp.concatenate([acc_lo, acc_hi], axis=-1)
    o_ref[...] = _layer_norm(acc, g_ref[...], b_ref[...])


def _final(z, gate, base, g2, b2, first_token):
    T, D = base.shape
    tb = TOK_BLOCK
    off = first_token // tb
    return pl.pallas_call(
        _final_kernel,
        grid=(z.shape[1] // tb,),
        in_specs=[pl.BlockSpec((TOP_K, tb, D // 2), lambda i: (0, i, 0)),
                  pl.BlockSpec((tb, TOP_K), lambda i: (i + off, 0)),
                  pl.BlockSpec((tb, D), lambda i: (i + off, 0)),
                  pl.BlockSpec((1, D), lambda i: (0, 0)),
                  pl.BlockSpec((1, D), lambda i: (0, 0))],
        out_specs=pl.BlockSpec((tb, D), lambda i: (i + off, 0)),
        out_shape=jax.ShapeDtypeStruct((T, D), F32),
        input_output_aliases={2: 0},
        compiler_params=_cparams("parallel"),
        name="final_sum_ln",
    )(z, gate, base, g2, b2)


def _layer(x, w_in, rel_bias, gain_a, gain_b, w_out, ln1_g, ln1_b, router_w, router_bias,
           w_gate, w_up, w_down, ws_gate, ws_up, ws_down, ln2_g, ln2_b, alpha):
    B, S, D = x.shape
    T = B * S
    d_a = gain_a.shape[0]
    d_b = gain_b.shape[0]
    assert S % TILE_B == 0 and T % 512 == 0 and d_a % LANES == 0 and d_b % WIDTH_B == 0 and (3 * d_a) % WIDTH_B == 0
    xt = x.reshape(T, D)

    scale = HEAD_DIM ** -0.5
    col = jnp.arange(w_in.shape[1])
    is_q = (col < d_a) | ((col >= 3 * d_a) & (col < 3 * d_a + d_b))
    w_in_b = (w_in * jnp.where(is_q, scale, 1.0)[None, :]).astype(BF16)

    qkv = _qkv_proj(xt, w_in_b).reshape(B, S, -1)
    ya = _attn_a(qkv, _rel_bias_by_offset(rel_bias), B, S, d_a).reshape(T, d_a)
    yb = _attn_b(qkv, B, S, d_a, d_b).reshape(T, d_b)

    x1p, base, idx, gate, rank, cnt = _post(
        ya, yb, xt, gain_a[None], gain_b[None], w_out.astype(BF16), ln1_g[None], ln1_b[None],
        router_w.T.astype(BF16), router_bias[:, None].astype(F32),
        ws_gate.astype(BF16), ws_up.astype(BF16), ws_down.astype(BF16), alpha)
    cnt = cnt[:, 0]
    padded = (cnt + ROW_BLOCK - 1) // ROW_BLOCK * ROW_BLOCK
    ends = jnp.cumsum(padded)
    start = (ends - padded).astype(jnp.int32)
    n_rows = T * TOP_K + N_EXPERTS * ROW_BLOCK
    first_block = start // ROW_BLOCK
    n_blocks = (padded // ROW_BLOCK).astype(jnp.int32)
    n_used = (ends[-1:] // ROW_BLOCK).astype(jnp.int32)
    xs, dest = _sc_scatter_rows(x1p, idx.reshape(-1), rank.reshape(-1), start, n_rows)
    dest = dest.reshape(TOP_K, T)
    ys = _experts(first_block, n_blocks, n_used, xs, w_gate, w_up, w_down)
    gate_t = gate.T
    out = base
    tg = T // COMBINE_GROUPS
    for q in range(COMBINE_GROUPS):
        z = _sc_gather_rows(ys, dest[:, q * tg:(q + 1) * tg].reshape(-1)).reshape(TOP_K, tg, D // 2)
        out = _final(z, gate_t, out, ln2_g[None], ln2_b[None], q * tg)
    return out.reshape(B, S, D)


def kernel(x, w_in, rel_bias, gain_a, gain_b, w_out, ln1_g, ln1_b, router_w, router_bias,
           w_gate, w_up, w_down, ws_gate, ws_up, ws_down, ln2_g, ln2_b):
    depth = w_in.shape[0]
    alpha = (2 * depth) ** 0.25
    for l in range(depth):
        x = _layer(x, w_in[l], rel_bias[l], gain_a[l], gain_b[l], w_out[l], ln1_g[l], ln1_b[l],
                   router_w[l], router_bias[l], w_gate[l], w_up[l], w_down[l],
                   ws_gate[l], ws_up[l], ws_down[l], ln2_g[l], ln2_b[l], alpha)
    return x
```

```python
import functools

import jax
import jax.numpy as jnp
from jax import lax
from jax.experimental import pallas as pl
from jax.experimental.pallas import tpu as pltpu
from jax.experimental.pallas import tpu_sc as plsc

CHUNK = 64
HEAD_DIM = 64
LEFT_CHUNKS = 8
MAX_REL = 128
N_EXPERTS = 256
TOP_K = 8
N_GROUPS = 8
TOPK_GROUPS = 4
EXPERTS_PER_GROUP = N_EXPERTS // N_GROUPS
ROUTED_SCALE = 2.5
LN_EPS = 1e-5
RMS_EPS = 1e-6

LANES = 128
HEADS_PER_BLOCK = LANES // HEAD_DIM
QBLK_A = 2 * CHUNK
BAND_A = (LEFT_CHUNKS + 2) * CHUNK
BIAS_W = BAND_A + QBLK_A
BLOCKS_PER_TRIP_A = 4
TILE_B = 256
WIDTH_B = 2 * HEAD_DIM
TILES_PER_TRIP_B = 3
ROW_BLOCK = 256
EXPERT_RING = 8
TOK_BLOCK = 256
COMBINE_GROUPS = 4
MASK_VALUE = -1e30
V7X_VMEM_BYTES = 64 * 1024 * 1024
VMEM_LIMIT = V7X_VMEM_BYTES * 3 // 4

F32 = jnp.float32
BF16 = jnp.bfloat16


def _cparams(*sem):
    return pltpu.CompilerParams(dimension_semantics=sem, vmem_limit_bytes=VMEM_LIMIT)


def _dot(a, b):
    return jnp.dot(a, b, preferred_element_type=F32)


def _dot_nt(a, b):
    return lax.dot_general(a, b, (((1,), (1,)), ((), ())), preferred_element_type=F32)


def _pack_bf16_halves(x):
    half = x.shape[1] // 2
    bits = lax.bitcast_convert_type(x.astype(BF16).astype(F32), jnp.uint32)
    return (bits[:, :half] >> 16) | (bits[:, half:] & jnp.uint32(0xFFFF0000))


def _unpack_bf16_halves(w):
    lo = lax.bitcast_convert_type(w << 16, F32)
    hi = lax.bitcast_convert_type(w & jnp.uint32(0xFFFF0000), F32)
    return lo, hi


def _qkv_kernel(x_ref, w_ref, o_ref, *, col_chunk):
    xb = x_ref[...].astype(BF16)
    for n in range(w_ref.shape[1] // col_chunk):
        cols = slice(n * col_chunk, (n + 1) * col_chunk)
        o_ref[:, cols] = _dot(xb, w_ref[:, cols]).astype(BF16)


def _qkv_proj(xt, w_b):
    T, D = xt.shape
    N = w_b.shape[1]
    tm = 512
    return pl.pallas_call(
        functools.partial(_qkv_kernel, col_chunk=512),
        grid=(T // tm,),
        in_specs=[pl.BlockSpec((tm, D), lambda i: (i, 0)),
                  pl.BlockSpec((D, N), lambda i: (0, 0))],
        out_specs=pl.BlockSpec((tm, N), lambda i: (i, 0)),
        out_shape=jax.ShapeDtypeStruct((T, N), BF16),
        compiler_params=_cparams("parallel"),
        name="qkv_proj",
    )(xt, w_b)


def _attn_a_kernel(q_ref, k_ref, v_ref, w_ref, o_ref, bias_ref):
    S = q_ref.shape[0]
    nblk = S // QBLK_A
    lead = LEFT_CHUNKS * CHUNK
    n_edge = min(lead // QBLK_A, nblk)

    qc = lax.broadcasted_iota(jnp.int32, (QBLK_A, BAND_A), 0) // CHUNK
    kc = lax.broadcasted_iota(jnp.int32, (QBLK_A, BAND_A), 1) // CHUNK
    allowed = (kc >= qc) & (kc <= qc + LEFT_CHUNKS)
    for h in range(HEADS_PER_BLOCK):
        wb = jnp.broadcast_to(w_ref[h], (QBLK_A, BIAS_W))
        toeplitz = pltpu.roll(wb, BIAS_W - (QBLK_A - 1), 1, stride=1, stride_axis=0)
        bias_ref[h] = jnp.where(allowed, toeplitz[:, :BAND_A], MASK_VALUE)

    lane = lax.broadcasted_iota(jnp.int32, (QBLK_A, LANES), 1)
    head_of_lane = lane // HEAD_DIM

    def blocks(specs):
        work = []
        for p, kstart, nk, bias_off in specs:
            q = q_ref[pl.ds(p * QBLK_A, QBLK_A), :]
            k = k_ref[pl.ds(kstart, nk), :]
            v = v_ref[pl.ds(kstart, nk), :]
            for h in range(HEADS_PER_BLOCK):
                qh = jnp.where(head_of_lane == h, q, jnp.zeros_like(q))
                work.append(dict(s=_dot_nt(qh, k), v=v, bias=bias_ref[h, :, bias_off:bias_off + nk]))
        for w in work:
            s = w["s"] + w["bias"]
            e = jnp.exp(s - jnp.max(s, axis=-1, keepdims=True))
            w["l"] = jnp.sum(e, axis=-1, keepdims=True)
            w["e"] = e.astype(BF16)
        outs = [_dot(w["e"], w["v"]) / w["l"] for w in work]
        for i, spec in enumerate(specs):
            o = outs[i * HEADS_PER_BLOCK]
            for h in range(1, HEADS_PER_BLOCK):
                o = jnp.where(head_of_lane == h, outs[i * HEADS_PER_BLOCK + h], o)
            o_ref[pl.ds(spec[0] * QBLK_A, QBLK_A), :] = o

    blocks([(p, 0, (p + 1) * QBLK_A, lead - p * QBLK_A) for p in range(n_edge)])

    def full(p):
        return (p, pl.multiple_of(p * QBLK_A - lead, QBLK_A), BAND_A, 0)

    n_full = nblk - n_edge
    per_trip = BLOCKS_PER_TRIP_A
    rem = n_full % per_trip
    if rem:
        blocks([full(n_edge + r) for r in range(rem)])
    if n_full >= per_trip:
        def body(i, c):
            p = n_edge + rem + per_trip * i
            blocks([full(p + r) for r in range(per_trip)])
            return c
        lax.fori_loop(0, n_full // per_trip, body, 0)


def _attn_a(qkv, bias_w, B, S, d_a):
    n_hb = d_a // LANES
    seg = d_a // LANES
    return pl.pallas_call(
        _attn_a_kernel,
        grid=(B, n_hb),
        in_specs=[pl.BlockSpec((None, S, LANES), lambda b, h: (b, 0, h)),
                  pl.BlockSpec((None, S, LANES), lambda b, h: (b, 0, seg + h)),
                  pl.BlockSpec((None, S, LANES), lambda b, h: (b, 0, 2 * seg + h)),
                  pl.BlockSpec((HEADS_PER_BLOCK, 1, BIAS_W), lambda b, h: (h, 0, 0))],
        out_specs=pl.BlockSpec((None, S, LANES), lambda b, h: (b, 0, h)),
        out_shape=jax.ShapeDtypeStruct((B, S, d_a), F32),
        scratch_shapes=[pltpu.VMEM((HEADS_PER_BLOCK, QBLK_A, BAND_A), F32)],
        compiler_params=_cparams("parallel", "parallel"),
        name="attn_chunked",
    )(qkv, qkv, qkv, bias_w)


def _rel_bias_by_offset(rel_bias):
    dist = jnp.clip(BAND_A - 1 - jnp.arange(BIAS_W), -MAX_REL, MAX_REL) + MAX_REL
    return rel_bias[:, dist].astype(F32)[:, None, :]


def _attn_b_kernel(q_ref, k_ref, v_ref, o_ref):
    t = TILE_B
    row = lax.broadcasted_iota(jnp.int32, (t, t), 0)
    col = lax.broadcasted_iota(jnp.int32, (t, t), 1)
    from_s = (row >= col).astype(BF16)
    from_s2 = jnp.concatenate([from_s, from_s], axis=0)
    causal = col < row

    def q_tile(qi, c):
        q = q_ref[pl.ds(pl.multiple_of(qi * t, t), t), :]
        n_heads = q_ref.shape[1] // HEAD_DIM
        head_of_lane = lax.broadcasted_iota(jnp.int32, q.shape, 1) // HEAD_DIM
        qs = [jnp.where(head_of_lane == h, q, jnp.zeros_like(q)) for h in range(n_heads)]

        def scores(js):
            zs = []
            for j in js:
                kt = k_ref[pl.ds(pl.multiple_of(j * t, t), t), :]
                zs += [_dot_nt(qh, kt) for qh in qs]
            return zs

        def tiles(js, zs, state, first_is_diag=False):
            accs, carries = list(state[0::2]), list(state[1::2])
            work = []
            for n, j in enumerate(js):
                vt = v_ref[pl.ds(pl.multiple_of(j * t, t), t), :]
                for h in range(n_heads):
                    work.append(dict(h=h, diag=first_is_diag and n == 0, v=vt, z=zs[n * n_heads + h]))
            for w in work:
                z = w["z"]
                sp = jnp.maximum(z, 0.0) + jnp.log(1.0 + jnp.exp(-jnp.abs(z)))
                if w["diag"]:
                    sp = jnp.where(causal, sp, 0.0)
                hi = sp.astype(BF16)
                lo = (sp - hi.astype(F32)).astype(BF16)
                w["suffix"] = _dot(jnp.concatenate([hi, lo], axis=-1), from_s2)
                w["rowsum"] = jnp.sum(sp, axis=-1, keepdims=True)
            for w in work:
                h = w["h"]
                if w["diag"]:
                    a = jnp.where(causal, jnp.exp(w["z"] - w["suffix"]), 0.0)
                    accs[h] = _dot(a.astype(BF16), w["v"])
                    carries[h] = w["rowsum"]
                else:
                    a = jnp.exp(w["z"] - w["suffix"] - carries[h])
                    accs[h] = accs[h] + _dot(a.astype(BF16), w["v"])
                    carries[h] = carries[h] + w["rowsum"]
            return tuple(x for pair in zip(accs, carries) for x in pair)

        per_trip = TILES_PER_TRIP_B
        empty = (None,) * (2 * n_heads)

        def opening(extra):
            js = [qi - r for r in range(extra + 1)]
            return lambda: tiles(js, scores(js), empty, True)

        state = lax.switch(qi % per_trip, [opening(r) for r in range(per_trip)])
        first = qi - 1 - qi % per_trip

        def body(it, st):
            js = [first - per_trip * it - r for r in range(per_trip)]
            return tiles(js, scores(js), st)

        state = lax.fori_loop(0, qi // per_trip, body, state)
        out = state[0]
        for h in range(1, n_heads):
            out = jnp.where(head_of_lane == h, state[2 * h], out)
        o_ref[pl.ds(pl.multiple_of(qi * t, t), t), :] = out
        return c

    lax.fori_loop(0, q_ref.shape[0] // t, q_tile, 0)


def _attn_b(qkv, B, S, d_a, d_b):
    w = WIDTH_B
    n_hb = d_b // w
    base = 3 * d_a // w
    seg = d_b // w
    return pl.pallas_call(
        _attn_b_kernel,
        grid=(B, n_hb),
        in_specs=[pl.BlockSpec((None, S, w), lambda b, h: (b, 0, base + h)),
                  pl.BlockSpec((None, S, w), lambda b, h: (b, 0, base + seg + h)),
                  pl.BlockSpec((None, S, w), lambda b, h: (b, 0, base + 2 * seg + h))],
        out_specs=pl.BlockSpec((None, S, w), lambda b, h: (b, 0, h)),
        out_shape=jax.ShapeDtypeStruct((B, S, d_b), F32),
        compiler_params=_cparams("parallel", "parallel"),
        name="attn_stickbreak",
    )(qkv, qkv, qkv)


def _layer_norm(r, g, b):
    mu = jnp.mean(r, axis=-1, keepdims=True)
    c = r - mu
    var = jnp.mean(c * c, axis=-1, keepdims=True)
    return c * lax.rsqrt(var + LN_EPS) * g + b


def _rms_norm(y, g):
    ms = jnp.mean(y * y, axis=-1, keepdims=True)
    return y * lax.rsqrt(ms + RMS_EPS) * g


def _silu(g):
    return g * jax.nn.sigmoid(g)


def _post_kernel(ya_ref, yb_ref, x_ref, ga_ref, gb_ref, wo_ref, g1_ref, b1_ref, rw_ref, rbias_ref,
                 wsg_ref, wsu_ref, wsd_ref, x1_ref, base_ref, idx_ref, gate_ref, rank_ref, cnt_ref, carry_ref,
                 *, alpha):
    @pl.when(pl.program_id(0) == 0)
    def _():
        carry_ref[...] = jnp.zeros_like(carry_ref)

    d_a = ya_ref.shape[1]
    na = _rms_norm(ya_ref[...], ga_ref[...]).astype(BF16)
    nb = _rms_norm(yb_ref[...], gb_ref[...]).astype(BF16)
    h = _dot(na, wo_ref[:d_a, :]) + _dot(nb, wo_ref[d_a:, :])
    x1 = _layer_norm(alpha * x_ref[...] + h, g1_ref[...], b1_ref[...])
    x1_ref[...] = _pack_bf16_halves(x1)
    xb = x1.astype(BF16)
    logits = _dot_nt(rw_ref[...], xb)
    hs = _silu(_dot(xb, wsg_ref[...])) * _dot(xb, wsu_ref[...])
    base_ref[...] = alpha * x1 + _dot(hs.astype(BF16), wsd_ref[...])

    tb = TOK_BLOCK
    row = lax.broadcasted_iota(jnp.int32, (tb, tb), 0)
    col = lax.broadcasted_iota(jnp.int32, (tb, tb), 1)
    earlier = (row < col).astype(BF16)
    carry = carry_ref[...]
    for n in range(logits.shape[1] // tb):
        cols = slice(n * tb, (n + 1) * tb)
        idx, gate, rank, carry = _route_block(logits[:, cols], rbias_ref[...], carry, earlier)
        idx_ref[:, cols] = idx
        gate_ref[:, cols] = gate
        rank_ref[:, cols] = rank
    carry_ref[...] = carry
    cnt_ref[...] = carry.astype(jnp.int32)


def _post(ya, yb, xt, gain_a, gain_b, wo_b, g1, b1, rw_b, rbias, wsg_b, wsu_b, wsd_b, alpha):
    T, D = xt.shape
    d_a, d_b = ya.shape[1], yb.shape[1]
    E = rw_b.shape[0]
    De = wsg_b.shape[1]
    tm = 4 * TOK_BLOCK
    row = lambda i: (i, 0)
    col = lambda i: (0, i)
    fix = lambda i: (0, 0)
    return pl.pallas_call(
        functools.partial(_post_kernel, alpha=alpha),
        grid=(T // tm,),
        in_specs=[pl.BlockSpec((tm, d_a), row), pl.BlockSpec((tm, d_b), row), pl.BlockSpec((tm, D), row),
                  pl.BlockSpec((1, d_a), fix), pl.BlockSpec((1, d_b), fix),
                  pl.BlockSpec((d_a + d_b, D), fix), pl.BlockSpec((1, D), fix), pl.BlockSpec((1, D), fix),
                  pl.BlockSpec((E, D), fix), pl.BlockSpec((E, 1), fix),
                  pl.BlockSpec((D, De), fix), pl.BlockSpec((D, De), fix), pl.BlockSpec((De, D), fix)],
        out_specs=[pl.BlockSpec((tm, D // 2), row), pl.BlockSpec((tm, D), row),
                   pl.BlockSpec((TOP_K, tm), col), pl.BlockSpec((TOP_K, tm), col), pl.BlockSpec((TOP_K, tm), col),
                   pl.BlockSpec((E, 1), fix)],
        out_shape=[jax.ShapeDtypeStruct((T, D // 2), jnp.uint32), jax.ShapeDtypeStruct((T, D), F32),
                   jax.ShapeDtypeStruct((TOP_K, T), jnp.int32), jax.ShapeDtypeStruct((TOP_K, T), F32),
                   jax.ShapeDtypeStruct((TOP_K, T), jnp.int32), jax.ShapeDtypeStruct((E, 1), jnp.int32)],
        scratch_shapes=[pltpu.VMEM((E, 1), F32)],
        compiler_params=_cparams("arbitrary"),
        name="post_attn_route",
    )(ya, yb, xt, gain_a, gain_b, wo_b, g1, b1, rw_b, rbias, wsg_b, wsu_b, wsd_b)


def _route_block(logits, rbias, carry, earlier):
    E, tm = logits.shape
    neg = -jnp.inf
    scores = jax.nn.sigmoid(logits)
    sel = scores + rbias
    eidx = lax.broadcasted_iota(jnp.int32, (E, tm), 0).astype(F32)

    def first_argmax(v, ids):
        m = jnp.max(v, axis=0, keepdims=True)
        return m, jnp.min(jnp.where(v == m, ids, float(E)), axis=0, keepdims=True)

    grp_scores = []
    ids = lax.broadcasted_iota(jnp.int32, (EXPERTS_PER_GROUP, tm), 0).astype(F32)
    for g in range(N_GROUPS):
        v = sel[g * EXPERTS_PER_GROUP:(g + 1) * EXPERTS_PER_GROUP, :]
        m1, i1 = first_argmax(v, ids)
        m2 = jnp.max(jnp.where(ids == i1, neg, v), axis=0, keepdims=True)
        grp_scores.append(m1 + m2)
    parts = []
    for g in range(N_GROUPS):
        beaten = jnp.zeros((1, tm), jnp.int32)
        for o in range(N_GROUPS):
            if o != g:
                s, t = grp_scores[o], grp_scores[g]
                beaten = beaten + ((s > t) | ((s == t) & (o < g))).astype(jnp.int32)
        rows = slice(g * EXPERTS_PER_GROUP, (g + 1) * EXPERTS_PER_GROUP)
        parts.append(jnp.where(beaten < TOPK_GROUPS, sel[rows, :], neg))
    selm = jnp.concatenate(parts, axis=0)

    hits, gates, ids_k = [], [], []
    chosen = jnp.zeros((E, tm), F32)
    gate_sum = jnp.zeros((1, tm), F32)
    for k in range(TOP_K):
        _, ik = first_argmax(selm, eidx)
        hit = eidx == ik
        gk = jnp.sum(jnp.where(hit, scores, 0.0), axis=0, keepdims=True)
        selm = jnp.where(hit, neg, selm)
        chosen = jnp.where(hit, 1.0, chosen)
        gate_sum = gate_sum + gk
        hits.append(hit)
        gates.append(gk)
        ids_k.append(ik)

    before = _dot(chosen.astype(BF16), earlier) + carry
    ranks = [jnp.sum(jnp.where(hit, before, 0.0), axis=0, keepdims=True) for hit in hits]
    return (jnp.concatenate(ids_k, axis=0).astype(jnp.int32),
            jnp.concatenate(gates, axis=0) / gate_sum * ROUTED_SCALE,
            jnp.concatenate(ranks, axis=0).astype(jnp.int32),
            carry + jnp.sum(chosen, axis=1, keepdims=True))


def _dest_kernel(idx_ref, rank_ref, start_ref, dest_ref):
    K, tm = idx_ref.shape
    E = start_ref.shape[0]
    eidx = lax.broadcasted_iota(jnp.int32, (E, tm), 0)
    idx = idx_ref[...]
    start = start_ref[...]
    rows = [jnp.sum(jnp.where(eidx == idx[k:k + 1, :], start, 0.0), axis=0, keepdims=True) for k in range(K)]
    dest_ref[...] = jnp.concatenate(rows, axis=0).astype(jnp.int32) + rank_ref[...]


def _dest(idx, rank, start_f):
    K, T = idx.shape
    E = start_f.shape[0]
    tm = 2 * TOK_BLOCK
    col = lambda i: (0, i)
    return pl.pallas_call(
        _dest_kernel,
        grid=(T // tm,),
        in_specs=[pl.BlockSpec((K, tm), col), pl.BlockSpec((K, tm), col), pl.BlockSpec((E, 1), lambda i: (0, 0))],
        out_specs=pl.BlockSpec((K, tm), col),
        out_shape=jax.ShapeDtypeStruct((K, T), jnp.int32),
        compiler_params=_cparams("parallel"),
        name="dest_rows",
    )(idx, rank, start_f)


def _expert_kernel(first_ref, nblk_ref, nused_ref, xs_hbm, wg_ref, wu_ref, wd_ref, ys_hbm,
                   xbuf, ybuf, wgu_s, wd_s, in_sem, out_sem):
    ring = xbuf.shape[0]
    e = pl.program_id(0)
    De = wg_ref.shape[2]
    n = nblk_ref[e]
    g0 = first_ref[e]
    n_used = nused_ref[0]

    def rows(g):
        return pl.ds(pl.multiple_of(g * ROW_BLOCK, ROW_BLOCK), ROW_BLOCK)

    def in_copy(g, slot):
        return pltpu.make_async_copy(xs_hbm.at[rows(g), :], xbuf.at[slot], in_sem.at[slot])

    def out_copy(g, slot):
        return pltpu.make_async_copy(ybuf.at[slot], ys_hbm.at[rows(g), :], out_sem.at[slot])

    @pl.when(e == 0)
    def _():
        for g in range(ring - 2):
            @pl.when(g < n_used)
            def _():
                in_copy(g, g).start()

    def acquire(g, ahead):
        slot = g & (ring - 1)
        in_copy(g, slot).wait()

        @pl.when(g + ahead < n_used)
        def _():
            in_copy(g + ahead, (g + ahead) & (ring - 1)).start()

        @pl.when(g >= ring)
        def _():
            out_copy(g - ring, slot).wait()
        return slot

    def ffn(pieces):
        dh = xbuf.shape[2]
        gus = []
        for slot, rs in pieces:
            lo, hi = _unpack_bf16_halves(xbuf[slot, rs, :])
            gus.append(_dot(lo.astype(BF16), wgu_s[:dh, :]) + _dot(hi.astype(BF16), wgu_s[dh:, :]))
        hs = [(_silu(gu[:, :De]) * gu[:, De:]).astype(BF16) for gu in gus]
        for (slot, rs), h in zip(pieces, hs):
            ybuf[slot, rs, :] = _pack_bf16_halves(_dot(h, wd_s[...]))

    @pl.when(n > 0)
    def _():
        wgu_s[:, :De] = wg_ref[0].astype(BF16)
        wgu_s[:, De:] = wu_ref[0].astype(BF16)
        wd_s[...] = wd_ref[0].astype(BF16)

        def pair(j, c):
            g = g0 + 2 * j
            slots = [acquire(g + b, ring - 2) for b in range(2)]
            ffn([(slot, slice(None)) for slot in slots])
            for b, slot in enumerate(slots):
                out_copy(g + b, slot).start()
            return c

        lax.fori_loop(0, n // 2, pair, 0)

        @pl.when(n % 2 == 1)
        def _():
            g = g0 + n - 1
            slot = acquire(g, ring - 2)
            half = ROW_BLOCK // 2
            ffn([(slot, slice(0, half)), (slot, slice(half, ROW_BLOCK))])
            out_copy(g, slot).start()

    @pl.when(e == pl.num_programs(0) - 1)
    def _():
        for back in range(1, ring + 1):
            @pl.when(n_used >= back)
            def _():
                out_copy(n_used - back, (n_used - back) & (ring - 1)).wait()


def _experts(first_block, n_blocks, n_used, xs, w_gate, w_up, w_down):
    n_rows, Dh = xs.shape
    D = 2 * Dh
    E, _, De = w_gate.shape
    grid_spec = pltpu.PrefetchScalarGridSpec(
        num_scalar_prefetch=3,
        grid=(E,),
        in_specs=[pl.BlockSpec(memory_space=pl.ANY),
                  pl.BlockSpec((1, D, De), lambda e, *_: (e, 0, 0)),
                  pl.BlockSpec((1, D, De), lambda e, *_: (e, 0, 0)),
                  pl.BlockSpec((1, De, D), lambda e, *_: (e, 0, 0))],
        out_specs=pl.BlockSpec(memory_space=pl.ANY),
        scratch_shapes=[pltpu.VMEM((EXPERT_RING, ROW_BLOCK, Dh), jnp.uint32),
                        pltpu.VMEM((EXPERT_RING, ROW_BLOCK, Dh), jnp.uint32),
                        pltpu.VMEM((D, 2 * De), BF16), pltpu.VMEM((De, D), BF16),
                        pltpu.SemaphoreType.DMA((EXPERT_RING,)), pltpu.SemaphoreType.DMA((EXPERT_RING,))],
    )
    return pl.pallas_call(
        _expert_kernel,
        grid_spec=grid_spec,
        out_shape=jax.ShapeDtypeStruct((n_rows, Dh), jnp.uint32),
        compiler_params=_cparams("arbitrary"),
        name="expert_ffn",
    )(first_block, n_blocks, n_used, xs, w_gate, w_up, w_down)


SC_WINDOW = 128


def _sc_mesh():
    return plsc.VectorSubcoreMesh(core_axis_name="core", subcore_axis_name="subcore")


def _sc_workers():
    info = plsc.get_sparse_core_info()
    return info.num_cores, info.num_cores * info.num_subcores


def _sc_worker_id(num_cores):
    return lax.axis_index("subcore") * num_cores + lax.axis_index("core")


def _sc_scatter_rows(x, dest, n_rows):
    T, D = x.shape
    K = dest.shape[0]
    nc, nw = _sc_workers()
    per_w = T // nw
    assert T % nw == 0 and per_w % SC_WINDOW == 0

    @functools.partial(
        pl.kernel, out_type=jax.ShapeDtypeStruct((n_rows, D), x.dtype), mesh=_sc_mesh(),
        scratch_types=[pltpu.VMEM((K, SC_WINDOW), jnp.int32), pltpu.VMEM((SC_WINDOW, D), x.dtype),
                       pltpu.SemaphoreType.DMA((K,))],
        name="sc_dispatch")
    def run(x_hbm, i_hbm, o_hbm, idx_v, rows_v, sems):
        first = _sc_worker_id(nc) * per_w

        @pl.loop(0, per_w // SC_WINDOW)
        def _(c):
            base = first + c * SC_WINDOW
            pltpu.sync_copy(x_hbm.at[pl.ds(base, SC_WINDOW)], rows_v)
            pltpu.sync_copy(i_hbm.at[:, pl.ds(base, SC_WINDOW)], idx_v)
            copies = [pltpu.async_copy(rows_v, o_hbm.at[idx_v.at[k]], sems.at[k]) for k in range(K)]
            for cp in copies:
                cp.wait()

    return run(x, dest)


def _sc_gather_rows(ys, dest_flat):
    N = dest_flat.shape[0]
    D = ys.shape[1]
    nc, nw = _sc_workers()
    per_w = N // nw
    assert N % nw == 0 and per_w % SC_WINDOW == 0

    @functools.partial(
        pl.kernel, out_type=jax.ShapeDtypeStruct((N, D), ys.dtype), mesh=_sc_mesh(),
        scratch_types=[pltpu.VMEM((SC_WINDOW,), jnp.int32), pltpu.VMEM((SC_WINDOW, D), ys.dtype)],
        name="sc_combine_gather")
    def run(y_hbm, i_hbm, o_hbm, idx_v, rows_v):
        first = _sc_worker_id(nc) * per_w

        @pl.loop(0, per_w // SC_WINDOW)
        def _(c):
            base = first + c * SC_WINDOW
            pltpu.sync_copy(i_hbm.at[pl.ds(base, SC_WINDOW)], idx_v)
            pltpu.sync_copy(y_hbm.at[idx_v], rows_v)
            pltpu.sync_copy(rows_v, o_hbm.at[pl.ds(base, SC_WINDOW)])

    return run(ys, dest_flat)


def _final_kernel(z_ref, gate_ref, base_ref, g_ref, b_ref, o_ref):
    gate = gate_ref[...]
    acc_lo = acc_hi = None
    for k in range(TOP_K):
        lo, hi = _unpack_bf16_halves(z_ref[k])
        g = gate[:, k:k + 1]
        acc_lo = g * lo if acc_lo is None else acc_lo + g * lo
        acc_hi = g * hi if acc_hi is None else acc_hi + g * hi
    acc = base_ref[...] + jnp.concatenate([acc_lo, acc_hi], axis=-1)
    o_ref[...] = _layer_norm(acc, g_ref[...], b_ref[...])


def _final(z, gate, base, g2, b2, first_token):
    T, D = base.shape
    tb = TOK_BLOCK
    off = first_token // tb
    return pl.pallas_call(
        _final_kernel,
        grid=(z.shape[1] // tb,),
        in_specs=[pl.BlockSpec((TOP_K, tb, D // 2), lambda i: (0, i, 0)),
                  pl.BlockSpec((tb, TOP_K), lambda i: (i + off, 0)),
                  pl.BlockSpec((tb, D), lambda i: (i + off, 0)),
                  pl.BlockSpec((1, D), lambda i: (0, 0)),
                  pl.BlockSpec((1, D), lambda i: (0, 0))],
        out_specs=pl.BlockSpec((tb, D), lambda i: (i + off, 0)),
        out_shape=jax.ShapeDtypeStruct((T, D), F32),
        input_output_aliases={2: 0},
        compiler_params=_cparams("parallel"),
        name="final_sum_ln",
    )(z, gate, base, g2, b2)


def _layer(x, w_in, rel_bias, gain_a, gain_b, w_out, ln1_g, ln1_b, router_w, router_bias,
           w_gate, w_up, w_down, ws_gate, ws_up, ws_down, ln2_g, ln2_b, alpha):
    B, S, D = x.shape
    T = B * S
    d_a = gain_a.shape[0]
    d_b = gain_b.shape[0]
    assert S % TILE_B == 0 and T % 512 == 0 and d_a % LANES == 0 and d_b % WIDTH_B == 0 and (3 * d_a) % WIDTH_B == 0
    xt = x.reshape(T, D)

    scale = HEAD_DIM ** -0.5
    col = jnp.arange(w_in.shape[1])
    is_q = (col < d_a) | ((col >= 3 * d_a) & (col < 3 * d_a + d_b))
    w_in_b = (w_in * jnp.where(is_q, scale, 1.0)[None, :]).astype(BF16)

    qkv = _qkv_proj(xt, w_in_b).reshape(B, S, -1)
    ya = _attn_a(qkv, _rel_bias_by_offset(rel_bias), B, S, d_a).reshape(T, d_a)
    yb = _attn_b(qkv, B, S, d_a, d_b).reshape(T, d_b)

    x1p, base, idx, gate, rank, cnt = _post(
        ya, yb, xt, gain_a[None], gain_b[None], w_out.astype(BF16), ln1_g[None], ln1_b[None],
        router_w.T.astype(BF16), router_bias[:, None].astype(F32),
        ws_gate.astype(BF16), ws_up.astype(BF16), ws_down.astype(BF16), alpha)
    cnt = cnt[:, 0]
    padded = (cnt + ROW_BLOCK - 1) // ROW_BLOCK * ROW_BLOCK
    ends = jnp.cumsum(padded)
    start = (ends - padded).astype(jnp.int32)
    n_rows = T * TOP_K + N_EXPERTS * ROW_BLOCK
    first_block = start // ROW_BLOCK
    n_blocks = (padded // ROW_BLOCK).astype(jnp.int32)
    n_used = (ends[-1:] // ROW_BLOCK).astype(jnp.int32)
    dest = _dest(idx, rank, start.astype(F32)[:, None])

    xs = _sc_scatter_rows(x1p, dest, n_rows)
    ys = _experts(first_block, n_blocks, n_used, xs, w_gate, w_up, w_down)
    gate_t = gate.T
    out = base
    tg = T // COMBINE_GROUPS
    for q in range(COMBINE_GROUPS):
        z = _sc_gather_rows(ys, dest[:, q * tg:(q + 1) * tg].reshape(-1)).reshape(TOP_K, tg, D // 2)
        out = _final(z, gate_t, out, ln2_g[None], ln2_b[None], q * tg)
    return out.reshape(B, S, D)


def kernel(x, w_in, rel_bias, gain_a, gain_b, w_out, ln1_g, ln1_b, router_w, router_bias,
           w_gate, w_up, w_down, ws_gate, ws_up, ws_down, ln2_g, ln2_b):
    depth = w_in.shape[0]
    alpha = (2 * depth) ** 0.25
    for l in range(depth):
        x = _layer(x, w_in[l], rel_bias[l], gain_a[l], gain_b[l], w_out[l], ln1_g[l], ln1_b[l],
                   router_w[l], router_bias[l], w_gate[l], w_up[l], w_down[l],
                   ws_gate[l], ws_up[l], ws_down[l], ln2_g[l], ln2_b[l], alpha)
    return x
```

```python
import functools

import jax
import jax.numpy as jnp
from jax import lax
from jax.experimental import pallas as pl
from jax.experimental.pallas import tpu as pltpu
from jax.experimental.pallas import tpu_sc as plsc

CHUNK = 64
HEAD_DIM = 64
LEFT_CHUNKS = 8
MAX_REL = 128
N_EXPERTS = 256
TOP_K = 8
N_GROUPS = 8
TOPK_GROUPS = 4
EXPERTS_PER_GROUP = N_EXPERTS // N_GROUPS
ROUTED_SCALE = 2.5
LN_EPS = 1e-5
RMS_EPS = 1e-6

LANES = 128
HEADS_PER_BLOCK = LANES // HEAD_DIM
QBLK_A = 2 * CHUNK
BAND_A = (LEFT_CHUNKS + 2) * CHUNK
BIAS_W = BAND_A + QBLK_A
BLOCKS_PER_TRIP_A = 4
TILE_B = 256
WIDTH_B = 2 * HEAD_DIM
TILES_PER_TRIP_B = 3
ROW_BLOCK = 256
EXPERT_RING = 8
TOK_BLOCK = 256
COMBINE_SHARES = (2, 4, 5, 5)
MASK_VALUE = -1e30
V7X_VMEM_BYTES = 64 * 1024 * 1024
VMEM_LIMIT = V7X_VMEM_BYTES * 3 // 4

F32 = jnp.float32
BF16 = jnp.bfloat16


def _cparams(*sem):
    return pltpu.CompilerParams(dimension_semantics=sem, vmem_limit_bytes=VMEM_LIMIT)


def _dot(a, b):
    return jnp.dot(a, b, preferred_element_type=F32)


def _dot_nt(a, b):
    return lax.dot_general(a, b, (((1,), (1,)), ((), ())), preferred_element_type=F32)


def _pack_bf16_halves(x):
    half = x.shape[1] // 2
    bits = lax.bitcast_convert_type(x.astype(BF16).astype(F32), jnp.uint32)
    return (bits[:, :half] >> 16) | (bits[:, half:] & jnp.uint32(0xFFFF0000))


def _unpack_bf16_halves(w):
    lo = lax.bitcast_convert_type(w << 16, F32)
    hi = lax.bitcast_convert_type(w & jnp.uint32(0xFFFF0000), F32)
    return lo, hi


def _qkv_kernel(x_ref, w_ref, o_ref, *, col_chunk):
    xb = x_ref[...].astype(BF16)
    for n in range(w_ref.shape[1] // col_chunk):
        cols = slice(n * col_chunk, (n + 1) * col_chunk)
        o_ref[:, cols] = _dot(xb, w_ref[:, cols]).astype(BF16)


def _qkv_proj(xt, w_b):
    T, D = xt.shape
    N = w_b.shape[1]
    tm = 512
    return pl.pallas_call(
        functools.partial(_qkv_kernel, col_chunk=512),
        grid=(T // tm,),
        in_specs=[pl.BlockSpec((tm, D), lambda i: (i, 0)),
                  pl.BlockSpec((D, N), lambda i: (0, 0))],
        out_specs=pl.BlockSpec((tm, N), lambda i: (i, 0)),
        out_shape=jax.ShapeDtypeStruct((T, N), BF16),
        compiler_params=_cparams("parallel"),
        name="qkv_proj",
    )(xt, w_b)


def _attn_a_kernel(q_ref, k_ref, v_ref, w_ref, o_ref, bias_ref):
    S = q_ref.shape[0]
    nblk = S // QBLK_A
    lead = LEFT_CHUNKS * CHUNK
    n_edge = min(lead // QBLK_A, nblk)

    qc = lax.broadcasted_iota(jnp.int32, (QBLK_A, BAND_A), 0) // CHUNK
    kc = lax.broadcasted_iota(jnp.int32, (QBLK_A, BAND_A), 1) // CHUNK
    allowed = (kc >= qc) & (kc <= qc + LEFT_CHUNKS)
    for h in range(HEADS_PER_BLOCK):
        wb = jnp.broadcast_to(w_ref[h], (QBLK_A, BIAS_W))
        toeplitz = pltpu.roll(wb, BIAS_W - (QBLK_A - 1), 1, stride=1, stride_axis=0)
        bias_ref[h] = jnp.where(allowed, toeplitz[:, :BAND_A], MASK_VALUE)

    lane = lax.broadcasted_iota(jnp.int32, (QBLK_A, LANES), 1)
    head_of_lane = lane // HEAD_DIM

    def blocks(specs):
        work = []
        for p, kstart, nk, bias_off in specs:
            q = q_ref[pl.ds(p * QBLK_A, QBLK_A), :]
            k = k_ref[pl.ds(kstart, nk), :]
            v = v_ref[pl.ds(kstart, nk), :]
            for h in range(HEADS_PER_BLOCK):
                qh = jnp.where(head_of_lane == h, q, jnp.zeros_like(q))
                work.append(dict(s=_dot_nt(qh, k), v=v, bias=bias_ref[h, :, bias_off:bias_off + nk]))
        for w in work:
            s = w["s"] + w["bias"]
            e = jnp.exp(s - jnp.max(s, axis=-1, keepdims=True))
            w["l"] = jnp.sum(e, axis=-1, keepdims=True)
            w["e"] = e.astype(BF16)
        outs = [_dot(w["e"], w["v"]) / w["l"] for w in work]
        for i, spec in enumerate(specs):
            o = outs[i * HEADS_PER_BLOCK]
            for h in range(1, HEADS_PER_BLOCK):
                o = jnp.where(head_of_lane == h, outs[i * HEADS_PER_BLOCK + h], o)
            o_ref[pl.ds(spec[0] * QBLK_A, QBLK_A), :] = o

    blocks([(p, 0, (p + 1) * QBLK_A, lead - p * QBLK_A) for p in range(n_edge)])

    def full(p):
        return (p, pl.multiple_of(p * QBLK_A - lead, QBLK_A), BAND_A, 0)

    n_full = nblk - n_edge
    per_trip = BLOCKS_PER_TRIP_A
    rem = n_full % per_trip
    if rem:
        blocks([full(n_edge + r) for r in range(rem)])
    if n_full >= per_trip:
        def body(i, c):
            p = n_edge + rem + per_trip * i
            blocks([full(p + r) for r in range(per_trip)])
            return c
        lax.fori_loop(0, n_full // per_trip, body, 0)


def _attn_a(qkv, bias_w, B, S, d_a):
    n_hb = d_a // LANES
    seg = d_a // LANES
    return pl.pallas_call(
        _attn_a_kernel,
        grid=(B, n_hb),
        in_specs=[pl.BlockSpec((None, S, LANES), lambda b, h: (b, 0, h)),
                  pl.BlockSpec((None, S, LANES), lambda b, h: (b, 0, seg + h)),
                  pl.BlockSpec((None, S, LANES), lambda b, h: (b, 0, 2 * seg + h)),
                  pl.BlockSpec((HEADS_PER_BLOCK, 1, BIAS_W), lambda b, h: (h, 0, 0))],
        out_specs=pl.BlockSpec((None, S, LANES), lambda b, h: (b, 0, h)),
        out_shape=jax.ShapeDtypeStruct((B, S, d_a), F32),
        scratch_shapes=[pltpu.VMEM((HEADS_PER_BLOCK, QBLK_A, BAND_A), F32)],
        compiler_params=_cparams("parallel", "parallel"),
        name="attn_chunked",
    )(qkv, qkv, qkv, bias_w)


def _rel_bias_by_offset(rel_bias):
    dist = jnp.clip(BAND_A - 1 - jnp.arange(BIAS_W), -MAX_REL, MAX_REL) + MAX_REL
    return rel_bias[:, dist].astype(F32)[:, None, :]


def _attn_b_kernel(q_ref, k_ref, v_ref, o_ref):
    t = TILE_B
    row = lax.broadcasted_iota(jnp.int32, (t, t), 0)
    col = lax.broadcasted_iota(jnp.int32, (t, t), 1)
    from_s = (row >= col).astype(BF16)
    from_s2 = jnp.concatenate([from_s, from_s], axis=0)
    causal = col < row

    def q_tile(qi, c):
        q = q_ref[pl.ds(pl.multiple_of(qi * t, t), t), :]
        n_heads = q_ref.shape[1] // HEAD_DIM
        head_of_lane = lax.broadcasted_iota(jnp.int32, q.shape, 1) // HEAD_DIM
        qs = [jnp.where(head_of_lane == h, q, jnp.zeros_like(q)) for h in range(n_heads)]

        def scores(js):
            zs = []
            for j in js:
                kt = k_ref[pl.ds(pl.multiple_of(j * t, t), t), :]
                zs += [_dot_nt(qh, kt) for qh in qs]
            return zs

        def tiles(js, zs, state, first_is_diag=False):
            accs, carries = list(state[0::2]), list(state[1::2])
            work = []
            for n, j in enumerate(js):
                vt = v_ref[pl.ds(pl.multiple_of(j * t, t), t), :]
                for h in range(n_heads):
                    work.append(dict(h=h, diag=first_is_diag and n == 0, v=vt, z=zs[n * n_heads + h]))
            for w in work:
                z = w["z"]
                sp = jnp.maximum(z, 0.0) + jnp.log(1.0 + jnp.exp(-jnp.abs(z)))
                if w["diag"]:
                    sp = jnp.where(causal, sp, 0.0)
                hi = sp.astype(BF16)
                lo = (sp - hi.astype(F32)).astype(BF16)
                w["suffix"] = _dot(jnp.concatenate([hi, lo], axis=-1), from_s2)
                w["rowsum"] = jnp.sum(sp, axis=-1, keepdims=True)
            for w in work:
                h = w["h"]
                if w["diag"]:
                    a = jnp.where(causal, jnp.exp(w["z"] - w["suffix"]), 0.0)
                    accs[h] = _dot(a.astype(BF16), w["v"])
                    carries[h] = w["rowsum"]
                else:
                    a = jnp.exp(w["z"] - w["suffix"] - carries[h])
                    accs[h] = accs[h] + _dot(a.astype(BF16), w["v"])
                    carries[h] = carries[h] + w["rowsum"]
            return tuple(x for pair in zip(accs, carries) for x in pair)

        per_trip = TILES_PER_TRIP_B
        empty = (None,) * (2 * n_heads)

        def opening(extra):
            js = [qi - r for r in range(extra + 1)]
            return lambda: tiles(js, scores(js), empty, True)

        state = lax.switch(qi % per_trip, [opening(r) for r in range(per_trip)])
        first = qi - 1 - qi % per_trip

        def body(it, st):
            js = [first - per_trip * it - r for r in range(per_trip)]
            return tiles(js, scores(js), st)

        state = lax.fori_loop(0, qi // per_trip, body, state)
        out = state[0]
        for h in range(1, n_heads):
            out = jnp.where(head_of_lane == h, state[2 * h], out)
        o_ref[pl.ds(pl.multiple_of(qi * t, t), t), :] = out
        return c

    lax.fori_loop(0, q_ref.shape[0] // t, q_tile, 0)


def _attn_b(qkv, B, S, d_a, d_b):
    w = WIDTH_B
    n_hb = d_b // w
    base = 3 * d_a // w
    seg = d_b // w
    return pl.pallas_call(
        _attn_b_kernel,
        grid=(B, n_hb),
        in_specs=[pl.BlockSpec((None, S, w), lambda b, h: (b, 0, base + h)),
                  pl.BlockSpec((None, S, w), lambda b, h: (b, 0, base + seg + h)),
                  pl.BlockSpec((None, S, w), lambda b, h: (b, 0, base + 2 * seg + h))],
        out_specs=pl.BlockSpec((None, S, w), lambda b, h: (b, 0, h)),
        out_shape=jax.ShapeDtypeStruct((B, S, d_b), F32),
        compiler_params=_cparams("parallel", "parallel"),
        name="attn_stickbreak",
    )(qkv, qkv, qkv)


def _layer_norm(r, g, b):
    mu = jnp.mean(r, axis=-1, keepdims=True)
    c = r - mu
    var = jnp.mean(c * c, axis=-1, keepdims=True)
    return c * lax.rsqrt(var + LN_EPS) * g + b


def _rms_norm(y, g):
    ms = jnp.mean(y * y, axis=-1, keepdims=True)
    return y * lax.rsqrt(ms + RMS_EPS) * g


def _silu(g):
    return g * jax.nn.sigmoid(g)


def _post_kernel(ya_ref, yb_ref, x_ref, ga_ref, gb_ref, wo_ref, g1_ref, b1_ref, rw_ref, rbias_ref,
                 wsg_ref, wsu_ref, wsd_ref, x1_ref, base_ref, idx_ref, gate_ref, rank_ref, cnt_ref, carry_ref,
                 *, alpha):
    @pl.when(pl.program_id(0) == 0)
    def _():
        carry_ref[...] = jnp.zeros_like(carry_ref)

    d_a = ya_ref.shape[1]
    na = _rms_norm(ya_ref[...], ga_ref[...]).astype(BF16)
    nb = _rms_norm(yb_ref[...], gb_ref[...]).astype(BF16)
    h = _dot(na, wo_ref[:d_a, :]) + _dot(nb, wo_ref[d_a:, :])
    x1 = _layer_norm(alpha * x_ref[...] + h, g1_ref[...], b1_ref[...])
    x1_ref[...] = _pack_bf16_halves(x1)
    xb = x1.astype(BF16)
    logits = _dot_nt(rw_ref[...], xb)
    hs = _silu(_dot(xb, wsg_ref[...])) * _dot(xb, wsu_ref[...])
    base_ref[...] = alpha * x1 + _dot(hs.astype(BF16), wsd_ref[...])

    tb = TOK_BLOCK
    row = lax.broadcasted_iota(jnp.int32, (tb, tb), 0)
    col = lax.broadcasted_iota(jnp.int32, (tb, tb), 1)
    earlier = (row < col).astype(BF16)
    carry = carry_ref[...]
    for n in range(logits.shape[1] // tb):
        cols = slice(n * tb, (n + 1) * tb)
        idx, gate, rank, carry = _route_block(logits[:, cols], rbias_ref[...], carry, earlier)
        idx_ref[:, cols] = idx
        gate_ref[:, cols] = gate
        rank_ref[:, cols] = rank
    carry_ref[...] = carry
    cnt_ref[...] = carry.astype(jnp.int32)


def _post(ya, yb, xt, gain_a, gain_b, wo_b, g1, b1, rw_b, rbias, wsg_b, wsu_b, wsd_b, alpha):
    T, D = xt.shape
    d_a, d_b = ya.shape[1], yb.shape[1]
    E = rw_b.shape[0]
    De = wsg_b.shape[1]
    tm = 4 * TOK_BLOCK
    row = lambda i: (i, 0)
    col = lambda i: (0, i)
    fix = lambda i: (0, 0)
    return pl.pallas_call(
        functools.partial(_post_kernel, alpha=alpha),
        grid=(T // tm,),
        in_specs=[pl.BlockSpec((tm, d_a), row), pl.BlockSpec((tm, d_b), row), pl.BlockSpec((tm, D), row),
                  pl.BlockSpec((1, d_a), fix), pl.BlockSpec((1, d_b), fix),
                  pl.BlockSpec((d_a + d_b, D), fix), pl.BlockSpec((1, D), fix), pl.BlockSpec((1, D), fix),
                  pl.BlockSpec((E, D), fix), pl.BlockSpec((E, 1), fix),
                  pl.BlockSpec((D, De), fix), pl.BlockSpec((D, De), fix), pl.BlockSpec((De, D), fix)],
        out_specs=[pl.BlockSpec((tm, D // 2), row), pl.BlockSpec((tm, D), row),
                   pl.BlockSpec((TOP_K, tm), col), pl.BlockSpec((TOP_K, tm), col), pl.BlockSpec((TOP_K, tm), col),
                   pl.BlockSpec((E, 1), fix)],
        out_shape=[jax.ShapeDtypeStruct((T, D // 2), jnp.uint32), jax.ShapeDtypeStruct((T, D), F32),
                   jax.ShapeDtypeStruct((TOP_K, T), jnp.int32), jax.ShapeDtypeStruct((TOP_K, T), F32),
                   jax.ShapeDtypeStruct((TOP_K, T), jnp.int32), jax.ShapeDtypeStruct((E, 1), jnp.int32)],
        scratch_shapes=[pltpu.VMEM((E, 1), F32)],
        compiler_params=_cparams("arbitrary"),
        name="post_attn_route",
    )(ya, yb, xt, gain_a, gain_b, wo_b, g1, b1, rw_b, rbias, wsg_b, wsu_b, wsd_b)


def _route_block(logits, rbias, carry, earlier):
    E, tm = logits.shape
    neg = -jnp.inf
    scores = jax.nn.sigmoid(logits)
    sel = scores + rbias
    eidx = lax.broadcasted_iota(jnp.int32, (E, tm), 0).astype(F32)

    def first_argmax(v, ids):
        m = jnp.max(v, axis=0, keepdims=True)
        return m, jnp.min(jnp.where(v == m, ids, float(E)), axis=0, keepdims=True)

    grp_scores = []
    ids = lax.broadcasted_iota(jnp.int32, (EXPERTS_PER_GROUP, tm), 0).astype(F32)
    for g in range(N_GROUPS):
        v = sel[g * EXPERTS_PER_GROUP:(g + 1) * EXPERTS_PER_GROUP, :]
        m1, i1 = first_argmax(v, ids)
        m2 = jnp.max(jnp.where(ids == i1, neg, v), axis=0, keepdims=True)
        grp_scores.append(m1 + m2)
    parts = []
    for g in range(N_GROUPS):
        beaten = jnp.zeros((1, tm), jnp.int32)
        for o in range(N_GROUPS):
            if o != g:
                s, t = grp_scores[o], grp_scores[g]
                beaten = beaten + ((s > t) | ((s == t) & (o < g))).astype(jnp.int32)
        rows = slice(g * EXPERTS_PER_GROUP, (g + 1) * EXPERTS_PER_GROUP)
        parts.append(jnp.where(beaten < TOPK_GROUPS, sel[rows, :], neg))
    selm = jnp.concatenate(parts, axis=0)

    hits, gates, ids_k = [], [], []
    chosen = jnp.zeros((E, tm), F32)
    gate_sum = jnp.zeros((1, tm), F32)
    for k in range(TOP_K):
        _, ik = first_argmax(selm, eidx)
        hit = eidx == ik
        gk = jnp.sum(jnp.where(hit, scores, 0.0), axis=0, keepdims=True)
        selm = jnp.where(hit, neg, selm)
        chosen = jnp.where(hit, 1.0, chosen)
        gate_sum = gate_sum + gk
        hits.append(hit)
        gates.append(gk)
        ids_k.append(ik)

    before = _dot(chosen.astype(BF16), earlier) + carry
    ranks = [jnp.sum(jnp.where(hit, before, 0.0), axis=0, keepdims=True) for hit in hits]
    return (jnp.concatenate(ids_k, axis=0).astype(jnp.int32),
            jnp.concatenate(gates, axis=0) / gate_sum * ROUTED_SCALE,
            jnp.concatenate(ranks, axis=0).astype(jnp.int32),
            carry + jnp.sum(chosen, axis=1, keepdims=True))


def _dest_kernel(idx_ref, rank_ref, start_ref, dest_ref):
    K, tm = idx_ref.shape
    E = start_ref.shape[0]
    eidx = lax.broadcasted_iota(jnp.int32, (E, tm), 0)
    idx = idx_ref[...]
    start = start_ref[...]
    rows = [jnp.sum(jnp.where(eidx == idx[k:k + 1, :], start, 0.0), axis=0, keepdims=True) for k in range(K)]
    dest_ref[...] = jnp.concatenate(rows, axis=0).astype(jnp.int32) + rank_ref[...]


def _dest(idx, rank, start_f):
    K, T = idx.shape
    E = start_f.shape[0]
    tm = 2 * TOK_BLOCK
    col = lambda i: (0, i)
    return pl.pallas_call(
        _dest_kernel,
        grid=(T // tm,),
        in_specs=[pl.BlockSpec((K, tm), col), pl.BlockSpec((K, tm), col), pl.BlockSpec((E, 1), lambda i: (0, 0))],
        out_specs=pl.BlockSpec((K, tm), col),
        out_shape=jax.ShapeDtypeStruct((K, T), jnp.int32),
        compiler_params=_cparams("parallel"),
        name="dest_rows",
    )(idx, rank, start_f)


def _expert_kernel(first_ref, nblk_ref, nused_ref, xs_hbm, wg_ref, wu_ref, wd_ref, ys_hbm,
                   xbuf, ybuf, wgu_s, wd_s, in_sem, out_sem):
    ring = xbuf.shape[0]
    e = pl.program_id(0)
    De = wg_ref.shape[2]
    n = nblk_ref[e]
    g0 = first_ref[e]
    n_used = nused_ref[0]

    def rows(g):
        return pl.ds(pl.multiple_of(g * ROW_BLOCK, ROW_BLOCK), ROW_BLOCK)

    def in_copy(g, slot):
        return pltpu.make_async_copy(xs_hbm.at[rows(g), :], xbuf.at[slot], in_sem.at[slot])

    def out_copy(g, slot):
        return pltpu.make_async_copy(ybuf.at[slot], ys_hbm.at[rows(g), :], out_sem.at[slot])

    @pl.when(e == 0)
    def _():
        for g in range(ring - 2):
            @pl.when(g < n_used)
            def _():
                in_copy(g, g).start()

    def acquire(g, ahead):
        slot = g & (ring - 1)
        in_copy(g, slot).wait()

        @pl.when(g + ahead < n_used)
        def _():
            in_copy(g + ahead, (g + ahead) & (ring - 1)).start()

        @pl.when(g >= ring)
        def _():
            out_copy(g - ring, slot).wait()
        return slot

    def ffn(pieces):
        dh = xbuf.shape[2]
        gus = []
        for slot, rs in pieces:
            lo, hi = _unpack_bf16_halves(xbuf[slot, rs, :])
            gus.append(_dot(lo.astype(BF16), wgu_s[:dh, :]) + _dot(hi.astype(BF16), wgu_s[dh:, :]))
        hs = [(_silu(gu[:, :De]) * gu[:, De:]).astype(BF16) for gu in gus]
        for (slot, rs), h in zip(pieces, hs):
            ybuf[slot, rs, :] = _pack_bf16_halves(_dot(h, wd_s[...]))

    @pl.when(n > 0)
    def _():
        wgu_s[:, :De] = wg_ref[0].astype(BF16)
        wgu_s[:, De:] = wu_ref[0].astype(BF16)
        wd_s[...] = wd_ref[0].astype(BF16)

        def pair(j, c):
            g = g0 + 2 * j
            slots = [acquire(g + b, ring - 2) for b in range(2)]
            ffn([(slot, slice(None)) for slot in slots])
            for b, slot in enumerate(slots):
                out_copy(g + b, slot).start()
            return c

        lax.fori_loop(0, n // 2, pair, 0)

        @pl.when(n % 2 == 1)
        def _():
            g = g0 + n - 1
            slot = acquire(g, ring - 2)
            half = ROW_BLOCK // 2
            ffn([(slot, slice(0, half)), (slot, slice(half, ROW_BLOCK))])
            out_copy(g, slot).start()

    @pl.when(e == pl.num_programs(0) - 1)
    def _():
        for back in range(1, ring + 1):
            @pl.when(n_used >= back)
            def _():
                out_copy(n_used - back, (n_used - back) & (ring - 1)).wait()


def _experts(first_block, n_blocks, n_used, xs, w_gate, w_up, w_down):
    n_rows, Dh = xs.shape
    D = 2 * Dh
    E, _, De = w_gate.shape
    grid_spec = pltpu.PrefetchScalarGridSpec(
        num_scalar_prefetch=3,
        grid=(E,),
        in_specs=[pl.BlockSpec(memory_space=pl.ANY),
                  pl.BlockSpec((1, D, De), lambda e, *_: (e, 0, 0)),
                  pl.BlockSpec((1, D, De), lambda e, *_: (e, 0, 0)),
                  pl.BlockSpec((1, De, D), lambda e, *_: (e, 0, 0))],
        out_specs=pl.BlockSpec(memory_space=pl.ANY),
        scratch_shapes=[pltpu.VMEM((EXPERT_RING, ROW_BLOCK, Dh), jnp.uint32),
                        pltpu.VMEM((EXPERT_RING, ROW_BLOCK, Dh), jnp.uint32),
                        pltpu.VMEM((D, 2 * De), BF16), pltpu.VMEM((De, D), BF16),
                        pltpu.SemaphoreType.DMA((EXPERT_RING,)), pltpu.SemaphoreType.DMA((EXPERT_RING,))],
    )
    return pl.pallas_call(
        _expert_kernel,
        grid_spec=grid_spec,
        out_shape=jax.ShapeDtypeStruct((n_rows, Dh), jnp.uint32),
        compiler_params=_cparams("arbitrary"),
        name="expert_ffn",
    )(first_block, n_blocks, n_used, xs, w_gate, w_up, w_down)


SC_WINDOW = 128


def _sc_mesh():
    return plsc.VectorSubcoreMesh(core_axis_name="core", subcore_axis_name="subcore")


def _sc_workers():
    info = plsc.get_sparse_core_info()
    return info.num_cores, info.num_cores * info.num_subcores


def _sc_worker_id(num_cores):
    return lax.axis_index("subcore") * num_cores + lax.axis_index("core")


def _sc_scatter_rows(x, dest, n_rows):
    T, D = x.shape
    K = dest.shape[0]
    nc, nw = _sc_workers()
    per_w = T // nw
    assert T % nw == 0 and per_w % SC_WINDOW == 0

    @functools.partial(
        pl.kernel, out_type=jax.ShapeDtypeStruct((n_rows, D), x.dtype), mesh=_sc_mesh(),
        scratch_types=[pltpu.VMEM((K, SC_WINDOW), jnp.int32), pltpu.VMEM((SC_WINDOW, D), x.dtype),
                       pltpu.SemaphoreType.DMA((K,))],
        name="sc_dispatch")
    def run(x_hbm, i_hbm, o_hbm, idx_v, rows_v, sems):
        first = _sc_worker_id(nc) * per_w

        @pl.loop(0, per_w // SC_WINDOW)
        def _(c):
            base = first + c * SC_WINDOW
            pltpu.sync_copy(x_hbm.at[pl.ds(base, SC_WINDOW)], rows_v)
            pltpu.sync_copy(i_hbm.at[:, pl.ds(base, SC_WINDOW)], idx_v)
            copies = [pltpu.async_copy(rows_v, o_hbm.at[idx_v.at[k]], sems.at[k]) for k in range(K)]
            for cp in copies:
                cp.wait()

    return run(x, dest)


def _sc_gather_rows(ys, dest_flat):
    N = dest_flat.shape[0]
    D = ys.shape[1]
    nc, nw = _sc_workers()
    per_w = N // nw
    assert N % nw == 0 and per_w % SC_WINDOW == 0

    @functools.partial(
        pl.kernel, out_type=jax.ShapeDtypeStruct((N, D), ys.dtype), mesh=_sc_mesh(),
        scratch_types=[pltpu.VMEM((SC_WINDOW,), jnp.int32), pltpu.VMEM((SC_WINDOW, D), ys.dtype)],
        name="sc_combine_gather")
    def run(y_hbm, i_hbm, o_hbm, idx_v, rows_v):
        first = _sc_worker_id(nc) * per_w

        @pl.loop(0, per_w // SC_WINDOW)
        def _(c):
            base = first + c * SC_WINDOW
            pltpu.sync_copy(i_hbm.at[pl.ds(base, SC_WINDOW)], idx_v)
            pltpu.sync_copy(y_hbm.at[idx_v], rows_v)
            pltpu.sync_copy(rows_v, o_hbm.at[pl.ds(base, SC_WINDOW)])

    return run(ys, dest_flat)


def _final_kernel(z_ref, gate_ref, base_ref, g_ref, b_ref, o_ref):
    gate = gate_ref[...]
    acc_lo = acc_hi = None
    for k in range(TOP_K):
        lo, hi = _unpack_bf16_halves(z_ref[k])
        g = gate[:, k:k + 1]
        acc_lo = g * lo if acc_lo is None else acc_lo + g * lo
        acc_hi = g * hi if acc_hi is None else acc_hi + g * hi
    acc = base_ref[...] + jnp.concatenate([acc_lo, acc_hi], axis=-1)
    o_ref[...] = _layer_norm(acc, g_ref[...], b_ref[...])


def _final(z, gate, base, g2, b2, first_token):
    T, D = base.shape
    tb = TOK_BLOCK
    off = first_token // tb
    return pl.pallas_call(
        _final_kernel,
        grid=(z.shape[1] // tb,),
        in_specs=[pl.BlockSpec((TOP_K, tb, D // 2), lambda i: (0, i, 0)),
                  pl.BlockSpec((tb, TOP_K), lambda i: (i + off, 0)),
                  pl.BlockSpec((tb, D), lambda i: (i + off, 0)),
                  pl.BlockSpec((1, D), lambda i: (0, 0)),
                  pl.BlockSpec((1, D), lambda i: (0, 0))],
        out_specs=pl.BlockSpec((tb, D), lambda i: (i + off, 0)),
        out_shape=jax.ShapeDtypeStruct((T, D), F32),
        input_output_aliases={2: 0},
        compiler_params=_cparams("parallel"),
        name="final_sum_ln",
    )(z, gate, base, g2, b2)


def _layer(x, w_in, rel_bias, gain_a, gain_b, w_out, ln1_g, ln1_b, router_w, router_bias,
           w_gate, w_up, w_down, ws_gate, ws_up, ws_down, ln2_g, ln2_b, alpha):
    B, S, D = x.shape
    T = B * S
    d_a = gain_a.shape[0]
    d_b = gain_b.shape[0]
    assert S % TILE_B == 0 and T % 512 == 0 and d_a % LANES == 0 and d_b % WIDTH_B == 0 and (3 * d_a) % WIDTH_B == 0
    xt = x.reshape(T, D)

    scale = HEAD_DIM ** -0.5
    col = jnp.arange(w_in.shape[1])
    is_q = (col < d_a) | ((col >= 3 * d_a) & (col < 3 * d_a + d_b))
    w_in_b = (w_in * jnp.where(is_q, scale, 1.0)[None, :]).astype(BF16)

    qkv = _qkv_proj(xt, w_in_b).reshape(B, S, -1)
    ya = _attn_a(qkv, _rel_bias_by_offset(rel_bias), B, S, d_a).reshape(T, d_a)
    yb = _attn_b(qkv, B, S, d_a, d_b).reshape(T, d_b)

    x1p, base, idx, gate, rank, cnt = _post(
        ya, yb, xt, gain_a[None], gain_b[None], w_out.astype(BF16), ln1_g[None], ln1_b[None],
        router_w.T.astype(BF16), router_bias[:, None].astype(F32),
        ws_gate.astype(BF16), ws_up.astype(BF16), ws_down.astype(BF16), alpha)
    cnt = cnt[:, 0]
    padded = (cnt + ROW_BLOCK - 1) // ROW_BLOCK * ROW_BLOCK
    ends = jnp.cumsum(padded)
    start = (ends - padded).astype(jnp.int32)
    n_rows = T * TOP_K + N_EXPERTS * ROW_BLOCK
    first_block = start // ROW_BLOCK
    n_blocks = (padded // ROW_BLOCK).astype(jnp.int32)
    n_used = (ends[-1:] // ROW_BLOCK).astype(jnp.int32)
    dest = _dest(idx, rank, start.astype(F32)[:, None])

    xs = _sc_scatter_rows(x1p, dest, n_rows)
    ys = _experts(first_block, n_blocks, n_used, xs, w_gate, w_up, w_down)
    gate_t = gate.T
    out = base
    t0 = 0
    for share in COMBINE_SHARES:
        tg = T * share // sum(COMBINE_SHARES)
        assert tg % TOK_BLOCK == 0 and tg * sum(COMBINE_SHARES) == T * share
        z = _sc_gather_rows(ys, dest[:, t0:t0 + tg].reshape(-1)).reshape(TOP_K, tg, D // 2)
        out = _final(z, gate_t, out, ln2_g[None], ln2_b[None], t0)
        t0 += tg
    return out.reshape(B, S, D)


def kernel(x, w_in, rel_bias, gain_a, gain_b, w_out, ln1_g, ln1_b, router_w, router_bias,
           w_gate, w_up, w_down, ws_gate, ws_up, ws_down, ln2_g, ln2_b):
    depth = w_in.shape[0]
    alpha = (2 * depth) ** 0.25
    for l in range(depth):
        x = _layer(x, w_in[l], rel_bias[l], gain_a[l], gain_b[l], w_out[l], ln1_g[l], ln1_b[l],
                   router_w[l], router_bias[l], w_gate[l], w_up[l], w_down[l],
                   ws_gate[l], ws_up[l], ws_down[l], ln2_g[l], ln2_b[l], alpha)
    return x
```

```python
import functools

import jax
import jax.numpy as jnp
from jax import lax
from jax.experimental import pallas as pl
from jax.experimental.pallas import tpu as pltpu
from jax.experimental.pallas import tpu_sc as plsc

CHUNK = 64
HEAD_DIM = 64
LEFT_CHUNKS = 8
MAX_REL = 128
N_EXPERTS = 256
TOP_K = 8
N_GROUPS = 8
TOPK_GROUPS = 4
EXPERTS_PER_GROUP = N_EXPERTS // N_GROUPS
ROUTED_SCALE = 2.5
LN_EPS = 1e-5
RMS_EPS = 1e-6

LANES = 128
HEADS_PER_BLOCK = LANES // HEAD_DIM
QBLK_A = 2 * CHUNK
BAND_A = (LEFT_CHUNKS + 2) * CHUNK
BIAS_W = BAND_A + QBLK_A
BLOCKS_PER_TRIP_A = 4
TILE_B = 256
WIDTH_B = 2 * HEAD_DIM
TILES_PER_TRIP_B = 3
ROW_BLOCK = 256
EXPERT_RING = 8
TOK_BLOCK = 256
COMBINE_GROUPS = 4
MASK_VALUE = -1e30
V7X_VMEM_BYTES = 64 * 1024 * 1024
VMEM_LIMIT = V7X_VMEM_BYTES * 3 // 4

F32 = jnp.float32
BF16 = jnp.bfloat16


def _cparams(*sem):
    return pltpu.CompilerParams(dimension_semantics=sem, vmem_limit_bytes=VMEM_LIMIT)


def _dot(a, b):
    return jnp.dot(a, b, preferred_element_type=F32)


def _dot_nt(a, b):
    return lax.dot_general(a, b, (((1,), (1,)), ((), ())), preferred_element_type=F32)


def _pack_bf16_halves(x):
    half = x.shape[1] // 2
    bits = lax.bitcast_convert_type(x.astype(BF16).astype(F32), jnp.uint32)
    return (bits[:, :half] >> 16) | (bits[:, half:] & jnp.uint32(0xFFFF0000))


def _unpack_bf16_halves(w):
    lo = lax.bitcast_convert_type(w << 16, F32)
    hi = lax.bitcast_convert_type(w & jnp.uint32(0xFFFF0000), F32)
    return lo, hi


def _qkv_kernel(x_ref, w_ref, o_ref, *, col_chunk):
    xb = x_ref[...].astype(BF16)
    for n in range(w_ref.shape[1] // col_chunk):
        cols = slice(n * col_chunk, (n + 1) * col_chunk)
        o_ref[:, cols] = _dot(xb, w_ref[:, cols]).astype(BF16)


def _qkv_proj(xt, w_b):
    T, D = xt.shape
    N = w_b.shape[1]
    tm = 512
    return pl.pallas_call(
        functools.partial(_qkv_kernel, col_chunk=512),
        grid=(T // tm,),
        in_specs=[pl.BlockSpec((tm, D), lambda i: (i, 0)),
                  pl.BlockSpec((D, N), lambda i: (0, 0))],
        out_specs=pl.BlockSpec((tm, N), lambda i: (i, 0)),
        out_shape=jax.ShapeDtypeStruct((T, N), BF16),
        compiler_params=_cparams("parallel"),
        name="qkv_proj",
    )(xt, w_b)


def _attn_a_kernel(q_ref, k_ref, v_ref, w_ref, o_ref, bias_ref):
    S = q_ref.shape[0]
    nblk = S // QBLK_A
    lead = LEFT_CHUNKS * CHUNK
    n_edge = min(lead // QBLK_A, nblk)

    qc = lax.broadcasted_iota(jnp.int32, (QBLK_A, BAND_A), 0) // CHUNK
    kc = lax.broadcasted_iota(jnp.int32, (QBLK_A, BAND_A), 1) // CHUNK
    allowed = (kc >= qc) & (kc <= qc + LEFT_CHUNKS)
    for h in range(HEADS_PER_BLOCK):
        wb = jnp.broadcast_to(w_ref[h], (QBLK_A, BIAS_W))
        toeplitz = pltpu.roll(wb, BIAS_W - (QBLK_A - 1), 1, stride=1, stride_axis=0)
        bias_ref[h] = jnp.where(allowed, toeplitz[:, :BAND_A], MASK_VALUE)

    lane = lax.broadcasted_iota(jnp.int32, (QBLK_A, LANES), 1)
    head_of_lane = lane // HEAD_DIM

    def blocks(specs):
        work = []
        for p, kstart, nk, bias_off in specs:
            q = q_ref[pl.ds(p * QBLK_A, QBLK_A), :]
            k = k_ref[pl.ds(kstart, nk), :]
            v = v_ref[pl.ds(kstart, nk), :]
            for h in range(HEADS_PER_BLOCK):
                qh = jnp.where(head_of_lane == h, q, jnp.zeros_like(q))
                work.append(dict(s=_dot_nt(qh, k), v=v, bias=bias_ref[h, :, bias_off:bias_off + nk]))
        for w in work:
            s = w["s"] + w["bias"]
            e = jnp.exp(s - jnp.max(s, axis=-1, keepdims=True))
            w["l"] = jnp.sum(e, axis=-1, keepdims=True)
            w["e"] = e.astype(BF16)
        outs = [_dot(w["e"], w["v"]) / w["l"] for w in work]
        for i, spec in enumerate(specs):
            o = outs[i * HEADS_PER_BLOCK]
            for h in range(1, HEADS_PER_BLOCK):
                o = jnp.where(head_of_lane == h, outs[i * HEADS_PER_BLOCK + h], o)
            o_ref[pl.ds(spec[0] * QBLK_A, QBLK_A), :] = o

    blocks([(p, 0, (p + 1) * QBLK_A, lead - p * QBLK_A) for p in range(n_edge)])

    def full(p):
        return (p, pl.multiple_of(p * QBLK_A - lead, QBLK_A), BAND_A, 0)

    n_full = nblk - n_edge
    per_trip = BLOCKS_PER_TRIP_A
    rem = n_full % per_trip
    if rem:
        blocks([full(n_edge + r) for r in range(rem)])
    if n_full >= per_trip:
        def body(i, c):
            p = n_edge + rem + per_trip * i
            blocks([full(p + r) for r in range(per_trip)])
            return c
        lax.fori_loop(0, n_full // per_trip, body, 0)


def _attn_a(qkv, bias_w, B, S, d_a):
    n_hb = d_a // LANES
    seg = d_a // LANES
    return pl.pallas_call(
        _attn_a_kernel,
        grid=(B, n_hb),
        in_specs=[pl.BlockSpec((None, S, LANES), lambda b, h: (b, 0, h)),
                  pl.BlockSpec((None, S, LANES), lambda b, h: (b, 0, seg + h)),
                  pl.BlockSpec((None, S, LANES), lambda b, h: (b, 0, 2 * seg + h)),
                  pl.BlockSpec((HEADS_PER_BLOCK, 1, BIAS_W), lambda b, h: (h, 0, 0))],
        out_specs=pl.BlockSpec((None, S, LANES), lambda b, h: (b, 0, h)),
        out_shape=jax.ShapeDtypeStruct((B, S, d_a), F32),
        scratch_shapes=[pltpu.VMEM((HEADS_PER_BLOCK, QBLK_A, BAND_A), F32)],
        compiler_params=_cparams("parallel", "parallel"),
        name="attn_chunked",
    )(qkv, qkv, qkv, bias_w)


def _rel_bias_by_offset(rel_bias):
    dist = jnp.clip(BAND_A - 1 - jnp.arange(BIAS_W), -MAX_REL, MAX_REL) + MAX_REL
    return rel_bias[:, dist].astype(F32)[:, None, :]


def _attn_b_kernel(q_ref, k_ref, v_ref, o_ref):
    t = TILE_B
    row = lax.broadcasted_iota(jnp.int32, (t, t), 0)
    col = lax.broadcasted_iota(jnp.int32, (t, t), 1)
    from_s = (row >= col).astype(BF16)
    from_s2 = jnp.concatenate([from_s, from_s], axis=0)
    causal = col < row

    def q_tile(qi, c):
        q = q_ref[pl.ds(pl.multiple_of(qi * t, t), t), :]
        n_heads = q_ref.shape[1] // HEAD_DIM
        head_of_lane = lax.broadcasted_iota(jnp.int32, q.shape, 1) // HEAD_DIM
        qs = [jnp.where(head_of_lane == h, q, jnp.zeros_like(q)) for h in range(n_heads)]

        def scores(js):
            zs = []
            for j in js:
                kt = k_ref[pl.ds(pl.multiple_of(j * t, t), t), :]
                zs += [_dot_nt(qh, kt) for qh in qs]
            return zs

        def tiles(js, zs, state, first_is_diag=False):
            accs, carries = list(state[0::2]), list(state[1::2])
            work = []
            for n, j in enumerate(js):
                vt = v_ref[pl.ds(pl.multiple_of(j * t, t), t), :]
                for h in range(n_heads):
                    work.append(dict(h=h, diag=first_is_diag and n == 0, v=vt, z=zs[n * n_heads + h]))
            for w in work:
                z = w["z"]
                sp = jnp.maximum(z, 0.0) + jnp.log(1.0 + jnp.exp(-jnp.abs(z)))
                if w["diag"]:
                    sp = jnp.where(causal, sp, 0.0)
                hi = sp.astype(BF16)
                lo = (sp - hi.astype(F32)).astype(BF16)
                w["suffix"] = _dot(jnp.concatenate([hi, lo], axis=-1), from_s2)
                w["rowsum"] = jnp.sum(sp, axis=-1, keepdims=True)
            for w in work:
                h = w["h"]
                if w["diag"]:
                    a = jnp.where(causal, jnp.exp(w["z"] - w["suffix"]), 0.0)
                    accs[h] = _dot(a.astype(BF16), w["v"])
                    carries[h] = w["rowsum"]
                else:
                    a = jnp.exp(w["z"] - w["suffix"] - carries[h])
                    accs[h] = accs[h] + _dot(a.astype(BF16), w["v"])
                    carries[h] = carries[h] + w["rowsum"]
            return tuple(x for pair in zip(accs, carries) for x in pair)

        per_trip = TILES_PER_TRIP_B
        empty = (None,) * (2 * n_heads)

        def opening(extra):
            js = [qi - r for r in range(extra + 1)]
            return lambda: tiles(js, scores(js), empty, True)

        state = lax.switch(qi % per_trip, [opening(r) for r in range(per_trip)])
        first = qi - 1 - qi % per_trip

        def body(it, st):
            js = [first - per_trip * it - r for r in range(per_trip)]
            return tiles(js, scores(js), st)

        state = lax.fori_loop(0, qi // per_trip, body, state)
        out = state[0]
        for h in range(1, n_heads):
            out = jnp.where(head_of_lane == h, state[2 * h], out)
        o_ref[pl.ds(pl.multiple_of(qi * t, t), t), :] = out
        return c

    lax.fori_loop(0, q_ref.shape[0] // t, q_tile, 0)


def _attn_b(qkv, B, S, d_a, d_b):
    w = WIDTH_B
    n_hb = d_b // w
    base = 3 * d_a // w
    seg = d_b // w
    return pl.pallas_call(
        _attn_b_kernel,
        grid=(B, n_hb),
        in_specs=[pl.BlockSpec((None, S, w), lambda b, h: (b, 0, base + h)),
                  pl.BlockSpec((None, S, w), lambda b, h: (b, 0, base + seg + h)),
                  pl.BlockSpec((None, S, w), lambda b, h: (b, 0, base + 2 * seg + h))],
        out_specs=pl.BlockSpec((None, S, w), lambda b, h: (b, 0, h)),
        out_shape=jax.ShapeDtypeStruct((B, S, d_b), F32),
        compiler_params=_cparams("parallel", "parallel"),
        name="attn_stickbreak",
    )(qkv, qkv, qkv)


def _layer_norm(r, g, b):
    mu = jnp.mean(r, axis=-1, keepdims=True)
    c = r - mu
    var = jnp.mean(c * c, axis=-1, keepdims=True)
    return c * lax.rsqrt(var + LN_EPS) * g + b


def _rms_norm(y, g):
    ms = jnp.mean(y * y, axis=-1, keepdims=True)
    return y * lax.rsqrt(ms + RMS_EPS) * g


def _silu(g):
    return g * jax.nn.sigmoid(g)


def _post_kernel(ya_ref, yb_ref, x_ref, ga_ref, gb_ref, wo_ref, g1_ref, b1_ref, rw_ref, rbias_ref,
                 wsg_ref, wsu_ref, wsd_ref, x1_ref, base_ref, idx_ref, gate_ref, rank_ref, cnt_ref, carry_ref,
                 *, alpha):
    @pl.when(pl.program_id(0) == 0)
    def _():
        carry_ref[...] = jnp.zeros_like(carry_ref)

    d_a = ya_ref.shape[1]
    na = _rms_norm(ya_ref[...], ga_ref[...]).astype(BF16)
    nb = _rms_norm(yb_ref[...], gb_ref[...]).astype(BF16)
    h = _dot(na, wo_ref[:d_a, :]) + _dot(nb, wo_ref[d_a:, :])
    x1 = _layer_norm(alpha * x_ref[...] + h, g1_ref[...], b1_ref[...])
    x1_ref[...] = _pack_bf16_halves(x1)
    xb = x1.astype(BF16)
    logits = _dot_nt(rw_ref[...], xb)
    hs = _silu(_dot(xb, wsg_ref[...])) * _dot(xb, wsu_ref[...])
    base_ref[...] = alpha * x1 + _dot(hs.astype(BF16), wsd_ref[...])

    tb = TOK_BLOCK
    row = lax.broadcasted_iota(jnp.int32, (tb, tb), 0)
    col = lax.broadcasted_iota(jnp.int32, (tb, tb), 1)
    earlier = (row < col).astype(BF16)
    carry = carry_ref[...]
    for n in range(logits.shape[1] // tb):
        cols = slice(n * tb, (n + 1) * tb)
        idx, gate, rank, carry = _route_block(logits[:, cols], rbias_ref[...], carry, earlier)
        idx_ref[:, cols] = idx
        gate_ref[:, cols] = gate
        rank_ref[:, cols] = rank
    carry_ref[...] = carry
    cnt_ref[...] = carry.astype(jnp.int32)


def _post(ya, yb, xt, gain_a, gain_b, wo_b, g1, b1, rw_b, rbias, wsg_b, wsu_b, wsd_b, alpha):
    T, D = xt.shape
    d_a, d_b = ya.shape[1], yb.shape[1]
    E = rw_b.shape[0]
    De = wsg_b.shape[1]
    tm = 4 * TOK_BLOCK
    row = lambda i: (i, 0)
    col = lambda i: (0, i)
    fix = lambda i: (0, 0)
    return pl.pallas_call(
        functools.partial(_post_kernel, alpha=alpha),
        grid=(T // tm,),
        in_specs=[pl.BlockSpec((tm, d_a), row), pl.BlockSpec((tm, d_b), row), pl.BlockSpec((tm, D), row),
                  pl.BlockSpec((1, d_a), fix), pl.BlockSpec((1, d_b), fix),
                  pl.BlockSpec((d_a + d_b, D), fix), pl.BlockSpec((1, D), fix), pl.BlockSpec((1, D), fix),
                  pl.BlockSpec((E, D), fix), pl.BlockSpec((E, 1), fix),
                  pl.BlockSpec((D, De), fix), pl.BlockSpec((D, De), fix), pl.BlockSpec((De, D), fix)],
        out_specs=[pl.BlockSpec((tm, D // 2), row), pl.BlockSpec((tm, D), row),
                   pl.BlockSpec((TOP_K, tm), col), pl.BlockSpec((TOP_K, tm), col), pl.BlockSpec((TOP_K, tm), col),
                   pl.BlockSpec((E, 1), fix)],
        out_shape=[jax.ShapeDtypeStruct((T, D // 2), jnp.uint32), jax.ShapeDtypeStruct((T, D), F32),
                   jax.ShapeDtypeStruct((TOP_K, T), jnp.int32), jax.ShapeDtypeStruct((TOP_K, T), F32),
                   jax.ShapeDtypeStruct((TOP_K, T), jnp.int32), jax.ShapeDtypeStruct((E, 1), jnp.int32)],
        scratch_shapes=[pltpu.VMEM((E, 1), F32)],
        compiler_params=_cparams("arbitrary"),
        name="post_attn_route",
    )(ya, yb, xt, gain_a, gain_b, wo_b, g1, b1, rw_b, rbias, wsg_b, wsu_b, wsd_b)


def _route_block(logits, rbias, carry, earlier):
    E, tm = logits.shape
    neg = -jnp.inf
    scores = jax.nn.sigmoid(logits)
    sel = scores + rbias
    eidx = lax.broadcasted_iota(jnp.int32, (E, tm), 0).astype(F32)

    def first_argmax(v, ids):
        m = jnp.max(v, axis=0, keepdims=True)
        return m, jnp.min(jnp.where(v == m, ids, float(E)), axis=0, keepdims=True)

    grp_scores = []
    ids = lax.broadcasted_iota(jnp.int32, (EXPERTS_PER_GROUP, tm), 0).astype(F32)
    for g in range(N_GROUPS):
        v = sel[g * EXPERTS_PER_GROUP:(g + 1) * EXPERTS_PER_GROUP, :]
        m1, i1 = first_argmax(v, ids)
        m2 = jnp.max(jnp.where(ids == i1, neg, v), axis=0, keepdims=True)
        grp_scores.append(m1 + m2)
    parts = []
    for g in range(N_GROUPS):
        beaten = jnp.zeros((1, tm), jnp.int32)
        for o in range(N_GROUPS):
            if o != g:
                s, t = grp_scores[o], grp_scores[g]
                beaten = beaten + ((s > t) | ((s == t) & (o < g))).astype(jnp.int32)
        rows = slice(g * EXPERTS_PER_GROUP, (g + 1) * EXPERTS_PER_GROUP)
        parts.append(jnp.where(beaten < TOPK_GROUPS, sel[rows, :], neg))
    selm = jnp.concatenate(parts, axis=0)

    hits, gates, ids_k = [], [], []
    chosen = jnp.zeros((E, tm), F32)
    gate_sum = jnp.zeros((1, tm), F32)
    for k in range(TOP_K):
        _, ik = first_argmax(selm, eidx)
        hit = eidx == ik
        gk = jnp.sum(jnp.where(hit, scores, 0.0), axis=0, keepdims=True)
        selm = jnp.where(hit, neg, selm)
        chosen = jnp.where(hit, 1.0, chosen)
        gate_sum = gate_sum + gk
        hits.append(hit)
        gates.append(gk)
        ids_k.append(ik)

    before = _dot(chosen.astype(BF16), earlier) + carry
    ranks = [jnp.sum(jnp.where(hit, before, 0.0), axis=0, keepdims=True) for hit in hits]
    return (jnp.concatenate(ids_k, axis=0).astype(jnp.int32),
            jnp.concatenate(gates, axis=0) / gate_sum * ROUTED_SCALE,
            jnp.concatenate(ranks, axis=0).astype(jnp.int32),
            carry + jnp.sum(chosen, axis=1, keepdims=True))


def _dest_kernel(idx_ref, rank_ref, start_ref, dest_ref):
    K, tm = idx_ref.shape
    E = start_ref.shape[0]
    eidx = lax.broadcasted_iota(jnp.int32, (E, tm), 0)
    idx = idx_ref[...]
    start = start_ref[...]
    rows = [jnp.sum(jnp.where(eidx == idx[k:k + 1, :], start, 0.0), axis=0, keepdims=True) for k in range(K)]
    dest_ref[...] = jnp.concatenate(rows, axis=0).astype(jnp.int32) + rank_ref[...]


def _dest(idx, rank, start_f):
    K, T = idx.shape
    E = start_f.shape[0]
    tm = 2 * TOK_BLOCK
    col = lambda i: (0, i)
    return pl.pallas_call(
        _dest_kernel,
        grid=(T // tm,),
        in_specs=[pl.BlockSpec((K, tm), col), pl.BlockSpec((K, tm), col), pl.BlockSpec((E, 1), lambda i: (0, 0))],
        out_specs=pl.BlockSpec((K, tm), col),
        out_shape=jax.ShapeDtypeStruct((K, T), jnp.int32),
        compiler_params=_cparams("parallel"),
        name="dest_rows",
    )(idx, rank, start_f)


def _expert_kernel(first_ref, nblk_ref, nused_ref, xs_hbm, wg_ref, wu_ref, wd_ref, ys_hbm,
                   xbuf, ybuf, wgu_s, wd_s, in_sem, out_sem):
    ring = xbuf.shape[0]
    e = pl.program_id(0)
    De = wg_ref.shape[2]
    n = nblk_ref[e]
    g0 = first_ref[e]
    n_used = nused_ref[0]

    def rows(g):
        return pl.ds(pl.multiple_of(g * ROW_BLOCK, ROW_BLOCK), ROW_BLOCK)

    def in_copy(g, slot):
        return pltpu.make_async_copy(xs_hbm.at[rows(g), :], xbuf.at[slot], in_sem.at[slot])

    def out_copy(g, slot):
        return pltpu.make_async_copy(ybuf.at[slot], ys_hbm.at[rows(g), :], out_sem.at[slot])

    @pl.when(e == 0)
    def _():
        for g in range(ring - 2):
            @pl.when(g < n_used)
            def _():
                in_copy(g, g).start()

    def acquire(g, ahead):
        slot = g & (ring - 1)
        in_copy(g, slot).wait()

        @pl.when(g + ahead < n_used)
        def _():
            in_copy(g + ahead, (g + ahead) & (ring - 1)).start()

        @pl.when(g >= ring)
        def _():
            out_copy(g - ring, slot).wait()
        return slot

    def ffn(pieces):
        dh = xbuf.shape[2]
        gus = []
        for slot, rs in pieces:
            lo, hi = _unpack_bf16_halves(xbuf[slot, rs, :])
            gus.append(_dot(lo.astype(BF16), wgu_s[:dh, :]) + _dot(hi.astype(BF16), wgu_s[dh:, :]))
        hs = [(_silu(gu[:, :De]) * gu[:, De:]).astype(BF16) for gu in gus]
        for (slot, rs), h in zip(pieces, hs):
            ybuf[slot, rs, :] = _pack_bf16_halves(_dot(h, wd_s[...]))

    @pl.when(n > 0)
    def _():
        wgu_s[:, :De] = wg_ref[0].astype(BF16)
        wgu_s[:, De:] = wu_ref[0].astype(BF16)
        wd_s[...] = wd_ref[0].astype(BF16)

        def pair(j, c):
            g = g0 + 2 * j
            slots = [acquire(g + b, ring - 2) for b in range(2)]
            ffn([(slot, slice(None)) for slot in slots])
            for b, slot in enumerate(slots):
                out_copy(g + b, slot).start()
            return c

        lax.fori_loop(0, n // 2, pair, 0)

        @pl.when(n % 2 == 1)
        def _():
            g = g0 + n - 1
            slot = acquire(g, ring - 2)
            half = ROW_BLOCK // 2
            ffn([(slot, slice(0, half)), (slot, slice(half, ROW_BLOCK))])
            out_copy(g, slot).start()

    @pl.when(e == pl.num_programs(0) - 1)
    def _():
        for back in range(1, ring + 1):
            @pl.when(n_used >= back)
            def _():
                out_copy(n_used - back, (n_used - back) & (ring - 1)).wait()


def _experts(first_block, n_blocks, n_used, xs, w_gate, w_up, w_down):
    n_rows, Dh = xs.shape
    D = 2 * Dh
    E, _, De = w_gate.shape
    grid_spec = pltpu.PrefetchScalarGridSpec(
        num_scalar_prefetch=3,
        grid=(E,),
        in_specs=[pl.BlockSpec(memory_space=pl.ANY),
                  pl.BlockSpec((1, D, De), lambda e, *_: (e, 0, 0)),
                  pl.BlockSpec((1, D, De), lambda e, *_: (e, 0, 0)),
                  pl.BlockSpec((1, De, D), lambda e, *_: (e, 0, 0))],
        out_specs=pl.BlockSpec(memory_space=pl.ANY),
        scratch_shapes=[pltpu.VMEM((EXPERT_RING, ROW_BLOCK, Dh), jnp.uint32),
                        pltpu.VMEM((EXPERT_RING, ROW_BLOCK, Dh), jnp.uint32),
                        pltpu.VMEM((D, 2 * De), BF16), pltpu.VMEM((De, D), BF16),
                        pltpu.SemaphoreType.DMA((EXPERT_RING,)), pltpu.SemaphoreType.DMA((EXPERT_RING,))],
    )
    return pl.pallas_call(
        _expert_kernel,
        grid_spec=grid_spec,
        out_shape=jax.ShapeDtypeStruct((n_rows, Dh), jnp.uint32),
        compiler_params=_cparams("arbitrary"),
        name="expert_ffn",
    )(first_block, n_blocks, n_used, xs, w_gate, w_up, w_down)


SC_WINDOW = 128


def _sc_mesh():
    return plsc.VectorSubcoreMesh(core_axis_name="core", subcore_axis_name="subcore")


def _sc_workers():
    info = plsc.get_sparse_core_info()
    return info.num_cores, info.num_cores * info.num_subcores


def _sc_worker_id(num_cores):
    return lax.axis_index("subcore") * num_cores + lax.axis_index("core")


def _sc_scatter_rows(x, dest, n_rows):
    T, D = x.shape
    K = dest.shape[0]
    nc, nw = _sc_workers()
    per_w = T // nw
    assert T % nw == 0 and per_w % SC_WINDOW == 0

    @functools.partial(
        pl.kernel, out_type=jax.ShapeDtypeStruct((n_rows, D), x.dtype), mesh=_sc_mesh(),
        scratch_types=[pltpu.VMEM((K, SC_WINDOW), jnp.int32), pltpu.VMEM((SC_WINDOW, D), x.dtype),
                       pltpu.SemaphoreType.DMA((K,))],
        name="sc_dispatch")
    def run(x_hbm, i_hbm, o_hbm, idx_v, rows_v, sems):
        first = _sc_worker_id(nc) * per_w

        @pl.loop(0, per_w // SC_WINDOW)
        def _(c):
            base = first + c * SC_WINDOW
            pltpu.sync_copy(x_hbm.at[pl.ds(base, SC_WINDOW)], rows_v)
            pltpu.sync_copy(i_hbm.at[:, pl.ds(base, SC_WINDOW)], idx_v)
            copies = [pltpu.async_copy(rows_v, o_hbm.at[idx_v.at[k]], sems.at[k]) for k in range(K)]
            for cp in copies:
                cp.wait()

    return run(x, dest)


def _sc_gather_rows(ys, dest_flat):
    N = dest_flat.shape[0]
    D = ys.shape[1]
    nc, nw = _sc_workers()
    per_w = N // nw
    assert N % nw == 0 and per_w % SC_WINDOW == 0

    half = SC_WINDOW // 2

    @functools.partial(
        pl.kernel, out_type=jax.ShapeDtypeStruct((N, D), ys.dtype), mesh=_sc_mesh(),
        scratch_types=[pltpu.VMEM((2, half), jnp.int32), pltpu.VMEM((SC_WINDOW, D), ys.dtype),
                       pltpu.SemaphoreType.DMA((4,))],
        name="sc_combine_gather")
    def run(y_hbm, i_hbm, o_hbm, idx_v, rows_v, sems):
        first = _sc_worker_id(nc) * per_w

        @pl.loop(0, per_w // SC_WINDOW)
        def _(c):
            base = first + c * SC_WINDOW
            for j in range(2):
                pltpu.sync_copy(i_hbm.at[pl.ds(base + j * half, half)], idx_v.at[j])
            parts = [pl.ds(j * half, half) for j in range(2)]
            gathers = [pltpu.async_copy(y_hbm.at[idx_v.at[j]], rows_v.at[parts[j]], sems.at[j]) for j in range(2)]
            writes = []
            for j in range(2):
                gathers[j].wait()
                writes.append(pltpu.async_copy(rows_v.at[parts[j]], o_hbm.at[pl.ds(base + j * half, half)],
                                               sems.at[2 + j]))
            for w in writes:
                w.wait()

    return run(ys, dest_flat)


def _final_kernel(z_ref, gate_ref, base_ref, g_ref, b_ref, o_ref):
    gate = gate_ref[...]
    acc_lo = acc_hi = None
    for k in range(TOP_K):
        lo, hi = _unpack_bf16_halves(z_ref[k])
        g = gate[:, k:k + 1]
        acc_lo = g * lo if acc_lo is None else acc_lo + g * lo
        acc_hi = g * hi if acc_hi is None else acc_hi + g * hi
    acc = base_ref[...] + jnp.concatenate([acc_lo, acc_hi], axis=-1)
    o_ref[...] = _layer_norm(acc, g_ref[...], b_ref[...])


def _final(z, gate, base, g2, b2, first_token):
    T, D = base.shape
    tb = TOK_BLOCK
    off = first_token // tb
    return pl.pallas_call(
        _final_kernel,
        grid=(z.shape[1] // tb,),
        in_specs=[pl.BlockSpec((TOP_K, tb, D // 2), lambda i: (0, i, 0)),
                  pl.BlockSpec((tb, TOP_K), lambda i: (i + off, 0)),
                  pl.BlockSpec((tb, D), lambda i: (i + off, 0)),
                  pl.BlockSpec((1, D), lambda i: (0, 0)),
                  pl.BlockSpec((1, D), lambda i: (0, 0))],
        out_specs=pl.BlockSpec((tb, D), lambda i: (i + off, 0)),
        out_shape=jax.ShapeDtypeStruct((T, D), F32),
        input_output_aliases={2: 0},
        compiler_params=_cparams("parallel"),
        name="final_sum_ln",
    )(z, gate, base, g2, b2)


def _layer(x, w_in, rel_bias, gain_a, gain_b, w_out, ln1_g, ln1_b, router_w, router_bias,
           w_gate, w_up, w_down, ws_gate, ws_up, ws_down, ln2_g, ln2_b, alpha):
    B, S, D = x.shape
    T = B * S
    d_a = gain_a.shape[0]
    d_b = gain_b.shape[0]
    assert S % TILE_B == 0 and T % 512 == 0 and d_a % LANES == 0 and d_b % WIDTH_B == 0 and (3 * d_a) % WIDTH_B == 0
    xt = x.reshape(T, D)

    scale = HEAD_DIM ** -0.5
    col = jnp.arange(w_in.shape[1])
    is_q = (col < d_a) | ((col >= 3 * d_a) & (col < 3 * d_a + d_b))
    w_in_b = (w_in * jnp.where(is_q, scale, 1.0)[None, :]).astype(BF16)

    qkv = _qkv_proj(xt, w_in_b).reshape(B, S, -1)
    ya = _attn_a(qkv, _rel_bias_by_offset(rel_bias), B, S, d_a).reshape(T, d_a)
    yb = _attn_b(qkv, B, S, d_a, d_b).reshape(T, d_b)

    x1p, base, idx, gate, rank, cnt = _post(
        ya, yb, xt, gain_a[None], gain_b[None], w_out.astype(BF16), ln1_g[None], ln1_b[None],
        router_w.T.astype(BF16), router_bias[:, None].astype(F32),
        ws_gate.astype(BF16), ws_up.astype(BF16), ws_down.astype(BF16), alpha)
    cnt = cnt[:, 0]
    padded = (cnt + ROW_BLOCK - 1) // ROW_BLOCK * ROW_BLOCK
    ends = jnp.cumsum(padded)
    start = (ends - padded).astype(jnp.int32)
    n_rows = T * TOP_K + N_EXPERTS * ROW_BLOCK
    first_block = start // ROW_BLOCK
    n_blocks = (padded // ROW_BLOCK).astype(jnp.int32)
    n_used = (ends[-1:] // ROW_BLOCK).astype(jnp.int32)
    dest = _dest(idx, rank, start.astype(F32)[:, None])

    xs = _sc_scatter_rows(x1p, dest, n_rows)
    ys = _experts(first_block, n_blocks, n_used, xs, w_gate, w_up, w_down)
    gate_t = gate.T
    out = base
    tg = T // COMBINE_GROUPS
    for q in range(COMBINE_GROUPS):
        z = _sc_gather_rows(ys, dest[:, q * tg:(q + 1) * tg].reshape(-1)).reshape(TOP_K, tg, D // 2)
        out = _final(z, gate_t, out, ln2_g[None], ln2_b[None], q * tg)
    return out.reshape(B, S, D)


def kernel(x, w_in, rel_bias, gain_a, gain_b, w_out, ln1_g, ln1_b, router_w, router_bias,
           w_gate, w_up, w_down, ws_gate, ws_up, ws_down, ln2_g, ln2_b):
    depth = w_in.shape[0]
    alpha = (2 * depth) ** 0.25
    for l in range(depth):
        x = _layer(x, w_in[l], rel_bias[l], gain_a[l], gain_b[l], w_out[l], ln1_g[l], ln1_b[l],
                   router_w[l], router_bias[l], w_gate[l], w_up[l], w_down[l],
                   ws_gate[l], ws_up[l], ws_down[l], ln2_g[l], ln2_b[l], alpha)
    return x
```

```python
import functools

import jax
import jax.numpy as jnp
from jax import lax
from jax.experimental import pallas as pl
from jax.experimental.pallas import tpu as pltpu
from jax.experimental.pallas import tpu_sc as plsc

CHUNK = 64
HEAD_DIM = 64
LEFT_CHUNKS = 8
MAX_REL = 128
N_EXPERTS = 256
TOP_K = 8
N_GROUPS = 8
TOPK_GROUPS = 4
EXPERTS_PER_GROUP = N_EXPERTS // N_GROUPS
ROUTED_SCALE = 2.5
LN_EPS = 1e-5
RMS_EPS = 1e-6

LANES = 128
HEADS_PER_BLOCK = LANES // HEAD_DIM
QBLK_A = 2 * CHUNK
BAND_A = (LEFT_CHUNKS + 2) * CHUNK
BIAS_W = BAND_A + QBLK_A
BLOCKS_PER_TRIP_A = 4
TILE_B = 256
WIDTH_B = 2 * HEAD_DIM
TILES_PER_TRIP_B = 3
STAGE_ROWS = 64
ROW_BLOCK = 256
EXPERT_RING = 8
TOK_BLOCK = 256
COMBINE_GROUPS = 4
MASK_VALUE = -1e30
V7X_VMEM_BYTES = 64 * 1024 * 1024
VMEM_LIMIT = V7X_VMEM_BYTES * 3 // 4

F32 = jnp.float32
BF16 = jnp.bfloat16


def _cparams(*sem):
    return pltpu.CompilerParams(dimension_semantics=sem, vmem_limit_bytes=VMEM_LIMIT)


def _dot(a, b):
    return jnp.dot(a, b, preferred_element_type=F32)


def _dot_nt(a, b):
    return lax.dot_general(a, b, (((1,), (1,)), ((), ())), preferred_element_type=F32)


def _pack_bf16_halves(x):
    half = x.shape[1] // 2
    bits = lax.bitcast_convert_type(x.astype(BF16).astype(F32), jnp.uint32)
    return (bits[:, :half] >> 16) | (bits[:, half:] & jnp.uint32(0xFFFF0000))


def _unpack_bf16_halves(w):
    lo = lax.bitcast_convert_type(w << 16, F32)
    hi = lax.bitcast_convert_type(w & jnp.uint32(0xFFFF0000), F32)
    return lo, hi


def _qkv_kernel(x_ref, w_ref, o_ref, *, col_chunk):
    xb = x_ref[...].astype(BF16)
    for n in range(w_ref.shape[1] // col_chunk):
        cols = slice(n * col_chunk, (n + 1) * col_chunk)
        o_ref[:, cols] = _dot(xb, w_ref[:, cols]).astype(BF16)


def _qkv_proj(xt, w_b):
    T, D = xt.shape
    N = w_b.shape[1]
    tm = 512
    return pl.pallas_call(
        functools.partial(_qkv_kernel, col_chunk=512),
        grid=(T // tm,),
        in_specs=[pl.BlockSpec((tm, D), lambda i: (i, 0)),
                  pl.BlockSpec((D, N), lambda i: (0, 0))],
        out_specs=pl.BlockSpec((tm, N), lambda i: (i, 0)),
        out_shape=jax.ShapeDtypeStruct((T, N), BF16),
        compiler_params=_cparams("parallel"),
        name="qkv_proj",
    )(xt, w_b)


def _attn_a_kernel(q_ref, k_ref, v_ref, w_ref, o_ref, bias_ref):
    S = q_ref.shape[0]
    nblk = S // QBLK_A
    lead = LEFT_CHUNKS * CHUNK
    n_edge = min(lead // QBLK_A, nblk)

    qc = lax.broadcasted_iota(jnp.int32, (QBLK_A, BAND_A), 0) // CHUNK
    kc = lax.broadcasted_iota(jnp.int32, (QBLK_A, BAND_A), 1) // CHUNK
    allowed = (kc >= qc) & (kc <= qc + LEFT_CHUNKS)
    for h in range(HEADS_PER_BLOCK):
        wb = jnp.broadcast_to(w_ref[h], (QBLK_A, BIAS_W))
        toeplitz = pltpu.roll(wb, BIAS_W - (QBLK_A - 1), 1, stride=1, stride_axis=0)
        bias_ref[h] = jnp.where(allowed, toeplitz[:, :BAND_A], MASK_VALUE)

    lane = lax.broadcasted_iota(jnp.int32, (QBLK_A, LANES), 1)
    head_of_lane = lane // HEAD_DIM

    def blocks(specs):
        work = []
        for p, kstart, nk, bias_off in specs:
            q = q_ref[pl.ds(p * QBLK_A, QBLK_A), :]
            k = k_ref[pl.ds(kstart, nk), :]
            v = v_ref[pl.ds(kstart, nk), :]
            for h in range(HEADS_PER_BLOCK):
                qh = jnp.where(head_of_lane == h, q, jnp.zeros_like(q))
                work.append(dict(s=_dot_nt(qh, k), v=v, bias=bias_ref[h, :, bias_off:bias_off + nk]))
        for w in work:
            s = w["s"] + w["bias"]
            e = jnp.exp(s - jnp.max(s, axis=-1, keepdims=True))
            w["l"] = jnp.sum(e, axis=-1, keepdims=True)
            w["e"] = e.astype(BF16)
        outs = [_dot(w["e"], w["v"]) / w["l"] for w in work]
        for i, spec in enumerate(specs):
            o = outs[i * HEADS_PER_BLOCK]
            for h in range(1, HEADS_PER_BLOCK):
                o = jnp.where(head_of_lane == h, outs[i * HEADS_PER_BLOCK + h], o)
            o_ref[pl.ds(spec[0] * QBLK_A, QBLK_A), :] = o

    blocks([(p, 0, (p + 1) * QBLK_A, lead - p * QBLK_A) for p in range(n_edge)])

    def full(p):
        return (p, pl.multiple_of(p * QBLK_A - lead, QBLK_A), BAND_A, 0)

    n_full = nblk - n_edge
    per_trip = BLOCKS_PER_TRIP_A
    rem = n_full % per_trip
    if rem:
        blocks([full(n_edge + r) for r in range(rem)])
    if n_full >= per_trip:
        def body(i, c):
            p = n_edge + rem + per_trip * i
            blocks([full(p + r) for r in range(per_trip)])
            return c
        lax.fori_loop(0, n_full // per_trip, body, 0)


def _attn_a(qkv, bias_w, B, S, d_a):
    n_hb = d_a // LANES
    seg = d_a // LANES
    return pl.pallas_call(
        _attn_a_kernel,
        grid=(B, n_hb),
        in_specs=[pl.BlockSpec((None, S, LANES), lambda b, h: (b, 0, h)),
                  pl.BlockSpec((None, S, LANES), lambda b, h: (b, 0, seg + h)),
                  pl.BlockSpec((None, S, LANES), lambda b, h: (b, 0, 2 * seg + h)),
                  pl.BlockSpec((HEADS_PER_BLOCK, 1, BIAS_W), lambda b, h: (h, 0, 0))],
        out_specs=pl.BlockSpec((None, S, LANES), lambda b, h: (b, 0, h)),
        out_shape=jax.ShapeDtypeStruct((B, S, d_a), F32),
        scratch_shapes=[pltpu.VMEM((HEADS_PER_BLOCK, QBLK_A, BAND_A), F32)],
        compiler_params=_cparams("parallel", "parallel"),
        name="attn_chunked",
    )(qkv, qkv, qkv, bias_w)


def _rel_bias_by_offset(rel_bias):
    dist = jnp.clip(BAND_A - 1 - jnp.arange(BIAS_W), -MAX_REL, MAX_REL) + MAX_REL
    return rel_bias[:, dist].astype(F32)[:, None, :]


def _attn_b_kernel(q_ref, k_ref, v_ref, o_ref, z_buf, lhs_buf, suf_buf, a_buf):
    t = TILE_B
    row = lax.broadcasted_iota(jnp.int32, (t, t), 0)
    col = lax.broadcasted_iota(jnp.int32, (t, t), 1)
    from_s = (row >= col).astype(BF16)
    from_s2 = jnp.concatenate([from_s, from_s], axis=0)
    causal = col < row

    def q_tile(qi, c):
        q = q_ref[pl.ds(pl.multiple_of(qi * t, t), t), :]
        n_heads = q_ref.shape[1] // HEAD_DIM
        head_of_lane = lax.broadcasted_iota(jnp.int32, q.shape, 1) // HEAD_DIM
        qs = [jnp.where(head_of_lane == h, q, jnp.zeros_like(q)) for h in range(n_heads)]

        def tiles(js, state, first_is_diag=False):
            accs, carries = list(state[0::2]), list(state[1::2])
            items = [(n, j, h) for n, j in enumerate(js) for h in range(n_heads)]
            for i, (n, j, h) in enumerate(items):
                z_buf[i] = _dot_nt(qs[h], k_ref[pl.ds(pl.multiple_of(j * t, t), t), :])
            rowsums = []
            for i, (n, j, h) in enumerate(items):
                diag = first_is_diag and n == 0
                parts = []
                for r in range(0, t, STAGE_ROWS):
                    rows = slice(r, r + STAGE_ROWS)
                    z = z_buf[i, rows, :]
                    sp = jnp.maximum(z, 0.0) + jnp.log(1.0 + jnp.exp(-jnp.abs(z)))
                    if diag:
                        sp = jnp.where(causal[rows, :], sp, 0.0)
                    hi = sp.astype(BF16)
                    lhs_buf[i, rows, :t] = hi
                    lhs_buf[i, rows, t:] = (sp - hi.astype(F32)).astype(BF16)
                    parts.append(jnp.sum(sp, axis=-1, keepdims=True))
                rowsums.append(jnp.concatenate(parts, axis=0))
                suf_buf[i] = _dot(lhs_buf[i], from_s2)
            for i, (n, j, h) in enumerate(items):
                diag = first_is_diag and n == 0
                for r in range(0, t, STAGE_ROWS):
                    rows = slice(r, r + STAGE_ROWS)
                    arg = z_buf[i, rows, :] - suf_buf[i, rows, :]
                    if diag:
                        a = jnp.where(causal[rows, :], jnp.exp(arg), 0.0)
                    else:
                        a = jnp.exp(arg - carries[h][rows, :])
                    a_buf[i, rows, :] = a.astype(BF16)
                pv = _dot(a_buf[i], v_ref[pl.ds(pl.multiple_of(j * t, t), t), :])
                accs[h] = pv if diag else accs[h] + pv
                carries[h] = rowsums[i] if diag else carries[h] + rowsums[i]
            return tuple(x for pair in zip(accs, carries) for x in pair)

        per_trip = TILES_PER_TRIP_B
        empty = (None,) * (2 * n_heads)

        def opening(extra):
            return lambda: tiles([qi - r for r in range(extra + 1)], empty, True)

        state = lax.switch(qi % per_trip, [opening(r) for r in range(per_trip)])
        first = qi - 1 - qi % per_trip

        def body(it, st):
            return tiles([first - per_trip * it - r for r in range(per_trip)], st)

        state = lax.fori_loop(0, qi // per_trip, body, state)
        out = state[0]
        for h in range(1, n_heads):
            out = jnp.where(head_of_lane == h, state[2 * h], out)
        o_ref[pl.ds(pl.multiple_of(qi * t, t), t), :] = out
        return c

    lax.fori_loop(0, q_ref.shape[0] // t, q_tile, 0)


def _attn_b(qkv, B, S, d_a, d_b):
    w = WIDTH_B
    n_items = TILES_PER_TRIP_B * (w // HEAD_DIM)
    n_hb = d_b // w
    base = 3 * d_a // w
    seg = d_b // w
    return pl.pallas_call(
        _attn_b_kernel,
        grid=(B, n_hb),
        in_specs=[pl.BlockSpec((None, S, w), lambda b, h: (b, 0, base + h)),
                  pl.BlockSpec((None, S, w), lambda b, h: (b, 0, base + seg + h)),
                  pl.BlockSpec((None, S, w), lambda b, h: (b, 0, base + 2 * seg + h))],
        out_specs=pl.BlockSpec((None, S, w), lambda b, h: (b, 0, h)),
        out_shape=jax.ShapeDtypeStruct((B, S, d_b), F32),
        scratch_shapes=[pltpu.VMEM((n_items, TILE_B, TILE_B), F32), pltpu.VMEM((n_items, TILE_B, 2 * TILE_B), BF16),
                        pltpu.VMEM((n_items, TILE_B, TILE_B), F32), pltpu.VMEM((n_items, TILE_B, TILE_B), BF16)],
        compiler_params=_cparams("parallel", "parallel"),
        name="attn_stickbreak",
    )(qkv, qkv, qkv)


def _layer_norm(r, g, b):
    mu = jnp.mean(r, axis=-1, keepdims=True)
    c = r - mu
    var = jnp.mean(c * c, axis=-1, keepdims=True)
    return c * lax.rsqrt(var + LN_EPS) * g + b


def _rms_norm(y, g):
    ms = jnp.mean(y * y, axis=-1, keepdims=True)
    return y * lax.rsqrt(ms + RMS_EPS) * g


def _silu(g):
    return g * jax.nn.sigmoid(g)


def _post_kernel(ya_ref, yb_ref, x_ref, ga_ref, gb_ref, wo_ref, g1_ref, b1_ref, rw_ref, rbias_ref,
                 wsg_ref, wsu_ref, wsd_ref, x1_ref, base_ref, idx_ref, gate_ref, rank_ref, cnt_ref, carry_ref,
                 *, alpha):
    @pl.when(pl.program_id(0) == 0)
    def _():
        carry_ref[...] = jnp.zeros_like(carry_ref)

    d_a = ya_ref.shape[1]
    na = _rms_norm(ya_ref[...], ga_ref[...]).astype(BF16)
    nb = _rms_norm(yb_ref[...], gb_ref[...]).astype(BF16)
    h = _dot(na, wo_ref[:d_a, :]) + _dot(nb, wo_ref[d_a:, :])
    x1 = _layer_norm(alpha * x_ref[...] + h, g1_ref[...], b1_ref[...])
    x1_ref[...] = _pack_bf16_halves(x1)
    xb = x1.astype(BF16)
    logits = _dot_nt(rw_ref[...], xb)
    hs = _silu(_dot(xb, wsg_ref[...])) * _dot(xb, wsu_ref[...])
    base_ref[...] = alpha * x1 + _dot(hs.astype(BF16), wsd_ref[...])

    tb = TOK_BLOCK
    row = lax.broadcasted_iota(jnp.int32, (tb, tb), 0)
    col = lax.broadcasted_iota(jnp.int32, (tb, tb), 1)
    earlier = (row < col).astype(BF16)
    carry = carry_ref[...]
    for n in range(logits.shape[1] // tb):
        cols = slice(n * tb, (n + 1) * tb)
        idx, gate, rank, carry = _route_block(logits[:, cols], rbias_ref[...], carry, earlier)
        idx_ref[:, cols] = idx
        gate_ref[:, cols] = gate
        rank_ref[:, cols] = rank
    carry_ref[...] = carry
    cnt_ref[...] = carry.astype(jnp.int32)


def _post(ya, yb, xt, gain_a, gain_b, wo_b, g1, b1, rw_b, rbias, wsg_b, wsu_b, wsd_b, alpha):
    T, D = xt.shape
    d_a, d_b = ya.shape[1], yb.shape[1]
    E = rw_b.shape[0]
    De = wsg_b.shape[1]
    tm = 4 * TOK_BLOCK
    row = lambda i: (i, 0)
    col = lambda i: (0, i)
    fix = lambda i: (0, 0)
    return pl.pallas_call(
        functools.partial(_post_kernel, alpha=alpha),
        grid=(T // tm,),
        in_specs=[pl.BlockSpec((tm, d_a), row), pl.BlockSpec((tm, d_b), row), pl.BlockSpec((tm, D), row),
                  pl.BlockSpec((1, d_a), fix), pl.BlockSpec((1, d_b), fix),
                  pl.BlockSpec((d_a + d_b, D), fix), pl.BlockSpec((1, D), fix), pl.BlockSpec((1, D), fix),
                  pl.BlockSpec((E, D), fix), pl.BlockSpec((E, 1), fix),
                  pl.BlockSpec((D, De), fix), pl.BlockSpec((D, De), fix), pl.BlockSpec((De, D), fix)],
        out_specs=[pl.BlockSpec((tm, D // 2), row), pl.BlockSpec((tm, D), row),
                   pl.BlockSpec((TOP_K, tm), col), pl.BlockSpec((TOP_K, tm), col), pl.BlockSpec((TOP_K, tm), col),
                   pl.BlockSpec((E, 1), fix)],
        out_shape=[jax.ShapeDtypeStruct((T, D // 2), jnp.uint32), jax.ShapeDtypeStruct((T, D), F32),
                   jax.ShapeDtypeStruct((TOP_K, T), jnp.int32), jax.ShapeDtypeStruct((TOP_K, T), F32),
                   jax.ShapeDtypeStruct((TOP_K, T), jnp.int32), jax.ShapeDtypeStruct((E, 1), jnp.int32)],
        scratch_shapes=[pltpu.VMEM((E, 1), F32)],
        compiler_params=_cparams("arbitrary"),
        name="post_attn_route",
    )(ya, yb, xt, gain_a, gain_b, wo_b, g1, b1, rw_b, rbias, wsg_b, wsu_b, wsd_b)


def _route_block(logits, rbias, carry, earlier):
    E, tm = logits.shape
    neg = -jnp.inf
    scores = jax.nn.sigmoid(logits)
    sel = scores + rbias
    eidx = lax.broadcasted_iota(jnp.int32, (E, tm), 0).astype(F32)

    def first_argmax(v, ids):
        m = jnp.max(v, axis=0, keepdims=True)
        return m, jnp.min(jnp.where(v == m, ids, float(E)), axis=0, keepdims=True)

    grp_scores = []
    ids = lax.broadcasted_iota(jnp.int32, (EXPERTS_PER_GROUP, tm), 0).astype(F32)
    for g in range(N_GROUPS):
        v = sel[g * EXPERTS_PER_GROUP:(g + 1) * EXPERTS_PER_GROUP, :]
        m1, i1 = first_argmax(v, ids)
        m2 = jnp.max(jnp.where(ids == i1, neg, v), axis=0, keepdims=True)
        grp_scores.append(m1 + m2)
    parts = []
    for g in range(N_GROUPS):
        beaten = jnp.zeros((1, tm), jnp.int32)
        for o in range(N_GROUPS):
            if o != g:
                s, t = grp_scores[o], grp_scores[g]
                beaten = beaten + ((s > t) | ((s == t) & (o < g))).astype(jnp.int32)
        rows = slice(g * EXPERTS_PER_GROUP, (g + 1) * EXPERTS_PER_GROUP)
        parts.append(jnp.where(beaten < TOPK_GROUPS, sel[rows, :], neg))
    selm = jnp.concatenate(parts, axis=0)

    hits, gates, ids_k = [], [], []
    chosen = jnp.zeros((E, tm), F32)
    gate_sum = jnp.zeros((1, tm), F32)
    for k in range(TOP_K):
        _, ik = first_argmax(selm, eidx)
        hit = eidx == ik
        gk = jnp.sum(jnp.where(hit, scores, 0.0), axis=0, keepdims=True)
        selm = jnp.where(hit, neg, selm)
        chosen = jnp.where(hit, 1.0, chosen)
        gate_sum = gate_sum + gk
        hits.append(hit)
        gates.append(gk)
        ids_k.append(ik)

    before = _dot(chosen.astype(BF16), earlier) + carry
    ranks = [jnp.sum(jnp.where(hit, before, 0.0), axis=0, keepdims=True) for hit in hits]
    return (jnp.concatenate(ids_k, axis=0).astype(jnp.int32),
            jnp.concatenate(gates, axis=0) / gate_sum * ROUTED_SCALE,
            jnp.concatenate(ranks, axis=0).astype(jnp.int32),
            carry + jnp.sum(chosen, axis=1, keepdims=True))


def _dest_kernel(idx_ref, rank_ref, start_ref, dest_ref):
    K, tm = idx_ref.shape
    E = start_ref.shape[0]
    eidx = lax.broadcasted_iota(jnp.int32, (E, tm), 0)
    idx = idx_ref[...]
    start = start_ref[...]
    rows = [jnp.sum(jnp.where(eidx == idx[k:k + 1, :], start, 0.0), axis=0, keepdims=True) for k in range(K)]
    dest_ref[...] = jnp.concatenate(rows, axis=0).astype(jnp.int32) + rank_ref[...]


def _dest(idx, rank, start_f):
    K, T = idx.shape
    E = start_f.shape[0]
    tm = 2 * TOK_BLOCK
    col = lambda i: (0, i)
    return pl.pallas_call(
        _dest_kernel,
        grid=(T // tm,),
        in_specs=[pl.BlockSpec((K, tm), col), pl.BlockSpec((K, tm), col), pl.BlockSpec((E, 1), lambda i: (0, 0))],
        out_specs=pl.BlockSpec((K, tm), col),
        out_shape=jax.ShapeDtypeStruct((K, T), jnp.int32),
        compiler_params=_cparams("parallel"),
        name="dest_rows",
    )(idx, rank, start_f)


def _expert_kernel(first_ref, nblk_ref, nused_ref, xs_hbm, wg_ref, wu_ref, wd_ref, ys_hbm,
                   xbuf, ybuf, wgu_s, wd_s, in_sem, out_sem):
    ring = xbuf.shape[0]
    e = pl.program_id(0)
    De = wg_ref.shape[2]
    n = nblk_ref[e]
    g0 = first_ref[e]
    n_used = nused_ref[0]

    def rows(g):
        return pl.ds(pl.multiple_of(g * ROW_BLOCK, ROW_BLOCK), ROW_BLOCK)

    def in_copy(g, slot):
        return pltpu.make_async_copy(xs_hbm.at[rows(g), :], xbuf.at[slot], in_sem.at[slot])

    def out_copy(g, slot):
        return pltpu.make_async_copy(ybuf.at[slot], ys_hbm.at[rows(g), :], out_sem.at[slot])

    @pl.when(e == 0)
    def _():
        for g in range(ring - 2):
            @pl.when(g < n_used)
            def _():
                in_copy(g, g).start()

    def acquire(g, ahead):
        slot = g & (ring - 1)
        in_copy(g, slot).wait()

        @pl.when(g + ahead < n_used)
        def _():
            in_copy(g + ahead, (g + ahead) & (ring - 1)).start()

        @pl.when(g >= ring)
        def _():
            out_copy(g - ring, slot).wait()
        return slot

    def ffn(pieces):
        dh = xbuf.shape[2]
        gus = []
        for slot, rs in pieces:
            lo, hi = _unpack_bf16_halves(xbuf[slot, rs, :])
            gus.append(_dot(lo.astype(BF16), wgu_s[:dh, :]) + _dot(hi.astype(BF16), wgu_s[dh:, :]))
        hs = [(_silu(gu[:, :De]) * gu[:, De:]).astype(BF16) for gu in gus]
        for (slot, rs), h in zip(pieces, hs):
            ybuf[slot, rs, :] = _pack_bf16_halves(_dot(h, wd_s[...]))

    @pl.when(n > 0)
    def _():
        wgu_s[:, :De] = wg_ref[0].astype(BF16)
        wgu_s[:, De:] = wu_ref[0].astype(BF16)
        wd_s[...] = wd_ref[0].astype(BF16)

        def pair(j, c):
            g = g0 + 2 * j
            slots = [acquire(g + b, ring - 2) for b in range(2)]
            ffn([(slot, slice(None)) for slot in slots])
            for b, slot in enumerate(slots):
                out_copy(g + b, slot).start()
            return c

        lax.fori_loop(0, n // 2, pair, 0)

        @pl.when(n % 2 == 1)
        def _():
            g = g0 + n - 1
            slot = acquire(g, ring - 2)
            half = ROW_BLOCK // 2
            ffn([(slot, slice(0, half)), (slot, slice(half, ROW_BLOCK))])
            out_copy(g, slot).start()

    @pl.when(e == pl.num_programs(0) - 1)
    def _():
        for back in range(1, ring + 1):
            @pl.when(n_used >= back)
            def _():
                out_copy(n_used - back, (n_used - back) & (ring - 1)).wait()


def _experts(first_block, n_blocks, n_used, xs, w_gate, w_up, w_down):
    n_rows, Dh = xs.shape
    D = 2 * Dh
    E, _, De = w_gate.shape
    grid_spec = pltpu.PrefetchScalarGridSpec(
        num_scalar_prefetch=3,
        grid=(E,),
        in_specs=[pl.BlockSpec(memory_space=pl.ANY),
                  pl.BlockSpec((1, D, De), lambda e, *_: (e, 0, 0)),
                  pl.BlockSpec((1, D, De), lambda e, *_: (e, 0, 0)),
                  pl.BlockSpec((1, De, D), lambda e, *_: (e, 0, 0))],
        out_specs=pl.BlockSpec(memory_space=pl.ANY),
        scratch_shapes=[pltpu.VMEM((EXPERT_RING, ROW_BLOCK, Dh), jnp.uint32),
                        pltpu.VMEM((EXPERT_RING, ROW_BLOCK, Dh), jnp.uint32),
                        pltpu.VMEM((D, 2 * De), BF16), pltpu.VMEM((De, D), BF16),
                        pltpu.SemaphoreType.DMA((EXPERT_RING,)), pltpu.SemaphoreType.DMA((EXPERT_RING,))],
    )
    return pl.pallas_call(
        _expert_kernel,
        grid_spec=grid_spec,
        out_shape=jax.ShapeDtypeStruct((n_rows, Dh), jnp.uint32),
        compiler_params=_cparams("arbitrary"),
        name="expert_ffn",
    )(first_block, n_blocks, n_used, xs, w_gate, w_up, w_down)


SC_WINDOW = 128


def _sc_mesh():
    return plsc.VectorSubcoreMesh(core_axis_name="core", subcore_axis_name="subcore")


def _sc_workers():
    info = plsc.get_sparse_core_info()
    return info.num_cores, info.num_cores * info.num_subcores


def _sc_worker_id(num_cores):
    return lax.axis_index("subcore") * num_cores + lax.axis_index("core")


def _sc_scatter_rows(x, dest, n_rows):
    T, D = x.shape
    K = dest.shape[0]
    nc, nw = _sc_workers()
    per_w = T // nw
    assert T % nw == 0 and per_w % SC_WINDOW == 0

    @functools.partial(
        pl.kernel, out_type=jax.ShapeDtypeStruct((n_rows, D), x.dtype), mesh=_sc_mesh(),
        scratch_types=[pltpu.VMEM((K, SC_WINDOW), jnp.int32), pltpu.VMEM((SC_WINDOW, D), x.dtype),
                       pltpu.SemaphoreType.DMA((K,))],
        name="sc_dispatch")
    def run(x_hbm, i_hbm, o_hbm, idx_v, rows_v, sems):
        first = _sc_worker_id(nc) * per_w

        @pl.loop(0, per_w // SC_WINDOW)
        def _(c):
            base = first + c * SC_WINDOW
            pltpu.sync_copy(x_hbm.at[pl.ds(base, SC_WINDOW)], rows_v)
            pltpu.sync_copy(i_hbm.at[:, pl.ds(base, SC_WINDOW)], idx_v)
            copies = [pltpu.async_copy(rows_v, o_hbm.at[idx_v.at[k]], sems.at[k]) for k in range(K)]
            for cp in copies:
                cp.wait()

    return run(x, dest)


def _sc_gather_rows(ys, dest_flat):
    N = dest_flat.shape[0]
    D = ys.shape[1]
    nc, nw = _sc_workers()
    per_w = N // nw
    assert N % nw == 0 and per_w % SC_WINDOW == 0

    @functools.partial(
        pl.kernel, out_type=jax.ShapeDtypeStruct((N, D), ys.dtype), mesh=_sc_mesh(),
        scratch_types=[pltpu.VMEM((SC_WINDOW,), jnp.int32), pltpu.VMEM((SC_WINDOW, D), ys.dtype)],
        name="sc_combine_gather")
    def run(y_hbm, i_hbm, o_hbm, idx_v, rows_v):
        first = _sc_worker_id(nc) * per_w

        @pl.loop(0, per_w // SC_WINDOW)
        def _(c):
            base = first + c * SC_WINDOW
            pltpu.sync_copy(i_hbm.at[pl.ds(base, SC_WINDOW)], idx_v)
            pltpu.sync_copy(y_hbm.at[idx_v], rows_v)
            pltpu.sync_copy(rows_v, o_hbm.at[pl.ds(base, SC_WINDOW)])

    return run(ys, dest_flat)


def _final_kernel(z_ref, gate_ref, base_ref, g_ref, b_ref, o_ref):
    gate = gate_ref[...]
    acc_lo = acc_hi = None
    for k in range(TOP_K):
        lo, hi = _unpack_bf16_halves(z_ref[k])
        g = gate[:, k:k + 1]
        acc_lo = g * lo if acc_lo is None else acc_lo + g * lo
        acc_hi = g * hi if acc_hi is None else acc_hi + g * hi
    acc = base_ref[...] + jnp.concatenate([acc_lo, acc_hi], axis=-1)
    o_ref[...] = _layer_norm(acc, g_ref[...], b_ref[...])


def _final(z, gate, base, g2, b2, first_token):
    T, D = base.shape
    tb = TOK_BLOCK
    off = first_token // tb
    return pl.pallas_call(
        _final_kernel,
        grid=(z.shape[1] // tb,),
        in_specs=[pl.BlockSpec((TOP_K, tb, D // 2), lambda i: (0, i, 0)),
                  pl.BlockSpec((tb, TOP_K), lambda i: (i + off, 0)),
                  pl.BlockSpec((tb, D), lambda i: (i + off, 0)),
                  pl.BlockSpec((1, D), lambda i: (0, 0)),
                  pl.BlockSpec((1, D), lambda i: (0, 0))],
        out_specs=pl.BlockSpec((tb, D), lambda i: (i + off, 0)),
        out_shape=jax.ShapeDtypeStruct((T, D), F32),
        input_output_aliases={2: 0},
        compiler_params=_cparams("parallel"),
        name="final_sum_ln",
    )(z, gate, base, g2, b2)


def _layer(x, w_in, rel_bias, gain_a, gain_b, w_out, ln1_g, ln1_b, router_w, router_bias,
           w_gate, w_up, w_down, ws_gate, ws_up, ws_down, ln2_g, ln2_b, alpha):
    B, S, D = x.shape
    T = B * S
    d_a = gain_a.shape[0]
    d_b = gain_b.shape[0]
    assert S % TILE_B == 0 and T % 512 == 0 and d_a % LANES == 0 and d_b % WIDTH_B == 0 and (3 * d_a) % WIDTH_B == 0
    xt = x.reshape(T, D)

    scale = HEAD_DIM ** -0.5
    col = jnp.arange(w_in.shape[1])
    is_q = (col < d_a) | ((col >= 3 * d_a) & (col < 3 * d_a + d_b))
    w_in_b = (w_in * jnp.where(is_q, scale, 1.0)[None, :]).astype(BF16)

    qkv = _qkv_proj(xt, w_in_b).reshape(B, S, -1)
    ya = _attn_a(qkv, _rel_bias_by_offset(rel_bias), B, S, d_a).reshape(T, d_a)
    yb = _attn_b(qkv, B, S, d_a, d_b).reshape(T, d_b)

    x1p, base, idx, gate, rank, cnt = _post(
        ya, yb, xt, gain_a[None], gain_b[None], w_out.astype(BF16), ln1_g[None], ln1_b[None],
        router_w.T.astype(BF16), router_bias[:, None].astype(F32),
        ws_gate.astype(BF16), ws_up.astype(BF16), ws_down.astype(BF16), alpha)
    cnt = cnt[:, 0]
    padded = (cnt + ROW_BLOCK - 1) // ROW_BLOCK * ROW_BLOCK
    ends = jnp.cumsum(padded)
    start = (ends - padded).astype(jnp.int32)
    n_rows = T * TOP_K + N_EXPERTS * ROW_BLOCK
    first_block = start // ROW_BLOCK
    n_blocks = (padded // ROW_BLOCK).astype(jnp.int32)
    n_used = (ends[-1:] // ROW_BLOCK).astype(jnp.int32)
    dest = _dest(idx, rank, start.astype(F32)[:, None])

    xs = _sc_scatter_rows(x1p, dest, n_rows)
    ys = _experts(first_block, n_blocks, n_used, xs, w_gate, w_up, w_down)
    gate_t = gate.T
    out = base
    tg = T // COMBINE_GROUPS
    for q in range(COMBINE_GROUPS):
        z = _sc_gather_rows(ys, dest[:, q * tg:(q + 1) * tg].reshape(-1)).reshape(TOP_K, tg, D // 2)
        out = _final(z, gate_t, out, ln2_g[None], ln2_b[None], q * tg)
    return out.reshape(B, S, D)


def kernel(x, w_in, rel_bias, gain_a, gain_b, w_out, ln1_g, ln1_b, router_w, router_bias,
           w_gate, w_up, w_down, ws_gate, ws_up, ws_down, ln2_g, ln2_b):
    depth = w_in.shape[0]
    alpha = (2 * depth) ** 0.25
    for l in range(depth):
        x = _layer(x, w_in[l], rel_bias[l], gain_a[l], gain_b[l], w_out[l], ln1_g[l], ln1_b[l],
                   router_w[l], router_bias[l], w_gate[l], w_up[l], w_down[l],
                   ws_gate[l], ws_up[l], ws_down[l], ln2_g[l], ln2_b[l], alpha)
    return x
```
